```python
import jax, jax.numpy as jnp
from jax import lax
import numpy as np

D_MODEL = 1024
BATCH = 2
SEQ = 8192
DEPTH = 1

D_MIX = D_MODEL
D_ATTN = D_MIX // 2
HEAD_DIM_ATTN = 64
N_HEADS_ATTN = D_ATTN // HEAD_DIM_ATTN
D_MLSTM = D_MIX - D_ATTN
HEAD_DIM_MLSTM = 128
N_HEADS_MLSTM = D_MLSTM // HEAD_DIM_MLSTM
IN_SPLITS = (D_ATTN, 2 * D_ATTN, 3 * D_ATTN, 3 * D_ATTN + D_MLSTM, 3 * D_ATTN + 2 * D_MLSTM,
             3 * D_ATTN + 3 * D_MLSTM, 3 * D_ATTN + 3 * D_MLSTM + N_HEADS_MLSTM)
D_IN = 3 * D_ATTN + 3 * D_MLSTM + 2 * N_HEADS_MLSTM
DILATED_BRANCHES = ((128, 1), (512, 4), (2048, 16))
MAX_WINDOW = 2048
Q_BLOCK = 128
CONV_K = 4
MLSTM_CHUNK = 64
N_EXPERTS = 32
TOP_K = 4
D_FF = D_MODEL
SWIGLU_LIMIT = 7.0
SWIGLU_ALPHA = 1.702
EXPERT_BLOCK = 128
EPS = 1e-6

kernel_name = 'hybrid_dilated_attn_mlstm_moe'


def rmsnorm(x, g):
    xf = x.astype(jnp.float32)
    y = xf * lax.rsqrt(jnp.mean(xf * xf, axis=-1, keepdims=True) + EPS)
    return (y * g.astype(jnp.float32)).astype(x.dtype)


def alibi_slopes(n):
    return jnp.exp2(-8.0 * jnp.arange(1, n + 1, dtype=jnp.float32) / n)


def dilated_sliding_attention(q, k, v, slopes):
    B, H, S, Dh = q.shape
    pad = ((0, 0), (0, 0), (MAX_WINDOW, 0), (0, 0))
    k_pad = jnp.pad(k, pad)
    v_pad = jnp.pad(v, pad)
    n_blocks = S // Q_BLOCK

    def query_block(blk):
        t0 = blk * Q_BLOCK
        qb = lax.dynamic_slice_in_dim(q, t0, Q_BLOCK, axis=2)
        pos = t0 + jnp.arange(Q_BLOCK)
        outs, lses = [], []
        for window, dil in DILATED_BRANCHES:
            dist = dil * jnp.arange(window // dil + 1)
            kpos = pos[:, None] - dist[None, :]
            idx = (kpos + MAX_WINDOW).reshape(-1)
            kg = jnp.take(k_pad, idx, axis=2).reshape(B, H, Q_BLOCK, -1, Dh)
            vg = jnp.take(v_pad, idx, axis=2).reshape(B, H, Q_BLOCK, -1, Dh)
            s = jnp.einsum('bhqd,bhqnd->bhqn', qb, kg).astype(jnp.float32)
            s = s - slopes[:, None, None] * dist.astype(jnp.float32)
            s = jnp.where(kpos >= 0, s, -jnp.inf)
            m = jnp.max(s, axis=-1, keepdims=True)
            p = jnp.exp(s - m)
            denom = jnp.sum(p, axis=-1)
            o = jnp.einsum('bhqn,bhqnd->bhqd', p, vg.astype(jnp.float32)) / denom[..., None]
            outs.append(o)
            lses.append(m[..., 0] + jnp.log(denom))
        w = jax.nn.softmax(jnp.stack(lses), axis=0)
        return jnp.einsum('rbhq,rbhqd->bhqd', w, jnp.stack(outs))

    o = lax.map(query_block, jnp.arange(n_blocks))
    return jnp.moveaxis(o, 0, 2).reshape(B, H, S, Dh).astype(q.dtype)


def mlstm_chunkwise(q, k, v, i_pre, f_pre):
    B, NH, S, DH = q.shape
    L = MLSTM_CHUNK
    nc = S // L

    def chunks(t):
        return jnp.moveaxis(t.astype(jnp.float32).reshape(B, NH, nc, L, *t.shape[3:]), 2, 0)

    log_f = jax.nn.log_sigmoid(f_pre.astype(jnp.float32))
    causal = jnp.tril(jnp.ones((L, L), dtype=bool))

    def step(carry, inp):
        C, n, m = carry
        qc, kc, vc, ic, lfc = inp
        b = jnp.cumsum(lfc, axis=-1)
        log_d = jnp.where(causal, b[..., :, None] - b[..., None, :] + ic[..., None, :], -jnp.inf)
        log_inter = b + m[..., None]
        m_t = jnp.maximum(log_inter, jnp.max(log_d, axis=-1))
        d = jnp.exp(log_d - m_t[..., None])
        inter = jnp.exp(log_inter - m_t)
        s = jnp.einsum('bhtd,bhsd->bhts', qc, kc) * d
        num = jnp.einsum('bhts,bhse->bhte', s, vc) + inter[..., None] * jnp.einsum('bhtd,bhde->bhte', qc, C)
        den = jnp.sum(s, axis=-1) + inter * jnp.einsum('bhtd,bhd->bht', qc, n)
        h = num / jnp.maximum(jnp.abs(den), jnp.exp(-m_t))[..., None]
        b_last = b[..., -1]
        log_w = b_last[..., None] - b + ic
        m_new = jnp.maximum(b_last + m, jnp.max(log_w, axis=-1))
        w = jnp.exp(log_w - m_new[..., None])
        decay = jnp.exp(b_last + m - m_new)
        C_new = decay[..., None, None] * C + jnp.einsum('bhs,bhsd,bhse->bhde', w, kc, vc)
        n_new = decay[..., None] * n + jnp.einsum('bhs,bhsd->bhd', w, kc)
        return (C_new, n_new, m_new), h

    init = (jnp.zeros((B, NH, DH, DH), jnp.float32), jnp.zeros((B, NH, DH), jnp.float32),
            jnp.zeros((B, NH), jnp.float32))
    _, hs = lax.scan(step, init, (chunks(q), chunks(k), chunks(v), chunks(i_pre), chunks(log_f)))
    return jnp.moveaxis(hs, 0, 2).reshape(B, NH, S, DH)


def moe_ffn(h, w_router, b_router, w_gate_up, b_gate_up, w_down, b_down):
    B, S, D = h.shape
    T = B * S
    A = T * TOP_K
    xf = h.reshape(T, D)
    logits = jnp.dot(xf, w_router).astype(jnp.float32) + b_router.astype(jnp.float32)
    top_v, top_e = lax.top_k(logits, TOP_K)
    gates = jax.nn.softmax(top_v, axis=-1)
    flat_e = top_e.reshape(A)
    order = jnp.argsort(flat_e).astype(jnp.int32)
    sizes = jnp.bincount(flat_e, length=N_EXPERTS)
    starts = jnp.cumsum(sizes) - sizes
    psizes = (sizes + EXPERT_BLOCK - 1) // EXPERT_BLOCK * EXPERT_BLOCK
    pends = jnp.cumsum(psizes)
    pstarts = pends - psizes
    e_sorted = flat_e[order]
    dest = pstarts[e_sorted] + (jnp.arange(A) - starts[e_sorted])
    n_blocks = -(-A // EXPERT_BLOCK) + N_EXPERTS
    n_slots = n_blocks * EXPERT_BLOCK
    slot_assign = jnp.full((n_slots,), A, jnp.int32).at[dest].set(order)
    slot_token = slot_assign // TOP_K
    slot_gate = jnp.concatenate([gates.reshape(A), jnp.zeros((1,), jnp.float32)])[slot_assign]
    block_expert = jnp.minimum(jnp.searchsorted(pends, jnp.arange(n_blocks) * EXPERT_BLOCK, side='right'),
                               N_EXPERTS - 1)
    x_ext = jnp.concatenate([xf, jnp.zeros((1, D), xf.dtype)], axis=0)

    def expert_block(args):
        tok, e = args
        xb = x_ext[tok]
        gu = jnp.dot(xb, w_gate_up[e]) + b_gate_up[e]
        gate = jnp.minimum(gu[:, :D_FF], SWIGLU_LIMIT)
        up = jnp.clip(gu[:, D_FF:], -SWIGLU_LIMIT, SWIGLU_LIMIT)
        act = gate * jax.nn.sigmoid(SWIGLU_ALPHA * gate) * (up + 1.0)
        return jnp.dot(act, w_down[e]) + b_down[e]

    ys = lax.map(expert_block, (slot_token.reshape(n_blocks, EXPERT_BLOCK), block_expert))
    ys = ys.reshape(n_slots, D) * slot_gate[:, None].astype(ys.dtype)
    out = jax.ops.segment_sum(ys, slot_token, num_segments=T + 1)[:T]
    return out.reshape(B, S, D)


def setup_inputs(seed: int = 0) -> dict:
    key = jax.random.key(seed)
    ks = jax.random.split(key, 20)

    def nrm(k, shape, scale):
        return scale * jax.random.normal(k, shape, jnp.float32)

    L = DEPTH
    return {
        'x': nrm(ks[0], (BATCH, SEQ, D_MODEL), 1.0),
        'norm1_g': 1.0 + nrm(ks[1], (L, D_MODEL), 0.1),
        'w_in': nrm(ks[2], (L, D_MODEL, D_IN), D_MODEL ** -0.5),
        'q_norm_g': 1.0 + nrm(ks[3], (L, HEAD_DIM_ATTN), 0.1),
        'k_norm_g': 1.0 + nrm(ks[4], (L, HEAD_DIM_ATTN), 0.1),
        'conv_w': nrm(ks[5], (L, CONV_K, D_MLSTM), CONV_K ** -0.5),
        'conv_b': nrm(ks[6], (L, D_MLSTM), 0.02),
        'w_mq': nrm(ks[7], (L, N_HEADS_MLSTM, HEAD_DIM_MLSTM, HEAD_DIM_MLSTM), HEAD_DIM_MLSTM ** -0.5),
        'w_mk': nrm(ks[8], (L, N_HEADS_MLSTM, HEAD_DIM_MLSTM, HEAD_DIM_MLSTM), HEAD_DIM_MLSTM ** -0.5),
        'b_igate': nrm(ks[9], (L, N_HEADS_MLSTM), 0.1),
        'b_fgate': jnp.linspace(3.0, 6.0, N_HEADS_MLSTM)[None, :] + nrm(ks[10], (L, N_HEADS_MLSTM), 0.1),
        'mh_norm_g': 1.0 + nrm(ks[11], (L, N_HEADS_MLSTM, HEAD_DIM_MLSTM), 0.1),
        'w_out': nrm(ks[12], (L, D_MIX, D_MODEL), D_MIX ** -0.5),
        'norm2_g': 1.0 + nrm(ks[13], (L, D_MODEL), 0.1),
        'w_router': nrm(ks[14], (L, D_MODEL, N_EXPERTS), D_MODEL ** -0.5),
        'b_router': nrm(ks[15], (L, N_EXPERTS), 0.01),
        'w_gate_up': nrm(ks[16], (L, N_EXPERTS, D_MODEL, 2 * D_FF), D_MODEL ** -0.5),
        'b_gate_up': nrm(ks[17], (L, N_EXPERTS, 2 * D_FF), 0.02),
        'w_down': nrm(ks[18], (L, N_EXPERTS, D_FF, D_MODEL), D_FF ** -0.5),
        'b_down': nrm(ks[19], (L, N_EXPERTS, D_MODEL), 0.02),
    }


def reference(x, norm1_g, w_in, q_norm_g, k_norm_g, conv_w, conv_b, w_mq, w_mk, b_igate, b_fgate,
              mh_norm_g, w_out, norm2_g, w_router, b_router, w_gate_up, b_gate_up, w_down, b_down):
    B, S, _ = x.shape
    slopes = alibi_slopes(N_HEADS_ATTN)

    def heads(t, nh, dh):
        return t.reshape(B, S, nh, dh).transpose(0, 2, 1, 3)

    for l in range(DEPTH):
        h = rmsnorm(x, norm1_g[l])
        z = jnp.einsum('bsd,de->bse', h, w_in[l])
        q_a, k_a, v_a, x_m, v_m, o_m, i_pre, f_pre = jnp.split(z, IN_SPLITS, axis=-1)

        qa = rmsnorm(heads(q_a, N_HEADS_ATTN, HEAD_DIM_ATTN), q_norm_g[l]) * (HEAD_DIM_ATTN ** -0.5)
        ka = rmsnorm(heads(k_a, N_HEADS_ATTN, HEAD_DIM_ATTN), k_norm_g[l])
        va = heads(v_a, N_HEADS_ATTN, HEAD_DIM_ATTN)
        attn = dilated_sliding_attention(qa, ka, va, slopes)
        attn = attn.transpose(0, 2, 1, 3).reshape(B, S, D_ATTN).astype(x.dtype)

        xp = jnp.pad(x_m, ((0, 0), (CONV_K - 1, 0), (0, 0)))
        xc = conv_b[l] + xp[:, 0:S, :] * conv_w[l, 0]
        for j in range(1, CONV_K):
            xc = xc + xp[:, j:j + S, :] * conv_w[l, j]
        xc = jax.nn.silu(xc).reshape(B, S, N_HEADS_MLSTM, HEAD_DIM_MLSTM)
        qm = jnp.einsum('bshd,hde->bhse', xc, w_mq[l])
        km = jnp.einsum('bshd,hde->bhse', xc, w_mk[l]) * (HEAD_DIM_MLSTM ** -0.5)
        vm = heads(v_m, N_HEADS_MLSTM, HEAD_DIM_MLSTM)
        ig = (i_pre + b_igate[l]).transpose(0, 2, 1)
        fg = (f_pre + b_fgate[l]).transpose(0, 2, 1)
        hm = mlstm_chunkwise(qm, km, vm, ig, fg)
        hm = rmsnorm(hm, mh_norm_g[l][:, None, :])
        hm = hm.transpose(0, 2, 1, 3).reshape(B, S, D_MLSTM).astype(x.dtype)
        mlstm_out = jax.nn.sigmoid(o_m) * hm

        mixed = jnp.concatenate([attn, mlstm_out], axis=-1)
        x = x + jnp.einsum('bse,ed->bsd', mixed, w_out[l])

        x = x + moe_ffn(rmsnorm(x, norm2_g[l]), w_router[l], b_router[l], w_gate_up[l], b_gate_up[l],
                        w_down[l], b_down[l])
    return x
```

```python
import functools

import jax
import jax.numpy as jnp
from jax import lax
from jax.experimental import pallas as pl
from jax.experimental.pallas import tpu as pltpu

F32 = jnp.float32
BF16 = jnp.bfloat16
NEG_INF = float("-inf")

D_MODEL = 1024
D_ATTN = 512
HEAD_DIM_ATTN = 64
N_HEADS_ATTN = 8
D_MLSTM = 512
HEAD_DIM_MLSTM = 128
N_HEADS_MLSTM = 4
CONV_K = 4
N_EXPERTS = 32
TOP_K = 4
D_FF = 1024
SWIGLU_LIMIT = 7.0
SWIGLU_ALPHA = 1.702
EPS = 1e-6
DILATIONS = (1, 4, 16)
BRANCH_SPAN = 128

LANES = 128
SUBLANES = 8

IN_PROJ_ROWS = 256
ATTN_ROWS = 2048
MLSTM_CHUNK = 256
ROUTE_ROWS = 256
EXPERT_ROWS = 256
FF_CHUNK = 512
GATHER_ROWS = 1024
COMBINE_ROWS = 128


def _vmem_limit(mib):
    return pltpu.CompilerParams(vmem_limit_bytes=mib * 1024 * 1024)


def _split3(a):
    p1 = a.astype(BF16)
    r1 = a - p1.astype(F32)
    p2 = r1.astype(BF16)
    r2 = r1 - p2.astype(F32)
    return p1, p2, r2.astype(BF16)


def _dot(a, b):
    return jnp.dot(a, b, preferred_element_type=F32)


def _dot_nt(a, b):
    return lax.dot_general(a, b, (((1,), (1,)), ((), ())), preferred_element_type=F32)


def _dot_tn(a, b):
    return lax.dot_general(a, b, (((0,), (0,)), ((), ())), preferred_element_type=F32)


def _in_proj_kernel(x_ref, g1_ref, wqkv_ref, wm_ref, wgh_ref, wgl_ref, qg_ref, kg_ref, hsum_ref,
                    q_ref, k_ref, v_ref, xm_ref, vm_ref, om_ref, gates_ref):
    x = x_ref[...]
    h = x * lax.rsqrt(jnp.mean(x * x, axis=-1, keepdims=True) + EPS) * g1_ref[...]
    hb = h.astype(BF16)
    hl = (h - hb.astype(F32)).astype(BF16)

    def head_norm(z, g_ref):
        parts = []
        for c in range(D_ATTN // LANES):
            zc = z[:, c * LANES:(c + 1) * LANES]
            sq = zc * zc
            sh = sq.astype(BF16)
            sl = (sq - sh.astype(F32)).astype(BF16)
            ms = (_dot(sh, hsum_ref[...]) + _dot(sl, hsum_ref[...])) * (1.0 / HEAD_DIM_ATTN)
            parts.append(zc * lax.rsqrt(ms + EPS))
        return jnp.concatenate(parts, axis=-1) * g_ref[...]

    zq = _dot(hb, wqkv_ref[:, 0:D_ATTN])
    q_ref[...] = head_norm(zq, qg_ref) * (HEAD_DIM_ATTN ** -0.5)
    zk = _dot(hb, wqkv_ref[:, D_ATTN:2 * D_ATTN])
    k_ref[...] = head_norm(zk, kg_ref)
    v_ref[...] = _dot(hb, wqkv_ref[:, 2 * D_ATTN:3 * D_ATTN])
    xm_ref[...] = _dot(hb, wm_ref[:, 0:D_MLSTM])
    vm_ref[...] = _dot(hb, wm_ref[:, D_MLSTM:2 * D_MLSTM])
    om_ref[...] = _dot(hb, wm_ref[:, 2 * D_MLSTM:3 * D_MLSTM])
    gates_ref[...] = _dot(hb, wgh_ref[...]) + _dot(hl, wgh_ref[...]) + _dot(hb, wgl_ref[...])


def _in_proj(x2, g1, w_in, q_g, k_g):
    T = x2.shape[0]
    tm = IN_PROJ_ROWS
    n_qkv = 3 * D_ATTN
    n_m = 3 * D_MLSTM
    wqkv = w_in[:, :n_qkv].astype(BF16)
    wm = w_in[:, n_qkv:n_qkv + n_m].astype(BF16)
    wg = jnp.pad(w_in[:, n_qkv + n_m:], ((0, 0), (0, LANES - 2 * N_HEADS_MLSTM)))
    wgh = wg.astype(BF16)
    wgl = (wg - wgh.astype(F32)).astype(BF16)
    lane = jnp.arange(LANES)
    hsum = (lane[:, None] // HEAD_DIM_ATTN == lane[None, :] // HEAD_DIM_ATTN).astype(BF16)
    qg = jnp.tile(q_g, N_HEADS_ATTN)[None, :]
    kg = jnp.tile(k_g, N_HEADS_ATTN)[None, :]
    row = lambda n: pl.BlockSpec((tm, n), lambda i: (i, 0))
    full = lambda a: pl.BlockSpec(a.shape, lambda i: (0,) * a.ndim)
    outs = [jax.ShapeDtypeStruct((T, D_ATTN), F32)] * 3 + [jax.ShapeDtypeStruct((T, D_MLSTM), F32)] * 3
    outs.append(jax.ShapeDtypeStruct((T, LANES), F32))
    g1r = g1[None, :]
    return pl.pallas_call(
        _in_proj_kernel,
        grid=(T // tm,),
        in_specs=[row(D_MODEL), full(g1r), full(wqkv), full(wm), full(wgh), full(wgl), full(qg), full(kg),
                  full(hsum)],
        out_specs=[row(D_ATTN)] * 3 + [row(D_MLSTM)] * 3 + [row(LANES)],
        out_shape=outs,
        compiler_params=_vmem_limit(48),
        name="in_proj",
    )(x2, g1r, wqkv, wm, wgh, wgl, qg, kg, hsum)


def _attn_tiles():
    tiles = []
    for br, d in enumerate(DILATIONS):
        group = d * BRANCH_SPAN
        for u in range(ATTN_ROWS // group):
            for c in range(d):
                tiles.append((br, d, u * group + c))
    return tiles


def _attn_kernel(slope_ref, q_ref, kp_ref, kc_ref, vp_ref, vc_ref, o_ref,
                 kwin, vwin, bias_scr, m_scr, l_scr, acc_scr):
    W = ATTN_ROWS
    n = BRANCH_SPAN
    step = pl.program_id(2)
    kwin[0:W, :] = kp_ref[...]
    kwin[W:2 * W, :] = kc_ref[...]
    vwin[0:W, :] = vp_ref[...]
    vwin[W:2 * W, :] = vc_ref[...]

    lane = lax.broadcasted_iota(jnp.int32, (n, LANES), 1)
    first_head = lane < HEAD_DIM_ATTN
    row = lax.broadcasted_iota(jnp.int32, (n, 2 * n), 0)
    col = lax.broadcasted_iota(jnp.int32, (n, 2 * n), 1)
    j = n + row - col
    valid = jnp.logical_and(j >= 0, j <= n)
    jf = j.astype(F32)
    for br, d in enumerate(DILATIONS):
        for hh in range(2):
            slope = slope_ref[:, hh * HEAD_DIM_ATTN:hh * HEAD_DIM_ATTN + 1]
            b = jnp.where(valid, -(slope * float(d)) * jf, NEG_INF)
            bias_scr[br, hh, 0] = b
            bias_scr[br, hh, 1] = jnp.where(col >= n, b, NEG_INF)

    for br, d, qs in _attn_tiles():
        lo = W + qs - n * d
        q = q_ref[pl.ds(qs, n, stride=d), :]
        kk = jnp.concatenate([kwin[pl.ds(lo, n, stride=d), :], kwin[pl.ds(W + qs, n, stride=d), :]],
                             axis=0).astype(BF16)
        vv = jnp.concatenate([vwin[pl.ds(lo, n, stride=d), :], vwin[pl.ds(W + qs, n, stride=d), :]],
                             axis=0).astype(BF16)
        if lo < W:
            sel = jnp.where(step == 0, 1, 0)
        else:
            sel = 0
        res = []
        for hh in range(2):
            qm = jnp.where(first_head if hh == 0 else jnp.logical_not(first_head), q, 0.0).astype(BF16)
            s = _dot_nt(qm, kk) + bias_scr[br, hh, sel]
            m = jnp.max(s, axis=-1, keepdims=True)
            p = jnp.exp(s - m)
            l = jnp.sum(p, axis=-1, keepdims=True)
            res.append((m, l, _dot(p.astype(BF16), vv)))
        rows = pl.ds(qs, n, stride=d)
        m_scr[br, rows, :] = jnp.where(first_head, res[0][0], res[1][0])
        l_scr[br, rows, :] = jnp.where(first_head, res[0][1], res[1][1])
        acc_scr[br, rows, :] = jnp.where(first_head, res[0][2], res[1][2])

    chunk = 256

    def combine(i, carry):
        r = pl.ds(pl.multiple_of(i * chunk, chunk), chunk)
        m0, m1, m2 = m_scr[0, r, :], m_scr[1, r, :], m_scr[2, r, :]
        mm = jnp.maximum(jnp.maximum(m0, m1), m2)
        w0, w1, w2 = jnp.exp(m0 - mm), jnp.exp(m1 - mm), jnp.exp(m2 - mm)
        num = w0 * acc_scr[0, r, :] + w1 * acc_scr[1, r, :] + w2 * acc_scr[2, r, :]
        den = w0 * l_scr[0, r, :] + w1 * l_scr[1, r, :] + w2 * l_scr[2, r, :]
        o_ref[r, :] = num / den
        return carry

    lax.fori_loop(0, W // chunk, combine, 0)


def _attention(q, k, v, B, S):
    W = ATTN_ROWS
    nb = S // W
    n_pairs = D_ATTN // LANES
    slopes = jnp.exp2(-8.0 * jnp.arange(1, N_HEADS_ATTN + 1, dtype=F32) / N_HEADS_ATTN)
    slope_l = jnp.repeat(slopes, HEAD_DIM_ATTN).reshape(n_pairs, 1, LANES)
    cur = pl.BlockSpec((W, LANES), lambda b, p, i: (b * nb + i, p))
    prev = pl.BlockSpec((W, LANES), lambda b, p, i: (b * nb + jnp.maximum(i - 1, 0), p))
    n = BRANCH_SPAN
    nbr = len(DILATIONS)
    return pl.pallas_call(
        _attn_kernel,
        grid=(B, n_pairs, nb),
        in_specs=[pl.BlockSpec((None, 1, LANES), lambda b, p, i: (p, 0, 0)), cur, prev, cur, prev, cur],
        out_specs=cur,
        out_shape=jax.ShapeDtypeStruct((B * S, D_ATTN), F32),
        scratch_shapes=[
            pltpu.VMEM((2 * W, LANES), F32), pltpu.VMEM((2 * W, LANES), F32),
            pltpu.VMEM((nbr, 2, 2, n, 2 * n), F32),
            pltpu.VMEM((nbr, W, LANES), F32), pltpu.VMEM((nbr, W, LANES), F32),
            pltpu.VMEM((nbr, W, LANES), F32),
        ],
        compiler_params=_vmem_limit(48),
        name="attention",
    )(slope_l, q, k, k, v, v)


def _mlstm_kernel(xm_ref, vm_ref, om_ref, gcol_ref, grow_ref, cw_ref, cb_ref, wq_ref, wk_ref,
                  bcol_ref, brow_ref, g_ref, o_ref, xs_ref, xtail_ref, c_ref, n_ref, m_ref):
    L = MLSTM_CHUNK
    dh = HEAD_DIM_MLSTM
    nh = N_HEADS_MLSTM

    @pl.when(pl.program_id(1) == 0)
    def _():
        xtail_ref[...] = jnp.zeros_like(xtail_ref)
        c_ref[...] = jnp.zeros_like(c_ref)
        n_ref[...] = jnp.zeros_like(n_ref)
        m_ref[...] = jnp.zeros_like(m_ref)

    x = xm_ref[...]
    xs_ref[0:SUBLANES, :] = xtail_ref[...]
    xs_ref[SUBLANES:SUBLANES + L, :] = x
    xtail_ref[...] = x[L - SUBLANES:L, :]
    xc = cb_ref[...] + x * cw_ref[CONV_K - 1:CONV_K, :]
    for back in range(1, CONV_K):
        xc = xc + xs_ref[SUBLANES - back:SUBLANES - back + L, :] * cw_ref[CONV_K - 1 - back:CONV_K - back, :]
    xc = xc * jax.nn.sigmoid(xc)

    gc = gcol_ref[...] + bcol_ref[...]
    gr = grow_ref[...] + brow_ref[...]
    ri = lax.broadcasted_iota(jnp.int32, (L, L), 0)
    ci = lax.broadcasted_iota(jnp.int32, (L, L), 1)
    causal = ri >= ci
    tril = jnp.where(causal, 1.0, 0.0).astype(BF16)
    triu = jnp.where(ri <= ci, 1.0, 0.0).astype(BF16)
    c1, c2, c3 = _split3(jax.nn.log_sigmoid(gc))
    bc = _dot(tril, c1) + _dot(tril, c2) + _dot(tril, c3)
    r1, r2, r3 = _split3(jax.nn.log_sigmoid(gr))
    brw = _dot(r1, triu) + _dot(r2, triu) + _dot(r3, triu)

    for h in range(nh):
        hs = slice(h * dh, (h + 1) * dh)
        b_col = bc[:, nh + h:nh + h + 1]
        a_col = gc[:, h:h + 1] - b_col
        a_row = gr[h:h + 1, :] - brw[nh + h:nh + h + 1, :]
        m_prev = m_ref[h:h + 1, 0:1]
        log_d = jnp.where(causal, b_col + a_row, NEG_INF)
        log_inter = b_col + m_prev
        m_t = jnp.maximum(log_inter, jnp.max(log_d, axis=-1, keepdims=True))
        d = jnp.exp(log_d - m_t)
        inter = jnp.exp(log_inter - m_t)

        xh = xc[:, hs].astype(BF16)
        qf = _dot(xh, wq_ref[h])
        kf = _dot(xh, wk_ref[h]) * (dh ** -0.5)
        qb = qf.astype(BF16)
        vb = vm_ref[:, hs].astype(BF16)
        s = _dot_nt(qb, kf.astype(BF16)) * d
        c_old = c_ref[h]
        n_old = n_ref[h:h + 1, :]
        num = _dot(s.astype(BF16), vb) + inter * _dot(qb, c_old.astype(BF16))
        den = jnp.sum(s, axis=-1, keepdims=True) + inter * jnp.sum(qf * n_old, axis=-1, keepdims=True)
        hh = num / jnp.maximum(jnp.abs(den), jnp.exp(-m_t))

        b_last = b_col[L - 1:L, :]
        log_w = b_last + a_col
        m_new = jnp.maximum(b_last + m_prev, jnp.max(log_w, axis=0, keepdims=True))
        kw = kf * jnp.exp(log_w - m_new)
        decay = jnp.exp(b_last + m_prev - m_new)
        c_ref[h] = decay * c_old + _dot_tn(kw.astype(BF16), vb)
        n_ref[h:h + 1, :] = decay * n_old + jnp.sum(kw, axis=0, keepdims=True)
        m_ref[h:h + 1, :] = jnp.broadcast_to(m_new, (1, LANES))

        hn = hh * lax.rsqrt(jnp.mean(hh * hh, axis=-1, keepdims=True) + EPS) * g_ref[:, hs]
        o_ref[:, hs] = jax.nn.sigmoid(om_ref[:, hs]) * hn


def _mlstm(xm, vm, om, gates, conv_w, conv_b, w_mq, w_mk, b_i, b_f, mh_g, B, S):
    L = MLSTM_CHUNK
    nc = S // L
    nh = N_HEADS_MLSTM
    ng = 2 * nh
    grow = gates[:, :ng].reshape(B, S, ng).transpose(0, 2, 1)
    bias = jnp.concatenate([b_i, b_f])
    bcol = jnp.pad(bias, (0, LANES - ng))[None, :]
    brow = bias[:, None]
    wq = w_mq.astype(BF16)
    wk = w_mk.astype(BF16)
    cb = conv_b[None, :]
    g = mh_g.reshape(1, D_MLSTM)
    rows = pl.BlockSpec((L, D_MLSTM), lambda b, c: (b * nc + c, 0))
    full = lambda a: pl.BlockSpec(a.shape, lambda b, c: (0,) * a.ndim)
    return pl.pallas_call(
        _mlstm_kernel,
        grid=(B, nc),
        in_specs=[rows, rows, rows, pl.BlockSpec((L, LANES), lambda b, c: (b * nc + c, 0)),
                  pl.BlockSpec((None, ng, L), lambda b, c: (b, 0, c)),
                  full(conv_w), full(cb), full(wq), full(wk), full(bcol), full(brow), full(g)],
        out_specs=rows,
        out_shape=jax.ShapeDtypeStruct((B * S, D_MLSTM), F32),
        scratch_shapes=[
            pltpu.VMEM((L + SUBLANES, D_MLSTM), F32), pltpu.VMEM((SUBLANES, D_MLSTM), F32),
            pltpu.VMEM((nh, HEAD_DIM_MLSTM, HEAD_DIM_MLSTM), F32),
            pltpu.VMEM((SUBLANES, LANES), F32), pltpu.VMEM((SUBLANES, LANES), F32),
        ],
        compiler_params=_vmem_limit(32),
        name="mlstm",
    )(xm, vm, om, gates, grow, conv_w, cb, wq, wk, bcol, brow, g)


def _out_route_kernel(attn_ref, ml_ref, x_ref, wo_ref, g2_ref, wrh_ref, wrl_ref, br_ref,
                      x1_ref, h2_ref, sel_ref, gate_ref, rank_ref, cnt_ref, carry_ref):
    tm = ROUTE_ROWS

    @pl.when(pl.program_id(0) == 0)
    def _():
        carry_ref[...] = jnp.zeros_like(carry_ref)

    x1 = (x_ref[...] + _dot(attn_ref[...].astype(BF16), wo_ref[0:D_ATTN, :])
          + _dot(ml_ref[...].astype(BF16), wo_ref[D_ATTN:D_ATTN + D_MLSTM, :]))
    x1_ref[...] = x1
    h2 = x1 * lax.rsqrt(jnp.mean(x1 * x1, axis=-1, keepdims=True) + EPS) * g2_ref[...]
    h2_ref[...] = h2
    hb = h2.astype(BF16)
    hl = (h2 - hb.astype(F32)).astype(BF16)
    lane = lax.broadcasted_iota(jnp.int32, (tm, LANES), 1)
    logits = _dot(hb, wrh_ref[...]) + _dot(hl, wrh_ref[...]) + _dot(hb, wrl_ref[...]) + br_ref[...]
    logits = jnp.where(lane < N_EXPERTS, logits, NEG_INF)

    vals, hots = [], []
    work = logits
    for _ in range(TOP_K):
        mx = jnp.max(work, axis=-1, keepdims=True)
        idx = jnp.min(jnp.where(work == mx, lane, LANES), axis=-1, keepdims=True)
        hot = lane == idx
        vals.append(mx)
        hots.append(hot)
        work = jnp.where(hot, NEG_INF, work)
    ex = [jnp.exp(v - vals[0]) for v in vals]
    tot = ex[0] + ex[1] + ex[2] + ex[3]

    onehot = jnp.zeros((tm, LANES), F32)
    for hot in hots:
        onehot = onehot + jnp.where(hot, 1.0, 0.0)
    ri = lax.broadcasted_iota(jnp.int32, (tm, tm), 0)
    ci = lax.broadcasted_iota(jnp.int32, (tm, tm), 1)
    strict = jnp.where(ri > ci, 1.0, 0.0).astype(BF16)
    before = _dot(strict, onehot.astype(BF16)) + carry_ref[...]
    carry = carry_ref[...] + jnp.sum(onehot, axis=0, keepdims=True)
    carry_ref[...] = carry
    cnt_ref[...] = carry

    sel = jnp.zeros((tm, LANES), jnp.int32)
    gate = jnp.zeros((tm, LANES), F32)
    rank = jnp.zeros((tm, LANES), F32)
    for kk in range(TOP_K):
        at = lane == kk
        idx = jnp.sum(jnp.where(hots[kk], lane, 0), axis=-1, keepdims=True)
        rk = jnp.sum(jnp.where(hots[kk], before, 0.0), axis=-1, keepdims=True)
        sel = jnp.where(at, idx, sel)
        gate = jnp.where(at, ex[kk] / tot, gate)
        rank = jnp.where(at, rk, rank)
    sel_ref[...] = sel
    gate_ref[...] = gate
    rank_ref[...] = rank.astype(jnp.int32)


def _out_route(attn, ml, x2, w_out, g2, w_router, b_router):
    T = x2.shape[0]
    tm = ROUTE_ROWS
    wo = w_out.astype(BF16)
    wr = jnp.pad(w_router, ((0, 0), (0, LANES - N_EXPERTS)))
    wrh = wr.astype(BF16)
    wrl = (wr - wrh.astype(F32)).astype(BF16)
    br = jnp.pad(b_router, (0, LANES - N_EXPERTS))[None, :]
    g2r = g2[None, :]
    row = lambda n: pl.BlockSpec((tm, n), lambda i: (i, 0))
    full = lambda a: pl.BlockSpec(a.shape, lambda i: (0,) * a.ndim)
    return pl.pallas_call(
        _out_route_kernel,
        grid=(T // tm,),
        in_specs=[row(D_ATTN), row(D_MLSTM), row(D_MODEL), full(wo), full(g2r), full(wrh), full(wrl), full(br)],
        out_specs=[row(D_MODEL), row(D_MODEL), row(LANES), row(LANES), row(LANES),
                   pl.BlockSpec((1, LANES), lambda i: (0, 0))],
        out_shape=[jax.ShapeDtypeStruct((T, D_MODEL), F32), jax.ShapeDtypeStruct((T, D_MODEL), F32),
                   jax.ShapeDtypeStruct((T, LANES), jnp.int32), jax.ShapeDtypeStruct((T, LANES), F32),
                   jax.ShapeDtypeStruct((T, LANES), jnp.int32), jax.ShapeDtypeStruct((1, LANES), F32)],
        scratch_shapes=[pltpu.VMEM((1, LANES), F32)],
        compiler_params=_vmem_limit(32),
        name="out_route",
    )(attn, ml, x2, wo, g2r, wrh, wrl, br)


def _row_copy(src_ref, dst_ref, sem, src_row, dst_row):
    return pltpu.make_async_copy(src_ref.at[pl.ds(src_row, 1)], dst_ref.at[pl.ds(dst_row, 1)], sem)


def _dispatch_kernel(tok_ref, h_ref, xs_ref, sem):
    gb = GATHER_ROWS
    base = pl.program_id(0) * gb

    def issue(r, carry):
        _row_copy(h_ref, xs_ref, sem, tok_ref[r], base + r).start()
        return carry

    lax.fori_loop(0, gb, issue, 0)

    def drain(r, carry):
        _row_copy(h_ref, xs_ref, sem, 0, base + r).wait()
        return carry

    lax.fori_loop(0, gb, drain, 0)


def _dispatch(h2, slot_token):
    n_slots = slot_token.shape[0]
    gb = GATHER_ROWS
    return pl.pallas_call(
        _dispatch_kernel,
        grid=(n_slots // gb,),
        in_specs=[pl.BlockSpec((gb,), lambda i: (i,), memory_space=pltpu.SMEM),
                  pl.BlockSpec(memory_space=pl.ANY)],
        out_specs=pl.BlockSpec(memory_space=pl.ANY),
        out_shape=jax.ShapeDtypeStruct((n_slots, D_MODEL), F32),
        scratch_shapes=[pltpu.SemaphoreType.DMA(())],
        name="dispatch",
    )(slot_token, h2)


def _expert_kernel(bexp_ref, nused_ref, xs_ref, gate_ref, wgu_ref, bgu_ref, wd_ref, bd_ref, y_ref):
    tm = EXPERT_ROWS
    blk = pl.program_id(0)

    @pl.when(blk * tm < nused_ref[0])
    def _():
        xb = xs_ref[...].astype(BF16)
        acc = jnp.zeros((tm, D_MODEL), F32)
        for f in range(D_FF // FF_CHUNK):
            lo = f * FF_CHUNK
            g = _dot(xb, wgu_ref[:, lo:lo + FF_CHUNK]) + bgu_ref[:, lo:lo + FF_CHUNK]
            u = _dot(xb, wgu_ref[:, D_FF + lo:D_FF + lo + FF_CHUNK]) + bgu_ref[:, D_FF + lo:D_FF + lo + FF_CHUNK]
            g = jnp.minimum(g, SWIGLU_LIMIT)
            u = jnp.clip(u, -SWIGLU_LIMIT, SWIGLU_LIMIT)
            act = g * jax.nn.sigmoid(SWIGLU_ALPHA * g) * (u + 1.0)
            acc = acc + _dot(act.astype(BF16), wd_ref[lo:lo + FF_CHUNK, :])
        y_ref[...] = (acc + bd_ref[...]) * gate_ref[...]

    @pl.when(blk * tm >= nused_ref[0])
    def _():
        y_ref[...] = jnp.zeros_like(y_ref)


def _experts(xs, slot_gate, block_expert, n_used, w_gate_up, b_gate_up, w_down, b_down):
    n_slots = xs.shape[0]
    tm = EXPERT_ROWS
    wgu = w_gate_up.astype(BF16)
    wd = w_down.astype(BF16)
    bgu = b_gate_up[:, None, :]
    bd = b_down[:, None, :]
    return pl.pallas_call(
        _expert_kernel,
        grid_spec=pltpu.PrefetchScalarGridSpec(
            num_scalar_prefetch=2,
            grid=(n_slots // tm,),
            in_specs=[
                pl.BlockSpec((tm, D_MODEL), lambda i, be, nu: (i, 0)),
                pl.BlockSpec((tm, 1), lambda i, be, nu: (i, 0)),
                pl.BlockSpec((None, D_MODEL, 2 * D_FF), lambda i, be, nu: (be[i], 0, 0)),
                pl.BlockSpec((None, 1, 2 * D_FF), lambda i, be, nu: (be[i], 0, 0)),
                pl.BlockSpec((None, D_FF, D_MODEL), lambda i, be, nu: (be[i], 0, 0)),
                pl.BlockSpec((None, 1, D_MODEL), lambda i, be, nu: (be[i], 0, 0)),
            ],
            out_specs=pl.BlockSpec((tm, D_MODEL), lambda i, be, nu: (i, 0)),
        ),
        out_shape=jax.ShapeDtypeStruct((n_slots, D_MODEL), F32),
        compiler_params=_vmem_limit(48),
        name="experts",
    )(block_expert, n_used, xs, slot_gate, wgu, bgu, wd, bd)


def _combine_kernel(dest_ref, x1_ref, y_ref, o_ref, buf, sem):
    tc = COMBINE_ROWS

    def issue(r, carry):
        for kk in range(TOP_K):
            pltpu.make_async_copy(y_ref.at[pl.ds(dest_ref[r * TOP_K + kk], 1)],
                                  buf.at[kk, pl.ds(r, 1)], sem).start()
        return carry

    lax.fori_loop(0, tc, issue, 0)

    def drain(r, carry):
        for kk in range(TOP_K):
            pltpu.make_async_copy(y_ref.at[pl.ds(0, 1)], buf.at[kk, pl.ds(r, 1)], sem).wait()
        return carry

    lax.fori_loop(0, tc, drain, 0)
    o_ref[...] = x1_ref[...] + ((buf[0] + buf[1]) + (buf[2] + buf[3]))


def _combine(dest_flat, x1, y):
    T = x1.shape[0]
    tc = COMBINE_ROWS
    return pl.pallas_call(
        _combine_kernel,
        grid=(T // tc,),
        in_specs=[pl.BlockSpec((tc * TOP_K,), lambda i: (i,), memory_space=pltpu.SMEM),
                  pl.BlockSpec((tc, D_MODEL), lambda i: (i, 0)),
                  pl.BlockSpec(memory_space=pl.ANY)],
        out_specs=pl.BlockSpec((tc, D_MODEL), lambda i: (i, 0)),
        out_shape=jax.ShapeDtypeStruct((T, D_MODEL), F32),
        scratch_shapes=[pltpu.VMEM((TOP_K, tc, D_MODEL), F32), pltpu.SemaphoreType.DMA(())],
        name="combine",
    )(dest_flat, x1, y)


def _routing_tables(sel, gate, rank, counts, T):
    tm = EXPERT_ROWS
    A = T * TOP_K
    n_slots = (A // tm + N_EXPERTS) * tm
    n_slots = -(-n_slots // GATHER_ROWS) * GATHER_ROWS
    n_blocks = n_slots // tm
    sizes = counts[0, :N_EXPERTS].astype(jnp.int32)
    psizes = (sizes + tm - 1) // tm * tm
    pends = jnp.cumsum(psizes)
    pstarts = pends - psizes
    e = sel[:, :TOP_K]
    dest = pstarts[e] + rank[:, :TOP_K]
    dest_flat = dest.reshape(A)
    tok = jnp.broadcast_to(jnp.arange(T, dtype=jnp.int32)[:, None], (T, TOP_K)).reshape(A)
    slot_token = jnp.zeros((n_slots,), jnp.int32).at[dest_flat].set(tok)
    slot_gate = jnp.zeros((n_slots,), F32).at[dest_flat].set(gate[:, :TOP_K].reshape(A))
    block_expert = jnp.minimum(
        jnp.searchsorted(pends, jnp.arange(n_blocks, dtype=jnp.int32) * tm, side='right'),
        N_EXPERTS - 1).astype(jnp.int32)
    n_used = pends[-1:].astype(jnp.int32)
    return dest_flat, slot_token, slot_gate[:, None], block_expert, n_used


def kernel(x, norm1_g, w_in, q_norm_g, k_norm_g, conv_w, conv_b, w_mq, w_mk, b_igate, b_fgate, mh_norm_g,
           w_out, norm2_g, w_router, b_router, w_gate_up, b_gate_up, w_down, b_down):
    B, S, D = x.shape
    T = B * S
    assert D == D_MODEL and S % ATTN_ROWS == 0 and norm1_g.shape[0] == 1
    x2 = x.reshape(T, D)
    q, k, v, xm, vm, om, gates = _in_proj(x2, norm1_g[0], w_in[0], q_norm_g[0], k_norm_g[0])
    attn = _attention(q, k, v, B, S)
    ml = _mlstm(xm, vm, om, gates, conv_w[0], conv_b[0], w_mq[0], w_mk[0], b_igate[0], b_fgate[0],
                mh_norm_g[0], B, S)
    x1, h2, sel, gate, rank, counts = _out_route(attn, ml, x2, w_out[0], norm2_g[0], w_router[0], b_router[0])
    dest_flat, slot_token, slot_gate, block_expert, n_used = _routing_tables(sel, gate, rank, counts, T)
    xs = _dispatch(h2, slot_token)
    y = _experts(xs, slot_gate, block_expert, n_used, w_gate_up[0], b_gate_up[0], w_down[0], b_down[0])
    out = _combine(dest_flat, x1, y)
    return out.reshape(B, S, D)
```

```python
import functools

import jax
import jax.numpy as jnp
from jax import lax
from jax.experimental import pallas as pl
from jax.experimental.pallas import tpu as pltpu

F32 = jnp.float32
BF16 = jnp.bfloat16
NEG_INF = float("-inf")

D_MODEL = 1024
D_ATTN = 512
HEAD_DIM_ATTN = 64
N_HEADS_ATTN = 8
D_MLSTM = 512
HEAD_DIM_MLSTM = 128
N_HEADS_MLSTM = 4
CONV_K = 4
N_EXPERTS = 32
TOP_K = 4
D_FF = 1024
SWIGLU_LIMIT = 7.0
SWIGLU_ALPHA = 1.702
EPS = 1e-6
DILATIONS = (1, 4, 16)
BRANCH_SPAN = 128

LANES = 128
SUBLANES = 8

IN_PROJ_ROWS = 256
ATTN_ROWS = 2048
MLSTM_CHUNK = 256
ROUTE_ROWS = 256
EXPERT_ROWS = 256
FF_CHUNK = 512
CAST_ROWS = 128
DISPATCH_ROWS = 256
COMBINE_ROWS = 128


def _vmem_limit(mib):
    return pltpu.CompilerParams(vmem_limit_bytes=mib * 1024 * 1024)


def _split3(a):
    p1 = a.astype(BF16)
    r1 = a - p1.astype(F32)
    p2 = r1.astype(BF16)
    r2 = r1 - p2.astype(F32)
    return p1, p2, r2.astype(BF16)


def _dot(a, b):
    return jnp.dot(a, b, preferred_element_type=F32)


def _dot_nt(a, b):
    return lax.dot_general(a, b, (((1,), (1,)), ((), ())), preferred_element_type=F32)


def _dot_tn(a, b):
    return lax.dot_general(a, b, (((0,), (0,)), ((), ())), preferred_element_type=F32)


def _in_proj_kernel(x_ref, g1_ref, wqkv_ref, wm_ref, wgh_ref, wgl_ref, qg_ref, kg_ref, hsum_ref,
                    q_ref, k_ref, v_ref, xm_ref, vm_ref, om_ref, gates_ref):
    x = x_ref[...]
    h = x * lax.rsqrt(jnp.mean(x * x, axis=-1, keepdims=True) + EPS) * g1_ref[...]
    hb = h.astype(BF16)
    hl = (h - hb.astype(F32)).astype(BF16)

    def head_norm(z, g_ref):
        parts = []
        for c in range(D_ATTN // LANES):
            zc = z[:, c * LANES:(c + 1) * LANES]
            sq = zc * zc
            sh = sq.astype(BF16)
            sl = (sq - sh.astype(F32)).astype(BF16)
            ms = (_dot(sh, hsum_ref[...]) + _dot(sl, hsum_ref[...])) * (1.0 / HEAD_DIM_ATTN)
            parts.append(zc * lax.rsqrt(ms + EPS))
        return jnp.concatenate(parts, axis=-1) * g_ref[...]

    zq = _dot(hb, wqkv_ref[:, 0:D_ATTN])
    q_ref[...] = head_norm(zq, qg_ref) * (HEAD_DIM_ATTN ** -0.5)
    zk = _dot(hb, wqkv_ref[:, D_ATTN:2 * D_ATTN])
    k_ref[...] = head_norm(zk, kg_ref)
    v_ref[...] = _dot(hb, wqkv_ref[:, 2 * D_ATTN:3 * D_ATTN])
    xm_ref[...] = _dot(hb, wm_ref[:, 0:D_MLSTM])
    vm_ref[...] = _dot(hb, wm_ref[:, D_MLSTM:2 * D_MLSTM])
    om_ref[...] = _dot(hb, wm_ref[:, 2 * D_MLSTM:3 * D_MLSTM])
    gates_ref[...] = _dot(hb, wgh_ref[...]) + _dot(hl, wgh_ref[...]) + _dot(hb, wgl_ref[...])


def _in_proj(x2, g1, w_in, q_g, k_g):
    T = x2.shape[0]
    tm = IN_PROJ_ROWS
    n_qkv = 3 * D_ATTN
    n_m = 3 * D_MLSTM
    wqkv = w_in[:, :n_qkv].astype(BF16)
    wm = w_in[:, n_qkv:n_qkv + n_m].astype(BF16)
    wg = jnp.pad(w_in[:, n_qkv + n_m:], ((0, 0), (0, LANES - 2 * N_HEADS_MLSTM)))
    wgh = wg.astype(BF16)
    wgl = (wg - wgh.astype(F32)).astype(BF16)
    lane = jnp.arange(LANES)
    hsum = (lane[:, None] // HEAD_DIM_ATTN == lane[None, :] // HEAD_DIM_ATTN).astype(BF16)
    qg = jnp.tile(q_g, N_HEADS_ATTN)[None, :]
    kg = jnp.tile(k_g, N_HEADS_ATTN)[None, :]
    row = lambda n: pl.BlockSpec((tm, n), lambda i: (i, 0))
    full = lambda a: pl.BlockSpec(a.shape, lambda i: (0,) * a.ndim)
    outs = [jax.ShapeDtypeStruct((T, D_ATTN), F32)] * 3 + [jax.ShapeDtypeStruct((T, D_MLSTM), F32)] * 3
    outs.append(jax.ShapeDtypeStruct((T, LANES), F32))
    g1r = g1[None, :]
    return pl.pallas_call(
        _in_proj_kernel,
        grid=(T // tm,),
        in_specs=[row(D_MODEL), full(g1r), full(wqkv), full(wm), full(wgh), full(wgl), full(qg), full(kg),
                  full(hsum)],
        out_specs=[row(D_ATTN)] * 3 + [row(D_MLSTM)] * 3 + [row(LANES)],
        out_shape=outs,
        compiler_params=_vmem_limit(48),
        name="in_proj",
    )(x2, g1r, wqkv, wm, wgh, wgl, qg, kg, hsum)


def _attn_tiles():
    tiles = []
    for br, d in enumerate(DILATIONS):
        group = d * BRANCH_SPAN
        for u in range(ATTN_ROWS // group):
            for c in range(d):
                tiles.append((br, d, u * group + c))
    return tiles


def _attn_kernel(slope_ref, q_ref, kp_ref, kc_ref, vp_ref, vc_ref, o_ref,
                 kwin, vwin, bias_scr, m_scr, l_scr, acc_scr):
    W = ATTN_ROWS
    n = BRANCH_SPAN
    step = pl.program_id(2)
    kwin[0:W, :] = kp_ref[...]
    kwin[W:2 * W, :] = kc_ref[...]
    vwin[0:W, :] = vp_ref[...]
    vwin[W:2 * W, :] = vc_ref[...]

    lane = lax.broadcasted_iota(jnp.int32, (n, LANES), 1)
    first_head = lane < HEAD_DIM_ATTN
    row = lax.broadcasted_iota(jnp.int32, (n, 2 * n), 0)
    col = lax.broadcasted_iota(jnp.int32, (n, 2 * n), 1)
    j = n + row - col
    valid = jnp.logical_and(j >= 0, j <= n)
    jf = j.astype(F32)
    for br, d in enumerate(DILATIONS):
        for hh in range(2):
            slope = slope_ref[:, hh * HEAD_DIM_ATTN:hh * HEAD_DIM_ATTN + 1]
            b = jnp.where(valid, -(slope * float(d)) * jf, NEG_INF)
            bias_scr[br, hh, 0] = b
            bias_scr[br, hh, 1] = jnp.where(col >= n, b, NEG_INF)

    for br, d, qs in _attn_tiles():
        lo = W + qs - n * d
        q = q_ref[pl.ds(qs, n, stride=d), :]
        kk = jnp.concatenate([kwin[pl.ds(lo, n, stride=d), :], kwin[pl.ds(W + qs, n, stride=d), :]],
                             axis=0).astype(BF16)
        vv = jnp.concatenate([vwin[pl.ds(lo, n, stride=d), :], vwin[pl.ds(W + qs, n, stride=d), :]],
                             axis=0).astype(BF16)
        if lo < W:
            sel = jnp.where(step == 0, 1, 0)
        else:
            sel = 0
        res = []
        for hh in range(2):
            qm = jnp.where(first_head if hh == 0 else jnp.logical_not(first_head), q, 0.0).astype(BF16)
            s = _dot_nt(qm, kk) + bias_scr[br, hh, sel]
            m = jnp.max(s, axis=-1, keepdims=True)
            p = jnp.exp(s - m)
            l = jnp.sum(p, axis=-1, keepdims=True)
            res.append((m, l, _dot(p.astype(BF16), vv)))
        rows = pl.ds(qs, n, stride=d)
        m_scr[br, rows, :] = jnp.where(first_head, res[0][0], res[1][0])
        l_scr[br, rows, :] = jnp.where(first_head, res[0][1], res[1][1])
        acc_scr[br, rows, :] = jnp.where(first_head, res[0][2], res[1][2])

    chunk = 256

    def combine(i, carry):
        r = pl.ds(pl.multiple_of(i * chunk, chunk), chunk)
        m0, m1, m2 = m_scr[0, r, :], m_scr[1, r, :], m_scr[2, r, :]
        mm = jnp.maximum(jnp.maximum(m0, m1), m2)
        w0, w1, w2 = jnp.exp(m0 - mm), jnp.exp(m1 - mm), jnp.exp(m2 - mm)
        num = w0 * acc_scr[0, r, :] + w1 * acc_scr[1, r, :] + w2 * acc_scr[2, r, :]
        den = w0 * l_scr[0, r, :] + w1 * l_scr[1, r, :] + w2 * l_scr[2, r, :]
        o_ref[r, :] = num / den
        return carry

    lax.fori_loop(0, W // chunk, combine, 0)


def _attention(q, k, v, B, S):
    W = ATTN_ROWS
    nb = S // W
    n_pairs = D_ATTN // LANES
    slopes = jnp.exp2(-8.0 * jnp.arange(1, N_HEADS_ATTN + 1, dtype=F32) / N_HEADS_ATTN)
    slope_l = jnp.repeat(slopes, HEAD_DIM_ATTN).reshape(n_pairs, 1, LANES)
    cur = pl.BlockSpec((W, LANES), lambda b, p, i: (b * nb + i, p))
    prev = pl.BlockSpec((W, LANES), lambda b, p, i: (b * nb + jnp.maximum(i - 1, 0), p))
    n = BRANCH_SPAN
    nbr = len(DILATIONS)
    return pl.pallas_call(
        _attn_kernel,
        grid=(B, n_pairs, nb),
        in_specs=[pl.BlockSpec((None, 1, LANES), lambda b, p, i: (p, 0, 0)), cur, prev, cur, prev, cur],
        out_specs=cur,
        out_shape=jax.ShapeDtypeStruct((B * S, D_ATTN), F32),
        scratch_shapes=[
            pltpu.VMEM((2 * W, LANES), F32), pltpu.VMEM((2 * W, LANES), F32),
            pltpu.VMEM((nbr, 2, 2, n, 2 * n), F32),
            pltpu.VMEM((nbr, W, LANES), F32), pltpu.VMEM((nbr, W, LANES), F32),
            pltpu.VMEM((nbr, W, LANES), F32),
        ],
        compiler_params=_vmem_limit(48),
        name="attention",
    )(slope_l, q, k, k, v, v)


def _mlstm_kernel(xm_ref, vm_ref, om_ref, gcol_ref, grow_ref, cw_ref, cb_ref, wq_ref, wk_ref,
                  bcol_ref, brow_ref, g_ref, o_ref, xs_ref, xtail_ref, c_ref, n_ref, m_ref):
    L = MLSTM_CHUNK
    dh = HEAD_DIM_MLSTM
    nh = N_HEADS_MLSTM

    @pl.when(pl.program_id(1) == 0)
    def _():
        xtail_ref[...] = jnp.zeros_like(xtail_ref)
        c_ref[...] = jnp.zeros_like(c_ref)
        n_ref[...] = jnp.zeros_like(n_ref)
        m_ref[...] = jnp.zeros_like(m_ref)

    x = xm_ref[...]
    xs_ref[0:SUBLANES, :] = xtail_ref[...]
    xs_ref[SUBLANES:SUBLANES + L, :] = x
    xtail_ref[...] = x[L - SUBLANES:L, :]
    xc = cb_ref[...] + x * cw_ref[CONV_K - 1:CONV_K, :]
    for back in range(1, CONV_K):
        xc = xc + xs_ref[SUBLANES - back:SUBLANES - back + L, :] * cw_ref[CONV_K - 1 - back:CONV_K - back, :]
    xc = xc * jax.nn.sigmoid(xc)

    gc = gcol_ref[...] + bcol_ref[...]
    gr = grow_ref[...] + brow_ref[...]
    ri = lax.broadcasted_iota(jnp.int32, (L, L), 0)
    ci = lax.broadcasted_iota(jnp.int32, (L, L), 1)
    causal = ri >= ci
    tril = jnp.where(causal, 1.0, 0.0).astype(BF16)
    triu = jnp.where(ri <= ci, 1.0, 0.0).astype(BF16)
    c1, c2, c3 = _split3(jax.nn.log_sigmoid(gc))
    bc = _dot(tril, c1) + _dot(tril, c2) + _dot(tril, c3)
    r1, r2, r3 = _split3(jax.nn.log_sigmoid(gr))
    brw = _dot(r1, triu) + _dot(r2, triu) + _dot(r3, triu)

    for h in range(nh):
        hs = slice(h * dh, (h + 1) * dh)
        b_col = bc[:, nh + h:nh + h + 1]
        a_col = gc[:, h:h + 1] - b_col
        a_row = gr[h:h + 1, :] - brw[nh + h:nh + h + 1, :]
        m_prev = m_ref[h:h + 1, 0:1]
        log_d = jnp.where(causal, b_col + a_row, NEG_INF)
        log_inter = b_col + m_prev
        m_t = jnp.maximum(log_inter, jnp.max(log_d, axis=-1, keepdims=True))
        d = jnp.exp(log_d - m_t)
        inter = jnp.exp(log_inter - m_t)

        xh = xc[:, hs].astype(BF16)
        qf = _dot(xh, wq_ref[h])
        kf = _dot(xh, wk_ref[h]) * (dh ** -0.5)
        qb = qf.astype(BF16)
        vb = vm_ref[:, hs].astype(BF16)
        s = _dot_nt(qb, kf.astype(BF16)) * d
        c_old = c_ref[h]
        n_old = n_ref[h:h + 1, :]
        num = _dot(s.astype(BF16), vb) + inter * _dot(qb, c_old.astype(BF16))
        den = jnp.sum(s, axis=-1, keepdims=True) + inter * jnp.sum(qf * n_old, axis=-1, keepdims=True)
        hh = num / jnp.maximum(jnp.abs(den), jnp.exp(-m_t))

        b_last = b_col[L - 1:L, :]
        log_w = b_last + a_col
        m_new = jnp.maximum(b_last + m_prev, jnp.max(log_w, axis=0, keepdims=True))
        kw = kf * jnp.exp(log_w - m_new)
        decay = jnp.exp(b_last + m_prev - m_new)
        c_ref[h] = decay * c_old + _dot_tn(kw.astype(BF16), vb)
        n_ref[h:h + 1, :] = decay * n_old + jnp.sum(kw, axis=0, keepdims=True)
        m_ref[h:h + 1, :] = jnp.broadcast_to(m_new, (1, LANES))

        hn = hh * lax.rsqrt(jnp.mean(hh * hh, axis=-1, keepdims=True) + EPS) * g_ref[:, hs]
        o_ref[:, hs] = jax.nn.sigmoid(om_ref[:, hs]) * hn


def _mlstm(xm, vm, om, gates, conv_w, conv_b, w_mq, w_mk, b_i, b_f, mh_g, B, S):
    L = MLSTM_CHUNK
    nc = S // L
    nh = N_HEADS_MLSTM
    ng = 2 * nh
    grow = gates[:, :ng].reshape(B, S, ng).transpose(0, 2, 1)
    bias = jnp.concatenate([b_i, b_f])
    bcol = jnp.pad(bias, (0, LANES - ng))[None, :]
    brow = bias[:, None]
    wq = w_mq.astype(BF16)
    wk = w_mk.astype(BF16)
    cb = conv_b[None, :]
    g = mh_g.reshape(1, D_MLSTM)
    rows = pl.BlockSpec((L, D_MLSTM), lambda b, c: (b * nc + c, 0))
    full = lambda a: pl.BlockSpec(a.shape, lambda b, c: (0,) * a.ndim)
    return pl.pallas_call(
        _mlstm_kernel,
        grid=(B, nc),
        in_specs=[rows, rows, rows, pl.BlockSpec((L, LANES), lambda b, c: (b * nc + c, 0)),
                  pl.BlockSpec((None, ng, L), lambda b, c: (b, 0, c)),
                  full(conv_w), full(cb), full(wq), full(wk), full(bcol), full(brow), full(g)],
        out_specs=rows,
        out_shape=jax.ShapeDtypeStruct((B * S, D_MLSTM), F32),
        scratch_shapes=[
            pltpu.VMEM((L + SUBLANES, D_MLSTM), F32), pltpu.VMEM((SUBLANES, D_MLSTM), F32),
            pltpu.VMEM((nh, HEAD_DIM_MLSTM, HEAD_DIM_MLSTM), F32),
            pltpu.VMEM((SUBLANES, LANES), F32), pltpu.VMEM((SUBLANES, LANES), F32),
        ],
        compiler_params=_vmem_limit(32),
        name="mlstm",
    )(xm, vm, om, gates, grow, conv_w, cb, wq, wk, bcol, brow, g)


def _out_route_kernel(attn_ref, ml_ref, x_ref, wo_ref, g2_ref, wrh_ref, wrl_ref, br_ref,
                      x1_ref, h2_ref, sel_ref, gate_ref, rank_ref, cnt_ref, carry_ref):
    tm = ROUTE_ROWS

    @pl.when(pl.program_id(0) == 0)
    def _():
        carry_ref[...] = jnp.zeros_like(carry_ref)

    x1 = (x_ref[...] + _dot(attn_ref[...].astype(BF16), wo_ref[0:D_ATTN, :])
          + _dot(ml_ref[...].astype(BF16), wo_ref[D_ATTN:D_ATTN + D_MLSTM, :]))
    x1_ref[...] = x1
    h2 = x1 * lax.rsqrt(jnp.mean(x1 * x1, axis=-1, keepdims=True) + EPS) * g2_ref[...]
    h2_ref[...] = h2
    hb = h2.astype(BF16)
    hl = (h2 - hb.astype(F32)).astype(BF16)
    lane = lax.broadcasted_iota(jnp.int32, (tm, LANES), 1)
    logits = _dot(hb, wrh_ref[...]) + _dot(hl, wrh_ref[...]) + _dot(hb, wrl_ref[...]) + br_ref[...]
    logits = jnp.where(lane < N_EXPERTS, logits, NEG_INF)

    vals, hots = [], []
    work = logits
    for _ in range(TOP_K):
        mx = jnp.max(work, axis=-1, keepdims=True)
        idx = jnp.min(jnp.where(work == mx, lane, LANES), axis=-1, keepdims=True)
        hot = lane == idx
        vals.append(mx)
        hots.append(hot)
        work = jnp.where(hot, NEG_INF, work)
    ex = [jnp.exp(v - vals[0]) for v in vals]
    tot = ex[0] + ex[1] + ex[2] + ex[3]

    onehot = jnp.zeros((tm, LANES), F32)
    for hot in hots:
        onehot = onehot + jnp.where(hot, 1.0, 0.0)
    ri = lax.broadcasted_iota(jnp.int32, (tm, tm), 0)
    ci = lax.broadcasted_iota(jnp.int32, (tm, tm), 1)
    strict = jnp.where(ri > ci, 1.0, 0.0).astype(BF16)
    before = _dot(strict, onehot.astype(BF16)) + carry_ref[...]
    carry = carry_ref[...] + jnp.sum(onehot, axis=0, keepdims=True)
    carry_ref[...] = carry
    cnt_ref[...] = carry

    sel = jnp.zeros((tm, LANES), jnp.int32)
    gate = jnp.zeros((tm, LANES), F32)
    rank = jnp.zeros((tm, LANES), F32)
    for kk in range(TOP_K):
        at = lane == kk
        idx = jnp.sum(jnp.where(hots[kk], lane, 0), axis=-1, keepdims=True)
        rk = jnp.sum(jnp.where(hots[kk], before, 0.0), axis=-1, keepdims=True)
        sel = jnp.where(at, idx, sel)
        gate = jnp.where(at, ex[kk] / tot, gate)
        rank = jnp.where(at, rk, rank)
    sel_ref[...] = sel
    gate_ref[...] = gate
    rank_ref[...] = rank.astype(jnp.int32)


def _out_route(attn, ml, x2, w_out, g2, w_router, b_router):
    T = x2.shape[0]
    tm = ROUTE_ROWS
    wo = w_out.astype(BF16)
    wr = jnp.pad(w_router, ((0, 0), (0, LANES - N_EXPERTS)))
    wrh = wr.astype(BF16)
    wrl = (wr - wrh.astype(F32)).astype(BF16)
    br = jnp.pad(b_router, (0, LANES - N_EXPERTS))[None, :]
    g2r = g2[None, :]
    row = lambda n: pl.BlockSpec((tm, n), lambda i: (i, 0))
    full = lambda a: pl.BlockSpec(a.shape, lambda i: (0,) * a.ndim)
    return pl.pallas_call(
        _out_route_kernel,
        grid=(T // tm,),
        in_specs=[row(D_ATTN), row(D_MLSTM), row(D_MODEL), full(wo), full(g2r), full(wrh), full(wrl), full(br)],
        out_specs=[row(D_MODEL), row(D_MODEL), row(LANES), row(LANES), row(LANES),
                   pl.BlockSpec((1, LANES), lambda i: (0, 0))],
        out_shape=[jax.ShapeDtypeStruct((T, D_MODEL), F32), jax.ShapeDtypeStruct((T, D_MODEL), F32),
                   jax.ShapeDtypeStruct((T, LANES), jnp.int32), jax.ShapeDtypeStruct((T, LANES), F32),
                   jax.ShapeDtypeStruct((T, LANES), jnp.int32), jax.ShapeDtypeStruct((1, LANES), F32)],
        scratch_shapes=[pltpu.VMEM((1, LANES), F32)],
        compiler_params=_vmem_limit(32),
        name="out_route",
    )(attn, ml, x2, wo, g2r, wrh, wrl, br)


def _row_copy(src_ref, dst_ref, sem, src_row, dst_row):
    return pltpu.make_async_copy(src_ref.at[pl.ds(src_row, 1)], dst_ref.at[pl.ds(dst_row, 1)], sem)


def _dispatch_kernel(padstart_ref, npad_ref, dest_ref, h_ref, xs_ref, zero_ref, sem, zsem):
    td = DISPATCH_ROWS

    def issue(r, carry):
        for kk in range(TOP_K):
            _row_copy(h_ref, xs_ref, sem, r, dest_ref[r * TOP_K + kk]).start()
        return carry

    lax.fori_loop(0, td, issue, 0)

    @pl.when(pl.program_id(0) == 0)
    def _():
        zero_ref[...] = jnp.zeros_like(zero_ref)

        def fill(wait):
            def per_run(e, carry):
                start = padstart_ref[e]
                head = (-start) & (SUBLANES - 1)

                def single(r, c):
                    cp = _row_copy(zero_ref, xs_ref, zsem, 0, start + r)
                    cp.wait() if wait else cp.start()
                    return c

                lax.fori_loop(0, head, single, 0)
                off = pl.multiple_of(start + head, SUBLANES)
                left = npad_ref[e] - head

                def whole(b, c):
                    dst = xs_ref.at[pl.ds(pl.multiple_of(off + b * td, SUBLANES), td)]
                    cp = pltpu.make_async_copy(zero_ref, dst, zsem)
                    cp.wait() if wait else cp.start()
                    return c

                n_whole = left // td
                lax.fori_loop(0, n_whole, whole, 0)
                off = pl.multiple_of(off + n_whole * td, SUBLANES)
                p = td // 2
                while p >= SUBLANES:
                    take = (left & p) != 0

                    @pl.when(take)
                    def _(off=off, p=p):
                        cp = pltpu.make_async_copy(zero_ref.at[pl.ds(0, p)], xs_ref.at[pl.ds(off, p)], zsem)
                        cp.wait() if wait else cp.start()

                    off = pl.multiple_of(off + jnp.where(take, p, 0), SUBLANES)
                    p //= 2
                return carry

            lax.fori_loop(0, N_EXPERTS + 1, per_run, 0)

        fill(wait=False)
        fill(wait=True)

    for kk in range(TOP_K):
        pltpu.make_async_copy(h_ref, xs_ref.at[pl.ds(0, td)], sem).wait()


def _dispatch(h2, dest_flat, padstart, npad, n_slots):
    T = h2.shape[0]
    td = DISPATCH_ROWS
    return pl.pallas_call(
        _dispatch_kernel,
        grid_spec=pltpu.PrefetchScalarGridSpec(
            num_scalar_prefetch=2,
            grid=(T // td,),
            in_specs=[pl.BlockSpec((td * TOP_K,), lambda i, ps, npd: (i,), memory_space=pltpu.SMEM),
                      pl.BlockSpec((td, D_MODEL), lambda i, ps, npd: (i, 0))],
            out_specs=pl.BlockSpec(memory_space=pl.ANY),
            scratch_shapes=[pltpu.VMEM((td, D_MODEL), F32), pltpu.SemaphoreType.DMA(()),
                            pltpu.SemaphoreType.DMA(())],
        ),
        out_shape=jax.ShapeDtypeStruct((n_slots, D_MODEL), F32),
        name="dispatch",
    )(padstart, npad, dest_flat, h2)


def _expert_kernel(bexp_ref, first_ref, nused_ref, xs_ref, wgu_ref, bgu_ref, wd_ref, bd_ref, y_ref,
                   wgu_bf, wd_bf):
    tm = EXPERT_ROWS
    blk = pl.program_id(0)
    active = blk * tm < nused_ref[0]

    @pl.when(jnp.logical_and(active, first_ref[blk] == 1))
    def _():
        rc = CAST_ROWS

        def cast(i, carry):
            r = pl.ds(pl.multiple_of(i * rc, rc), rc)
            wgu_bf[r, :] = wgu_ref[r, :].astype(BF16)
            wd_bf[r, :] = wd_ref[r, :].astype(BF16)
            return carry

        lax.fori_loop(0, D_MODEL // rc, cast, 0)

    @pl.when(active)
    def _():
        xb = xs_ref[...].astype(BF16)
        acc = jnp.zeros((tm, D_MODEL), F32)
        for f in range(D_FF // FF_CHUNK):
            lo = f * FF_CHUNK
            g = _dot(xb, wgu_bf[:, lo:lo + FF_CHUNK]) + bgu_ref[:, lo:lo + FF_CHUNK]
            u = _dot(xb, wgu_bf[:, D_FF + lo:D_FF + lo + FF_CHUNK]) + bgu_ref[:, D_FF + lo:D_FF + lo + FF_CHUNK]
            g = jnp.minimum(g, SWIGLU_LIMIT)
            u = jnp.clip(u, -SWIGLU_LIMIT, SWIGLU_LIMIT)
            act = g * jax.nn.sigmoid(SWIGLU_ALPHA * g) * (u + 1.0)
            acc = acc + _dot(act.astype(BF16), wd_bf[lo:lo + FF_CHUNK, :])
        y_ref[...] = acc + bd_ref[...]

    @pl.when(jnp.logical_not(active))
    def _():
        y_ref[...] = jnp.zeros_like(y_ref)


def _experts(xs, block_expert, first_block, n_used, w_gate_up, b_gate_up, w_down, b_down):
    n_slots = xs.shape[0]
    tm = EXPERT_ROWS
    assert D_FF == D_MODEL
    bgu = b_gate_up[:, None, :]
    bd = b_down[:, None, :]
    return pl.pallas_call(
        _expert_kernel,
        grid_spec=pltpu.PrefetchScalarGridSpec(
            num_scalar_prefetch=3,
            grid=(n_slots // tm,),
            in_specs=[
                pl.BlockSpec((tm, D_MODEL), lambda i, be, fb, nu: (i, 0)),
                pl.BlockSpec((None, D_MODEL, 2 * D_FF), lambda i, be, fb, nu: (be[i], 0, 0)),
                pl.BlockSpec((None, 1, 2 * D_FF), lambda i, be, fb, nu: (be[i], 0, 0)),
                pl.BlockSpec((None, D_FF, D_MODEL), lambda i, be, fb, nu: (be[i], 0, 0)),
                pl.BlockSpec((None, 1, D_MODEL), lambda i, be, fb, nu: (be[i], 0, 0)),
            ],
            out_specs=pl.BlockSpec((tm, D_MODEL), lambda i, be, fb, nu: (i, 0)),
            scratch_shapes=[pltpu.VMEM((D_MODEL, 2 * D_FF), BF16), pltpu.VMEM((D_FF, D_MODEL), BF16)],
        ),
        out_shape=jax.ShapeDtypeStruct((n_slots, D_MODEL), F32),
        compiler_params=_vmem_limit(56),
        name="experts",
    )(block_expert, first_block, n_used, xs, w_gate_up, bgu, w_down, bd)


def _combine_kernel(dest_ref, x1_ref, gate_ref, y_ref, o_ref, buf, sem):
    tc = COMBINE_ROWS

    def issue(r, carry):
        for kk in range(TOP_K):
            pltpu.make_async_copy(y_ref.at[pl.ds(dest_ref[r * TOP_K + kk], 1)],
                                  buf.at[kk, pl.ds(r, 1)], sem).start()
        return carry

    lax.fori_loop(0, tc, issue, 0)
    for kk in range(TOP_K):
        pltpu.make_async_copy(y_ref.at[pl.ds(0, tc)], buf.at[kk], sem).wait()
    g = gate_ref[...]
    acc = x1_ref[...]
    for kk in range(TOP_K):
        acc = acc + g[:, kk:kk + 1] * buf[kk]
    o_ref[...] = acc


def _combine(dest_flat, x1, gate, y):
    T = x1.shape[0]
    tc = COMBINE_ROWS
    return pl.pallas_call(
        _combine_kernel,
        grid=(T // tc,),
        in_specs=[pl.BlockSpec((tc * TOP_K,), lambda i: (i,), memory_space=pltpu.SMEM),
                  pl.BlockSpec((tc, D_MODEL), lambda i: (i, 0)),
                  pl.BlockSpec((tc, LANES), lambda i: (i, 0)),
                  pl.BlockSpec(memory_space=pl.ANY)],
        out_specs=pl.BlockSpec((tc, D_MODEL), lambda i: (i, 0)),
        out_shape=jax.ShapeDtypeStruct((T, D_MODEL), F32),
        scratch_shapes=[pltpu.VMEM((TOP_K, tc, D_MODEL), F32), pltpu.SemaphoreType.DMA(())],
        name="combine",
    )(dest_flat, x1, gate, y)


def _routing_tables(sel, rank, counts, T):
    tm = EXPERT_ROWS
    A = T * TOP_K
    n_blocks = A // tm + N_EXPERTS
    n_slots = n_blocks * tm
    sizes = counts[0, :N_EXPERTS].astype(jnp.int32)
    psizes = (sizes + tm - 1) // tm * tm
    pends = jnp.cumsum(psizes)
    pstarts = pends - psizes
    e = sel[:, :TOP_K]
    dest_flat = (pstarts[e] + rank[:, :TOP_K]).reshape(A)
    blk_start = jnp.arange(n_blocks, dtype=jnp.int32) * tm
    block_expert = jnp.minimum(jnp.sum(pends[None, :] <= blk_start[:, None], axis=1),
                               N_EXPERTS - 1).astype(jnp.int32)
    first_block = jnp.concatenate([jnp.ones((1,), jnp.int32),
                                   (block_expert[1:] != block_expert[:-1]).astype(jnp.int32)])
    n_used = pends[-1:]
    padstart = jnp.concatenate([pstarts + sizes, n_used])
    npad = jnp.concatenate([psizes - sizes, n_slots - n_used])
    return dest_flat, block_expert, first_block, n_used, padstart, npad, n_slots


def kernel(x, norm1_g, w_in, q_norm_g, k_norm_g, conv_w, conv_b, w_mq, w_mk, b_igate, b_fgate, mh_norm_g,
           w_out, norm2_g, w_router, b_router, w_gate_up, b_gate_up, w_down, b_down):
    B, S, D = x.shape
    T = B * S
    assert D == D_MODEL and S % ATTN_ROWS == 0 and norm1_g.shape[0] == 1
    x2 = x.reshape(T, D)
    q, k, v, xm, vm, om, gates = _in_proj(x2, norm1_g[0], w_in[0], q_norm_g[0], k_norm_g[0])
    attn = _attention(q, k, v, B, S)
    ml = _mlstm(xm, vm, om, gates, conv_w[0], conv_b[0], w_mq[0], w_mk[0], b_igate[0], b_fgate[0],
                mh_norm_g[0], B, S)
    x1, h2, sel, gate, rank, counts = _out_route(attn, ml, x2, w_out[0], norm2_g[0], w_router[0], b_router[0])
    dest_flat, block_expert, first_block, n_used, padstart, npad, n_slots = _routing_tables(sel, rank, counts, T)
    xs = _dispatch(h2, dest_flat, padstart, npad, n_slots)
    y = _experts(xs, block_expert, first_block, n_used, w_gate_up[0], b_gate_up[0], w_down[0], b_down[0])
    out = _combine(dest_flat, x1, gate, y)
    return out.reshape(B, S, D)
```

```python
import functools

import jax
import jax.numpy as jnp
from jax import lax
from jax.experimental import pallas as pl
from jax.experimental.pallas import tpu as pltpu

F32 = jnp.float32
BF16 = jnp.bfloat16
NEG_INF = float("-inf")

D_MODEL = 1024
D_ATTN = 512
HEAD_DIM_ATTN = 64
N_HEADS_ATTN = 8
D_MLSTM = 512
HEAD_DIM_MLSTM = 128
N_HEADS_MLSTM = 4
CONV_K = 4
N_EXPERTS = 32
TOP_K = 4
D_FF = 1024
SWIGLU_LIMIT = 7.0
SWIGLU_ALPHA = 1.702
EPS = 1e-6
DILATIONS = (1, 4, 16)
BRANCH_SPAN = 128

LANES = 128
SUBLANES = 8

IN_PROJ_ROWS = 256
ATTN_ROWS = 2048
MLSTM_CHUNK = 256
ROUTE_ROWS = 256
EXPERT_ROWS = 256
FF_CHUNK = 512
CAST_ROWS = 128
DISPATCH_ROWS = 256
HEADNORM_LANES = 256
BF16_TILE_ROWS = 16
RUN_ROWS = 64
RUN_GROUP_ROWS = 256
NOT_ROUTED = -1.0e9


def _vmem_limit(mib):
    return pltpu.CompilerParams(vmem_limit_bytes=mib * 1024 * 1024)


def _split3(a):
    p1 = a.astype(BF16)
    r1 = a - p1.astype(F32)
    p2 = r1.astype(BF16)
    r2 = r1 - p2.astype(F32)
    return p1, p2, r2.astype(BF16)


def _dot(a, b):
    return jnp.dot(a, b, preferred_element_type=F32)


def _dot_nt(a, b):
    return lax.dot_general(a, b, (((1,), (1,)), ((), ())), preferred_element_type=F32)


def _dot_tn(a, b):
    return lax.dot_general(a, b, (((0,), (0,)), ((), ())), preferred_element_type=F32)


def _in_proj_kernel(x_ref, g1_ref, wqkv_ref, wm_ref, wgc_ref, qg_ref, kg_ref, hsum_ref,
                    q_ref, k_ref, v_ref, xm_ref, vm_ref, om_ref, gates_ref):
    x = x_ref[...]
    h = x * lax.rsqrt(jnp.mean(x * x, axis=-1, keepdims=True) + EPS) * g1_ref[...]
    hb = h.astype(BF16)
    hl = (h - hb.astype(F32)).astype(BF16)
    gw = HEADNORM_LANES

    def head_norm(z, g_ref):
        parts = []
        for c in range(D_ATTN // gw):
            zc = z[:, c * gw:(c + 1) * gw]
            ms = _dot((zc * zc).astype(BF16), hsum_ref[...]) * (1.0 / HEAD_DIM_ATTN)
            parts.append(zc * lax.rsqrt(ms + EPS))
        return jnp.concatenate(parts, axis=-1) * g_ref[...]

    zq = _dot(hb, wqkv_ref[:, 0:D_ATTN])
    q_ref[...] = head_norm(zq, qg_ref) * (HEAD_DIM_ATTN ** -0.5)
    zk = _dot(hb, wqkv_ref[:, D_ATTN:2 * D_ATTN])
    k_ref[...] = head_norm(zk, kg_ref)
    v_ref[...] = _dot(hb, wqkv_ref[:, 2 * D_ATTN:3 * D_ATTN])
    xm_ref[...] = _dot(hb, wm_ref[:, 0:D_MLSTM])
    vm_ref[...] = _dot(hb, wm_ref[:, D_MLSTM:2 * D_MLSTM])
    om_ref[...] = _dot(hb, wm_ref[:, 2 * D_MLSTM:3 * D_MLSTM])
    ng = 2 * N_HEADS_MLSTM
    gsum = _dot(hb, wgc_ref[...]) + _dot(hl, wgc_ref[...])
    gsum = gsum + pltpu.roll(gsum, LANES - ng, axis=1)
    lane = lax.broadcasted_iota(jnp.int32, gsum.shape, 1)
    gates_ref[...] = jnp.where(lane < ng, gsum, 0.0)


def _in_proj(x2, g1, w_in, q_g, k_g):
    T = x2.shape[0]
    tm = IN_PROJ_ROWS
    n_qkv = 3 * D_ATTN
    n_m = 3 * D_MLSTM
    ng = 2 * N_HEADS_MLSTM
    wqkv = w_in[:, :n_qkv].astype(BF16)
    wm = w_in[:, n_qkv:n_qkv + n_m].astype(BF16)
    wg = w_in[:, n_qkv + n_m:]
    wgh = wg.astype(BF16)
    wgl = (wg - wgh.astype(F32)).astype(BF16)
    wgc = jnp.pad(jnp.concatenate([wgh, wgl], axis=1), ((0, 0), (0, LANES - 2 * ng)))
    lane = jnp.arange(HEADNORM_LANES)
    hsum = (lane[:, None] // HEAD_DIM_ATTN == lane[None, :] // HEAD_DIM_ATTN).astype(BF16)
    qg = jnp.tile(q_g, N_HEADS_ATTN)[None, :]
    kg = jnp.tile(k_g, N_HEADS_ATTN)[None, :]
    row = lambda n: pl.BlockSpec((tm, n), lambda i: (i, 0))
    full = lambda a: pl.BlockSpec(a.shape, lambda i: (0,) * a.ndim)
    outs = [jax.ShapeDtypeStruct((T, D_ATTN), F32)] * 3 + [jax.ShapeDtypeStruct((T, D_MLSTM), F32)] * 3
    outs.append(jax.ShapeDtypeStruct((T, LANES), F32))
    g1r = g1[None, :]
    return pl.pallas_call(
        _in_proj_kernel,
        grid=(T // tm,),
        in_specs=[row(D_MODEL), full(g1r), full(wqkv), full(wm), full(wgc), full(qg), full(kg), full(hsum)],
        out_specs=[row(D_ATTN)] * 3 + [row(D_MLSTM)] * 3 + [row(LANES)],
        out_shape=outs,
        compiler_params=_vmem_limit(48),
        name="in_proj",
    )(x2, g1r, wqkv, wm, wgc, qg, kg, hsum)


def _attn_tiles():
    tiles = []
    for br, d in enumerate(DILATIONS):
        group = d * BRANCH_SPAN
        for u in range(ATTN_ROWS // group):
            for c in range(d):
                tiles.append((br, d, u * group + c))
    return tiles


def _attn_kernel(slope_ref, q_ref, kp_ref, kc_ref, vp_ref, vc_ref, o_ref,
                 kwin, vwin, bias_scr, m_scr, l_scr, acc_scr):
    W = ATTN_ROWS
    n = BRANCH_SPAN
    step = pl.program_id(2)
    kwin[0:W, :] = kp_ref[...]
    kwin[W:2 * W, :] = kc_ref[...]
    vwin[0:W, :] = vp_ref[...]
    vwin[W:2 * W, :] = vc_ref[...]

    lane = lax.broadcasted_iota(jnp.int32, (n, LANES), 1)
    first_head = lane < HEAD_DIM_ATTN
    row = lax.broadcasted_iota(jnp.int32, (n, 2 * n), 0)
    col = lax.broadcasted_iota(jnp.int32, (n, 2 * n), 1)
    j = n + row - col
    valid = jnp.logical_and(j >= 0, j <= n)
    jf = j.astype(F32)
    for br, d in enumerate(DILATIONS):
        for hh in range(2):
            slope = slope_ref[:, hh * HEAD_DIM_ATTN:hh * HEAD_DIM_ATTN + 1]
            b = jnp.where(valid, -(slope * float(d)) * jf, NEG_INF)
            bias_scr[br, hh, 0] = b
            bias_scr[br, hh, 1] = jnp.where(col >= n, b, NEG_INF)

    for br, d, qs in _attn_tiles():
        lo = W + qs - n * d
        q = q_ref[pl.ds(qs, n, stride=d), :]
        kk = jnp.concatenate([kwin[pl.ds(lo, n, stride=d), :], kwin[pl.ds(W + qs, n, stride=d), :]],
                             axis=0).astype(BF16)
        vv = jnp.concatenate([vwin[pl.ds(lo, n, stride=d), :], vwin[pl.ds(W + qs, n, stride=d), :]],
                             axis=0).astype(BF16)
        if lo < W:
            sel = jnp.where(step == 0, 1, 0)
        else:
            sel = 0
        res = []
        for hh in range(2):
            qm = jnp.where(first_head if hh == 0 else jnp.logical_not(first_head), q, 0.0).astype(BF16)
            s = _dot_nt(qm, kk) + bias_scr[br, hh, sel]
            m = jnp.max(s, axis=-1, keepdims=True)
            p = jnp.exp(s - m)
            l = jnp.sum(p, axis=-1, keepdims=True)
            res.append((m, l, _dot(p.astype(BF16), vv)))
        rows = pl.ds(qs, n, stride=d)
        m_scr[br, rows, :] = jnp.where(first_head, res[0][0], res[1][0])
        l_scr[br, rows, :] = jnp.where(first_head, res[0][1], res[1][1])
        acc_scr[br, rows, :] = jnp.where(first_head, res[0][2], res[1][2])

    chunk = 256

    def combine(i, carry):
        r = pl.ds(pl.multiple_of(i * chunk, chunk), chunk)
        m0, m1, m2 = m_scr[0, r, :], m_scr[1, r, :], m_scr[2, r, :]
        mm = jnp.maximum(jnp.maximum(m0, m1), m2)
        w0, w1, w2 = jnp.exp(m0 - mm), jnp.exp(m1 - mm), jnp.exp(m2 - mm)
        num = w0 * acc_scr[0, r, :] + w1 * acc_scr[1, r, :] + w2 * acc_scr[2, r, :]
        den = w0 * l_scr[0, r, :] + w1 * l_scr[1, r, :] + w2 * l_scr[2, r, :]
        o_ref[r, :] = num / den
        return carry

    lax.fori_loop(0, W // chunk, combine, 0)


def _attention(q, k, v, B, S):
    W = ATTN_ROWS
    nb = S // W
    n_pairs = D_ATTN // LANES
    slopes = jnp.exp2(-8.0 * jnp.arange(1, N_HEADS_ATTN + 1, dtype=F32) / N_HEADS_ATTN)
    slope_l = jnp.repeat(slopes, HEAD_DIM_ATTN).reshape(n_pairs, 1, LANES)
    cur = pl.BlockSpec((W, LANES), lambda b, p, i: (b * nb + i, p))
    prev = pl.BlockSpec((W, LANES), lambda b, p, i: (b * nb + jnp.maximum(i - 1, 0), p))
    n = BRANCH_SPAN
    nbr = len(DILATIONS)
    return pl.pallas_call(
        _attn_kernel,
        grid=(B, n_pairs, nb),
        in_specs=[pl.BlockSpec((None, 1, LANES), lambda b, p, i: (p, 0, 0)), cur, prev, cur, prev, cur],
        out_specs=cur,
        out_shape=jax.ShapeDtypeStruct((B * S, D_ATTN), F32),
        scratch_shapes=[
            pltpu.VMEM((2 * W, LANES), F32), pltpu.VMEM((2 * W, LANES), F32),
            pltpu.VMEM((nbr, 2, 2, n, 2 * n), F32),
            pltpu.VMEM((nbr, W, LANES), F32), pltpu.VMEM((nbr, W, LANES), F32),
            pltpu.VMEM((nbr, W, LANES), F32),
        ],
        compiler_params=_vmem_limit(48),
        name="attention",
    )(slope_l, q, k, k, v, v)


def _mlstm_kernel(xm_ref, vm_ref, om_ref, gcol_ref, grow_ref, cw_ref, cb_ref, wq_ref, wk_ref,
                  bcol_ref, brow_ref, g_ref, o_ref, xs_ref, xtail_ref, c_ref, n_ref, m_ref):
    L = MLSTM_CHUNK
    dh = HEAD_DIM_MLSTM
    nh = N_HEADS_MLSTM

    @pl.when(pl.program_id(1) == 0)
    def _():
        xtail_ref[...] = jnp.zeros_like(xtail_ref)
        c_ref[...] = jnp.zeros_like(c_ref)
        n_ref[...] = jnp.zeros_like(n_ref)
        m_ref[...] = jnp.zeros_like(m_ref)

    x = xm_ref[...]
    xs_ref[0:SUBLANES, :] = xtail_ref[...]
    xs_ref[SUBLANES:SUBLANES + L, :] = x
    xtail_ref[...] = x[L - SUBLANES:L, :]
    xc = cb_ref[...] + x * cw_ref[CONV_K - 1:CONV_K, :]
    for back in range(1, CONV_K):
        xc = xc + xs_ref[SUBLANES - back:SUBLANES - back + L, :] * cw_ref[CONV_K - 1 - back:CONV_K - back, :]
    xc = xc * jax.nn.sigmoid(xc)

    gc = gcol_ref[...] + bcol_ref[...]
    gr = grow_ref[...] + brow_ref[...]
    ri = lax.broadcasted_iota(jnp.int32, (L, L), 0)
    ci = lax.broadcasted_iota(jnp.int32, (L, L), 1)
    causal = ri >= ci
    tril = jnp.where(causal, 1.0, 0.0).astype(BF16)
    triu = jnp.where(ri <= ci, 1.0, 0.0).astype(BF16)
    c1, c2, c3 = _split3(jax.nn.log_sigmoid(gc))
    bc = _dot(tril, c1) + _dot(tril, c2) + _dot(tril, c3)
    r1, r2, r3 = _split3(jax.nn.log_sigmoid(gr))
    brw = _dot(r1, triu) + _dot(r2, triu) + _dot(r3, triu)

    for h in range(nh):
        hs = slice(h * dh, (h + 1) * dh)
        b_col = bc[:, nh + h:nh + h + 1]
        a_col = gc[:, h:h + 1] - b_col
        a_row = gr[h:h + 1, :] - brw[nh + h:nh + h + 1, :]
        m_prev = m_ref[h:h + 1, 0:1]
        log_d = jnp.where(causal, b_col + a_row, NEG_INF)
        log_inter = b_col + m_prev
        m_t = jnp.maximum(log_inter, jnp.max(log_d, axis=-1, keepdims=True))
        d = jnp.exp(log_d - m_t)
        inter = jnp.exp(log_inter - m_t)

        xh = xc[:, hs].astype(BF16)
        qf = _dot(xh, wq_ref[h])
        kf = _dot(xh, wk_ref[h]) * (dh ** -0.5)
        qb = qf.astype(BF16)
        vb = vm_ref[:, hs].astype(BF16)
        s = _dot_nt(qb, kf.astype(BF16)) * d
        c_old = c_ref[h]
        n_old = n_ref[h:h + 1, :]
        num = _dot(s.astype(BF16), vb) + inter * _dot(qb, c_old.astype(BF16))
        den = jnp.sum(s, axis=-1, keepdims=True) + inter * jnp.sum(qf * n_old, axis=-1, keepdims=True)
        hh = num / jnp.maximum(jnp.abs(den), jnp.exp(-m_t))

        b_last = b_col[L - 1:L, :]
        log_w = b_last + a_col
        m_new = jnp.maximum(b_last + m_prev, jnp.max(log_w, axis=0, keepdims=True))
        kw = kf * jnp.exp(log_w - m_new)
        decay = jnp.exp(b_last + m_prev - m_new)
        c_ref[h] = decay * c_old + _dot_tn(kw.astype(BF16), vb)
        n_ref[h:h + 1, :] = decay * n_old + jnp.sum(kw, axis=0, keepdims=True)
        m_ref[h:h + 1, :] = jnp.broadcast_to(m_new, (1, LANES))

        hn = hh * lax.rsqrt(jnp.mean(hh * hh, axis=-1, keepdims=True) + EPS) * g_ref[:, hs]
        o_ref[:, hs] = jax.nn.sigmoid(om_ref[:, hs]) * hn


def _mlstm(xm, vm, om, gates, conv_w, conv_b, w_mq, w_mk, b_i, b_f, mh_g, B, S):
    L = MLSTM_CHUNK
    nc = S // L
    nh = N_HEADS_MLSTM
    ng = 2 * nh
    grow = gates[:, :ng].reshape(B, S, ng).transpose(0, 2, 1)
    bias = jnp.concatenate([b_i, b_f])
    bcol = jnp.pad(bias, (0, LANES - ng))[None, :]
    brow = bias[:, None]
    wq = w_mq.astype(BF16)
    wk = w_mk.astype(BF16)
    cb = conv_b[None, :]
    g = mh_g.reshape(1, D_MLSTM)
    rows = pl.BlockSpec((L, D_MLSTM), lambda b, c: (b * nc + c, 0))
    full = lambda a: pl.BlockSpec(a.shape, lambda b, c: (0,) * a.ndim)
    return pl.pallas_call(
        _mlstm_kernel,
        grid=(B, nc),
        in_specs=[rows, rows, rows, pl.BlockSpec((L, LANES), lambda b, c: (b * nc + c, 0)),
                  pl.BlockSpec((None, ng, L), lambda b, c: (b, 0, c)),
                  full(conv_w), full(cb), full(wq), full(wk), full(bcol), full(brow), full(g)],
        out_specs=rows,
        out_shape=jax.ShapeDtypeStruct((B * S, D_MLSTM), F32),
        scratch_shapes=[
            pltpu.VMEM((L + SUBLANES, D_MLSTM), F32), pltpu.VMEM((SUBLANES, D_MLSTM), F32),
            pltpu.VMEM((nh, HEAD_DIM_MLSTM, HEAD_DIM_MLSTM), F32),
            pltpu.VMEM((SUBLANES, LANES), F32), pltpu.VMEM((SUBLANES, LANES), F32),
        ],
        compiler_params=_vmem_limit(32),
        name="mlstm",
    )(xm, vm, om, gates, grow, conv_w, cb, wq, wk, bcol, brow, g)


def _out_route_kernel(attn_ref, ml_ref, x_ref, wo_ref, g2_ref, wrh_ref, wrl_ref, br_ref,
                      x1_ref, h2_ref, sel_ref, rank_ref, rt_ref, gt_ref, base_ref, cnt_ref, carry_ref):
    tm = ROUTE_ROWS
    ne = N_EXPERTS

    @pl.when(pl.program_id(0) == 0)
    def _():
        carry_ref[...] = jnp.zeros_like(carry_ref)

    x1 = (x_ref[...] + _dot(attn_ref[...].astype(BF16), wo_ref[0:D_ATTN, :])
          + _dot(ml_ref[...].astype(BF16), wo_ref[D_ATTN:D_ATTN + D_MLSTM, :]))
    x1_ref[...] = x1
    h2 = x1 * lax.rsqrt(jnp.mean(x1 * x1, axis=-1, keepdims=True) + EPS) * g2_ref[...]
    h2_ref[...] = h2
    hb = h2.astype(BF16)
    hl = (h2 - hb.astype(F32)).astype(BF16)
    lt = (_dot_nt(wrh_ref[...], hb) + _dot_nt(wrh_ref[...], hl) + _dot_nt(wrl_ref[...], hb)) + br_ref[...]

    eidx = lax.broadcasted_iota(jnp.int32, (ne, tm), 0)
    vals, hots, idxs = [], [], []
    work = lt
    for _ in range(TOP_K):
        mx = jnp.max(work, axis=0, keepdims=True)
        idx = jnp.min(jnp.where(work == mx, eidx, ne), axis=0, keepdims=True)
        hot = eidx == idx
        vals.append(mx)
        hots.append(hot)
        idxs.append(idx)
        work = jnp.where(hot, NEG_INF, work)
    ex = [jnp.exp(v - vals[0]) for v in vals]
    tot = ex[0] + ex[1] + ex[2] + ex[3]

    onehot = jnp.zeros((ne, tm), F32)
    gt = jnp.zeros((ne, tm), F32)
    for kk in range(TOP_K):
        onehot = onehot + jnp.where(hots[kk], 1.0, 0.0)
        gt = gt + jnp.where(hots[kk], ex[kk] / tot, 0.0)
    ri = lax.broadcasted_iota(jnp.int32, (tm, tm), 0)
    ci = lax.broadcasted_iota(jnp.int32, (tm, tm), 1)
    earlier = jnp.where(ri < ci, 1.0, 0.0).astype(BF16)
    carry = carry_ref[...]
    before = _dot(onehot.astype(BF16), earlier) + carry[:, 0:1]
    base_ref[...] = carry
    carry = carry + jnp.sum(onehot, axis=1, keepdims=True)
    carry_ref[...] = carry
    cnt_ref[...] = carry
    rt_ref[...] = jnp.where(onehot > 0.5, before, NOT_ROUTED).astype(jnp.int32)
    gt_ref[...] = gt

    srow = lax.broadcasted_iota(jnp.int32, (SUBLANES, tm), 0)
    sel = jnp.zeros((SUBLANES, tm), jnp.int32)
    rank = jnp.zeros((SUBLANES, tm), F32)
    for kk in range(TOP_K):
        rk = jnp.sum(jnp.where(hots[kk], before, 0.0), axis=0, keepdims=True)
        sel = jnp.where(srow == kk, idxs[kk], sel)
        rank = jnp.where(srow == kk, rk, rank)
    sel_ref[...] = sel
    rank_ref[...] = rank.astype(jnp.int32)


def _out_route(attn, ml, x2, w_out, g2, w_router, b_router):
    T = x2.shape[0]
    tm = ROUTE_ROWS
    ne = N_EXPERTS
    wo = w_out.astype(BF16)
    wrt = w_router.T
    wrh = wrt.astype(BF16)
    wrl = (wrt - wrh.astype(F32)).astype(BF16)
    br = b_router[:, None]
    g2r = g2[None, :]
    row = lambda n: pl.BlockSpec((tm, n), lambda i: (i, 0))
    col = lambda n: pl.BlockSpec((n, tm), lambda i: (0, i))
    full = lambda a: pl.BlockSpec(a.shape, lambda i: (0,) * a.ndim)
    return pl.pallas_call(
        _out_route_kernel,
        grid=(T // tm,),
        in_specs=[row(D_ATTN), row(D_MLSTM), row(D_MODEL), full(wo), full(g2r), full(wrh), full(wrl), full(br)],
        out_specs=[row(D_MODEL), row(D_MODEL), col(SUBLANES), col(SUBLANES), col(ne), col(ne),
                   pl.BlockSpec((ne, LANES), lambda i: (i, 0)), pl.BlockSpec((ne, LANES), lambda i: (0, 0))],
        out_shape=[jax.ShapeDtypeStruct((T, D_MODEL), F32), jax.ShapeDtypeStruct((T, D_MODEL), F32),
                   jax.ShapeDtypeStruct((SUBLANES, T), jnp.int32), jax.ShapeDtypeStruct((SUBLANES, T), jnp.int32),
                   jax.ShapeDtypeStruct((ne, T), jnp.int32), jax.ShapeDtypeStruct((ne, T), F32),
                   jax.ShapeDtypeStruct((T // tm * ne, LANES), F32), jax.ShapeDtypeStruct((ne, LANES), F32)],
        scratch_shapes=[pltpu.VMEM((ne, LANES), F32)],
        compiler_params=_vmem_limit(32),
        name="out_route",
    )(attn, ml, x2, wo, g2r, wrh, wrl, br)


def _row_copy(src_ref, dst_ref, sem, src_row, dst_row):
    return pltpu.make_async_copy(src_ref.at[pl.ds(src_row, 1)], dst_ref.at[pl.ds(dst_row, 1)], sem)


def _dispatch_kernel(padstart_ref, npad_ref, dest_ref, h_ref, xs_ref, zero_ref, sem, zsem):
    td = DISPATCH_ROWS

    def issue(r, carry):
        for kk in range(TOP_K):
            _row_copy(h_ref, xs_ref, sem, r, dest_ref[r * TOP_K + kk]).start()
        return carry

    lax.fori_loop(0, td, issue, 0)

    @pl.when(pl.program_id(0) == 0)
    def _():
        zero_ref[...] = jnp.zeros_like(zero_ref)

        def fill(wait):
            def per_run(e, carry):
                start = padstart_ref[e]
                head = (-start) & (SUBLANES - 1)

                def single(r, c):
                    cp = _row_copy(zero_ref, xs_ref, zsem, 0, start + r)
                    cp.wait() if wait else cp.start()
                    return c

                lax.fori_loop(0, head, single, 0)
                off = pl.multiple_of(start + head, SUBLANES)
                left = npad_ref[e] - head

                def whole(b, c):
                    dst = xs_ref.at[pl.ds(pl.multiple_of(off + b * td, SUBLANES), td)]
                    cp = pltpu.make_async_copy(zero_ref, dst, zsem)
                    cp.wait() if wait else cp.start()
                    return c

                n_whole = left // td
                lax.fori_loop(0, n_whole, whole, 0)
                off = pl.multiple_of(off + n_whole * td, SUBLANES)
                p = td // 2
                while p >= SUBLANES:
                    take = (left & p) != 0

                    @pl.when(take)
                    def _(off=off, p=p):
                        cp = pltpu.make_async_copy(zero_ref.at[pl.ds(0, p)], xs_ref.at[pl.ds(off, p)], zsem)
                        cp.wait() if wait else cp.start()

                    off = pl.multiple_of(off + jnp.where(take, p, 0), SUBLANES)
                    p //= 2
                return carry

            lax.fori_loop(0, N_EXPERTS + 1, per_run, 0)

        fill(wait=False)
        fill(wait=True)

    for kk in range(TOP_K):
        pltpu.make_async_copy(h_ref, xs_ref.at[pl.ds(0, td)], sem).wait()


def _dispatch(h2, dest_flat, padstart, npad, n_slots):
    T = h2.shape[0]
    td = DISPATCH_ROWS
    return pl.pallas_call(
        _dispatch_kernel,
        grid_spec=pltpu.PrefetchScalarGridSpec(
            num_scalar_prefetch=2,
            grid=(T // td,),
            in_specs=[pl.BlockSpec((td * TOP_K,), lambda i, ps, npd: (i,), memory_space=pltpu.SMEM),
                      pl.BlockSpec((td, D_MODEL), lambda i, ps, npd: (i, 0))],
            out_specs=pl.BlockSpec(memory_space=pl.ANY),
            scratch_shapes=[pltpu.VMEM((td, D_MODEL), F32), pltpu.SemaphoreType.DMA(()),
                            pltpu.SemaphoreType.DMA(())],
        ),
        out_shape=jax.ShapeDtypeStruct((n_slots, D_MODEL), F32),
        name="dispatch",
    )(padstart, npad, dest_flat, h2)


def _expert_kernel(bexp_ref, first_ref, next_ref, nused_ref, xs_ref, wgu_hbm, bgu_ref, wd_hbm, bd_ref, y_ref,
                   wgu_st, wd_st, wgu_bf, wd_bf, wsem):
    tm = EXPERT_ROWS
    blk = pl.program_id(0)
    active = blk * tm < nused_ref[0]

    def weight_copies(e):
        return (pltpu.make_async_copy(wgu_hbm.at[e], wgu_st, wsem.at[0]),
                pltpu.make_async_copy(wd_hbm.at[e], wd_st, wsem.at[1]))

    @pl.when(blk == 0)
    def _():
        for cp in weight_copies(bexp_ref[0]):
            cp.start()

    @pl.when(jnp.logical_and(active, first_ref[blk] == 1))
    def _():
        for cp in weight_copies(bexp_ref[blk]):
            cp.wait()
        rc = CAST_ROWS

        def cast(i, carry):
            r = pl.ds(pl.multiple_of(i * rc, rc), rc)
            wgu_bf[r, :] = wgu_st[r, :].astype(BF16)
            wd_bf[r, :] = wd_st[r, :].astype(BF16)
            return carry

        lax.fori_loop(0, D_MODEL // rc, cast, 0)

        @pl.when(next_ref[blk] >= 0)
        def _():
            for cp in weight_copies(next_ref[blk]):
                cp.start()

    @pl.when(active)
    def _():
        xb = xs_ref[...].astype(BF16)
        acc = jnp.zeros((tm, D_MODEL), F32)
        for f in range(D_FF // FF_CHUNK):
            lo = f * FF_CHUNK
            g = _dot(xb, wgu_bf[:, lo:lo + FF_CHUNK]) + bgu_ref[:, lo:lo + FF_CHUNK]
            u = _dot(xb, wgu_bf[:, D_FF + lo:D_FF + lo + FF_CHUNK]) + bgu_ref[:, D_FF + lo:D_FF + lo + FF_CHUNK]
            g = jnp.minimum(g, SWIGLU_LIMIT)
            u = jnp.clip(u, -SWIGLU_LIMIT, SWIGLU_LIMIT)
            act = g * jax.nn.sigmoid(SWIGLU_ALPHA * g) * (u + 1.0)
            acc = acc + _dot(act.astype(BF16), wd_bf[lo:lo + FF_CHUNK, :])
        y_ref[...] = (acc + bd_ref[...]).astype(y_ref.dtype)

    @pl.when(jnp.logical_not(active))
    def _():
        y_ref[...] = jnp.zeros_like(y_ref)


def _experts(xs, block_expert, first_block, next_expert, n_used, w_gate_up, b_gate_up, w_down, b_down):
    n_blocks = block_expert.shape[0]
    tm = EXPERT_ROWS
    assert D_FF == D_MODEL and xs.shape[0] == (n_blocks - 1) * tm
    bgu = b_gate_up[:, None, :]
    bd = b_down[:, None, :]
    last = n_blocks - 2
    return pl.pallas_call(
        _expert_kernel,
        grid_spec=pltpu.PrefetchScalarGridSpec(
            num_scalar_prefetch=4,
            grid=(n_blocks,),
            in_specs=[
                pl.BlockSpec((tm, D_MODEL), lambda i, be, fb, nx, nu: (jnp.minimum(i, last), 0)),
                pl.BlockSpec(memory_space=pl.ANY),
                pl.BlockSpec((None, 1, 2 * D_FF), lambda i, be, fb, nx, nu: (be[i], 0, 0)),
                pl.BlockSpec(memory_space=pl.ANY),
                pl.BlockSpec((None, 1, D_MODEL), lambda i, be, fb, nx, nu: (be[i], 0, 0)),
            ],
            out_specs=pl.BlockSpec((tm, D_MODEL), lambda i, be, fb, nx, nu: (i, 0)),
            scratch_shapes=[pltpu.VMEM((D_MODEL, 2 * D_FF), F32), pltpu.VMEM((D_FF, D_MODEL), F32),
                            pltpu.VMEM((D_MODEL, 2 * D_FF), BF16), pltpu.VMEM((D_FF, D_MODEL), BF16),
                            pltpu.SemaphoreType.DMA((2,))],
        ),
        out_shape=jax.ShapeDtypeStruct((n_blocks * tm, D_MODEL), BF16),
        compiler_params=_vmem_limit(48),
        name="experts",
    )(block_expert, first_block, next_expert, n_used, xs, w_gate_up, bgu, w_down, bd)


def _combine_kernel(fetch_ref, shift_ref, nch_ref, x1_ref, rt_ref, gt_ref, y_ref, o_ref,
                    buf, xbuf, sem, xsem):
    tc = ROUTE_ROWS
    ne = N_EXPERTS
    rb = RUN_ROWS
    grp = RUN_GROUP_ROWS // rb
    base = pl.program_id(0) * ne

    def run_copy(e):
        start = pl.multiple_of(fetch_ref[base + e], BF16_TILE_ROWS)
        return pltpu.make_async_copy(y_ref.at[pl.ds(start, rb)], buf.at[pl.ds(e * rb, rb)], sem.at[e // grp])

    for e in range(ne):
        run_copy(e).start()

    riota = lax.broadcasted_iota(jnp.int32, (rb, tc), 0)

    def select(e_slice, shift):
        return jnp.where(riota == rt_ref[e_slice, :] - shift, gt_ref[e_slice, :], 0.0)

    acc = x1_ref[...]
    for g in range(ne // grp):
        parts = []
        for e in range(g * grp, (g + 1) * grp):
            run_copy(e).wait()
            parts.append(select(slice(e, e + 1), shift_ref[base + e]))
        pt = jnp.concatenate(parts, axis=0).astype(BF16)
        acc = acc + _dot_tn(pt, buf[g * RUN_GROUP_ROWS:(g + 1) * RUN_GROUP_ROWS, :])
    o_ref[...] = acc

    def per_expert(e, carry):
        def chunk(ch, c):
            start = pl.multiple_of(fetch_ref[base + e] + ch * rb, BF16_TILE_ROWS)
            cp = pltpu.make_async_copy(y_ref.at[pl.ds(start, rb)], xbuf, xsem)
            cp.start()
            cp.wait()
            pt = select(pl.ds(e, 1), shift_ref[base + e] + ch * rb).astype(BF16)
            o_ref[...] += _dot_tn(pt, xbuf[...])
            return c

        lax.fori_loop(1, nch_ref[base + e], chunk, 0)
        return carry

    lax.fori_loop(0, ne, per_expert, 0)


def _combine(fetch, shift, nch, x1, rt, gt, y):
    T = x1.shape[0]
    tc = ROUTE_ROWS
    ne = N_EXPERTS
    return pl.pallas_call(
        _combine_kernel,
        grid_spec=pltpu.PrefetchScalarGridSpec(
            num_scalar_prefetch=3,
            grid=(T // tc,),
            in_specs=[pl.BlockSpec((tc, D_MODEL), lambda i, f, s, n: (i, 0)),
                      pl.BlockSpec((ne, tc), lambda i, f, s, n: (0, i)),
                      pl.BlockSpec((ne, tc), lambda i, f, s, n: (0, i)),
                      pl.BlockSpec(memory_space=pl.ANY)],
            out_specs=pl.BlockSpec((tc, D_MODEL), lambda i, f, s, n: (i, 0)),
            scratch_shapes=[pltpu.VMEM((ne * RUN_ROWS, D_MODEL), BF16), pltpu.VMEM((RUN_ROWS, D_MODEL), BF16),
                            pltpu.SemaphoreType.DMA((ne * RUN_ROWS // RUN_GROUP_ROWS,)),
                            pltpu.SemaphoreType.DMA(())],
        ),
        out_shape=jax.ShapeDtypeStruct((T, D_MODEL), F32),
        compiler_params=_vmem_limit(32),
        name="combine",
    )(fetch, shift, nch, x1, rt, gt, y)


def _routing_tables(sel, rank, base, counts, T):
    tm = EXPERT_ROWS
    ne = N_EXPERTS
    A = T * TOP_K
    n_blocks = A // tm + ne
    n_slots = n_blocks * tm
    sizes = counts[:, 0].astype(jnp.int32)
    psizes = (sizes + tm - 1) // tm * tm
    pends = jnp.cumsum(psizes)
    pstarts = pends - psizes
    n_used = pends[-1:]
    dest_flat = (pstarts[sel[:TOP_K]] + rank[:TOP_K]).T.reshape(A)
    padstart = jnp.concatenate([pstarts + sizes, n_used])
    npad = jnp.concatenate([psizes - sizes, n_slots - n_used])
    nb = n_blocks + 1
    blk = jnp.arange(nb, dtype=jnp.int32)
    block_expert = jnp.minimum(jnp.sum(pends[None, :] <= (blk * tm)[:, None], axis=1), ne - 1).astype(jnp.int32)
    first_block = jnp.concatenate([jnp.ones((1,), jnp.int32),
                                   (block_expert[1:] != block_expert[:-1]).astype(jnp.int32)])
    starts_group = jnp.logical_and(first_block == 1, blk * tm < n_used)
    pos = jnp.where(starts_group, blk, nb)
    later = jnp.concatenate([lax.cummin(pos[::-1])[::-1][1:], jnp.full((1,), nb, jnp.int32)])
    next_expert = jnp.where(later < nb, block_expert[jnp.minimum(later, nb - 1)], -1).astype(jnp.int32)
    n_tiles = T // ROUTE_ROWS
    tile_base = base[:, 0].astype(jnp.int32).reshape(n_tiles, ne)
    tile_cnt = jnp.concatenate([tile_base[1:], sizes[None, :]], axis=0) - tile_base
    run_start = pstarts[None, :] + tile_base
    fetch = run_start // BF16_TILE_ROWS * BF16_TILE_ROWS
    shift = fetch - pstarts[None, :]
    nch = jnp.where(tile_cnt > 0, (run_start - fetch + tile_cnt + RUN_ROWS - 1) // RUN_ROWS, 0)
    flat = lambda a: a.reshape(n_tiles * ne).astype(jnp.int32)
    return (dest_flat, padstart, npad, n_slots, block_expert, first_block, next_expert, n_used,
            flat(fetch), flat(shift), flat(nch))


def kernel(x, norm1_g, w_in, q_norm_g, k_norm_g, conv_w, conv_b, w_mq, w_mk, b_igate, b_fgate, mh_norm_g,
           w_out, norm2_g, w_router, b_router, w_gate_up, b_gate_up, w_down, b_down):
    B, S, D = x.shape
    T = B * S
    assert D == D_MODEL and S % ATTN_ROWS == 0 and norm1_g.shape[0] == 1
    x2 = x.reshape(T, D)
    q, k, v, xm, vm, om, gates = _in_proj(x2, norm1_g[0], w_in[0], q_norm_g[0], k_norm_g[0])
    attn = _attention(q, k, v, B, S)
    ml = _mlstm(xm, vm, om, gates, conv_w[0], conv_b[0], w_mq[0], w_mk[0], b_igate[0], b_fgate[0],
                mh_norm_g[0], B, S)
    x1, h2, sel, rank, rt, gt, base, counts = _out_route(attn, ml, x2, w_out[0], norm2_g[0], w_router[0],
                                                         b_router[0])
    (dest_flat, padstart, npad, n_slots, block_expert, first_block, next_expert, n_used,
     fetch, shift, nch) = _routing_tables(sel, rank, base, counts, T)
    xs = _dispatch(h2, dest_flat, padstart, npad, n_slots)
    y = _experts(xs, block_expert, first_block, next_expert, n_used, w_gate_up[0], b_gate_up[0], w_down[0],
                 b_down[0])
    out = _combine(fetch, shift, nch, x1, rt, gt, y)
    return out.reshape(B, S, D)
```

```python
import functools

import jax
import jax.numpy as jnp
from jax import lax
from jax.experimental import pallas as pl
from jax.experimental.pallas import tpu as pltpu

F32 = jnp.float32
BF16 = jnp.bfloat16
NEG_INF = float("-inf")

D_MODEL = 1024
D_ATTN = 512
HEAD_DIM_ATTN = 64
N_HEADS_ATTN = 8
D_MLSTM = 512
HEAD_DIM_MLSTM = 128
N_HEADS_MLSTM = 4
CONV_K = 4
N_EXPERTS = 32
TOP_K = 4
D_FF = 1024
SWIGLU_LIMIT = 7.0
SWIGLU_ALPHA = 1.702
EPS = 1e-6
DILATIONS = (1, 4, 16)
BRANCH_SPAN = 128

LANES = 128
SUBLANES = 8

IN_PROJ_ROWS = 256
ATTN_ROWS = 2048
MLSTM_CHUNK = 256
ROUTE_ROWS = 256
EXPERT_ROWS = 256
FF_CHUNK = 512
CAST_ROWS = 128
DISPATCH_ROWS = 256
HEADNORM_LANES = 256
BF16_TILE_ROWS = 16
RUN_ROWS = 64
RUN_GROUP_ROWS = 256
NOT_ROUTED = -1.0e9


def _vmem_limit(mib):
    return pltpu.CompilerParams(vmem_limit_bytes=mib * 1024 * 1024)


def _split3(a):
    p1 = a.astype(BF16)
    r1 = a - p1.astype(F32)
    p2 = r1.astype(BF16)
    r2 = r1 - p2.astype(F32)
    return p1, p2, r2.astype(BF16)


def _dot(a, b):
    return jnp.dot(a, b, preferred_element_type=F32)


def _dot_nt(a, b):
    return lax.dot_general(a, b, (((1,), (1,)), ((), ())), preferred_element_type=F32)


def _dot_tn(a, b):
    return lax.dot_general(a, b, (((0,), (0,)), ((), ())), preferred_element_type=F32)


def _in_proj_kernel(x_ref, g1_ref, wqkv_ref, wm_ref, wgc_ref, qg_ref, kg_ref, hsum_ref,
                    q_ref, k_ref, v_ref, xm_ref, vm_ref, om_ref, gates_ref):
    x = x_ref[...]
    h = x * lax.rsqrt(jnp.mean(x * x, axis=-1, keepdims=True) + EPS) * g1_ref[...]
    hb = h.astype(BF16)
    hl = (h - hb.astype(F32)).astype(BF16)
    gw = HEADNORM_LANES

    def head_norm(z, g_ref):
        parts = []
        for c in range(D_ATTN // gw):
            zc = z[:, c * gw:(c + 1) * gw]
            ms = _dot((zc * zc).astype(BF16), hsum_ref[...]) * (1.0 / HEAD_DIM_ATTN)
            parts.append(zc * lax.rsqrt(ms + EPS))
        return jnp.concatenate(parts, axis=-1) * g_ref[...]

    zq = _dot(hb, wqkv_ref[:, 0:D_ATTN])
    q_ref[...] = head_norm(zq, qg_ref) * (HEAD_DIM_ATTN ** -0.5)
    zk = _dot(hb, wqkv_ref[:, D_ATTN:2 * D_ATTN])
    k_ref[...] = head_norm(zk, kg_ref)
    v_ref[...] = _dot(hb, wqkv_ref[:, 2 * D_ATTN:3 * D_ATTN])
    xm_ref[...] = _dot(hb, wm_ref[:, 0:D_MLSTM])
    vm_ref[...] = _dot(hb, wm_ref[:, D_MLSTM:2 * D_MLSTM])
    om_ref[...] = _dot(hb, wm_ref[:, 2 * D_MLSTM:3 * D_MLSTM])
    ng = 2 * N_HEADS_MLSTM
    gsum = _dot(hb, wgc_ref[...]) + _dot(hl, wgc_ref[...])
    gsum = gsum + pltpu.roll(gsum, LANES - ng, axis=1)
    lane = lax.broadcasted_iota(jnp.int32, gsum.shape, 1)
    gates_ref[...] = jnp.where(lane < ng, gsum, 0.0)


def _in_proj(x2, g1, w_in, q_g, k_g):
    T = x2.shape[0]
    tm = IN_PROJ_ROWS
    n_qkv = 3 * D_ATTN
    n_m = 3 * D_MLSTM
    ng = 2 * N_HEADS_MLSTM
    wqkv = w_in[:, :n_qkv].astype(BF16)
    wm = w_in[:, n_qkv:n_qkv + n_m].astype(BF16)
    wg = w_in[:, n_qkv + n_m:]
    wgh = wg.astype(BF16)
    wgl = (wg - wgh.astype(F32)).astype(BF16)
    wgc = jnp.pad(jnp.concatenate([wgh, wgl], axis=1), ((0, 0), (0, LANES - 2 * ng)))
    lane = jnp.arange(HEADNORM_LANES)
    hsum = (lane[:, None] // HEAD_DIM_ATTN == lane[None, :] // HEAD_DIM_ATTN).astype(BF16)
    qg = jnp.tile(q_g, N_HEADS_ATTN)[None, :]
    kg = jnp.tile(k_g, N_HEADS_ATTN)[None, :]
    row = lambda n: pl.BlockSpec((tm, n), lambda i: (i, 0))
    full = lambda a: pl.BlockSpec(a.shape, lambda i: (0,) * a.ndim)
    outs = [jax.ShapeDtypeStruct((T, D_ATTN), F32)] * 3 + [jax.ShapeDtypeStruct((T, D_MLSTM), F32)] * 3
    outs.append(jax.ShapeDtypeStruct((T, LANES), F32))
    g1r = g1[None, :]
    return pl.pallas_call(
        _in_proj_kernel,
        grid=(T // tm,),
        in_specs=[row(D_MODEL), full(g1r), full(wqkv), full(wm), full(wgc), full(qg), full(kg), full(hsum)],
        out_specs=[row(D_ATTN)] * 3 + [row(D_MLSTM)] * 3 + [row(LANES)],
        out_shape=outs,
        compiler_params=_vmem_limit(48),
        name="in_proj",
    )(x2, g1r, wqkv, wm, wgc, qg, kg, hsum)


def _attn_tiles():
    tiles = []
    for br, d in enumerate(DILATIONS):
        group = d * BRANCH_SPAN
        for u in range(ATTN_ROWS // group):
            for c in range(d):
                tiles.append((br, d, u * group + c))
    return tiles


def _attn_kernel(slope_ref, q_ref, kp_ref, kc_ref, vp_ref, vc_ref, o_ref,
                 kwin, vwin, bias_scr, m_scr, l_scr, acc_scr):
    W = ATTN_ROWS
    n = BRANCH_SPAN
    step = pl.program_id(2)
    kwin[0:W, :] = kp_ref[...]
    kwin[W:2 * W, :] = kc_ref[...]
    vwin[0:W, :] = vp_ref[...]
    vwin[W:2 * W, :] = vc_ref[...]

    lane = lax.broadcasted_iota(jnp.int32, (n, LANES), 1)
    first_head = lane < HEAD_DIM_ATTN
    row = lax.broadcasted_iota(jnp.int32, (n, 2 * n), 0)
    col = lax.broadcasted_iota(jnp.int32, (n, 2 * n), 1)
    j = n + row - col
    valid = jnp.logical_and(j >= 0, j <= n)
    jf = j.astype(F32)
    for br, d in enumerate(DILATIONS):
        for hh in range(2):
            slope = slope_ref[:, hh * HEAD_DIM_ATTN:hh * HEAD_DIM_ATTN + 1]
            b = jnp.where(valid, -(slope * float(d)) * jf, NEG_INF)
            bias_scr[br, hh, 0] = b
            bias_scr[br, hh, 1] = jnp.where(col >= n, b, NEG_INF)

    for br, d, qs in _attn_tiles():
        lo = W + qs - n * d
        q = q_ref[pl.ds(qs, n, stride=d), :]
        kk = jnp.concatenate([kwin[pl.ds(lo, n, stride=d), :], kwin[pl.ds(W + qs, n, stride=d), :]],
                             axis=0).astype(BF16)
        vv = jnp.concatenate([vwin[pl.ds(lo, n, stride=d), :], vwin[pl.ds(W + qs, n, stride=d), :]],
                             axis=0).astype(BF16)
        if lo < W:
            sel = jnp.where(step == 0, 1, 0)
        else:
            sel = 0
        res = []
        for hh in range(2):
            qm = jnp.where(first_head if hh == 0 else jnp.logical_not(first_head), q, 0.0).astype(BF16)
            s = _dot_nt(qm, kk) + bias_scr[br, hh, sel]
            m = jnp.max(s, axis=-1, keepdims=True)
            p = jnp.exp(s - m)
            l = jnp.sum(p, axis=-1, keepdims=True)
            res.append((m, l, _dot(p.astype(BF16), vv)))
        rows = pl.ds(qs, n, stride=d)
        m_scr[br, rows, :] = jnp.where(first_head, res[0][0], res[1][0])
        l_scr[br, rows, :] = jnp.where(first_head, res[0][1], res[1][1])
        acc_scr[br, rows, :] = jnp.where(first_head, res[0][2], res[1][2])

    chunk = 256

    def combine(i, carry):
        r = pl.ds(pl.multiple_of(i * chunk, chunk), chunk)
        m0, m1, m2 = m_scr[0, r, :], m_scr[1, r, :], m_scr[2, r, :]
        mm = jnp.maximum(jnp.maximum(m0, m1), m2)
        w0, w1, w2 = jnp.exp(m0 - mm), jnp.exp(m1 - mm), jnp.exp(m2 - mm)
        num = w0 * acc_scr[0, r, :] + w1 * acc_scr[1, r, :] + w2 * acc_scr[2, r, :]
        den = w0 * l_scr[0, r, :] + w1 * l_scr[1, r, :] + w2 * l_scr[2, r, :]
        o_ref[r, :] = num / den
        return carry

    lax.fori_loop(0, W // chunk, combine, 0)


def _attention(q, k, v, B, S):
    W = ATTN_ROWS
    nb = S // W
    n_pairs = D_ATTN // LANES
    slopes = jnp.exp2(-8.0 * jnp.arange(1, N_HEADS_ATTN + 1, dtype=F32) / N_HEADS_ATTN)
    slope_l = jnp.repeat(slopes, HEAD_DIM_ATTN).reshape(n_pairs, 1, LANES)
    cur = pl.BlockSpec((W, LANES), lambda b, p, i: (b * nb + i, p))
    prev = pl.BlockSpec((W, LANES), lambda b, p, i: (b * nb + jnp.maximum(i - 1, 0), p))
    n = BRANCH_SPAN
    nbr = len(DILATIONS)
    return pl.pallas_call(
        _attn_kernel,
        grid=(B, n_pairs, nb),
        in_specs=[pl.BlockSpec((None, 1, LANES), lambda b, p, i: (p, 0, 0)), cur, prev, cur, prev, cur],
        out_specs=cur,
        out_shape=jax.ShapeDtypeStruct((B * S, D_ATTN), F32),
        scratch_shapes=[
            pltpu.VMEM((2 * W, LANES), F32), pltpu.VMEM((2 * W, LANES), F32),
            pltpu.VMEM((nbr, 2, 2, n, 2 * n), F32),
            pltpu.VMEM((nbr, W, LANES), F32), pltpu.VMEM((nbr, W, LANES), F32),
            pltpu.VMEM((nbr, W, LANES), F32),
        ],
        compiler_params=_vmem_limit(48),
        name="attention",
    )(slope_l, q, k, k, v, v)


def _mlstm_kernel(xm_ref, vm_ref, om_ref, gcol_ref, grow_ref, cw_ref, cb_ref, wq_ref, wk_ref,
                  bcol_ref, brow_ref, g_ref, o_ref, xs_ref, xtail_ref, c_ref, n_ref, m_ref):
    L = MLSTM_CHUNK
    dh = HEAD_DIM_MLSTM
    nh = N_HEADS_MLSTM

    @pl.when(pl.program_id(1) == 0)
    def _():
        xtail_ref[...] = jnp.zeros_like(xtail_ref)
        c_ref[...] = jnp.zeros_like(c_ref)
        n_ref[...] = jnp.zeros_like(n_ref)
        m_ref[...] = jnp.zeros_like(m_ref)

    x = xm_ref[...]
    xs_ref[0:SUBLANES, :] = xtail_ref[...]
    xs_ref[SUBLANES:SUBLANES + L, :] = x
    xtail_ref[...] = x[L - SUBLANES:L, :]
    xc = cb_ref[...] + x * cw_ref[CONV_K - 1:CONV_K, :]
    for back in range(1, CONV_K):
        xc = xc + xs_ref[SUBLANES - back:SUBLANES - back + L, :] * cw_ref[CONV_K - 1 - back:CONV_K - back, :]
    xc = xc * jax.nn.sigmoid(xc)

    gc = gcol_ref[...] + bcol_ref[...]
    gr = grow_ref[...] + brow_ref[...]
    ri = lax.broadcasted_iota(jnp.int32, (L, L), 0)
    ci = lax.broadcasted_iota(jnp.int32, (L, L), 1)
    causal = ri >= ci
    tril = jnp.where(causal, 1.0, 0.0).astype(BF16)
    triu = jnp.where(ri <= ci, 1.0, 0.0).astype(BF16)
    c1, c2, c3 = _split3(jax.nn.log_sigmoid(gc))
    bc = _dot(tril, c1) + _dot(tril, c2) + _dot(tril, c3)
    r1, r2, r3 = _split3(jax.nn.log_sigmoid(gr))
    brw = _dot(r1, triu) + _dot(r2, triu) + _dot(r3, triu)

    for h in range(nh):
        hs = slice(h * dh, (h + 1) * dh)
        b_col = bc[:, nh + h:nh + h + 1]
        a_col = gc[:, h:h + 1] - b_col
        a_row = gr[h:h + 1, :] - brw[nh + h:nh + h + 1, :]
        m_prev = m_ref[h:h + 1, 0:1]
        log_d = jnp.where(causal, b_col + a_row, NEG_INF)
        log_inter = b_col + m_prev
        m_t = jnp.maximum(log_inter, jnp.max(log_d, axis=-1, keepdims=True))
        d = jnp.exp(log_d - m_t)
        inter = jnp.exp(log_inter - m_t)

        xh = xc[:, hs].astype(BF16)
        qf = _dot(xh, wq_ref[h])
        kf = _dot(xh, wk_ref[h]) * (dh ** -0.5)
        qb = qf.astype(BF16)
        vb = vm_ref[:, hs].astype(BF16)
        s = _dot_nt(qb, kf.astype(BF16)) * d
        c_old = c_ref[h]
        n_old = n_ref[h:h + 1, :]
        num = _dot(s.astype(BF16), vb) + inter * _dot(qb, c_old.astype(BF16))
        den = jnp.sum(s, axis=-1, keepdims=True) + inter * jnp.sum(qf * n_old, axis=-1, keepdims=True)
        hh = num / jnp.maximum(jnp.abs(den), jnp.exp(-m_t))

        b_last = b_col[L - 1:L, :]
        log_w = b_last + a_col
        m_new = jnp.maximum(b_last + m_prev, jnp.max(log_w, axis=0, keepdims=True))
        kw = kf * jnp.exp(log_w - m_new)
        decay = jnp.exp(b_last + m_prev - m_new)
        c_ref[h] = decay * c_old + _dot_tn(kw.astype(BF16), vb)
        n_ref[h:h + 1, :] = decay * n_old + jnp.sum(kw, axis=0, keepdims=True)
        m_ref[h:h + 1, :] = jnp.broadcast_to(m_new, (1, LANES))

        hn = hh * lax.rsqrt(jnp.mean(hh * hh, axis=-1, keepdims=True) + EPS) * g_ref[:, hs]
        o_ref[:, hs] = jax.nn.sigmoid(om_ref[:, hs]) * hn


def _mlstm(xm, vm, om, gates, conv_w, conv_b, w_mq, w_mk, b_i, b_f, mh_g, B, S):
    L = MLSTM_CHUNK
    nc = S // L
    nh = N_HEADS_MLSTM
    ng = 2 * nh
    grow = gates[:, :ng].reshape(B, S, ng).transpose(0, 2, 1)
    bias = jnp.concatenate([b_i, b_f])
    bcol = jnp.pad(bias, (0, LANES - ng))[None, :]
    brow = bias[:, None]
    wq = w_mq.astype(BF16)
    wk = w_mk.astype(BF16)
    cb = conv_b[None, :]
    g = mh_g.reshape(1, D_MLSTM)
    rows = pl.BlockSpec((L, D_MLSTM), lambda b, c: (b * nc + c, 0))
    full = lambda a: pl.BlockSpec(a.shape, lambda b, c: (0,) * a.ndim)
    return pl.pallas_call(
        _mlstm_kernel,
        grid=(B, nc),
        in_specs=[rows, rows, rows, pl.BlockSpec((L, LANES), lambda b, c: (b * nc + c, 0)),
                  pl.BlockSpec((None, ng, L), lambda b, c: (b, 0, c)),
                  full(conv_w), full(cb), full(wq), full(wk), full(bcol), full(brow), full(g)],
        out_specs=rows,
        out_shape=jax.ShapeDtypeStruct((B * S, D_MLSTM), F32),
        scratch_shapes=[
            pltpu.VMEM((L + SUBLANES, D_MLSTM), F32), pltpu.VMEM((SUBLANES, D_MLSTM), F32),
            pltpu.VMEM((nh, HEAD_DIM_MLSTM, HEAD_DIM_MLSTM), F32),
            pltpu.VMEM((SUBLANES, LANES), F32), pltpu.VMEM((SUBLANES, LANES), F32),
        ],
        compiler_params=_vmem_limit(32),
        name="mlstm",
    )(xm, vm, om, gates, grow, conv_w, cb, wq, wk, bcol, brow, g)


def _out_route_kernel(attn_ref, ml_ref, x_ref, wo_ref, g2_ref, wrh_ref, wrl_ref, br_ref, earlier_ref,
                      x1_ref, h2_ref, sel_ref, rank_ref, rt_ref, gt_ref, base_ref, cnt_ref, carry_ref):
    tm = ROUTE_ROWS
    ne = N_EXPERTS

    @pl.when(pl.program_id(0) == 0)
    def _():
        carry_ref[...] = jnp.zeros_like(carry_ref)

    x1 = (x_ref[...] + _dot(attn_ref[...].astype(BF16), wo_ref[0:D_ATTN, :])
          + _dot(ml_ref[...].astype(BF16), wo_ref[D_ATTN:D_ATTN + D_MLSTM, :]))
    x1_ref[...] = x1
    h2 = x1 * lax.rsqrt(jnp.mean(x1 * x1, axis=-1, keepdims=True) + EPS) * g2_ref[...]
    h2_ref[...] = h2
    hb = h2.astype(BF16)
    hl = (h2 - hb.astype(F32)).astype(BF16)
    lt = (_dot_nt(wrh_ref[...], hb) + _dot_nt(wrh_ref[...], hl) + _dot_nt(wrl_ref[...], hb)) + br_ref[...]

    eidx = lax.broadcasted_iota(jnp.int32, (ne, tm), 0)
    vals, hots, idxs = [], [], []
    work = lt
    for _ in range(TOP_K):
        mx = jnp.max(work, axis=0, keepdims=True)
        idx = jnp.min(jnp.where(work == mx, eidx, ne), axis=0, keepdims=True)
        hot = eidx == idx
        vals.append(mx)
        hots.append(hot)
        idxs.append(idx)
        work = jnp.where(hot, NEG_INF, work)
    ex = [jnp.exp(v - vals[0]) for v in vals]
    tot = ex[0] + ex[1] + ex[2] + ex[3]

    onehot = jnp.zeros((ne, tm), F32)
    gt = jnp.zeros((ne, tm), F32)
    for kk in range(TOP_K):
        onehot = onehot + jnp.where(hots[kk], 1.0, 0.0)
        gt = gt + jnp.where(hots[kk], ex[kk] / tot, 0.0)
    carry = carry_ref[...]
    before = _dot(onehot.astype(BF16), earlier_ref[...]) + carry[:, 0:1]
    base_ref[...] = carry
    carry = carry + jnp.sum(onehot, axis=1, keepdims=True)
    carry_ref[...] = carry
    cnt_ref[...] = carry
    rt_ref[...] = jnp.where(onehot > 0.5, before, NOT_ROUTED).astype(jnp.int32)
    gt_ref[...] = gt

    srow = lax.broadcasted_iota(jnp.int32, (SUBLANES, tm), 0)
    sel = jnp.zeros((SUBLANES, tm), jnp.int32)
    rank = jnp.zeros((SUBLANES, tm), F32)
    for kk in range(TOP_K):
        rk = jnp.sum(jnp.where(hots[kk], before, 0.0), axis=0, keepdims=True)
        sel = jnp.where(srow == kk, idxs[kk], sel)
        rank = jnp.where(srow == kk, rk, rank)
    sel_ref[...] = sel
    rank_ref[...] = rank.astype(jnp.int32)


def _out_route(attn, ml, x2, w_out, g2, w_router, b_router):
    T = x2.shape[0]
    tm = ROUTE_ROWS
    ne = N_EXPERTS
    wo = w_out.astype(BF16)
    wrt = w_router.T
    wrh = wrt.astype(BF16)
    wrl = (wrt - wrh.astype(F32)).astype(BF16)
    br = b_router[:, None]
    g2r = g2[None, :]
    tok = jnp.arange(tm)
    earlier = (tok[:, None] < tok[None, :]).astype(BF16)
    row = lambda n: pl.BlockSpec((tm, n), lambda i: (i, 0))
    col = lambda n: pl.BlockSpec((n, tm), lambda i: (0, i))
    full = lambda a: pl.BlockSpec(a.shape, lambda i: (0,) * a.ndim)
    return pl.pallas_call(
        _out_route_kernel,
        grid=(T // tm,),
        in_specs=[row(D_ATTN), row(D_MLSTM), row(D_MODEL), full(wo), full(g2r), full(wrh), full(wrl), full(br),
                  full(earlier)],
        out_specs=[row(D_MODEL), row(D_MODEL), col(SUBLANES), col(SUBLANES), col(ne), col(ne),
                   pl.BlockSpec((ne, LANES), lambda i: (i, 0)), pl.BlockSpec((ne, LANES), lambda i: (0, 0))],
        out_shape=[jax.ShapeDtypeStruct((T, D_MODEL), F32), jax.ShapeDtypeStruct((T, D_MODEL), F32),
                   jax.ShapeDtypeStruct((SUBLANES, T), jnp.int32), jax.ShapeDtypeStruct((SUBLANES, T), jnp.int32),
                   jax.ShapeDtypeStruct((ne, T), jnp.int32), jax.ShapeDtypeStruct((ne, T), F32),
                   jax.ShapeDtypeStruct((T // tm * ne, LANES), F32), jax.ShapeDtypeStruct((ne, LANES), F32)],
        scratch_shapes=[pltpu.VMEM((ne, LANES), F32)],
        compiler_params=_vmem_limit(32),
        name="out_route",
    )(attn, ml, x2, wo, g2r, wrh, wrl, br, earlier)


def _row_copy(src_ref, dst_ref, sem, src_row, dst_row):
    return pltpu.make_async_copy(src_ref.at[pl.ds(src_row, 1)], dst_ref.at[pl.ds(dst_row, 1)], sem)


def _dispatch_kernel(padstart_ref, npad_ref, dest_ref, h_ref, xs_ref, zero_ref, sem, zsem):
    td = DISPATCH_ROWS

    def issue(r, carry):
        for kk in range(TOP_K):
            _row_copy(h_ref, xs_ref, sem, r, dest_ref[r * TOP_K + kk]).start()
        return carry

    lax.fori_loop(0, td, issue, 0)

    @pl.when(pl.program_id(0) == 0)
    def _():
        zero_ref[...] = jnp.zeros_like(zero_ref)

        def fill(wait):
            def per_run(e, carry):
                start = padstart_ref[e]
                head = (-start) & (SUBLANES - 1)

                def single(r, c):
                    cp = _row_copy(zero_ref, xs_ref, zsem, 0, start + r)
                    cp.wait() if wait else cp.start()
                    return c

                lax.fori_loop(0, head, single, 0)
                off = pl.multiple_of(start + head, SUBLANES)
                left = npad_ref[e] - head

                def whole(b, c):
                    dst = xs_ref.at[pl.ds(pl.multiple_of(off + b * td, SUBLANES), td)]
                    cp = pltpu.make_async_copy(zero_ref, dst, zsem)
                    cp.wait() if wait else cp.start()
                    return c

                n_whole = left // td
                lax.fori_loop(0, n_whole, whole, 0)
                off = pl.multiple_of(off + n_whole * td, SUBLANES)
                p = td // 2
                while p >= SUBLANES:
                    take = (left & p) != 0

                    @pl.when(take)
                    def _(off=off, p=p):
                        cp = pltpu.make_async_copy(zero_ref.at[pl.ds(0, p)], xs_ref.at[pl.ds(off, p)], zsem)
                        cp.wait() if wait else cp.start()

                    off = pl.multiple_of(off + jnp.where(take, p, 0), SUBLANES)
                    p //= 2
                return carry

            lax.fori_loop(0, N_EXPERTS + 1, per_run, 0)

        fill(wait=False)
        fill(wait=True)

    for kk in range(TOP_K):
        pltpu.make_async_copy(h_ref, xs_ref.at[pl.ds(0, td)], sem).wait()


def _dispatch(h2, dest_flat, padstart, npad, n_slots):
    T = h2.shape[0]
    td = DISPATCH_ROWS
    return pl.pallas_call(
        _dispatch_kernel,
        grid_spec=pltpu.PrefetchScalarGridSpec(
            num_scalar_prefetch=2,
            grid=(T // td,),
            in_specs=[pl.BlockSpec((td * TOP_K,), lambda i, ps, npd: (i,), memory_space=pltpu.SMEM),
                      pl.BlockSpec((td, D_MODEL), lambda i, ps, npd: (i, 0))],
            out_specs=pl.BlockSpec(memory_space=pl.ANY),
            scratch_shapes=[pltpu.VMEM((td, D_MODEL), F32), pltpu.SemaphoreType.DMA(()),
                            pltpu.SemaphoreType.DMA(())],
        ),
        out_shape=jax.ShapeDtypeStruct((n_slots, D_MODEL), F32),
        name="dispatch",
    )(padstart, npad, dest_flat, h2)


def _expert_kernel(bexp_ref, first_ref, next_ref, nused_ref, xs_ref, wgu_hbm, bgu_ref, wd_hbm, bd_ref, y_ref,
                   wgu_st, wd_st, wgu_bf, wd_bf, wsem):
    tm = EXPERT_ROWS
    blk = pl.program_id(0)
    active = blk * tm < nused_ref[0]

    def weight_copies(e):
        return (pltpu.make_async_copy(wgu_hbm.at[e], wgu_st, wsem.at[0]),
                pltpu.make_async_copy(wd_hbm.at[e], wd_st, wsem.at[1]))

    @pl.when(blk == 0)
    def _():
        for cp in weight_copies(bexp_ref[0]):
            cp.start()

    @pl.when(jnp.logical_and(active, first_ref[blk] == 1))
    def _():
        for cp in weight_copies(bexp_ref[blk]):
            cp.wait()
        rc = CAST_ROWS

        def cast(i, carry):
            r = pl.ds(pl.multiple_of(i * rc, rc), rc)
            wgu_bf[r, :] = wgu_st[r, :].astype(BF16)
            wd_bf[r, :] = wd_st[r, :].astype(BF16)
            return carry

        lax.fori_loop(0, D_MODEL // rc, cast, 0)

        @pl.when(next_ref[blk] >= 0)
        def _():
            for cp in weight_copies(next_ref[blk]):
                cp.start()

    @pl.when(active)
    def _():
        xb = xs_ref[...].astype(BF16)
        acc = jnp.zeros((tm, D_MODEL), F32)
        for f in range(D_FF // FF_CHUNK):
            lo = f * FF_CHUNK
            g = _dot(xb, wgu_bf[:, lo:lo + FF_CHUNK]) + bgu_ref[:, lo:lo + FF_CHUNK]
            u = _dot(xb, wgu_bf[:, D_FF + lo:D_FF + lo + FF_CHUNK]) + bgu_ref[:, D_FF + lo:D_FF + lo + FF_CHUNK]
            g = jnp.minimum(g, SWIGLU_LIMIT)
            u = jnp.clip(u, -SWIGLU_LIMIT, SWIGLU_LIMIT)
            act = g * jax.nn.sigmoid(SWIGLU_ALPHA * g) * (u + 1.0)
            acc = acc + _dot(act.astype(BF16), wd_bf[lo:lo + FF_CHUNK, :])
        y_ref[...] = (acc + bd_ref[...]).astype(y_ref.dtype)

    @pl.when(jnp.logical_not(active))
    def _():
        y_ref[...] = jnp.zeros_like(y_ref)


def _experts(xs, block_expert, first_block, next_expert, n_used, w_gate_up, b_gate_up, w_down, b_down):
    n_blocks = block_expert.shape[0]
    tm = EXPERT_ROWS
    assert D_FF == D_MODEL and xs.shape[0] == (n_blocks - 1) * tm
    bgu = b_gate_up[:, None, :]
    bd = b_down[:, None, :]
    last = n_blocks - 2
    return pl.pallas_call(
        _expert_kernel,
        grid_spec=pltpu.PrefetchScalarGridSpec(
            num_scalar_prefetch=4,
            grid=(n_blocks,),
            in_specs=[
                pl.BlockSpec((tm, D_MODEL), lambda i, be, fb, nx, nu: (jnp.minimum(i, last), 0)),
                pl.BlockSpec(memory_space=pl.ANY),
                pl.BlockSpec((None, 1, 2 * D_FF), lambda i, be, fb, nx, nu: (be[i], 0, 0)),
                pl.BlockSpec(memory_space=pl.ANY),
                pl.BlockSpec((None, 1, D_MODEL), lambda i, be, fb, nx, nu: (be[i], 0, 0)),
            ],
            out_specs=pl.BlockSpec((tm, D_MODEL), lambda i, be, fb, nx, nu: (i, 0)),
            scratch_shapes=[pltpu.VMEM((D_MODEL, 2 * D_FF), F32), pltpu.VMEM((D_FF, D_MODEL), F32),
                            pltpu.VMEM((D_MODEL, 2 * D_FF), BF16), pltpu.VMEM((D_FF, D_MODEL), BF16),
                            pltpu.SemaphoreType.DMA((2,))],
        ),
        out_shape=jax.ShapeDtypeStruct((n_blocks * tm, D_MODEL), BF16),
        compiler_params=_vmem_limit(48),
        name="experts",
    )(block_expert, first_block, next_expert, n_used, xs, w_gate_up, bgu, w_down, bd)


def _combine_kernel(fetch_ref, shift_ref, nch_ref, x1_ref, rt_ref, gt_ref, y_ref, o_ref,
                    buf, xbuf, sem, xsem):
    tc = ROUTE_ROWS
    ne = N_EXPERTS
    rb = RUN_ROWS
    grp = RUN_GROUP_ROWS // rb
    tile = pl.program_id(0)
    base = tile * ne
    slot = tile % 2

    def run_copy(tile_base, buf_slot, e):
        start = pl.multiple_of(fetch_ref[tile_base + e], BF16_TILE_ROWS)
        return pltpu.make_async_copy(y_ref.at[pl.ds(start, rb)], buf.at[buf_slot, pl.ds(e * rb, rb)],
                                     sem.at[buf_slot, e // grp])

    @pl.when(tile == 0)
    def _():
        for e in range(ne):
            run_copy(base, slot, e).start()

    @pl.when(tile + 1 < pl.num_programs(0))
    def _():
        for e in range(ne):
            run_copy(base + ne, 1 - slot, e).start()

    riota = lax.broadcasted_iota(jnp.int32, (rb, tc), 0)

    def select(e_slice, shift):
        return jnp.where(riota == rt_ref[e_slice, :] - shift, gt_ref[e_slice, :], 0.0)

    acc = x1_ref[...]
    for g in range(ne // grp):
        parts = []
        for e in range(g * grp, (g + 1) * grp):
            run_copy(base, slot, e).wait()
            parts.append(select(slice(e, e + 1), shift_ref[base + e]))
        pt = jnp.concatenate(parts, axis=0).astype(BF16)
        acc = acc + _dot_tn(pt, buf[slot, g * RUN_GROUP_ROWS:(g + 1) * RUN_GROUP_ROWS, :])
    o_ref[...] = acc

    def per_expert(e, carry):
        def chunk(ch, c):
            start = pl.multiple_of(fetch_ref[base + e] + ch * rb, BF16_TILE_ROWS)
            cp = pltpu.make_async_copy(y_ref.at[pl.ds(start, rb)], xbuf, xsem)
            cp.start()
            cp.wait()
            pt = select(pl.ds(e, 1), shift_ref[base + e] + ch * rb).astype(BF16)
            o_ref[...] += _dot_tn(pt, xbuf[...])
            return c

        lax.fori_loop(1, nch_ref[base + e], chunk, 0)
        return carry

    lax.fori_loop(0, ne, per_expert, 0)


def _combine(fetch, shift, nch, x1, rt, gt, y):
    T = x1.shape[0]
    tc = ROUTE_ROWS
    ne = N_EXPERTS
    return pl.pallas_call(
        _combine_kernel,
        grid_spec=pltpu.PrefetchScalarGridSpec(
            num_scalar_prefetch=3,
            grid=(T // tc,),
            in_specs=[pl.BlockSpec((tc, D_MODEL), lambda i, f, s, n: (i, 0)),
                      pl.BlockSpec((ne, tc), lambda i, f, s, n: (0, i)),
                      pl.BlockSpec((ne, tc), lambda i, f, s, n: (0, i)),
                      pl.BlockSpec(memory_space=pl.ANY)],
            out_specs=pl.BlockSpec((tc, D_MODEL), lambda i, f, s, n: (i, 0)),
            scratch_shapes=[pltpu.VMEM((2, ne * RUN_ROWS, D_MODEL), BF16), pltpu.VMEM((RUN_ROWS, D_MODEL), BF16),
                            pltpu.SemaphoreType.DMA((2, ne * RUN_ROWS // RUN_GROUP_ROWS)),
                            pltpu.SemaphoreType.DMA(())],
        ),
        out_shape=jax.ShapeDtypeStruct((T, D_MODEL), F32),
        compiler_params=_vmem_limit(32),
        name="combine",
    )(fetch, shift, nch, x1, rt, gt, y)


def _routing_tables(sel, rank, base, counts, T):
    tm = EXPERT_ROWS
    ne = N_EXPERTS
    A = T * TOP_K
    n_blocks = A // tm + ne
    n_slots = n_blocks * tm
    sizes = counts[:, 0].astype(jnp.int32)
    psizes = (sizes + tm - 1) // tm * tm
    pends = jnp.cumsum(psizes)
    pstarts = pends - psizes
    n_used = pends[-1:]
    hot = sel[:TOP_K, None, :] == jnp.arange(ne, dtype=jnp.int32)[None, :, None]
    dest_flat = (jnp.sum(jnp.where(hot, pstarts[None, :, None], 0), axis=1) + rank[:TOP_K]).T.reshape(A)
    padstart = jnp.concatenate([pstarts + sizes, n_used])
    npad = jnp.concatenate([psizes - sizes, n_slots - n_used])
    nb = n_blocks + 1
    blk = jnp.arange(nb, dtype=jnp.int32)
    block_expert = jnp.minimum(jnp.sum(pends[None, :] <= (blk * tm)[:, None], axis=1), ne - 1).astype(jnp.int32)
    first_block = jnp.concatenate([jnp.ones((1,), jnp.int32),
                                   (block_expert[1:] != block_expert[:-1]).astype(jnp.int32)])
    starts_group = jnp.logical_and(first_block == 1, blk * tm < n_used)
    pos = jnp.where(starts_group, blk, nb)
    later = jnp.concatenate([lax.cummin(pos[::-1])[::-1][1:], jnp.full((1,), nb, jnp.int32)])
    next_expert = jnp.where(later < nb, block_expert[jnp.minimum(later, nb - 1)], -1).astype(jnp.int32)
    n_tiles = T // ROUTE_ROWS
    tile_base = base[:, 0].astype(jnp.int32).reshape(n_tiles, ne)
    tile_cnt = jnp.concatenate([tile_base[1:], sizes[None, :]], axis=0) - tile_base
    run_start = pstarts[None, :] + tile_base
    fetch = run_start // BF16_TILE_ROWS * BF16_TILE_ROWS
    shift = fetch - pstarts[None, :]
    nch = jnp.where(tile_cnt > 0, (run_start - fetch + tile_cnt + RUN_ROWS - 1) // RUN_ROWS, 0)
    flat = lambda a: a.reshape(n_tiles * ne).astype(jnp.int32)
    return (dest_flat, padstart, npad, n_slots, block_expert, first_block, next_expert, n_used,
            flat(fetch), flat(shift), flat(nch))


def kernel(x, norm1_g, w_in, q_norm_g, k_norm_g, conv_w, conv_b, w_mq, w_mk, b_igate, b_fgate, mh_norm_g,
           w_out, norm2_g, w_router, b_router, w_gate_up, b_gate_up, w_down, b_down):
    B, S, D = x.shape
    T = B * S
    assert D == D_MODEL and S % ATTN_ROWS == 0 and norm1_g.shape[0] == 1
    x2 = x.reshape(T, D)
    q, k, v, xm, vm, om, gates = _in_proj(x2, norm1_g[0], w_in[0], q_norm_g[0], k_norm_g[0])
    attn = _attention(q, k, v, B, S)
    ml = _mlstm(xm, vm, om, gates, conv_w[0], conv_b[0], w_mq[0], w_mk[0], b_igate[0], b_fgate[0],
                mh_norm_g[0], B, S)
    x1, h2, sel, rank, rt, gt, base, counts = _out_route(attn, ml, x2, w_out[0], norm2_g[0], w_router[0],
                                                         b_router[0])
    (dest_flat, padstart, npad, n_slots, block_expert, first_block, next_expert, n_used,
     fetch, shift, nch) = _routing_tables(sel, rank, base, counts, T)
    xs = _dispatch(h2, dest_flat, padstart, npad, n_slots)
    y = _experts(xs, block_expert, first_block, next_expert, n_used, w_gate_up[0], b_gate_up[0], w_down[0],
                 b_down[0])
    out = _combine(fetch, shift, nch, x1, rt, gt, y)
    return out.reshape(B, S, D)
```

```python
import functools

import jax
import jax.numpy as jnp
from jax import lax
from jax.experimental import pallas as pl
from jax.experimental.pallas import tpu as pltpu

F32 = jnp.float32
BF16 = jnp.bfloat16
NEG_INF = float("-inf")

D_MODEL = 1024
D_ATTN = 512
HEAD_DIM_ATTN = 64
N_HEADS_ATTN = 8
D_MLSTM = 512
HEAD_DIM_MLSTM = 128
N_HEADS_MLSTM = 4
CONV_K = 4
N_EXPERTS = 32
TOP_K = 4
D_FF = 1024
SWIGLU_LIMIT = 7.0
SWIGLU_ALPHA = 1.702
EPS = 1e-6
DILATIONS = (1, 4, 16)
BRANCH_SPAN = 128

LANES = 128
SUBLANES = 8

IN_PROJ_ROWS = 256
ATTN_ROWS = 2048
MLSTM_CHUNK = 256
ROUTE_ROWS = 256
EXPERT_ROWS = 256
FF_CHUNK = 512
CAST_ROWS = 128
DISPATCH_ROWS = 256
HEADNORM_LANES = 256
BF16_TILE_ROWS = 16
RUN_ROWS = 64
RUN_GROUP_ROWS = 256
NOT_ROUTED = -1.0e9


def _vmem_limit(mib):
    return pltpu.CompilerParams(vmem_limit_bytes=mib * 1024 * 1024)


def _split3(a):
    p1 = a.astype(BF16)
    r1 = a - p1.astype(F32)
    p2 = r1.astype(BF16)
    r2 = r1 - p2.astype(F32)
    return p1, p2, r2.astype(BF16)


def _dot(a, b):
    return jnp.dot(a, b, preferred_element_type=F32)


def _dot_nt(a, b):
    return lax.dot_general(a, b, (((1,), (1,)), ((), ())), preferred_element_type=F32)


def _dot_tn(a, b):
    return lax.dot_general(a, b, (((0,), (0,)), ((), ())), preferred_element_type=F32)


def _in_proj_kernel(x_ref, g1_ref, wqkv_ref, wm_ref, wgc_ref, qg_ref, kg_ref, hsum_ref,
                    q_ref, k_ref, v_ref, xm_ref, vm_ref, om_ref, gates_ref):
    x = x_ref[...]
    h = x * lax.rsqrt(jnp.mean(x * x, axis=-1, keepdims=True) + EPS) * g1_ref[...]
    hb = h.astype(BF16)
    hl = (h - hb.astype(F32)).astype(BF16)
    gw = HEADNORM_LANES

    def head_norm(z, g_ref):
        parts = []
        for c in range(D_ATTN // gw):
            zc = z[:, c * gw:(c + 1) * gw]
            ms = _dot((zc * zc).astype(BF16), hsum_ref[...]) * (1.0 / HEAD_DIM_ATTN)
            parts.append(zc * lax.rsqrt(ms + EPS))
        return jnp.concatenate(parts, axis=-1) * g_ref[...]

    zq = _dot(hb, wqkv_ref[:, 0:D_ATTN])
    q_ref[...] = head_norm(zq, qg_ref) * (HEAD_DIM_ATTN ** -0.5)
    zk = _dot(hb, wqkv_ref[:, D_ATTN:2 * D_ATTN])
    k_ref[...] = head_norm(zk, kg_ref)
    v_ref[...] = _dot(hb, wqkv_ref[:, 2 * D_ATTN:3 * D_ATTN])
    xm_ref[...] = _dot(hb, wm_ref[:, 0:D_MLSTM])
    vm_ref[...] = _dot(hb, wm_ref[:, D_MLSTM:2 * D_MLSTM])
    om_ref[...] = _dot(hb, wm_ref[:, 2 * D_MLSTM:3 * D_MLSTM])
    ng = 2 * N_HEADS_MLSTM
    gsum = _dot(hb, wgc_ref[...]) + _dot(hl, wgc_ref[...])
    gsum = gsum + pltpu.roll(gsum, LANES - ng, axis=1)
    lane = lax.broadcasted_iota(jnp.int32, gsum.shape, 1)
    gates_ref[...] = jnp.where(lane < ng, gsum, 0.0)


def _in_proj(x2, g1, w_in, q_g, k_g):
    T = x2.shape[0]
    tm = IN_PROJ_ROWS
    n_qkv = 3 * D_ATTN
    n_m = 3 * D_MLSTM
    ng = 2 * N_HEADS_MLSTM
    wqkv = w_in[:, :n_qkv].astype(BF16)
    wm = w_in[:, n_qkv:n_qkv + n_m].astype(BF16)
    wg = w_in[:, n_qkv + n_m:]
    wgh = wg.astype(BF16)
    wgl = (wg - wgh.astype(F32)).astype(BF16)
    wgc = jnp.pad(jnp.concatenate([wgh, wgl], axis=1), ((0, 0), (0, LANES - 2 * ng)))
    lane = jnp.arange(HEADNORM_LANES)
    hsum = (lane[:, None] // HEAD_DIM_ATTN == lane[None, :] // HEAD_DIM_ATTN).astype(BF16)
    qg = jnp.tile(q_g, N_HEADS_ATTN)[None, :]
    kg = jnp.tile(k_g, N_HEADS_ATTN)[None, :]
    row = lambda n: pl.BlockSpec((tm, n), lambda i: (i, 0))
    full = lambda a: pl.BlockSpec(a.shape, lambda i: (0,) * a.ndim)
    outs = [jax.ShapeDtypeStruct((T, D_ATTN), F32)] * 3 + [jax.ShapeDtypeStruct((T, D_MLSTM), F32)] * 3
    outs.append(jax.ShapeDtypeStruct((T, LANES), F32))
    g1r = g1[None, :]
    return pl.pallas_call(
        _in_proj_kernel,
        grid=(T // tm,),
        in_specs=[row(D_MODEL), full(g1r), full(wqkv), full(wm), full(wgc), full(qg), full(kg), full(hsum)],
        out_specs=[row(D_ATTN)] * 3 + [row(D_MLSTM)] * 3 + [row(LANES)],
        out_shape=outs,
        compiler_params=_vmem_limit(48),
        name="in_proj",
    )(x2, g1r, wqkv, wm, wgc, qg, kg, hsum)


def _attn_tiles():
    tiles = []
    for br, d in enumerate(DILATIONS):
        group = d * BRANCH_SPAN
        for u in range(ATTN_ROWS // group):
            for c in range(d):
                tiles.append((br, d, u * group + c))
    return tiles


def _attn_kernel(slope_ref, q_ref, kp_ref, kc_ref, vp_ref, vc_ref, o_ref,
                 kwin, vwin, bias_scr, m_scr, l_scr, acc_scr):
    W = ATTN_ROWS
    n = BRANCH_SPAN
    step = pl.program_id(2)
    kwin[0:W, :] = kp_ref[...]
    kwin[W:2 * W, :] = kc_ref[...]
    vwin[0:W, :] = vp_ref[...]
    vwin[W:2 * W, :] = vc_ref[...]

    lane = lax.broadcasted_iota(jnp.int32, (n, LANES), 1)
    first_head = lane < HEAD_DIM_ATTN
    row = lax.broadcasted_iota(jnp.int32, (n, 2 * n), 0)
    col = lax.broadcasted_iota(jnp.int32, (n, 2 * n), 1)
    j = n + row - col
    valid = jnp.logical_and(j >= 0, j <= n)
    jf = j.astype(F32)
    for br, d in enumerate(DILATIONS):
        for hh in range(2):
            slope = slope_ref[:, hh * HEAD_DIM_ATTN:hh * HEAD_DIM_ATTN + 1]
            b = jnp.where(valid, -(slope * float(d)) * jf, NEG_INF)
            bias_scr[br, hh, 0] = b
            bias_scr[br, hh, 1] = jnp.where(col >= n, b, NEG_INF)

    for br, d, qs in _attn_tiles():
        lo = W + qs - n * d
        q = q_ref[pl.ds(qs, n, stride=d), :]
        kk = jnp.concatenate([kwin[pl.ds(lo, n, stride=d), :], kwin[pl.ds(W + qs, n, stride=d), :]],
                             axis=0).astype(BF16)
        vv = jnp.concatenate([vwin[pl.ds(lo, n, stride=d), :], vwin[pl.ds(W + qs, n, stride=d), :]],
                             axis=0).astype(BF16)
        if lo < W:
            sel = jnp.where(step == 0, 1, 0)
        else:
            sel = 0
        res = []
        for hh in range(2):
            qm = jnp.where(first_head if hh == 0 else jnp.logical_not(first_head), q, 0.0).astype(BF16)
            s = _dot_nt(qm, kk) + bias_scr[br, hh, sel]
            m = jnp.max(s, axis=-1, keepdims=True)
            p = jnp.exp(s - m)
            l = jnp.sum(p, axis=-1, keepdims=True)
            res.append((m, l, _dot(p.astype(BF16), vv)))
        rows = pl.ds(qs, n, stride=d)
        m_scr[br, rows, :] = jnp.where(first_head, res[0][0], res[1][0])
        l_scr[br, rows, :] = jnp.where(first_head, res[0][1], res[1][1])
        acc_scr[br, rows, :] = jnp.where(first_head, res[0][2], res[1][2])

    chunk = 256

    def combine(i, carry):
        r = pl.ds(pl.multiple_of(i * chunk, chunk), chunk)
        m0, m1, m2 = m_scr[0, r, :], m_scr[1, r, :], m_scr[2, r, :]
        mm = jnp.maximum(jnp.maximum(m0, m1), m2)
        w0, w1, w2 = jnp.exp(m0 - mm), jnp.exp(m1 - mm), jnp.exp(m2 - mm)
        num = w0 * acc_scr[0, r, :] + w1 * acc_scr[1, r, :] + w2 * acc_scr[2, r, :]
        den = w0 * l_scr[0, r, :] + w1 * l_scr[1, r, :] + w2 * l_scr[2, r, :]
        o_ref[r, :] = num / den
        return carry

    lax.fori_loop(0, W // chunk, combine, 0)


def _attention(q, k, v, B, S):
    W = ATTN_ROWS
    nb = S // W
    n_pairs = D_ATTN // LANES
    slopes = jnp.exp2(-8.0 * jnp.arange(1, N_HEADS_ATTN + 1, dtype=F32) / N_HEADS_ATTN)
    slope_l = jnp.repeat(slopes, HEAD_DIM_ATTN).reshape(n_pairs, 1, LANES)
    cur = pl.BlockSpec((W, LANES), lambda b, p, i: (b * nb + i, p))
    prev = pl.BlockSpec((W, LANES), lambda b, p, i: (b * nb + jnp.maximum(i - 1, 0), p))
    n = BRANCH_SPAN
    nbr = len(DILATIONS)
    return pl.pallas_call(
        _attn_kernel,
        grid=(B, n_pairs, nb),
        in_specs=[pl.BlockSpec((None, 1, LANES), lambda b, p, i: (p, 0, 0)), cur, prev, cur, prev, cur],
        out_specs=cur,
        out_shape=jax.ShapeDtypeStruct((B * S, D_ATTN), F32),
        scratch_shapes=[
            pltpu.VMEM((2 * W, LANES), F32), pltpu.VMEM((2 * W, LANES), F32),
            pltpu.VMEM((nbr, 2, 2, n, 2 * n), F32),
            pltpu.VMEM((nbr, W, LANES), F32), pltpu.VMEM((nbr, W, LANES), F32),
            pltpu.VMEM((nbr, W, LANES), F32),
        ],
        compiler_params=_vmem_limit(48),
        name="attention",
    )(slope_l, q, k, k, v, v)


def _mlstm_kernel(xm_ref, vm_ref, om_ref, gcol_ref, grow_ref, cw_ref, cb_ref, wq_ref, wk_ref,
                  bcol_ref, brow_ref, g_ref, o_ref, xs_ref, xtail_ref, c_ref, n_ref, m_ref):
    L = MLSTM_CHUNK
    dh = HEAD_DIM_MLSTM
    nh = N_HEADS_MLSTM

    @pl.when(pl.program_id(1) == 0)
    def _():
        xtail_ref[...] = jnp.zeros_like(xtail_ref)
        c_ref[...] = jnp.zeros_like(c_ref)
        n_ref[...] = jnp.zeros_like(n_ref)
        m_ref[...] = jnp.zeros_like(m_ref)

    x = xm_ref[...]
    xs_ref[0:SUBLANES, :] = xtail_ref[...]
    xs_ref[SUBLANES:SUBLANES + L, :] = x
    xtail_ref[...] = x[L - SUBLANES:L, :]
    xc = cb_ref[...] + x * cw_ref[CONV_K - 1:CONV_K, :]
    for back in range(1, CONV_K):
        xc = xc + xs_ref[SUBLANES - back:SUBLANES - back + L, :] * cw_ref[CONV_K - 1 - back:CONV_K - back, :]
    xc = xc * jax.nn.sigmoid(xc)

    gc = gcol_ref[...] + bcol_ref[...]
    gr = grow_ref[...] + brow_ref[...]
    ri = lax.broadcasted_iota(jnp.int32, (L, L), 0)
    ci = lax.broadcasted_iota(jnp.int32, (L, L), 1)
    causal = ri >= ci
    tril = jnp.where(causal, 1.0, 0.0).astype(BF16)
    triu = jnp.where(ri <= ci, 1.0, 0.0).astype(BF16)
    c1, c2, c3 = _split3(jax.nn.log_sigmoid(gc))
    bc = _dot(tril, c1) + _dot(tril, c2) + _dot(tril, c3)
    r1, r2, r3 = _split3(jax.nn.log_sigmoid(gr))
    brw = _dot(r1, triu) + _dot(r2, triu) + _dot(r3, triu)

    for h in range(nh):
        hs = slice(h * dh, (h + 1) * dh)
        b_col = bc[:, nh + h:nh + h + 1]
        a_col = gc[:, h:h + 1] - b_col
        a_row = gr[h:h + 1, :] - brw[nh + h:nh + h + 1, :]
        m_prev = m_ref[h:h + 1, 0:1]
        log_d = jnp.where(causal, b_col + a_row, NEG_INF)
        log_inter = b_col + m_prev
        m_t = jnp.maximum(log_inter, jnp.max(log_d, axis=-1, keepdims=True))
        d = jnp.exp(log_d - m_t)
        inter = jnp.exp(log_inter - m_t)

        xh = xc[:, hs].astype(BF16)
        qf = _dot(xh, wq_ref[h])
        kf = _dot(xh, wk_ref[h]) * (dh ** -0.5)
        qb = qf.astype(BF16)
        vb = vm_ref[:, hs].astype(BF16)
        s = _dot_nt(qb, kf.astype(BF16)) * d
        c_old = c_ref[h]
        n_old = n_ref[h:h + 1, :]
        num = _dot(s.astype(BF16), vb) + inter * _dot(qb, c_old.astype(BF16))
        den = jnp.sum(s, axis=-1, keepdims=True) + inter * jnp.sum(qf * n_old, axis=-1, keepdims=True)
        hh = num / jnp.maximum(jnp.abs(den), jnp.exp(-m_t))

        b_last = b_col[L - 1:L, :]
        log_w = b_last + a_col
        m_new = jnp.maximum(b_last + m_prev, jnp.max(log_w, axis=0, keepdims=True))
        kw = kf * jnp.exp(log_w - m_new)
        decay = jnp.exp(b_last + m_prev - m_new)
        c_ref[h] = decay * c_old + _dot_tn(kw.astype(BF16), vb)
        n_ref[h:h + 1, :] = decay * n_old + jnp.sum(kw, axis=0, keepdims=True)
        m_ref[h:h + 1, :] = jnp.broadcast_to(m_new, (1, LANES))

        hn = hh * lax.rsqrt(jnp.mean(hh * hh, axis=-1, keepdims=True) + EPS) * g_ref[:, hs]
        o_ref[:, hs] = jax.nn.sigmoid(om_ref[:, hs]) * hn


def _mlstm(xm, vm, om, gates, conv_w, conv_b, w_mq, w_mk, b_i, b_f, mh_g, B, S):
    L = MLSTM_CHUNK
    nc = S // L
    nh = N_HEADS_MLSTM
    ng = 2 * nh
    grow = gates[:, :ng].reshape(B, S, ng).transpose(0, 2, 1)
    bias = jnp.concatenate([b_i, b_f])
    bcol = jnp.pad(bias, (0, LANES - ng))[None, :]
    brow = bias[:, None]
    wq = w_mq.astype(BF16)
    wk = w_mk.astype(BF16)
    cb = conv_b[None, :]
    g = mh_g.reshape(1, D_MLSTM)
    rows = pl.BlockSpec((L, D_MLSTM), lambda b, c: (b * nc + c, 0))
    full = lambda a: pl.BlockSpec(a.shape, lambda b, c: (0,) * a.ndim)
    return pl.pallas_call(
        _mlstm_kernel,
        grid=(B, nc),
        in_specs=[rows, rows, rows, pl.BlockSpec((L, LANES), lambda b, c: (b * nc + c, 0)),
                  pl.BlockSpec((None, ng, L), lambda b, c: (b, 0, c)),
                  full(conv_w), full(cb), full(wq), full(wk), full(bcol), full(brow), full(g)],
        out_specs=rows,
        out_shape=jax.ShapeDtypeStruct((B * S, D_MLSTM), F32),
        scratch_shapes=[
            pltpu.VMEM((L + SUBLANES, D_MLSTM), F32), pltpu.VMEM((SUBLANES, D_MLSTM), F32),
            pltpu.VMEM((nh, HEAD_DIM_MLSTM, HEAD_DIM_MLSTM), F32),
            pltpu.VMEM((SUBLANES, LANES), F32), pltpu.VMEM((SUBLANES, LANES), F32),
        ],
        compiler_params=_vmem_limit(32),
        name="mlstm",
    )(xm, vm, om, gates, grow, conv_w, cb, wq, wk, bcol, brow, g)


def _out_route_kernel(attn_ref, ml_ref, x_ref, wo_ref, g2_ref, wrh_ref, wrl_ref, br_ref, earlier_ref,
                      x1_ref, h2_ref, rt_ref, gt_ref, base_ref, cnt_ref, carry_ref):
    tm = ROUTE_ROWS
    ne = N_EXPERTS

    @pl.when(pl.program_id(0) == 0)
    def _():
        carry_ref[...] = jnp.zeros_like(carry_ref)

    x1 = (x_ref[...] + _dot(attn_ref[...].astype(BF16), wo_ref[0:D_ATTN, :])
          + _dot(ml_ref[...].astype(BF16), wo_ref[D_ATTN:D_ATTN + D_MLSTM, :]))
    x1_ref[...] = x1
    h2 = x1 * lax.rsqrt(jnp.mean(x1 * x1, axis=-1, keepdims=True) + EPS) * g2_ref[...]
    h2_ref[...] = h2
    hb = h2.astype(BF16)
    hl = (h2 - hb.astype(F32)).astype(BF16)
    lt = (_dot_nt(wrh_ref[...], hb) + _dot_nt(wrh_ref[...], hl) + _dot_nt(wrl_ref[...], hb)) + br_ref[...]

    eidx = lax.broadcasted_iota(jnp.int32, (ne, tm), 0)
    vals, hots, idxs = [], [], []
    work = lt
    for _ in range(TOP_K):
        mx = jnp.max(work, axis=0, keepdims=True)
        idx = jnp.min(jnp.where(work == mx, eidx, ne), axis=0, keepdims=True)
        hot = eidx == idx
        vals.append(mx)
        hots.append(hot)
        idxs.append(idx)
        work = jnp.where(hot, NEG_INF, work)
    ex = [jnp.exp(v - vals[0]) for v in vals]
    tot = ex[0] + ex[1] + ex[2] + ex[3]

    onehot = jnp.zeros((ne, tm), F32)
    gt = jnp.zeros((ne, tm), F32)
    for kk in range(TOP_K):
        onehot = onehot + jnp.where(hots[kk], 1.0, 0.0)
        gt = gt + jnp.where(hots[kk], ex[kk] / tot, 0.0)
    carry = carry_ref[...]
    before = _dot(onehot.astype(BF16), earlier_ref[...]) + carry[:, 0:1]
    base_ref[...] = carry
    cnt = jnp.sum(onehot, axis=1, keepdims=True)
    carry = carry + jnp.ceil(cnt * (1.0 / SUBLANES)) * float(SUBLANES)
    carry_ref[...] = carry
    cnt_ref[...] = carry
    rt_ref[...] = jnp.where(onehot > 0.5, before, NOT_ROUTED).astype(jnp.int32)
    gt_ref[...] = gt


def _out_route(attn, ml, x2, w_out, g2, w_router, b_router):
    T = x2.shape[0]
    tm = ROUTE_ROWS
    ne = N_EXPERTS
    wo = w_out.astype(BF16)
    wrt = w_router.T
    wrh = wrt.astype(BF16)
    wrl = (wrt - wrh.astype(F32)).astype(BF16)
    br = b_router[:, None]
    g2r = g2[None, :]
    tok = jnp.arange(tm)
    earlier = (tok[:, None] < tok[None, :]).astype(BF16)
    row = lambda n: pl.BlockSpec((tm, n), lambda i: (i, 0))
    col = lambda n: pl.BlockSpec((n, tm), lambda i: (0, i))
    full = lambda a: pl.BlockSpec(a.shape, lambda i: (0,) * a.ndim)
    return pl.pallas_call(
        _out_route_kernel,
        grid=(T // tm,),
        in_specs=[row(D_ATTN), row(D_MLSTM), row(D_MODEL), full(wo), full(g2r), full(wrh), full(wrl), full(br),
                  full(earlier)],
        out_specs=[row(D_MODEL), row(D_MODEL), col(ne), col(ne),
                   pl.BlockSpec((ne, LANES), lambda i: (i, 0)), pl.BlockSpec((ne, LANES), lambda i: (0, 0))],
        out_shape=[jax.ShapeDtypeStruct((T, D_MODEL), F32), jax.ShapeDtypeStruct((T, D_MODEL), F32),
                   jax.ShapeDtypeStruct((ne, T), jnp.int32), jax.ShapeDtypeStruct((ne, T), F32),
                   jax.ShapeDtypeStruct((T // tm * ne, LANES), F32), jax.ShapeDtypeStruct((ne, LANES), F32)],
        scratch_shapes=[pltpu.VMEM((ne, LANES), F32)],
        compiler_params=_vmem_limit(32),
        name="out_route",
    )(attn, ml, x2, wo, g2r, wrh, wrl, br, earlier)


def _row_copy(src_ref, dst_ref, sem, src_row, dst_row):
    return pltpu.make_async_copy(src_ref.at[pl.ds(src_row, 1)], dst_ref.at[pl.ds(dst_row, 1)], sem)


def _dispatch_kernel(runstart_ref, runlen_ref, lshift_ref, padstart_ref, npad_ref,
                     h_ref, rt_ref, xs_ref, cbuf, xbuf, zero_ref, sem, xsem, zsem):
    td = DISPATCH_ROWS
    ne = N_EXPERTS
    rb = RUN_ROWS
    grp = RUN_GROUP_ROWS // rb
    tile = pl.program_id(0)

    def pieces(src_ref, src_row, dst_row, length, dsem, action):
        off = jnp.int32(0)
        p = rb
        while p >= SUBLANES:
            take = (length & p) != 0

            @pl.when(take)
            def _(off=off, p=p):
                cp = pltpu.make_async_copy(
                    src_ref.at[pl.ds(pl.multiple_of(src_row + off, SUBLANES), p)],
                    xs_ref.at[pl.ds(pl.multiple_of(dst_row + off, SUBLANES), p)], dsem)
                action(cp)

            off = off + jnp.where(take, p, 0)
            p //= 2

    def first_chunk(tile_idx, e, action):
        i = tile_idx * ne + e
        pieces(cbuf, e * rb, runstart_ref[i], jnp.minimum(runlen_ref[i], rb), sem, action)

    @pl.when(tile > 0)
    def _():
        for e in range(ne):
            first_chunk(tile - 1, e, lambda cp: cp.wait())

    hb = h_ref[...].astype(BF16)
    riota = lax.broadcasted_iota(jnp.int32, (rb, td), 0)

    def select(e_slice, shift):
        return jnp.where(riota == rt_ref[e_slice, :] - shift, 1.0, 0.0)

    for g in range(ne // grp):
        parts = [select(slice(e, e + 1), lshift_ref[tile * ne + e]) for e in range(g * grp, (g + 1) * grp)]
        pt = jnp.concatenate(parts, axis=0).astype(BF16)
        cbuf[g * RUN_GROUP_ROWS:(g + 1) * RUN_GROUP_ROWS, :] = _dot(pt, hb)
        for e in range(g * grp, (g + 1) * grp):
            first_chunk(tile, e, lambda cp: cp.start())

    def per_expert(e, carry):
        i = tile * ne + e

        def chunk(ch, c):
            pt = select(pl.ds(e, 1), lshift_ref[i] + ch * rb).astype(BF16)
            xbuf[...] = _dot(pt, hb)
            length = jnp.minimum(runlen_ref[i] - ch * rb, rb)
            pieces(xbuf, 0, runstart_ref[i] + ch * rb, length, xsem, lambda cp: cp.start())
            pieces(xbuf, 0, runstart_ref[i] + ch * rb, length, xsem, lambda cp: cp.wait())
            return c

        lax.fori_loop(1, (runlen_ref[i] + rb - 1) // rb, chunk, 0)
        return carry

    lax.fori_loop(0, ne, per_expert, 0)

    @pl.when(tile == pl.num_programs(0) - 1)
    def _():
        for e in range(ne):
            first_chunk(tile, e, lambda cp: cp.wait())

    @pl.when(tile == 0)
    def _():
        zero_ref[...] = jnp.zeros_like(zero_ref)

        def fill(wait):
            def per_run(e, carry):
                start = padstart_ref[e]
                head = (-start) & (SUBLANES - 1)

                def single(r, c):
                    cp = _row_copy(zero_ref, xs_ref, zsem, 0, start + r)
                    cp.wait() if wait else cp.start()
                    return c

                lax.fori_loop(0, head, single, 0)
                off = pl.multiple_of(start + head, SUBLANES)
                left = npad_ref[e] - head

                def whole(b, c):
                    dst = xs_ref.at[pl.ds(pl.multiple_of(off + b * td, SUBLANES), td)]
                    cp = pltpu.make_async_copy(zero_ref, dst, zsem)
                    cp.wait() if wait else cp.start()
                    return c

                n_whole = left // td
                lax.fori_loop(0, n_whole, whole, 0)
                off = pl.multiple_of(off + n_whole * td, SUBLANES)
                p = td // 2
                while p >= SUBLANES:
                    take = (left & p) != 0

                    @pl.when(take)
                    def _(off=off, p=p):
                        cp = pltpu.make_async_copy(zero_ref.at[pl.ds(0, p)], xs_ref.at[pl.ds(off, p)], zsem)
                        cp.wait() if wait else cp.start()

                    off = pl.multiple_of(off + jnp.where(take, p, 0), SUBLANES)
                    p //= 2
                return carry

            lax.fori_loop(0, N_EXPERTS + 1, per_run, 0)

        fill(wait=False)
        fill(wait=True)


def _dispatch(h2, rt, runstart, runlen, lshift, padstart, npad, n_slots):
    T = h2.shape[0]
    td = DISPATCH_ROWS
    ne = N_EXPERTS
    assert td == ROUTE_ROWS
    return pl.pallas_call(
        _dispatch_kernel,
        grid_spec=pltpu.PrefetchScalarGridSpec(
            num_scalar_prefetch=5,
            grid=(T // td,),
            in_specs=[pl.BlockSpec((td, D_MODEL), lambda i, *_: (i, 0)),
                      pl.BlockSpec((ne, td), lambda i, *_: (0, i))],
            out_specs=pl.BlockSpec(memory_space=pl.ANY),
            scratch_shapes=[pltpu.VMEM((ne * RUN_ROWS, D_MODEL), F32), pltpu.VMEM((RUN_ROWS, D_MODEL), F32),
                            pltpu.VMEM((td, D_MODEL), F32), pltpu.SemaphoreType.DMA(()),
                            pltpu.SemaphoreType.DMA(()), pltpu.SemaphoreType.DMA(())],
        ),
        out_shape=jax.ShapeDtypeStruct((n_slots, D_MODEL), F32),
        compiler_params=_vmem_limit(32),
        name="dispatch",
    )(runstart, runlen, lshift, padstart, npad, h2, rt)


def _expert_kernel(bexp_ref, first_ref, next_ref, nused_ref, xs_ref, wgu_hbm, bgu_ref, wd_hbm, bd_ref, y_ref,
                   wgu_st, wd_st, wgu_bf, wd_bf, wsem):
    tm = EXPERT_ROWS
    blk = pl.program_id(0)
    active = blk * tm < nused_ref[0]

    def weight_copies(e):
        return (pltpu.make_async_copy(wgu_hbm.at[e], wgu_st, wsem.at[0]),
                pltpu.make_async_copy(wd_hbm.at[e], wd_st, wsem.at[1]))

    @pl.when(blk == 0)
    def _():
        for cp in weight_copies(bexp_ref[0]):
            cp.start()

    @pl.when(jnp.logical_and(active, first_ref[blk] == 1))
    def _():
        for cp in weight_copies(bexp_ref[blk]):
            cp.wait()
        rc = CAST_ROWS

        def cast(i, carry):
            r = pl.ds(pl.multiple_of(i * rc, rc), rc)
            wgu_bf[r, :] = wgu_st[r, :].astype(BF16)
            wd_bf[r, :] = wd_st[r, :].astype(BF16)
            return carry

        lax.fori_loop(0, D_MODEL // rc, cast, 0)

        @pl.when(next_ref[blk] >= 0)
        def _():
            for cp in weight_copies(next_ref[blk]):
                cp.start()

    @pl.when(active)
    def _():
        xb = xs_ref[...].astype(BF16)
        acc = jnp.zeros((tm, D_MODEL), F32)
        for f in range(D_FF // FF_CHUNK):
            lo = f * FF_CHUNK
            g = _dot(xb, wgu_bf[:, lo:lo + FF_CHUNK]) + bgu_ref[:, lo:lo + FF_CHUNK]
            u = _dot(xb, wgu_bf[:, D_FF + lo:D_FF + lo + FF_CHUNK]) + bgu_ref[:, D_FF + lo:D_FF + lo + FF_CHUNK]
            g = jnp.minimum(g, SWIGLU_LIMIT)
            u = jnp.clip(u, -SWIGLU_LIMIT, SWIGLU_LIMIT)
            act = g * jax.nn.sigmoid(SWIGLU_ALPHA * g) * (u + 1.0)
            acc = acc + _dot(act.astype(BF16), wd_bf[lo:lo + FF_CHUNK, :])
        y_ref[...] = (acc + bd_ref[...]).astype(y_ref.dtype)

    @pl.when(jnp.logical_not(active))
    def _():
        y_ref[...] = jnp.zeros_like(y_ref)


def _experts(xs, block_expert, first_block, next_expert, n_used, w_gate_up, b_gate_up, w_down, b_down):
    n_blocks = block_expert.shape[0]
    tm = EXPERT_ROWS
    assert D_FF == D_MODEL and xs.shape[0] == (n_blocks - 1) * tm
    bgu = b_gate_up[:, None, :]
    bd = b_down[:, None, :]
    last = n_blocks - 2
    return pl.pallas_call(
        _expert_kernel,
        grid_spec=pltpu.PrefetchScalarGridSpec(
            num_scalar_prefetch=4,
            grid=(n_blocks,),
            in_specs=[
                pl.BlockSpec((tm, D_MODEL), lambda i, be, fb, nx, nu: (jnp.minimum(i, last), 0)),
                pl.BlockSpec(memory_space=pl.ANY),
                pl.BlockSpec((None, 1, 2 * D_FF), lambda i, be, fb, nx, nu: (be[i], 0, 0)),
                pl.BlockSpec(memory_space=pl.ANY),
                pl.BlockSpec((None, 1, D_MODEL), lambda i, be, fb, nx, nu: (be[i], 0, 0)),
            ],
            out_specs=pl.BlockSpec((tm, D_MODEL), lambda i, be, fb, nx, nu: (i, 0)),
            scratch_shapes=[pltpu.VMEM((D_MODEL, 2 * D_FF), F32), pltpu.VMEM((D_FF, D_MODEL), F32),
                            pltpu.VMEM((D_MODEL, 2 * D_FF), BF16), pltpu.VMEM((D_FF, D_MODEL), BF16),
                            pltpu.SemaphoreType.DMA((2,))],
        ),
        out_shape=jax.ShapeDtypeStruct((n_blocks * tm, D_MODEL), BF16),
        compiler_params=_vmem_limit(48),
        name="experts",
    )(block_expert, first_block, next_expert, n_used, xs, w_gate_up, bgu, w_down, bd)


def _combine_kernel(fetch_ref, shift_ref, nch_ref, x1_ref, rt_ref, gt_ref, y_ref, o_ref,
                    buf, xbuf, sem, xsem):
    tc = ROUTE_ROWS
    ne = N_EXPERTS
    rb = RUN_ROWS
    grp = RUN_GROUP_ROWS // rb
    tile = pl.program_id(0)
    base = tile * ne
    slot = tile % 2

    def run_copy(tile_base, buf_slot, e):
        start = pl.multiple_of(fetch_ref[tile_base + e], BF16_TILE_ROWS)
        return pltpu.make_async_copy(y_ref.at[pl.ds(start, rb)], buf.at[buf_slot, pl.ds(e * rb, rb)],
                                     sem.at[buf_slot, e // grp])

    @pl.when(tile == 0)
    def _():
        for e in range(ne):
            run_copy(base, slot, e).start()

    @pl.when(tile + 1 < pl.num_programs(0))
    def _():
        for e in range(ne):
            run_copy(base + ne, 1 - slot, e).start()

    riota = lax.broadcasted_iota(jnp.int32, (rb, tc), 0)

    def select(e_slice, shift):
        return jnp.where(riota == rt_ref[e_slice, :] - shift, gt_ref[e_slice, :], 0.0)

    acc = x1_ref[...]
    for g in range(ne // grp):
        parts = []
        for e in range(g * grp, (g + 1) * grp):
            run_copy(base, slot, e).wait()
            parts.append(select(slice(e, e + 1), shift_ref[base + e]))
        pt = jnp.concatenate(parts, axis=0).astype(BF16)
        acc = acc + _dot_tn(pt, buf[slot, g * RUN_GROUP_ROWS:(g + 1) * RUN_GROUP_ROWS, :])
    o_ref[...] = acc

    def per_expert(e, carry):
        def chunk(ch, c):
            start = pl.multiple_of(fetch_ref[base + e] + ch * rb, BF16_TILE_ROWS)
            cp = pltpu.make_async_copy(y_ref.at[pl.ds(start, rb)], xbuf, xsem)
            cp.start()
            cp.wait()
            pt = select(pl.ds(e, 1), shift_ref[base + e] + ch * rb).astype(BF16)
            o_ref[...] += _dot_tn(pt, xbuf[...])
            return c

        lax.fori_loop(1, nch_ref[base + e], chunk, 0)
        return carry

    lax.fori_loop(0, ne, per_expert, 0)


def _combine(fetch, shift, nch, x1, rt, gt, y):
    T = x1.shape[0]
    tc = ROUTE_ROWS
    ne = N_EXPERTS
    return pl.pallas_call(
        _combine_kernel,
        grid_spec=pltpu.PrefetchScalarGridSpec(
            num_scalar_prefetch=3,
            grid=(T // tc,),
            in_specs=[pl.BlockSpec((tc, D_MODEL), lambda i, f, s, n: (i, 0)),
                      pl.BlockSpec((ne, tc), lambda i, f, s, n: (0, i)),
                      pl.BlockSpec((ne, tc), lambda i, f, s, n: (0, i)),
                      pl.BlockSpec(memory_space=pl.ANY)],
            out_specs=pl.BlockSpec((tc, D_MODEL), lambda i, f, s, n: (i, 0)),
            scratch_shapes=[pltpu.VMEM((2, ne * RUN_ROWS, D_MODEL), BF16), pltpu.VMEM((RUN_ROWS, D_MODEL), BF16),
                            pltpu.SemaphoreType.DMA((2, ne * RUN_ROWS // RUN_GROUP_ROWS)),
                            pltpu.SemaphoreType.DMA(())],
        ),
        out_shape=jax.ShapeDtypeStruct((T, D_MODEL), F32),
        compiler_params=_vmem_limit(32),
        name="combine",
    )(fetch, shift, nch, x1, rt, gt, y)


def _routing_tables(base, counts, T):
    tm = EXPERT_ROWS
    ne = N_EXPERTS
    n_tiles = T // ROUTE_ROWS
    n_blocks = -(-(T * TOP_K + n_tiles * ne * (SUBLANES - 1)) // tm) + ne
    n_slots = n_blocks * tm
    sizes = counts[:, 0].astype(jnp.int32)
    psizes = (sizes + tm - 1) // tm * tm
    pends = jnp.cumsum(psizes)
    pstarts = pends - psizes
    n_used = pends[-1:]
    padstart = jnp.concatenate([pstarts + sizes, n_used])
    npad = jnp.concatenate([psizes - sizes, n_slots - n_used])
    nb = n_blocks + 1
    blk = jnp.arange(nb, dtype=jnp.int32)
    block_expert = jnp.minimum(jnp.sum(pends[None, :] <= (blk * tm)[:, None], axis=1), ne - 1).astype(jnp.int32)
    first_block = jnp.concatenate([jnp.ones((1,), jnp.int32),
                                   (block_expert[1:] != block_expert[:-1]).astype(jnp.int32)])
    starts_group = jnp.logical_and(first_block == 1, blk * tm < n_used)
    pos = jnp.where(starts_group, blk, nb)
    later = jnp.concatenate([lax.cummin(pos[::-1])[::-1][1:], jnp.full((1,), nb, jnp.int32)])
    next_expert = jnp.where(later < nb, block_expert[jnp.minimum(later, nb - 1)], -1).astype(jnp.int32)
    tile_base = base[:, 0].astype(jnp.int32).reshape(n_tiles, ne)
    run_len = jnp.concatenate([tile_base[1:], sizes[None, :]], axis=0) - tile_base
    run_start = pstarts[None, :] + tile_base
    fetch = run_start // BF16_TILE_ROWS * BF16_TILE_ROWS
    shift = fetch - pstarts[None, :]
    nch = jnp.where(run_len > 0, (run_start - fetch + run_len + RUN_ROWS - 1) // RUN_ROWS, 0)
    flat = lambda a: a.reshape(n_tiles * ne).astype(jnp.int32)
    return (flat(run_start), flat(run_len), flat(tile_base), padstart, npad, n_slots,
            block_expert, first_block, next_expert, n_used, flat(fetch), flat(shift), flat(nch))


def kernel(x, norm1_g, w_in, q_norm_g, k_norm_g, conv_w, conv_b, w_mq, w_mk, b_igate, b_fgate, mh_norm_g,
           w_out, norm2_g, w_router, b_router, w_gate_up, b_gate_up, w_down, b_down):
    B, S, D = x.shape
    T = B * S
    assert D == D_MODEL and S % ATTN_ROWS == 0 and norm1_g.shape[0] == 1
    x2 = x.reshape(T, D)
    q, k, v, xm, vm, om, gates = _in_proj(x2, norm1_g[0], w_in[0], q_norm_g[0], k_norm_g[0])
    attn = _attention(q, k, v, B, S)
    ml = _mlstm(xm, vm, om, gates, conv_w[0], conv_b[0], w_mq[0], w_mk[0], b_igate[0], b_fgate[0],
                mh_norm_g[0], B, S)
    x1, h2, rt, gt, base, counts = _out_route(attn, ml, x2, w_out[0], norm2_g[0], w_router[0], b_router[0])
    (run_start, run_len, tile_base, padstart, npad, n_slots, block_expert, first_block, next_expert, n_used,
     fetch, shift, nch) = _routing_tables(base, counts, T)
    xs = _dispatch(h2, rt, run_start, run_len, tile_base, padstart, npad, n_slots)
    y = _experts(xs, block_expert, first_block, next_expert, n_used, w_gate_up[0], b_gate_up[0], w_down[0],
                 b_down[0])
    out = _combine(fetch, shift, nch, x1, rt, gt, y)
    return out.reshape(B, S, D)
```

```python
import functools

import jax
import jax.numpy as jnp
from jax import lax
from jax.experimental import pallas as pl
from jax.experimental.pallas import tpu as pltpu

F32 = jnp.float32
BF16 = jnp.bfloat16
NEG_INF = float("-inf")

D_MODEL = 1024
D_ATTN = 512
HEAD_DIM_ATTN = 64
N_HEADS_ATTN = 8
D_MLSTM = 512
HEAD_DIM_MLSTM = 128
N_HEADS_MLSTM = 4
CONV_K = 4
N_EXPERTS = 32
TOP_K = 4
D_FF = 1024
SWIGLU_LIMIT = 7.0
SWIGLU_ALPHA = 1.702
EPS = 1e-6
DILATIONS = (1, 4, 16)
BRANCH_SPAN = 128

LANES = 128
SUBLANES = 8

IN_PROJ_ROWS = 256
ATTN_ROWS = 2048
MLSTM_CHUNK = 256
ROUTE_ROWS = 256
EXPERT_ROWS = 256
FF_CHUNK = 512
CAST_ROWS = 128
DISPATCH_ROWS = 256
HEADNORM_LANES = 256
BF16_TILE_ROWS = 16
RUN_ROWS = 64
RUN_GROUP_ROWS = 256
NOT_ROUTED = -1.0e9


def _vmem_limit(mib):
    return pltpu.CompilerParams(vmem_limit_bytes=mib * 1024 * 1024)


def _split3(a):
    p1 = a.astype(BF16)
    r1 = a - p1.astype(F32)
    p2 = r1.astype(BF16)
    r2 = r1 - p2.astype(F32)
    return p1, p2, r2.astype(BF16)


def _dot(a, b):
    return jnp.dot(a, b, preferred_element_type=F32)


def _dot_nt(a, b):
    return lax.dot_general(a, b, (((1,), (1,)), ((), ())), preferred_element_type=F32)


def _dot_tn(a, b):
    return lax.dot_general(a, b, (((0,), (0,)), ((), ())), preferred_element_type=F32)


def _in_proj_kernel(x_ref, g1_ref, wqkv_ref, wm_ref, wgc_ref, qg_ref, kg_ref, hsum_ref,
                    q_ref, k_ref, v_ref, xm_ref, vm_ref, om_ref, gates_ref):
    x = x_ref[...]
    h = x * lax.rsqrt(jnp.mean(x * x, axis=-1, keepdims=True) + EPS) * g1_ref[...]
    hb = h.astype(BF16)
    hl = (h - hb.astype(F32)).astype(BF16)
    gw = HEADNORM_LANES

    def head_norm(z, g_ref):
        parts = []
        for c in range(D_ATTN // gw):
            zc = z[:, c * gw:(c + 1) * gw]
            ms = _dot((zc * zc).astype(BF16), hsum_ref[...]) * (1.0 / HEAD_DIM_ATTN)
            parts.append(zc * lax.rsqrt(ms + EPS))
        return jnp.concatenate(parts, axis=-1) * g_ref[...]

    zq = _dot(hb, wqkv_ref[:, 0:D_ATTN])
    q_ref[...] = head_norm(zq, qg_ref) * (HEAD_DIM_ATTN ** -0.5)
    zk = _dot(hb, wqkv_ref[:, D_ATTN:2 * D_ATTN])
    k_ref[...] = head_norm(zk, kg_ref)
    v_ref[...] = _dot(hb, wqkv_ref[:, 2 * D_ATTN:3 * D_ATTN])
    xm_ref[...] = _dot(hb, wm_ref[:, 0:D_MLSTM])
    vm_ref[...] = _dot(hb, wm_ref[:, D_MLSTM:2 * D_MLSTM])
    om_ref[...] = _dot(hb, wm_ref[:, 2 * D_MLSTM:3 * D_MLSTM])
    ng = 2 * N_HEADS_MLSTM
    gsum = _dot(hb, wgc_ref[...]) + _dot(hl, wgc_ref[...])
    gsum = gsum + pltpu.roll(gsum, LANES - ng, axis=1)
    lane = lax.broadcasted_iota(jnp.int32, gsum.shape, 1)
    gates_ref[...] = jnp.where(lane < ng, gsum, 0.0)


def _in_proj(x2, g1, w_in, q_g, k_g):
    T = x2.shape[0]
    tm = IN_PROJ_ROWS
    n_qkv = 3 * D_ATTN
    n_m = 3 * D_MLSTM
    ng = 2 * N_HEADS_MLSTM
    wqkv = w_in[:, :n_qkv].astype(BF16)
    wm = w_in[:, n_qkv:n_qkv + n_m].astype(BF16)
    wg = w_in[:, n_qkv + n_m:]
    wgh = wg.astype(BF16)
    wgl = (wg - wgh.astype(F32)).astype(BF16)
    wgc = jnp.pad(jnp.concatenate([wgh, wgl], axis=1), ((0, 0), (0, LANES - 2 * ng)))
    lane = jnp.arange(HEADNORM_LANES)
    hsum = (lane[:, None] // HEAD_DIM_ATTN == lane[None, :] // HEAD_DIM_ATTN).astype(BF16)
    qg = jnp.tile(q_g, N_HEADS_ATTN)[None, :]
    kg = jnp.tile(k_g, N_HEADS_ATTN)[None, :]
    row = lambda n: pl.BlockSpec((tm, n), lambda i: (i, 0))
    full = lambda a: pl.BlockSpec(a.shape, lambda i: (0,) * a.ndim)
    outs = [jax.ShapeDtypeStruct((T, D_ATTN), F32)] * 3 + [jax.ShapeDtypeStruct((T, D_MLSTM), F32)] * 3
    outs.append(jax.ShapeDtypeStruct((T, LANES), F32))
    g1r = g1[None, :]
    return pl.pallas_call(
        _in_proj_kernel,
        grid=(T // tm,),
        in_specs=[row(D_MODEL), full(g1r), full(wqkv), full(wm), full(wgc), full(qg), full(kg), full(hsum)],
        out_specs=[row(D_ATTN)] * 3 + [row(D_MLSTM)] * 3 + [row(LANES)],
        out_shape=outs,
        compiler_params=_vmem_limit(48),
        name="in_proj",
    )(x2, g1r, wqkv, wm, wgc, qg, kg, hsum)


def _attn_tiles():
    tiles = []
    for br, d in enumerate(DILATIONS):
        group = d * BRANCH_SPAN
        for u in range(ATTN_ROWS // group):
            for c in range(d):
                tiles.append((br, d, u * group + c))
    return tiles


def _attn_kernel(slope_ref, q_ref, kp_ref, kc_ref, vp_ref, vc_ref, o_ref,
                 kwin, vwin, bias_scr, m_scr, l_scr, acc_scr):
    W = ATTN_ROWS
    n = BRANCH_SPAN
    step = pl.program_id(2)
    kwin[0:W, :] = kp_ref[...]
    kwin[W:2 * W, :] = kc_ref[...]
    vwin[0:W, :] = vp_ref[...]
    vwin[W:2 * W, :] = vc_ref[...]

    lane = lax.broadcasted_iota(jnp.int32, (n, LANES), 1)
    first_head = lane < HEAD_DIM_ATTN
    row = lax.broadcasted_iota(jnp.int32, (n, 2 * n), 0)
    col = lax.broadcasted_iota(jnp.int32, (n, 2 * n), 1)
    j = n + row - col
    valid = jnp.logical_and(j >= 0, j <= n)
    jf = j.astype(F32)
    for br, d in enumerate(DILATIONS):
        for hh in range(2):
            slope = slope_ref[:, hh * HEAD_DIM_ATTN:hh * HEAD_DIM_ATTN + 1]
            b = jnp.where(valid, -(slope * float(d)) * jf, NEG_INF)
            bias_scr[br, hh, 0] = b
            bias_scr[br, hh, 1] = jnp.where(col >= n, b, NEG_INF)

    for br, d, qs in _attn_tiles():
        lo = W + qs - n * d
        q = q_ref[pl.ds(qs, n, stride=d), :]
        kk = jnp.concatenate([kwin[pl.ds(lo, n, stride=d), :], kwin[pl.ds(W + qs, n, stride=d), :]],
                             axis=0).astype(BF16)
        vv = jnp.concatenate([vwin[pl.ds(lo, n, stride=d), :], vwin[pl.ds(W + qs, n, stride=d), :]],
                             axis=0).astype(BF16)
        if lo < W:
            sel = jnp.where(step == 0, 1, 0)
        else:
            sel = 0
        res = []
        for hh in range(2):
            qm = jnp.where(first_head if hh == 0 else jnp.logical_not(first_head), q, 0.0).astype(BF16)
            s = _dot_nt(qm, kk) + bias_scr[br, hh, sel]
            m = jnp.max(s, axis=-1, keepdims=True)
            p = jnp.exp(s - m)
            l = jnp.sum(p, axis=-1, keepdims=True)
            res.append((m, l, _dot(p.astype(BF16), vv)))
        rows = pl.ds(qs, n, stride=d)
        m_scr[br, rows, :] = jnp.where(first_head, res[0][0], res[1][0])
        l_scr[br, rows, :] = jnp.where(first_head, res[0][1], res[1][1])
        acc_scr[br, rows, :] = jnp.where(first_head, res[0][2], res[1][2])

    chunk = 256

    def combine(i, carry):
        r = pl.ds(pl.multiple_of(i * chunk, chunk), chunk)
        m0, m1, m2 = m_scr[0, r, :], m_scr[1, r, :], m_scr[2, r, :]
        mm = jnp.maximum(jnp.maximum(m0, m1), m2)
        w0, w1, w2 = jnp.exp(m0 - mm), jnp.exp(m1 - mm), jnp.exp(m2 - mm)
        num = w0 * acc_scr[0, r, :] + w1 * acc_scr[1, r, :] + w2 * acc_scr[2, r, :]
        den = w0 * l_scr[0, r, :] + w1 * l_scr[1, r, :] + w2 * l_scr[2, r, :]
        o_ref[r, :] = num / den
        return carry

    lax.fori_loop(0, W // chunk, combine, 0)


def _attention(q, k, v, B, S):
    W = ATTN_ROWS
    nb = S // W
    n_pairs = D_ATTN // LANES
    slopes = jnp.exp2(-8.0 * jnp.arange(1, N_HEADS_ATTN + 1, dtype=F32) / N_HEADS_ATTN)
    slope_l = jnp.repeat(slopes, HEAD_DIM_ATTN).reshape(n_pairs, 1, LANES)
    cur = pl.BlockSpec((W, LANES), lambda b, p, i: (b * nb + i, p))
    prev = pl.BlockSpec((W, LANES), lambda b, p, i: (b * nb + jnp.maximum(i - 1, 0), p))
    n = BRANCH_SPAN
    nbr = len(DILATIONS)
    return pl.pallas_call(
        _attn_kernel,
        grid=(B, n_pairs, nb),
        in_specs=[pl.BlockSpec((None, 1, LANES), lambda b, p, i: (p, 0, 0)), cur, prev, cur, prev, cur],
        out_specs=cur,
        out_shape=jax.ShapeDtypeStruct((B * S, D_ATTN), F32),
        scratch_shapes=[
            pltpu.VMEM((2 * W, LANES), F32), pltpu.VMEM((2 * W, LANES), F32),
            pltpu.VMEM((nbr, 2, 2, n, 2 * n), F32),
            pltpu.VMEM((nbr, W, LANES), F32), pltpu.VMEM((nbr, W, LANES), F32),
            pltpu.VMEM((nbr, W, LANES), F32),
        ],
        compiler_params=_vmem_limit(48),
        name="attention",
    )(slope_l, q, k, k, v, v)


def _mlstm_kernel(xm_ref, vm_ref, om_ref, gcol_ref, grow_ref, cw_ref, cb_ref, wq_ref, wk_ref,
                  bcol_ref, brow_ref, g_ref, o_ref, xs_ref, xtail_ref, c_ref, n_ref, m_ref):
    L = MLSTM_CHUNK
    dh = HEAD_DIM_MLSTM
    nh = N_HEADS_MLSTM

    @pl.when(pl.program_id(1) == 0)
    def _():
        xtail_ref[...] = jnp.zeros_like(xtail_ref)
        c_ref[...] = jnp.zeros_like(c_ref)
        n_ref[...] = jnp.zeros_like(n_ref)
        m_ref[...] = jnp.zeros_like(m_ref)

    x = xm_ref[...]
    xs_ref[0:SUBLANES, :] = xtail_ref[...]
    xs_ref[SUBLANES:SUBLANES + L, :] = x
    xtail_ref[...] = x[L - SUBLANES:L, :]
    xc = cb_ref[...] + x * cw_ref[CONV_K - 1:CONV_K, :]
    for back in range(1, CONV_K):
        xc = xc + xs_ref[SUBLANES - back:SUBLANES - back + L, :] * cw_ref[CONV_K - 1 - back:CONV_K - back, :]
    xc = xc * jax.nn.sigmoid(xc)

    gc = gcol_ref[...] + bcol_ref[...]
    gr = grow_ref[...] + brow_ref[...]
    ri = lax.broadcasted_iota(jnp.int32, (L, L), 0)
    ci = lax.broadcasted_iota(jnp.int32, (L, L), 1)
    causal = ri >= ci
    tril = jnp.where(causal, 1.0, 0.0).astype(BF16)
    triu = jnp.where(ri <= ci, 1.0, 0.0).astype(BF16)
    c1, c2, c3 = _split3(jax.nn.log_sigmoid(gc))
    bc = _dot(tril, c1) + _dot(tril, c2) + _dot(tril, c3)
    r1, r2, r3 = _split3(jax.nn.log_sigmoid(gr))
    brw = _dot(r1, triu) + _dot(r2, triu) + _dot(r3, triu)

    for h in range(nh):
        hs = slice(h * dh, (h + 1) * dh)
        b_col = bc[:, nh + h:nh + h + 1]
        a_col = gc[:, h:h + 1] - b_col
        a_row = gr[h:h + 1, :] - brw[nh + h:nh + h + 1, :]
        m_prev = m_ref[h:h + 1, 0:1]
        log_d = jnp.where(causal, b_col + a_row, NEG_INF)
        log_inter = b_col + m_prev
        m_t = jnp.maximum(log_inter, jnp.max(log_d, axis=-1, keepdims=True))
        d = jnp.exp(log_d - m_t)
        inter = jnp.exp(log_inter - m_t)

        xh = xc[:, hs].astype(BF16)
        qf = _dot(xh, wq_ref[h])
        kf = _dot(xh, wk_ref[h]) * (dh ** -0.5)
        qb = qf.astype(BF16)
        vb = vm_ref[:, hs].astype(BF16)
        s = _dot_nt(qb, kf.astype(BF16)) * d
        c_old = c_ref[h]
        n_old = n_ref[h:h + 1, :]
        num = _dot(s.astype(BF16), vb) + inter * _dot(qb, c_old.astype(BF16))
        den = jnp.sum(s, axis=-1, keepdims=True) + inter * jnp.sum(qf * n_old, axis=-1, keepdims=True)
        hh = num / jnp.maximum(jnp.abs(den), jnp.exp(-m_t))

        b_last = b_col[L - 1:L, :]
        log_w = b_last + a_col
        m_new = jnp.maximum(b_last + m_prev, jnp.max(log_w, axis=0, keepdims=True))
        kw = kf * jnp.exp(log_w - m_new)
        decay = jnp.exp(b_last + m_prev - m_new)
        c_ref[h] = decay * c_old + _dot_tn(kw.astype(BF16), vb)
        n_ref[h:h + 1, :] = decay * n_old + jnp.sum(kw, axis=0, keepdims=True)
        m_ref[h:h + 1, :] = jnp.broadcast_to(m_new, (1, LANES))

        hn = hh * lax.rsqrt(jnp.mean(hh * hh, axis=-1, keepdims=True) + EPS) * g_ref[:, hs]
        o_ref[:, hs] = jax.nn.sigmoid(om_ref[:, hs]) * hn


def _mlstm(xm, vm, om, gates, conv_w, conv_b, w_mq, w_mk, b_i, b_f, mh_g, B, S):
    L = MLSTM_CHUNK
    nc = S // L
    nh = N_HEADS_MLSTM
    ng = 2 * nh
    grow = gates[:, :ng].reshape(B, S, ng).transpose(0, 2, 1)
    bias = jnp.concatenate([b_i, b_f])
    bcol = jnp.pad(bias, (0, LANES - ng))[None, :]
    brow = bias[:, None]
    wq = w_mq.astype(BF16)
    wk = w_mk.astype(BF16)
    cb = conv_b[None, :]
    g = mh_g.reshape(1, D_MLSTM)
    rows = pl.BlockSpec((L, D_MLSTM), lambda b, c: (b * nc + c, 0))
    full = lambda a: pl.BlockSpec(a.shape, lambda b, c: (0,) * a.ndim)
    return pl.pallas_call(
        _mlstm_kernel,
        grid=(B, nc),
        in_specs=[rows, rows, rows, pl.BlockSpec((L, LANES), lambda b, c: (b * nc + c, 0)),
                  pl.BlockSpec((None, ng, L), lambda b, c: (b, 0, c)),
                  full(conv_w), full(cb), full(wq), full(wk), full(bcol), full(brow), full(g)],
        out_specs=rows,
        out_shape=jax.ShapeDtypeStruct((B * S, D_MLSTM), F32),
        scratch_shapes=[
            pltpu.VMEM((L + SUBLANES, D_MLSTM), F32), pltpu.VMEM((SUBLANES, D_MLSTM), F32),
            pltpu.VMEM((nh, HEAD_DIM_MLSTM, HEAD_DIM_MLSTM), F32),
            pltpu.VMEM((SUBLANES, LANES), F32), pltpu.VMEM((SUBLANES, LANES), F32),
        ],
        compiler_params=_vmem_limit(32),
        name="mlstm",
    )(xm, vm, om, gates, grow, conv_w, cb, wq, wk, bcol, brow, g)


def _out_route_kernel(attn_ref, ml_ref, x_ref, wo_ref, g2_ref, wrh_ref, wrl_ref, br_ref, earlier_ref,
                      x1_ref, h2_ref, rt_ref, gt_ref, base_ref, cnt_ref, carry_ref):
    tm = ROUTE_ROWS
    ne = N_EXPERTS

    @pl.when(pl.program_id(0) == 0)
    def _():
        carry_ref[...] = jnp.zeros_like(carry_ref)

    x1 = (x_ref[...] + _dot(attn_ref[...].astype(BF16), wo_ref[0:D_ATTN, :])
          + _dot(ml_ref[...].astype(BF16), wo_ref[D_ATTN:D_ATTN + D_MLSTM, :]))
    x1_ref[...] = x1
    h2 = x1 * lax.rsqrt(jnp.mean(x1 * x1, axis=-1, keepdims=True) + EPS) * g2_ref[...]
    h2_ref[...] = h2
    hb = h2.astype(BF16)
    hl = (h2 - hb.astype(F32)).astype(BF16)
    lt = (_dot_nt(wrh_ref[...], hb) + _dot_nt(wrh_ref[...], hl) + _dot_nt(wrl_ref[...], hb)) + br_ref[...]

    eidx = lax.broadcasted_iota(jnp.int32, (ne, tm), 0)
    vals, hots, idxs = [], [], []
    work = lt
    for _ in range(TOP_K):
        mx = jnp.max(work, axis=0, keepdims=True)
        idx = jnp.min(jnp.where(work == mx, eidx, ne), axis=0, keepdims=True)
        hot = eidx == idx
        vals.append(mx)
        hots.append(hot)
        idxs.append(idx)
        work = jnp.where(hot, NEG_INF, work)
    ex = [jnp.exp(v - vals[0]) for v in vals]
    tot = ex[0] + ex[1] + ex[2] + ex[3]

    onehot = jnp.zeros((ne, tm), F32)
    gt = jnp.zeros((ne, tm), F32)
    for kk in range(TOP_K):
        onehot = onehot + jnp.where(hots[kk], 1.0, 0.0)
        gt = gt + jnp.where(hots[kk], ex[kk] / tot, 0.0)
    carry = carry_ref[...]
    before = _dot(onehot.astype(BF16), earlier_ref[...]) + carry[:, 0:1]
    base_ref[...] = carry
    carry = carry + jnp.sum(onehot, axis=1, keepdims=True)
    carry_ref[...] = carry
    cnt_ref[...] = carry
    rt_ref[...] = jnp.where(onehot > 0.5, before, NOT_ROUTED).astype(jnp.int32)
    gt_ref[...] = gt


def _out_route(attn, ml, x2, w_out, g2, w_router, b_router):
    T = x2.shape[0]
    tm = ROUTE_ROWS
    ne = N_EXPERTS
    wo = w_out.astype(BF16)
    wrt = w_router.T
    wrh = wrt.astype(BF16)
    wrl = (wrt - wrh.astype(F32)).astype(BF16)
    br = b_router[:, None]
    g2r = g2[None, :]
    tok = jnp.arange(tm)
    earlier = (tok[:, None] < tok[None, :]).astype(BF16)
    row = lambda n: pl.BlockSpec((tm, n), lambda i: (i, 0))
    col = lambda n: pl.BlockSpec((n, tm), lambda i: (0, i))
    full = lambda a: pl.BlockSpec(a.shape, lambda i: (0,) * a.ndim)
    return pl.pallas_call(
        _out_route_kernel,
        grid=(T // tm,),
        in_specs=[row(D_ATTN), row(D_MLSTM), row(D_MODEL), full(wo), full(g2r), full(wrh), full(wrl), full(br),
                  full(earlier)],
        out_specs=[row(D_MODEL), row(D_MODEL), col(ne), col(ne),
                   pl.BlockSpec((ne, LANES), lambda i: (i, 0)), pl.BlockSpec((ne, LANES), lambda i: (0, 0))],
        out_shape=[jax.ShapeDtypeStruct((T, D_MODEL), F32), jax.ShapeDtypeStruct((T, D_MODEL), F32),
                   jax.ShapeDtypeStruct((ne, T), jnp.int32), jax.ShapeDtypeStruct((ne, T), F32),
                   jax.ShapeDtypeStruct((T // tm * ne, LANES), F32), jax.ShapeDtypeStruct((ne, LANES), F32)],
        scratch_shapes=[pltpu.VMEM((ne, LANES), F32)],
        compiler_params=_vmem_limit(32),
        name="out_route",
    )(attn, ml, x2, wo, g2r, wrh, wrl, br, earlier)


def _row_copy(src_ref, dst_ref, sem, src_row, dst_row):
    return pltpu.make_async_copy(src_ref.at[pl.ds(src_row, 1)], dst_ref.at[pl.ds(dst_row, 1)], sem)


def _dispatch_kernel(runstart_ref, runlen_ref, lshift_ref, padstart_ref, npad_ref,
                     h_ref, rt_ref, xs_ref, cbuf, xbuf, zero_ref, stage_ref, sem, xsem, zsem):
    td = DISPATCH_ROWS
    ne = N_EXPERTS
    rb = RUN_ROWS
    grp = RUN_GROUP_ROWS // rb
    tile = pl.program_id(0)

    def pieces(src_ref, src_row, dst_row, length, dsem, action):
        off = jnp.int32(0)
        p = rb
        while p >= SUBLANES:
            take = (length & p) != 0

            @pl.when(take)
            def _(off=off, p=p):
                cp = pltpu.make_async_copy(
                    src_ref.at[pl.ds(pl.multiple_of(src_row + off, SUBLANES), p)],
                    xs_ref.at[pl.ds(pl.multiple_of(dst_row + off, SUBLANES), p)], dsem)
                action(cp)

            off = off + jnp.where(take, p, 0)
            p //= 2

    def window(i):
        off = runstart_ref[i] & (SUBLANES - 1)
        total = off + runlen_ref[i]
        rows = jnp.where(runlen_ref[i] > 0, (total + SUBLANES - 1) // SUBLANES * SUBLANES, 0)
        return off, total, rows

    def first_chunk(tile_idx, e, action):
        i = tile_idx * ne + e
        off, _, rows = window(i)
        pieces(cbuf, e * rb, runstart_ref[i] - off, jnp.minimum(rows, rb), sem, action)

    @pl.when(tile > 0)
    def _():
        for e in range(ne):
            first_chunk(tile - 1, e, lambda cp: cp.wait())

    @pl.when(tile == 0)
    def _():
        stage_ref[...] = jnp.zeros_like(stage_ref)

    hb = h_ref[...].astype(BF16)
    riota = lax.broadcasted_iota(jnp.int32, (rb, td), 0)
    row8 = lax.broadcasted_iota(jnp.int32, (SUBLANES, D_MODEL), 0)

    def select(e_slice, shift):
        return jnp.where(riota == rt_ref[e_slice, :] - shift, 1.0, 0.0)

    for g in range(ne // grp):
        experts = range(g * grp, (g + 1) * grp)
        parts = [select(slice(e, e + 1), lshift_ref[tile * ne + e] - window(tile * ne + e)[0]) for e in experts]
        pt = jnp.concatenate(parts, axis=0).astype(BF16)
        cbuf[g * RUN_GROUP_ROWS:(g + 1) * RUN_GROUP_ROWS, :] = _dot(pt, hb)
        for e in experts:
            i = tile * ne + e
            off, total, _ = window(i)
            head = pl.ds(e * rb, SUBLANES)
            cbuf[head, :] = jnp.where(row8 < off, stage_ref[e], cbuf[head, :])

            @pl.when(jnp.logical_and(runlen_ref[i] > 0, total < rb))
            def _(e=e, total=total):
                last = pl.multiple_of(e * rb + total // SUBLANES * SUBLANES, SUBLANES)
                stage_ref[e] = cbuf[pl.ds(last, SUBLANES), :]

            first_chunk(tile, e, lambda cp: cp.start())

    def per_expert(e, carry):
        i = tile * ne + e
        off, total, rows = window(i)

        def chunk(ch, c):
            pt = select(pl.ds(e, 1), lshift_ref[i] - off + ch * rb).astype(BF16)
            xbuf[...] = _dot(pt, hb)
            length = jnp.minimum(rows - ch * rb, rb)
            dst = runstart_ref[i] - off + ch * rb
            pieces(xbuf, 0, dst, length, xsem, lambda cp: cp.start())

            @pl.when(jnp.logical_and(ch == total // rb, total % rb != 0))
            def _():
                last = pl.multiple_of(total % rb // SUBLANES * SUBLANES, SUBLANES)
                stage_ref[e] = xbuf[pl.ds(last, SUBLANES), :]

            pieces(xbuf, 0, dst, length, xsem, lambda cp: cp.wait())
            return c

        lax.fori_loop(1, (rows + rb - 1) // rb, chunk, 0)
        return carry

    lax.fori_loop(0, ne, per_expert, 0)

    @pl.when(tile == pl.num_programs(0) - 1)
    def _():
        for e in range(ne):
            first_chunk(tile, e, lambda cp: cp.wait())

    @pl.when(tile == pl.num_programs(0) - 1)
    def _():
        zero_ref[...] = jnp.zeros_like(zero_ref)

        def fill(wait):
            def per_run(e, carry):
                start = padstart_ref[e]
                head = (-start) & (SUBLANES - 1)

                def single(r, c):
                    cp = _row_copy(zero_ref, xs_ref, zsem, 0, start + r)
                    cp.wait() if wait else cp.start()
                    return c

                lax.fori_loop(0, head, single, 0)
                off = pl.multiple_of(start + head, SUBLANES)
                left = npad_ref[e] - head

                def whole(b, c):
                    dst = xs_ref.at[pl.ds(pl.multiple_of(off + b * td, SUBLANES), td)]
                    cp = pltpu.make_async_copy(zero_ref, dst, zsem)
                    cp.wait() if wait else cp.start()
                    return c

                n_whole = left // td
                lax.fori_loop(0, n_whole, whole, 0)
                off = pl.multiple_of(off + n_whole * td, SUBLANES)
                p = td // 2
                while p >= SUBLANES:
                    take = (left & p) != 0

                    @pl.when(take)
                    def _(off=off, p=p):
                        cp = pltpu.make_async_copy(zero_ref.at[pl.ds(0, p)], xs_ref.at[pl.ds(off, p)], zsem)
                        cp.wait() if wait else cp.start()

                    off = pl.multiple_of(off + jnp.where(take, p, 0), SUBLANES)
                    p //= 2
                return carry

            lax.fori_loop(0, N_EXPERTS + 1, per_run, 0)

        fill(wait=False)
        fill(wait=True)


def _dispatch(h2, rt, runstart, runlen, lshift, padstart, npad, n_slots):
    T = h2.shape[0]
    td = DISPATCH_ROWS
    ne = N_EXPERTS
    assert td == ROUTE_ROWS
    return pl.pallas_call(
        _dispatch_kernel,
        grid_spec=pltpu.PrefetchScalarGridSpec(
            num_scalar_prefetch=5,
            grid=(T // td,),
            in_specs=[pl.BlockSpec((td, D_MODEL), lambda i, *_: (i, 0)),
                      pl.BlockSpec((ne, td), lambda i, *_: (0, i))],
            out_specs=pl.BlockSpec(memory_space=pl.ANY),
            scratch_shapes=[pltpu.VMEM((ne * RUN_ROWS, D_MODEL), F32), pltpu.VMEM((RUN_ROWS, D_MODEL), F32),
                            pltpu.VMEM((td, D_MODEL), F32), pltpu.VMEM((ne, SUBLANES, D_MODEL), F32),
                            pltpu.SemaphoreType.DMA(()), pltpu.SemaphoreType.DMA(()),
                            pltpu.SemaphoreType.DMA(())],
        ),
        out_shape=jax.ShapeDtypeStruct((n_slots, D_MODEL), F32),
        compiler_params=_vmem_limit(32),
        name="dispatch",
    )(runstart, runlen, lshift, padstart, npad, h2, rt)


def _expert_kernel(bexp_ref, first_ref, next_ref, nused_ref, xs_ref, wgu_hbm, bgu_ref, wd_hbm, bd_ref, y_ref,
                   wgu_st, wd_st, wgu_bf, wd_bf, wsem):
    tm = EXPERT_ROWS
    blk = pl.program_id(0)
    active = blk * tm < nused_ref[0]

    def weight_copies(e):
        return (pltpu.make_async_copy(wgu_hbm.at[e], wgu_st, wsem.at[0]),
                pltpu.make_async_copy(wd_hbm.at[e], wd_st, wsem.at[1]))

    @pl.when(blk == 0)
    def _():
        for cp in weight_copies(bexp_ref[0]):
            cp.start()

    @pl.when(jnp.logical_and(active, first_ref[blk] == 1))
    def _():
        for cp in weight_copies(bexp_ref[blk]):
            cp.wait()
        rc = CAST_ROWS

        def cast(i, carry):
            r = pl.ds(pl.multiple_of(i * rc, rc), rc)
            wgu_bf[r, :] = wgu_st[r, :].astype(BF16)
            wd_bf[r, :] = wd_st[r, :].astype(BF16)
            return carry

        lax.fori_loop(0, D_MODEL // rc, cast, 0)

        @pl.when(next_ref[blk] >= 0)
        def _():
            for cp in weight_copies(next_ref[blk]):
                cp.start()

    @pl.when(active)
    def _():
        xb = xs_ref[...].astype(BF16)
        acc = jnp.zeros((tm, D_MODEL), F32)
        for f in range(D_FF // FF_CHUNK):
            lo = f * FF_CHUNK
            g = _dot(xb, wgu_bf[:, lo:lo + FF_CHUNK]) + bgu_ref[:, lo:lo + FF_CHUNK]
            u = _dot(xb, wgu_bf[:, D_FF + lo:D_FF + lo + FF_CHUNK]) + bgu_ref[:, D_FF + lo:D_FF + lo + FF_CHUNK]
            g = jnp.minimum(g, SWIGLU_LIMIT)
            u = jnp.clip(u, -SWIGLU_LIMIT, SWIGLU_LIMIT)
            act = g * jax.nn.sigmoid(SWIGLU_ALPHA * g) * (u + 1.0)
            acc = acc + _dot(act.astype(BF16), wd_bf[lo:lo + FF_CHUNK, :])
        y_ref[...] = (acc + bd_ref[...]).astype(y_ref.dtype)

    @pl.when(jnp.logical_not(active))
    def _():
        y_ref[...] = jnp.zeros_like(y_ref)


def _experts(xs, block_expert, first_block, next_expert, n_used, w_gate_up, b_gate_up, w_down, b_down):
    n_blocks = block_expert.shape[0]
    tm = EXPERT_ROWS
    assert D_FF == D_MODEL and xs.shape[0] == (n_blocks - 1) * tm
    bgu = b_gate_up[:, None, :]
    bd = b_down[:, None, :]
    last = n_blocks - 2
    return pl.pallas_call(
        _expert_kernel,
        grid_spec=pltpu.PrefetchScalarGridSpec(
            num_scalar_prefetch=4,
            grid=(n_blocks,),
            in_specs=[
                pl.BlockSpec((tm, D_MODEL), lambda i, be, fb, nx, nu: (jnp.minimum(i, last), 0)),
                pl.BlockSpec(memory_space=pl.ANY),
                pl.BlockSpec((None, 1, 2 * D_FF), lambda i, be, fb, nx, nu: (be[i], 0, 0)),
                pl.BlockSpec(memory_space=pl.ANY),
                pl.BlockSpec((None, 1, D_MODEL), lambda i, be, fb, nx, nu: (be[i], 0, 0)),
            ],
            out_specs=pl.BlockSpec((tm, D_MODEL), lambda i, be, fb, nx, nu: (i, 0)),
            scratch_shapes=[pltpu.VMEM((D_MODEL, 2 * D_FF), F32), pltpu.VMEM((D_FF, D_MODEL), F32),
                            pltpu.VMEM((D_MODEL, 2 * D_FF), BF16), pltpu.VMEM((D_FF, D_MODEL), BF16),
                            pltpu.SemaphoreType.DMA((2,))],
        ),
        out_shape=jax.ShapeDtypeStruct((n_blocks * tm, D_MODEL), BF16),
        compiler_params=_vmem_limit(48),
        name="experts",
    )(block_expert, first_block, next_expert, n_used, xs, w_gate_up, bgu, w_down, bd)


def _combine_kernel(fetch_ref, shift_ref, nch_ref, x1_ref, rt_ref, gt_ref, y_ref, o_ref,
                    buf, xbuf, sem, xsem):
    tc = ROUTE_ROWS
    ne = N_EXPERTS
    rb = RUN_ROWS
    grp = RUN_GROUP_ROWS // rb
    tile = pl.program_id(0)
    base = tile * ne
    slot = tile % 2

    def run_copy(tile_base, buf_slot, e):
        start = pl.multiple_of(fetch_ref[tile_base + e], BF16_TILE_ROWS)
        return pltpu.make_async_copy(y_ref.at[pl.ds(start, rb)], buf.at[buf_slot, pl.ds(e * rb, rb)],
                                     sem.at[buf_slot, e // grp])

    @pl.when(tile == 0)
    def _():
        for e in range(ne):
            run_copy(base, slot, e).start()

    @pl.when(tile + 1 < pl.num_programs(0))
    def _():
        for e in range(ne):
            run_copy(base + ne, 1 - slot, e).start()

    riota = lax.broadcasted_iota(jnp.int32, (rb, tc), 0)

    def select(e_slice, shift):
        return jnp.where(riota == rt_ref[e_slice, :] - shift, gt_ref[e_slice, :], 0.0)

    acc = x1_ref[...]
    for g in range(ne // grp):
        parts = []
        for e in range(g * grp, (g + 1) * grp):
            run_copy(base, slot, e).wait()
            parts.append(select(slice(e, e + 1), shift_ref[base + e]))
        pt = jnp.concatenate(parts, axis=0).astype(BF16)
        acc = acc + _dot_tn(pt, buf[slot, g * RUN_GROUP_ROWS:(g + 1) * RUN_GROUP_ROWS, :])
    o_ref[...] = acc

    def per_expert(e, carry):
        def chunk(ch, c):
            start = pl.multiple_of(fetch_ref[base + e] + ch * rb, BF16_TILE_ROWS)
            cp = pltpu.make_async_copy(y_ref.at[pl.ds(start, rb)], xbuf, xsem)
            cp.start()
            cp.wait()
            pt = select(pl.ds(e, 1), shift_ref[base + e] + ch * rb).astype(BF16)
            o_ref[...] += _dot_tn(pt, xbuf[...])
            return c

        lax.fori_loop(1, nch_ref[base + e], chunk, 0)
        return carry

    lax.fori_loop(0, ne, per_expert, 0)


def _combine(fetch, shift, nch, x1, rt, gt, y):
    T = x1.shape[0]
    tc = ROUTE_ROWS
    ne = N_EXPERTS
    return pl.pallas_call(
        _combine_kernel,
        grid_spec=pltpu.PrefetchScalarGridSpec(
            num_scalar_prefetch=3,
            grid=(T // tc,),
            in_specs=[pl.BlockSpec((tc, D_MODEL), lambda i, f, s, n: (i, 0)),
                      pl.BlockSpec((ne, tc), lambda i, f, s, n: (0, i)),
                      pl.BlockSpec((ne, tc), lambda i, f, s, n: (0, i)),
                      pl.BlockSpec(memory_space=pl.ANY)],
            out_specs=pl.BlockSpec((tc, D_MODEL), lambda i, f, s, n: (i, 0)),
            scratch_shapes=[pltpu.VMEM((2, ne * RUN_ROWS, D_MODEL), BF16), pltpu.VMEM((RUN_ROWS, D_MODEL), BF16),
                            pltpu.SemaphoreType.DMA((2, ne * RUN_ROWS // RUN_GROUP_ROWS)),
                            pltpu.SemaphoreType.DMA(())],
        ),
        out_shape=jax.ShapeDtypeStruct((T, D_MODEL), F32),
        compiler_params=_vmem_limit(32),
        name="combine",
    )(fetch, shift, nch, x1, rt, gt, y)


def _routing_tables(base, counts, T):
    tm = EXPERT_ROWS
    ne = N_EXPERTS
    n_tiles = T // ROUTE_ROWS
    n_blocks = T * TOP_K // tm + ne
    n_slots = n_blocks * tm
    sizes = counts[:, 0].astype(jnp.int32)
    psizes = (sizes + tm - 1) // tm * tm
    pends = jnp.cumsum(psizes)
    pstarts = pends - psizes
    n_used = pends[-1:]
    padstart = jnp.concatenate([pstarts + sizes, n_used])
    npad = jnp.concatenate([psizes - sizes, n_slots - n_used])
    nb = n_blocks + 1
    blk = jnp.arange(nb, dtype=jnp.int32)
    block_expert = jnp.minimum(jnp.sum(pends[None, :] <= (blk * tm)[:, None], axis=1), ne - 1).astype(jnp.int32)
    first_block = jnp.concatenate([jnp.ones((1,), jnp.int32),
                                   (block_expert[1:] != block_expert[:-1]).astype(jnp.int32)])
    starts_group = jnp.logical_and(first_block == 1, blk * tm < n_used)
    pos = jnp.where(starts_group, blk, nb)
    later = jnp.concatenate([lax.cummin(pos[::-1])[::-1][1:], jnp.full((1,), nb, jnp.int32)])
    next_expert = jnp.where(later < nb, block_expert[jnp.minimum(later, nb - 1)], -1).astype(jnp.int32)
    tile_base = base[:, 0].astype(jnp.int32).reshape(n_tiles, ne)
    run_len = jnp.concatenate([tile_base[1:], sizes[None, :]], axis=0) - tile_base
    run_start = pstarts[None, :] + tile_base
    fetch = run_start // BF16_TILE_ROWS * BF16_TILE_ROWS
    shift = fetch - pstarts[None, :]
    nch = jnp.where(run_len > 0, (run_start - fetch + run_len + RUN_ROWS - 1) // RUN_ROWS, 0)
    flat = lambda a: a.reshape(n_tiles * ne).astype(jnp.int32)
    return (flat(run_start), flat(run_len), flat(tile_base), padstart, npad, n_slots,
            block_expert, first_block, next_expert, n_used, flat(fetch), flat(shift), flat(nch))


def kernel(x, norm1_g, w_in, q_norm_g, k_norm_g, conv_w, conv_b, w_mq, w_mk, b_igate, b_fgate, mh_norm_g,
           w_out, norm2_g, w_router, b_router, w_gate_up, b_gate_up, w_down, b_down):
    B, S, D = x.shape
    T = B * S
    assert D == D_MODEL and S % ATTN_ROWS == 0 and norm1_g.shape[0] == 1
    x2 = x.reshape(T, D)
    q, k, v, xm, vm, om, gates = _in_proj(x2, norm1_g[0], w_in[0], q_norm_g[0], k_norm_g[0])
    attn = _attention(q, k, v, B, S)
    ml = _mlstm(xm, vm, om, gates, conv_w[0], conv_b[0], w_mq[0], w_mk[0], b_igate[0], b_fgate[0],
                mh_norm_g[0], B, S)
    x1, h2, rt, gt, base, counts = _out_route(attn, ml, x2, w_out[0], norm2_g[0], w_router[0], b_router[0])
    (run_start, run_len, tile_base, padstart, npad, n_slots, block_expert, first_block, next_expert, n_used,
     fetch, shift, nch) = _routing_tables(base, counts, T)
    xs = _dispatch(h2, rt, run_start, run_len, tile_base, padstart, npad, n_slots)
    y = _experts(xs, block_expert, first_block, next_expert, n_used, w_gate_up[0], b_gate_up[0], w_down[0],
                 b_down[0])
    out = _combine(fetch, shift, nch, x1, rt, gt, y)
    return out.reshape(B, S, D)
```

```python
import functools

import jax
import jax.numpy as jnp
from jax import lax
from jax.experimental import pallas as pl
from jax.experimental.pallas import tpu as pltpu

F32 = jnp.float32
BF16 = jnp.bfloat16
NEG_INF = float("-inf")

D_MODEL = 1024
D_ATTN = 512
HEAD_DIM_ATTN = 64
N_HEADS_ATTN = 8
D_MLSTM = 512
HEAD_DIM_MLSTM = 128
N_HEADS_MLSTM = 4
CONV_K = 4
N_EXPERTS = 32
TOP_K = 4
D_FF = 1024
SWIGLU_LIMIT = 7.0
SWIGLU_ALPHA = 1.702
EPS = 1e-6
DILATIONS = (1, 4, 16)
BRANCH_SPAN = 128

LANES = 128
SUBLANES = 8

IN_PROJ_ROWS = 256
ATTN_ROWS = 2048
MLSTM_CHUNK = 256
ROUTE_ROWS = 256
EXPERT_ROWS = 256
FF_CHUNK = 512
CAST_ROWS = 128
DISPATCH_ROWS = 256
HEADNORM_LANES = 256
BF16_TILE_ROWS = 16
RUN_ROWS = 64
RUN_GROUP_ROWS = 256
NOT_ROUTED = -1.0e9


def _vmem_limit(mib):
    return pltpu.CompilerParams(vmem_limit_bytes=mib * 1024 * 1024)


def _split3(a):
    p1 = a.astype(BF16)
    r1 = a - p1.astype(F32)
    p2 = r1.astype(BF16)
    r2 = r1 - p2.astype(F32)
    return p1, p2, r2.astype(BF16)


def _dot(a, b):
    return jnp.dot(a, b, preferred_element_type=F32)


def _dot_nt(a, b):
    return lax.dot_general(a, b, (((1,), (1,)), ((), ())), preferred_element_type=F32)


def _dot_tn(a, b):
    return lax.dot_general(a, b, (((0,), (0,)), ((), ())), preferred_element_type=F32)


def _in_proj_kernel(x_ref, g1_ref, wqkv_ref, wm_ref, wgc_ref, qg_ref, kg_ref, hsum_ref,
                    q_ref, k_ref, v_ref, xm_ref, vm_ref, om_ref, gates_ref):
    x = x_ref[...]
    h = x * lax.rsqrt(jnp.mean(x * x, axis=-1, keepdims=True) + EPS) * g1_ref[...]
    hb = h.astype(BF16)
    hl = (h - hb.astype(F32)).astype(BF16)
    gw = HEADNORM_LANES

    def head_norm(z, g_ref):
        parts = []
        for c in range(D_ATTN // gw):
            zc = z[:, c * gw:(c + 1) * gw]
            ms = _dot((zc * zc).astype(BF16), hsum_ref[...]) * (1.0 / HEAD_DIM_ATTN)
            parts.append(zc * lax.rsqrt(ms + EPS))
        return jnp.concatenate(parts, axis=-1) * g_ref[...]

    zq = _dot(hb, wqkv_ref[:, 0:D_ATTN])
    q_ref[...] = head_norm(zq, qg_ref) * (HEAD_DIM_ATTN ** -0.5)
    zk = _dot(hb, wqkv_ref[:, D_ATTN:2 * D_ATTN])
    k_ref[...] = head_norm(zk, kg_ref)
    v_ref[...] = _dot(hb, wqkv_ref[:, 2 * D_ATTN:3 * D_ATTN])
    xm_ref[...] = _dot(hb, wm_ref[:, 0:D_MLSTM])
    vm_ref[...] = _dot(hb, wm_ref[:, D_MLSTM:2 * D_MLSTM])
    om_ref[...] = _dot(hb, wm_ref[:, 2 * D_MLSTM:3 * D_MLSTM])
    ng = 2 * N_HEADS_MLSTM
    gsum = _dot(hb, wgc_ref[...]) + _dot(hl, wgc_ref[...])
    gsum = gsum + pltpu.roll(gsum, LANES - ng, axis=1)
    lane = lax.broadcasted_iota(jnp.int32, gsum.shape, 1)
    gates_ref[...] = jnp.where(lane < ng, gsum, 0.0)


def _in_proj(x2, g1, w_in, q_g, k_g):
    T = x2.shape[0]
    tm = IN_PROJ_ROWS
    n_qkv = 3 * D_ATTN
    n_m = 3 * D_MLSTM
    ng = 2 * N_HEADS_MLSTM
    wqkv = w_in[:, :n_qkv].astype(BF16)
    wm = w_in[:, n_qkv:n_qkv + n_m].astype(BF16)
    wg = w_in[:, n_qkv + n_m:]
    wgh = wg.astype(BF16)
    wgl = (wg - wgh.astype(F32)).astype(BF16)
    wgc = jnp.pad(jnp.concatenate([wgh, wgl], axis=1), ((0, 0), (0, LANES - 2 * ng)))
    lane = jnp.arange(HEADNORM_LANES)
    hsum = (lane[:, None] // HEAD_DIM_ATTN == lane[None, :] // HEAD_DIM_ATTN).astype(BF16)
    qg = jnp.tile(q_g, N_HEADS_ATTN)[None, :]
    kg = jnp.tile(k_g, N_HEADS_ATTN)[None, :]
    row = lambda n: pl.BlockSpec((tm, n), lambda i: (i, 0))
    full = lambda a: pl.BlockSpec(a.shape, lambda i: (0,) * a.ndim)
    outs = [jax.ShapeDtypeStruct((T, D_ATTN), F32)] * 3 + [jax.ShapeDtypeStruct((T, D_MLSTM), F32)] * 3
    outs.append(jax.ShapeDtypeStruct((T, LANES), F32))
    g1r = g1[None, :]
    return pl.pallas_call(
        _in_proj_kernel,
        grid=(T // tm,),
        in_specs=[row(D_MODEL), full(g1r), full(wqkv), full(wm), full(wgc), full(qg), full(kg), full(hsum)],
        out_specs=[row(D_ATTN)] * 3 + [row(D_MLSTM)] * 3 + [row(LANES)],
        out_shape=outs,
        compiler_params=_vmem_limit(48),
        name="in_proj",
    )(x2, g1r, wqkv, wm, wgc, qg, kg, hsum)


def _attn_tiles():
    tiles = []
    for br, d in enumerate(DILATIONS):
        group = d * BRANCH_SPAN
        for u in range(ATTN_ROWS // group):
            for c in range(d):
                tiles.append((br, d, u * group + c))
    return tiles


def _attn_kernel(slope_ref, q_ref, kp_ref, kc_ref, vp_ref, vc_ref, o_ref,
                 kwin, vwin, bias_scr, m_scr, l_scr, acc_scr):
    W = ATTN_ROWS
    n = BRANCH_SPAN
    step = pl.program_id(2)
    kwin[0:W, :] = kp_ref[...]
    kwin[W:2 * W, :] = kc_ref[...]
    vwin[0:W, :] = vp_ref[...]
    vwin[W:2 * W, :] = vc_ref[...]

    lane = lax.broadcasted_iota(jnp.int32, (n, LANES), 1)
    first_head = lane < HEAD_DIM_ATTN
    row = lax.broadcasted_iota(jnp.int32, (n, 2 * n), 0)
    col = lax.broadcasted_iota(jnp.int32, (n, 2 * n), 1)
    j = n + row - col
    valid = jnp.logical_and(j >= 0, j <= n)
    jf = j.astype(F32)
    for br, d in enumerate(DILATIONS):
        for hh in range(2):
            slope = slope_ref[:, hh * HEAD_DIM_ATTN:hh * HEAD_DIM_ATTN + 1]
            b = jnp.where(valid, -(slope * float(d)) * jf, NEG_INF)
            bias_scr[br, hh, 0] = b
            bias_scr[br, hh, 1] = jnp.where(col >= n, b, NEG_INF)

    for br, d, qs in _attn_tiles():
        lo = W + qs - n * d
        q = q_ref[pl.ds(qs, n, stride=d), :]
        kk = jnp.concatenate([kwin[pl.ds(lo, n, stride=d), :], kwin[pl.ds(W + qs, n, stride=d), :]],
                             axis=0).astype(BF16)
        vv = jnp.concatenate([vwin[pl.ds(lo, n, stride=d), :], vwin[pl.ds(W + qs, n, stride=d), :]],
                             axis=0).astype(BF16)
        if lo < W:
            sel = jnp.where(step == 0, 1, 0)
        else:
            sel = 0
        res = []
        for hh in range(2):
            qm = jnp.where(first_head if hh == 0 else jnp.logical_not(first_head), q, 0.0).astype(BF16)
            s = _dot_nt(qm, kk) + bias_scr[br, hh, sel]
            m = jnp.max(s, axis=-1, keepdims=True)
            p = jnp.exp(s - m)
            l = jnp.sum(p, axis=-1, keepdims=True)
            res.append((m, l, _dot(p.astype(BF16), vv)))
        rows = pl.ds(qs, n, stride=d)
        m_scr[br, rows, :] = jnp.where(first_head, res[0][0], res[1][0])
        l_scr[br, rows, :] = jnp.where(first_head, res[0][1], res[1][1])
        acc_scr[br, rows, :] = jnp.where(first_head, res[0][2], res[1][2])

    chunk = 256

    def combine(i, carry):
        r = pl.ds(pl.multiple_of(i * chunk, chunk), chunk)
        m0, m1, m2 = m_scr[0, r, :], m_scr[1, r, :], m_scr[2, r, :]
        mm = jnp.maximum(jnp.maximum(m0, m1), m2)
        w0, w1, w2 = jnp.exp(m0 - mm), jnp.exp(m1 - mm), jnp.exp(m2 - mm)
        num = w0 * acc_scr[0, r, :] + w1 * acc_scr[1, r, :] + w2 * acc_scr[2, r, :]
        den = w0 * l_scr[0, r, :] + w1 * l_scr[1, r, :] + w2 * l_scr[2, r, :]
        o_ref[r, :] = num / den
        return carry

    lax.fori_loop(0, W // chunk, combine, 0)


def _attention(q, k, v, B, S):
    W = ATTN_ROWS
    nb = S // W
    n_pairs = D_ATTN // LANES
    slopes = jnp.exp2(-8.0 * jnp.arange(1, N_HEADS_ATTN + 1, dtype=F32) / N_HEADS_ATTN)
    slope_l = jnp.repeat(slopes, HEAD_DIM_ATTN).reshape(n_pairs, 1, LANES)
    cur = pl.BlockSpec((W, LANES), lambda b, p, i: (b * nb + i, p))
    prev = pl.BlockSpec((W, LANES), lambda b, p, i: (b * nb + jnp.maximum(i - 1, 0), p))
    n = BRANCH_SPAN
    nbr = len(DILATIONS)
    return pl.pallas_call(
        _attn_kernel,
        grid=(B, n_pairs, nb),
        in_specs=[pl.BlockSpec((None, 1, LANES), lambda b, p, i: (p, 0, 0)), cur, prev, cur, prev, cur],
        out_specs=cur,
        out_shape=jax.ShapeDtypeStruct((B * S, D_ATTN), F32),
        scratch_shapes=[
            pltpu.VMEM((2 * W, LANES), F32), pltpu.VMEM((2 * W, LANES), F32),
            pltpu.VMEM((nbr, 2, 2, n, 2 * n), F32),
            pltpu.VMEM((nbr, W, LANES), F32), pltpu.VMEM((nbr, W, LANES), F32),
            pltpu.VMEM((nbr, W, LANES), F32),
        ],
        compiler_params=_vmem_limit(48),
        name="attention",
    )(slope_l, q, k, k, v, v)


def _mlstm_kernel(xm_ref, vm_ref, om_ref, gcol_ref, grow_ref, cw_ref, cb_ref, wq_ref, wk_ref,
                  bcol_ref, brow_ref, g_ref, o_ref, xs_ref, xtail_ref, c_ref, n_ref, m_ref):
    L = MLSTM_CHUNK
    dh = HEAD_DIM_MLSTM
    nh = N_HEADS_MLSTM

    @pl.when(pl.program_id(1) == 0)
    def _():
        xtail_ref[...] = jnp.zeros_like(xtail_ref)
        c_ref[...] = jnp.zeros_like(c_ref)
        n_ref[...] = jnp.zeros_like(n_ref)
        m_ref[...] = jnp.zeros_like(m_ref)

    x = xm_ref[...]
    xs_ref[0:SUBLANES, :] = xtail_ref[...]
    xs_ref[SUBLANES:SUBLANES + L, :] = x
    xtail_ref[...] = x[L - SUBLANES:L, :]
    xc = cb_ref[...] + x * cw_ref[CONV_K - 1:CONV_K, :]
    for back in range(1, CONV_K):
        xc = xc + xs_ref[SUBLANES - back:SUBLANES - back + L, :] * cw_ref[CONV_K - 1 - back:CONV_K - back, :]
    xc = xc * jax.nn.sigmoid(xc)

    gc = gcol_ref[...] + bcol_ref[...]
    gr = grow_ref[...] + brow_ref[...]
    ri = lax.broadcasted_iota(jnp.int32, (L, L), 0)
    ci = lax.broadcasted_iota(jnp.int32, (L, L), 1)
    causal = ri >= ci
    tril = jnp.where(causal, 1.0, 0.0).astype(BF16)
    triu = jnp.where(ri <= ci, 1.0, 0.0).astype(BF16)
    c1, c2, c3 = _split3(jax.nn.log_sigmoid(gc))
    bc = _dot(tril, c1) + _dot(tril, c2) + _dot(tril, c3)
    r1, r2, r3 = _split3(jax.nn.log_sigmoid(gr))
    brw = _dot(r1, triu) + _dot(r2, triu) + _dot(r3, triu)

    for h in range(nh):
        hs = slice(h * dh, (h + 1) * dh)
        b_col = bc[:, nh + h:nh + h + 1]
        a_col = gc[:, h:h + 1] - b_col
        a_row = gr[h:h + 1, :] - brw[nh + h:nh + h + 1, :]
        m_prev = m_ref[h:h + 1, 0:1]
        log_d = jnp.where(causal, b_col + a_row, NEG_INF)
        log_inter = b_col + m_prev
        m_t = jnp.maximum(log_inter, jnp.max(log_d, axis=-1, keepdims=True))
        d = jnp.exp(log_d - m_t)
        inter = jnp.exp(log_inter - m_t)

        xh = xc[:, hs].astype(BF16)
        qf = _dot(xh, wq_ref[h])
        kf = _dot(xh, wk_ref[h]) * (dh ** -0.5)
        qb = qf.astype(BF16)
        vb = vm_ref[:, hs].astype(BF16)
        s = _dot_nt(qb, kf.astype(BF16)) * d
        c_old = c_ref[h]
        n_old = n_ref[h:h + 1, :]
        num = _dot(s.astype(BF16), vb) + inter * _dot(qb, c_old.astype(BF16))
        den = jnp.sum(s, axis=-1, keepdims=True) + inter * jnp.sum(qf * n_old, axis=-1, keepdims=True)
        hh = num / jnp.maximum(jnp.abs(den), jnp.exp(-m_t))

        b_last = b_col[L - 1:L, :]
        log_w = b_last + a_col
        m_new = jnp.maximum(b_last + m_prev, jnp.max(log_w, axis=0, keepdims=True))
        kw = kf * jnp.exp(log_w - m_new)
        decay = jnp.exp(b_last + m_prev - m_new)
        c_ref[h] = decay * c_old + _dot_tn(kw.astype(BF16), vb)
        n_ref[h:h + 1, :] = decay * n_old + jnp.sum(kw, axis=0, keepdims=True)
        m_ref[h:h + 1, :] = jnp.broadcast_to(m_new, (1, LANES))

        hn = hh * lax.rsqrt(jnp.mean(hh * hh, axis=-1, keepdims=True) + EPS) * g_ref[:, hs]
        o_ref[:, hs] = jax.nn.sigmoid(om_ref[:, hs]) * hn


def _mlstm(xm, vm, om, gates, conv_w, conv_b, w_mq, w_mk, b_i, b_f, mh_g, B, S):
    L = MLSTM_CHUNK
    nc = S // L
    nh = N_HEADS_MLSTM
    ng = 2 * nh
    grow = gates[:, :ng].reshape(B, S, ng).transpose(0, 2, 1)
    bias = jnp.concatenate([b_i, b_f])
    bcol = jnp.pad(bias, (0, LANES - ng))[None, :]
    brow = bias[:, None]
    wq = w_mq.astype(BF16)
    wk = w_mk.astype(BF16)
    cb = conv_b[None, :]
    g = mh_g.reshape(1, D_MLSTM)
    rows = pl.BlockSpec((L, D_MLSTM), lambda b, c: (b * nc + c, 0))
    full = lambda a: pl.BlockSpec(a.shape, lambda b, c: (0,) * a.ndim)
    return pl.pallas_call(
        _mlstm_kernel,
        grid=(B, nc),
        in_specs=[rows, rows, rows, pl.BlockSpec((L, LANES), lambda b, c: (b * nc + c, 0)),
                  pl.BlockSpec((None, ng, L), lambda b, c: (b, 0, c)),
                  full(conv_w), full(cb), full(wq), full(wk), full(bcol), full(brow), full(g)],
        out_specs=rows,
        out_shape=jax.ShapeDtypeStruct((B * S, D_MLSTM), F32),
        scratch_shapes=[
            pltpu.VMEM((L + SUBLANES, D_MLSTM), F32), pltpu.VMEM((SUBLANES, D_MLSTM), F32),
            pltpu.VMEM((nh, HEAD_DIM_MLSTM, HEAD_DIM_MLSTM), F32),
            pltpu.VMEM((SUBLANES, LANES), F32), pltpu.VMEM((SUBLANES, LANES), F32),
        ],
        compiler_params=_vmem_limit(32),
        name="mlstm",
    )(xm, vm, om, gates, grow, conv_w, cb, wq, wk, bcol, brow, g)


def _out_route_kernel(attn_ref, ml_ref, x_ref, wo_ref, g2_ref, wrh_ref, wrl_ref, br_ref, earlier_ref,
                      x1_ref, h2_ref, rt_ref, gt_ref, base_ref, cnt_ref, carry_ref):
    tm = ROUTE_ROWS
    ne = N_EXPERTS

    @pl.when(pl.program_id(0) == 0)
    def _():
        carry_ref[...] = jnp.zeros_like(carry_ref)

    x1 = (x_ref[...] + _dot(attn_ref[...].astype(BF16), wo_ref[0:D_ATTN, :])
          + _dot(ml_ref[...].astype(BF16), wo_ref[D_ATTN:D_ATTN + D_MLSTM, :]))
    x1_ref[...] = x1
    h2 = x1 * lax.rsqrt(jnp.mean(x1 * x1, axis=-1, keepdims=True) + EPS) * g2_ref[...]
    h2_ref[...] = h2
    hb = h2.astype(BF16)
    hl = (h2 - hb.astype(F32)).astype(BF16)
    lt = (_dot_nt(wrh_ref[...], hb) + _dot_nt(wrh_ref[...], hl) + _dot_nt(wrl_ref[...], hb)) + br_ref[...]

    eidx = lax.broadcasted_iota(jnp.int32, (ne, tm), 0)
    vals, hots, idxs = [], [], []
    work = lt
    for _ in range(TOP_K):
        mx = jnp.max(work, axis=0, keepdims=True)
        idx = jnp.min(jnp.where(work == mx, eidx, ne), axis=0, keepdims=True)
        hot = eidx == idx
        vals.append(mx)
        hots.append(hot)
        idxs.append(idx)
        work = jnp.where(hot, NEG_INF, work)
    ex = [jnp.exp(v - vals[0]) for v in vals]
    tot = ex[0] + ex[1] + ex[2] + ex[3]

    onehot = jnp.zeros((ne, tm), F32)
    gt = jnp.zeros((ne, tm), F32)
    for kk in range(TOP_K):
        onehot = onehot + jnp.where(hots[kk], 1.0, 0.0)
        gt = gt + jnp.where(hots[kk], ex[kk] / tot, 0.0)
    carry = carry_ref[...]
    before = _dot(onehot.astype(BF16), earlier_ref[...]) + carry[:, 0:1]
    base_ref[...] = carry
    carry = carry + jnp.sum(onehot, axis=1, keepdims=True)
    carry_ref[...] = carry
    cnt_ref[...] = carry
    rt_ref[...] = jnp.where(onehot > 0.5, before, NOT_ROUTED).astype(jnp.int32)
    gt_ref[...] = gt


def _out_route(attn, ml, x2, w_out, g2, w_router, b_router):
    T = x2.shape[0]
    tm = ROUTE_ROWS
    ne = N_EXPERTS
    wo = w_out.astype(BF16)
    wrt = w_router.T
    wrh = wrt.astype(BF16)
    wrl = (wrt - wrh.astype(F32)).astype(BF16)
    br = b_router[:, None]
    g2r = g2[None, :]
    tok = jnp.arange(tm)
    earlier = (tok[:, None] < tok[None, :]).astype(BF16)
    row = lambda n: pl.BlockSpec((tm, n), lambda i: (i, 0))
    col = lambda n: pl.BlockSpec((n, tm), lambda i: (0, i))
    full = lambda a: pl.BlockSpec(a.shape, lambda i: (0,) * a.ndim)
    return pl.pallas_call(
        _out_route_kernel,
        grid=(T // tm,),
        in_specs=[row(D_ATTN), row(D_MLSTM), row(D_MODEL), full(wo), full(g2r), full(wrh), full(wrl), full(br),
                  full(earlier)],
        out_specs=[row(D_MODEL), row(D_MODEL), col(ne), col(ne),
                   pl.BlockSpec((ne, LANES), lambda i: (i, 0)), pl.BlockSpec((ne, LANES), lambda i: (0, 0))],
        out_shape=[jax.ShapeDtypeStruct((T, D_MODEL), F32), jax.ShapeDtypeStruct((T, D_MODEL), F32),
                   jax.ShapeDtypeStruct((ne, T), jnp.int32), jax.ShapeDtypeStruct((ne, T), F32),
                   jax.ShapeDtypeStruct((T // tm * ne, LANES), F32), jax.ShapeDtypeStruct((ne, LANES), F32)],
        scratch_shapes=[pltpu.VMEM((ne, LANES), F32)],
        compiler_params=_vmem_limit(32),
        name="out_route",
    )(attn, ml, x2, wo, g2r, wrh, wrl, br, earlier)


def _row_copy(src_ref, dst_ref, sem, src_row, dst_row):
    return pltpu.make_async_copy(src_ref.at[pl.ds(src_row, 1)], dst_ref.at[pl.ds(dst_row, 1)], sem)


def _dispatch_kernel(runstart_ref, runlen_ref, lshift_ref, padstart_ref, npad_ref,
                     h_ref, rt_ref, xs_ref, cbuf, xbuf, zero_ref, stage_ref, sem, xsem, zsem):
    td = DISPATCH_ROWS
    ne = N_EXPERTS
    rb = RUN_ROWS
    half = rb // 2
    grp = RUN_GROUP_ROWS // rb
    tile = pl.program_id(0)

    def window(i):
        off = runstart_ref[i] & (SUBLANES - 1)
        return off, off + runlen_ref[i]

    def first_chunk(tile_idx, e, action):
        i = tile_idx * ne + e
        off, total = window(i)
        dst = pl.multiple_of(runstart_ref[i] - off, SUBLANES)
        action(pltpu.make_async_copy(cbuf.at[pl.ds(e * rb, half)], xs_ref.at[pl.ds(dst, half)], sem))

        @pl.when(total > half)
        def _():
            action(pltpu.make_async_copy(cbuf.at[pl.ds(e * rb + half, half)],
                                         xs_ref.at[pl.ds(pl.multiple_of(dst + half, SUBLANES), half)], sem))

    @pl.when(tile > 0)
    def _():
        for e in range(ne):
            first_chunk(tile - 1, e, lambda cp: cp.wait())

    @pl.when(tile == 0)
    def _():
        stage_ref[...] = jnp.zeros_like(stage_ref)

    hb = h_ref[...].astype(BF16)
    riota = lax.broadcasted_iota(jnp.int32, (rb, td), 0)
    row8 = lax.broadcasted_iota(jnp.int32, (SUBLANES, D_MODEL), 0)

    def select(e_slice, shift):
        return jnp.where(riota == rt_ref[e_slice, :] - shift, 1.0, 0.0)

    for g in range(ne // grp):
        experts = range(g * grp, (g + 1) * grp)
        parts = [select(slice(e, e + 1), lshift_ref[tile * ne + e] - window(tile * ne + e)[0]) for e in experts]
        pt = jnp.concatenate(parts, axis=0).astype(BF16)
        cbuf[g * RUN_GROUP_ROWS:(g + 1) * RUN_GROUP_ROWS, :] = _dot(pt, hb)

    for e in range(ne):
        i = tile * ne + e
        off, total = window(i)
        head = pl.ds(e * rb, SUBLANES)
        old = stage_ref[e]
        cbuf[head, :] = jnp.where(row8 < off, old, cbuf[head, :])
        last = e * rb + jnp.minimum(total // SUBLANES * SUBLANES, rb - SUBLANES)
        keep = jnp.logical_and(runlen_ref[i] > 0, total < rb)
        stage_ref[e] = jnp.where(keep, cbuf[pl.ds(pl.multiple_of(last, SUBLANES), SUBLANES), :], old)

    for e in range(ne):
        first_chunk(tile, e, lambda cp: cp.start())

    def per_expert(e, carry):
        i = tile * ne + e
        off, total = window(i)

        def chunk(ch, c):
            pt = select(pl.ds(e, 1), lshift_ref[i] - off + ch * rb).astype(BF16)
            xbuf[...] = _dot(pt, hb)
            dst = pl.multiple_of(runstart_ref[i] - off + ch * rb, SUBLANES)
            cp = pltpu.make_async_copy(xbuf, xs_ref.at[pl.ds(dst, rb)], xsem)
            cp.start()

            @pl.when(jnp.logical_and(ch == total // rb, total % rb != 0))
            def _():
                last = pl.multiple_of(total % rb // SUBLANES * SUBLANES, SUBLANES)
                stage_ref[e] = xbuf[pl.ds(last, SUBLANES), :]

            cp.wait()
            return c

        lax.fori_loop(1, (total + rb - 1) // rb, chunk, 0)
        return carry

    lax.fori_loop(0, ne, per_expert, 0)

    @pl.when(tile == pl.num_programs(0) - 1)
    def _():
        for e in range(ne):
            first_chunk(tile, e, lambda cp: cp.wait())

    @pl.when(tile == pl.num_programs(0) - 1)
    def _():
        zero_ref[...] = jnp.zeros_like(zero_ref)

        def fill(wait):
            def per_run(e, carry):
                start = padstart_ref[e]
                head = (-start) & (SUBLANES - 1)

                def single(r, c):
                    cp = _row_copy(zero_ref, xs_ref, zsem, 0, start + r)
                    cp.wait() if wait else cp.start()
                    return c

                lax.fori_loop(0, head, single, 0)
                off = pl.multiple_of(start + head, SUBLANES)
                left = npad_ref[e] - head

                def whole(b, c):
                    dst = xs_ref.at[pl.ds(pl.multiple_of(off + b * td, SUBLANES), td)]
                    cp = pltpu.make_async_copy(zero_ref, dst, zsem)
                    cp.wait() if wait else cp.start()
                    return c

                n_whole = left // td
                lax.fori_loop(0, n_whole, whole, 0)
                off = pl.multiple_of(off + n_whole * td, SUBLANES)
                p = td // 2
                while p >= SUBLANES:
                    take = (left & p) != 0

                    @pl.when(take)
                    def _(off=off, p=p):
                        cp = pltpu.make_async_copy(zero_ref.at[pl.ds(0, p)], xs_ref.at[pl.ds(off, p)], zsem)
                        cp.wait() if wait else cp.start()

                    off = pl.multiple_of(off + jnp.where(take, p, 0), SUBLANES)
                    p //= 2
                return carry

            lax.fori_loop(0, N_EXPERTS + 1, per_run, 0)

        fill(wait=False)
        fill(wait=True)


def _dispatch(h2, rt, runstart, runlen, lshift, padstart, npad, n_slots):
    T = h2.shape[0]
    td = DISPATCH_ROWS
    ne = N_EXPERTS
    assert td == ROUTE_ROWS
    return pl.pallas_call(
        _dispatch_kernel,
        grid_spec=pltpu.PrefetchScalarGridSpec(
            num_scalar_prefetch=5,
            grid=(T // td,),
            in_specs=[pl.BlockSpec((td, D_MODEL), lambda i, *_: (i, 0)),
                      pl.BlockSpec((ne, td), lambda i, *_: (0, i))],
            out_specs=pl.BlockSpec(memory_space=pl.ANY),
            scratch_shapes=[pltpu.VMEM((ne * RUN_ROWS, D_MODEL), F32), pltpu.VMEM((RUN_ROWS, D_MODEL), F32),
                            pltpu.VMEM((td, D_MODEL), F32), pltpu.VMEM((ne, SUBLANES, D_MODEL), F32),
                            pltpu.SemaphoreType.DMA(()), pltpu.SemaphoreType.DMA(()),
                            pltpu.SemaphoreType.DMA(())],
        ),
        out_shape=jax.ShapeDtypeStruct((n_slots, D_MODEL), F32),
        compiler_params=_vmem_limit(32),
        name="dispatch",
    )(runstart, runlen, lshift, padstart, npad, h2, rt)


def _expert_kernel(bexp_ref, first_ref, next_ref, nused_ref, xs_ref, wgu_hbm, bgu_ref, wd_hbm, bd_ref, y_ref,
                   wgu_st, wd_st, wgu_bf, wd_bf, wsem):
    tm = EXPERT_ROWS
    blk = pl.program_id(0)
    active = blk * tm < nused_ref[0]

    def weight_copies(e):
        return (pltpu.make_async_copy(wgu_hbm.at[e], wgu_st, wsem.at[0]),
                pltpu.make_async_copy(wd_hbm.at[e], wd_st, wsem.at[1]))

    @pl.when(blk == 0)
    def _():
        for cp in weight_copies(bexp_ref[0]):
            cp.start()

    @pl.when(jnp.logical_and(active, first_ref[blk] == 1))
    def _():
        for cp in weight_copies(bexp_ref[blk]):
            cp.wait()
        rc = CAST_ROWS

        def cast(i, carry):
            r = pl.ds(pl.multiple_of(i * rc, rc), rc)
            wgu_bf[r, :] = wgu_st[r, :].astype(BF16)
            wd_bf[r, :] = wd_st[r, :].astype(BF16)
            return carry

        lax.fori_loop(0, D_MODEL // rc, cast, 0)

        @pl.when(next_ref[blk] >= 0)
        def _():
            for cp in weight_copies(next_ref[blk]):
                cp.start()

    @pl.when(active)
    def _():
        xb = xs_ref[...].astype(BF16)
        acc = jnp.zeros((tm, D_MODEL), F32)
        for f in range(D_FF // FF_CHUNK):
            lo = f * FF_CHUNK
            g = _dot(xb, wgu_bf[:, lo:lo + FF_CHUNK]) + bgu_ref[:, lo:lo + FF_CHUNK]
            u = _dot(xb, wgu_bf[:, D_FF + lo:D_FF + lo + FF_CHUNK]) + bgu_ref[:, D_FF + lo:D_FF + lo + FF_CHUNK]
            g = jnp.minimum(g, SWIGLU_LIMIT)
            u = jnp.clip(u, -SWIGLU_LIMIT, SWIGLU_LIMIT)
            act = g * jax.nn.sigmoid(SWIGLU_ALPHA * g) * (u + 1.0)
            acc = acc + _dot(act.astype(BF16), wd_bf[lo:lo + FF_CHUNK, :])
        y_ref[...] = (acc + bd_ref[...]).astype(y_ref.dtype)

    @pl.when(jnp.logical_not(active))
    def _():
        y_ref[...] = jnp.zeros_like(y_ref)


def _experts(xs, block_expert, first_block, next_expert, n_used, w_gate_up, b_gate_up, w_down, b_down):
    n_blocks = block_expert.shape[0]
    tm = EXPERT_ROWS
    assert D_FF == D_MODEL and xs.shape[0] == (n_blocks - 1) * tm
    bgu = b_gate_up[:, None, :]
    bd = b_down[:, None, :]
    last = n_blocks - 2
    return pl.pallas_call(
        _expert_kernel,
        grid_spec=pltpu.PrefetchScalarGridSpec(
            num_scalar_prefetch=4,
            grid=(n_blocks,),
            in_specs=[
                pl.BlockSpec((tm, D_MODEL), lambda i, be, fb, nx, nu: (jnp.minimum(i, last), 0)),
                pl.BlockSpec(memory_space=pl.ANY),
                pl.BlockSpec((None, 1, 2 * D_FF), lambda i, be, fb, nx, nu: (be[i], 0, 0)),
                pl.BlockSpec(memory_space=pl.ANY),
                pl.BlockSpec((None, 1, D_MODEL), lambda i, be, fb, nx, nu: (be[i], 0, 0)),
            ],
            out_specs=pl.BlockSpec((tm, D_MODEL), lambda i, be, fb, nx, nu: (i, 0)),
            scratch_shapes=[pltpu.VMEM((D_MODEL, 2 * D_FF), F32), pltpu.VMEM((D_FF, D_MODEL), F32),
                            pltpu.VMEM((D_MODEL, 2 * D_FF), BF16), pltpu.VMEM((D_FF, D_MODEL), BF16),
                            pltpu.SemaphoreType.DMA((2,))],
        ),
        out_shape=jax.ShapeDtypeStruct((n_blocks * tm, D_MODEL), BF16),
        compiler_params=_vmem_limit(48),
        name="experts",
    )(block_expert, first_block, next_expert, n_used, xs, w_gate_up, bgu, w_down, bd)


def _combine_kernel(fetch_ref, shift_ref, nch_ref, x1_ref, rt_ref, gt_ref, y_ref, o_ref,
                    buf, xbuf, sem, xsem):
    tc = ROUTE_ROWS
    ne = N_EXPERTS
    rb = RUN_ROWS
    grp = RUN_GROUP_ROWS // rb
    tile = pl.program_id(0)
    base = tile * ne
    slot = tile % 2

    def run_copy(tile_base, buf_slot, e):
        start = pl.multiple_of(fetch_ref[tile_base + e], BF16_TILE_ROWS)
        return pltpu.make_async_copy(y_ref.at[pl.ds(start, rb)], buf.at[buf_slot, pl.ds(e * rb, rb)],
                                     sem.at[buf_slot, e // grp])

    @pl.when(tile == 0)
    def _():
        for e in range(ne):
            run_copy(base, slot, e).start()

    @pl.when(tile + 1 < pl.num_programs(0))
    def _():
        for e in range(ne):
            run_copy(base + ne, 1 - slot, e).start()

    riota = lax.broadcasted_iota(jnp.int32, (rb, tc), 0)

    def select(e_slice, shift):
        return jnp.where(riota == rt_ref[e_slice, :] - shift, gt_ref[e_slice, :], 0.0)

    acc = x1_ref[...]
    for g in range(ne // grp):
        parts = []
        for e in range(g * grp, (g + 1) * grp):
            run_copy(base, slot, e).wait()
            parts.append(select(slice(e, e + 1), shift_ref[base + e]))
        pt = jnp.concatenate(parts, axis=0).astype(BF16)
        acc = acc + _dot_tn(pt, buf[slot, g * RUN_GROUP_ROWS:(g + 1) * RUN_GROUP_ROWS, :])
    o_ref[...] = acc

    def per_expert(e, carry):
        def chunk(ch, c):
            start = pl.multiple_of(fetch_ref[base + e] + ch * rb, BF16_TILE_ROWS)
            cp = pltpu.make_async_copy(y_ref.at[pl.ds(start, rb)], xbuf, xsem)
            cp.start()
            cp.wait()
            pt = select(pl.ds(e, 1), shift_ref[base + e] + ch * rb).astype(BF16)
            o_ref[...] += _dot_tn(pt, xbuf[...])
            return c

        lax.fori_loop(1, nch_ref[base + e], chunk, 0)
        return carry

    lax.fori_loop(0, ne, per_expert, 0)


def _combine(fetch, shift, nch, x1, rt, gt, y):
    T = x1.shape[0]
    tc = ROUTE_ROWS
    ne = N_EXPERTS
    return pl.pallas_call(
        _combine_kernel,
        grid_spec=pltpu.PrefetchScalarGridSpec(
            num_scalar_prefetch=3,
            grid=(T // tc,),
            in_specs=[pl.BlockSpec((tc, D_MODEL), lambda i, f, s, n: (i, 0)),
                      pl.BlockSpec((ne, tc), lambda i, f, s, n: (0, i)),
                      pl.BlockSpec((ne, tc), lambda i, f, s, n: (0, i)),
                      pl.BlockSpec(memory_space=pl.ANY)],
            out_specs=pl.BlockSpec((tc, D_MODEL), lambda i, f, s, n: (i, 0)),
            scratch_shapes=[pltpu.VMEM((2, ne * RUN_ROWS, D_MODEL), BF16), pltpu.VMEM((RUN_ROWS, D_MODEL), BF16),
                            pltpu.SemaphoreType.DMA((2, ne * RUN_ROWS // RUN_GROUP_ROWS)),
                            pltpu.SemaphoreType.DMA(())],
        ),
        out_shape=jax.ShapeDtypeStruct((T, D_MODEL), F32),
        compiler_params=_vmem_limit(32),
        name="combine",
    )(fetch, shift, nch, x1, rt, gt, y)


def _routing_tables(base, counts, T):
    tm = EXPERT_ROWS
    ne = N_EXPERTS
    n_tiles = T // ROUTE_ROWS
    n_blocks = -(-(T * TOP_K + ne * RUN_ROWS) // tm) + ne
    n_slots = n_blocks * tm
    sizes = counts[:, 0].astype(jnp.int32)
    psizes = (sizes + RUN_ROWS + tm - 1) // tm * tm
    pends = jnp.cumsum(psizes)
    pstarts = pends - psizes
    n_used = pends[-1:]
    padstart = jnp.concatenate([pstarts + sizes, n_used])
    npad = jnp.concatenate([psizes - sizes, n_slots - n_used])
    nb = n_blocks + 1
    blk = jnp.arange(nb, dtype=jnp.int32)
    block_expert = jnp.minimum(jnp.sum(pends[None, :] <= (blk * tm)[:, None], axis=1), ne - 1).astype(jnp.int32)
    first_block = jnp.concatenate([jnp.ones((1,), jnp.int32),
                                   (block_expert[1:] != block_expert[:-1]).astype(jnp.int32)])
    starts_group = jnp.logical_and(first_block == 1, blk * tm < n_used)
    pos = jnp.where(starts_group, blk, nb)
    later = jnp.concatenate([lax.cummin(pos[::-1])[::-1][1:], jnp.full((1,), nb, jnp.int32)])
    next_expert = jnp.where(later < nb, block_expert[jnp.minimum(later, nb - 1)], -1).astype(jnp.int32)
    tile_base = base[:, 0].astype(jnp.int32).reshape(n_tiles, ne)
    run_len = jnp.concatenate([tile_base[1:], sizes[None, :]], axis=0) - tile_base
    run_start = pstarts[None, :] + tile_base
    fetch = run_start // BF16_TILE_ROWS * BF16_TILE_ROWS
    shift = fetch - pstarts[None, :]
    nch = jnp.where(run_len > 0, (run_start - fetch + run_len + RUN_ROWS - 1) // RUN_ROWS, 0)
    flat = lambda a: a.reshape(n_tiles * ne).astype(jnp.int32)
    return (flat(run_start), flat(run_len), flat(tile_base), padstart, npad, n_slots,
            block_expert, first_block, next_expert, n_used, flat(fetch), flat(shift), flat(nch))


def kernel(x, norm1_g, w_in, q_norm_g, k_norm_g, conv_w, conv_b, w_mq, w_mk, b_igate, b_fgate, mh_norm_g,
           w_out, norm2_g, w_router, b_router, w_gate_up, b_gate_up, w_down, b_down):
    B, S, D = x.shape
    T = B * S
    assert D == D_MODEL and S % ATTN_ROWS == 0 and norm1_g.shape[0] == 1
    x2 = x.reshape(T, D)
    q, k, v, xm, vm, om, gates = _in_proj(x2, norm1_g[0], w_in[0], q_norm_g[0], k_norm_g[0])
    attn = _attention(q, k, v, B, S)
    ml = _mlstm(xm, vm, om, gates, conv_w[0], conv_b[0], w_mq[0], w_mk[0], b_igate[0], b_fgate[0],
                mh_norm_g[0], B, S)
    x1, h2, rt, gt, base, counts = _out_route(attn, ml, x2, w_out[0], norm2_g[0], w_router[0], b_router[0])
    (run_start, run_len, tile_base, padstart, npad, n_slots, block_expert, first_block, next_expert, n_used,
     fetch, shift, nch) = _routing_tables(base, counts, T)
    xs = _dispatch(h2, rt, run_start, run_len, tile_base, padstart, npad, n_slots)
    y = _experts(xs, block_expert, first_block, next_expert, n_used, w_gate_up[0], b_gate_up[0], w_down[0],
                 b_down[0])
    out = _combine(fetch, shift, nch, x1, rt, gt, y)
    return out.reshape(B, S, D)
```

```python
import functools

import jax
import jax.numpy as jnp
from jax import lax
from jax.experimental import pallas as pl
from jax.experimental.pallas import tpu as pltpu

F32 = jnp.float32
BF16 = jnp.bfloat16
NEG_INF = float("-inf")

D_MODEL = 1024
D_ATTN = 512
HEAD_DIM_ATTN = 64
N_HEADS_ATTN = 8
D_MLSTM = 512
HEAD_DIM_MLSTM = 128
N_HEADS_MLSTM = 4
CONV_K = 4
N_EXPERTS = 32
TOP_K = 4
D_FF = 1024
SWIGLU_LIMIT = 7.0
SWIGLU_ALPHA = 1.702
EPS = 1e-6
DILATIONS = (1, 4, 16)
BRANCH_SPAN = 128

LANES = 128
SUBLANES = 8

IN_PROJ_ROWS = 256
ATTN_ROWS = 2048
MLSTM_CHUNK = 256
ROUTE_ROWS = 256
EXPERT_ROWS = 256
FF_CHUNK = 512
CAST_ROWS = 128
DISPATCH_ROWS = 256
HEADNORM_LANES = 256
BF16_TILE_ROWS = 16
RUN_ROWS = 64
RUN_GROUP_ROWS = 256
NOT_ROUTED = -1.0e9


def _vmem_limit(mib):
    return pltpu.CompilerParams(vmem_limit_bytes=mib * 1024 * 1024)


def _split3(a):
    p1 = a.astype(BF16)
    r1 = a - p1.astype(F32)
    p2 = r1.astype(BF16)
    r2 = r1 - p2.astype(F32)
    return p1, p2, r2.astype(BF16)


def _dot(a, b):
    return jnp.dot(a, b, preferred_element_type=F32)


def _dot_nt(a, b):
    return lax.dot_general(a, b, (((1,), (1,)), ((), ())), preferred_element_type=F32)


def _dot_tn(a, b):
    return lax.dot_general(a, b, (((0,), (0,)), ((), ())), preferred_element_type=F32)


def _in_proj_kernel(x_ref, g1_ref, wqkv_ref, wm_ref, wgc_ref, qg_ref, kg_ref, hsum_ref,
                    q_ref, k_ref, v_ref, xm_ref, vm_ref, om_ref, gates_ref):
    x = x_ref[...]
    h = x * lax.rsqrt(jnp.mean(x * x, axis=-1, keepdims=True) + EPS) * g1_ref[...]
    hb = h.astype(BF16)
    hl = (h - hb.astype(F32)).astype(BF16)
    gw = HEADNORM_LANES

    def head_norm(z, g_ref):
        parts = []
        for c in range(D_ATTN // gw):
            zc = z[:, c * gw:(c + 1) * gw]
            ms = _dot((zc * zc).astype(BF16), hsum_ref[...]) * (1.0 / HEAD_DIM_ATTN)
            parts.append(zc * lax.rsqrt(ms + EPS))
        return jnp.concatenate(parts, axis=-1) * g_ref[...]

    zq = _dot(hb, wqkv_ref[:, 0:D_ATTN])
    q_ref[...] = head_norm(zq, qg_ref) * (HEAD_DIM_ATTN ** -0.5)
    zk = _dot(hb, wqkv_ref[:, D_ATTN:2 * D_ATTN])
    k_ref[...] = head_norm(zk, kg_ref)
    v_ref[...] = _dot(hb, wqkv_ref[:, 2 * D_ATTN:3 * D_ATTN])
    xm_ref[...] = _dot(hb, wm_ref[:, 0:D_MLSTM])
    vm_ref[...] = _dot(hb, wm_ref[:, D_MLSTM:2 * D_MLSTM])
    om_ref[...] = _dot(hb, wm_ref[:, 2 * D_MLSTM:3 * D_MLSTM])
    ng = 2 * N_HEADS_MLSTM
    gsum = _dot(hb, wgc_ref[...]) + _dot(hl, wgc_ref[...])
    gsum = gsum + pltpu.roll(gsum, LANES - ng, axis=1)
    lane = lax.broadcasted_iota(jnp.int32, gsum.shape, 1)
    gates_ref[...] = jnp.where(lane < ng, gsum, 0.0)


def _in_proj(x2, g1, w_in, q_g, k_g):
    T = x2.shape[0]
    tm = IN_PROJ_ROWS
    n_qkv = 3 * D_ATTN
    n_m = 3 * D_MLSTM
    ng = 2 * N_HEADS_MLSTM
    wqkv = w_in[:, :n_qkv].astype(BF16)
    wm = w_in[:, n_qkv:n_qkv + n_m].astype(BF16)
    wg = w_in[:, n_qkv + n_m:]
    wgh = wg.astype(BF16)
    wgl = (wg - wgh.astype(F32)).astype(BF16)
    wgc = jnp.pad(jnp.concatenate([wgh, wgl], axis=1), ((0, 0), (0, LANES - 2 * ng)))
    lane = jnp.arange(HEADNORM_LANES)
    hsum = (lane[:, None] // HEAD_DIM_ATTN == lane[None, :] // HEAD_DIM_ATTN).astype(BF16)
    qg = jnp.tile(q_g, N_HEADS_ATTN)[None, :]
    kg = jnp.tile(k_g, N_HEADS_ATTN)[None, :]
    row = lambda n: pl.BlockSpec((tm, n), lambda i: (i, 0))
    full = lambda a: pl.BlockSpec(a.shape, lambda i: (0,) * a.ndim)
    outs = [jax.ShapeDtypeStruct((T, D_ATTN), F32)] * 3 + [jax.ShapeDtypeStruct((T, D_MLSTM), F32)] * 3
    outs.append(jax.ShapeDtypeStruct((T, LANES), F32))
    g1r = g1[None, :]
    return pl.pallas_call(
        _in_proj_kernel,
        grid=(T // tm,),
        in_specs=[row(D_MODEL), full(g1r), full(wqkv), full(wm), full(wgc), full(qg), full(kg), full(hsum)],
        out_specs=[row(D_ATTN)] * 3 + [row(D_MLSTM)] * 3 + [row(LANES)],
        out_shape=outs,
        compiler_params=_vmem_limit(48),
        name="in_proj",
    )(x2, g1r, wqkv, wm, wgc, qg, kg, hsum)


def _attn_tiles():
    tiles = []
    for br, d in enumerate(DILATIONS):
        group = d * BRANCH_SPAN
        for u in range(ATTN_ROWS // group):
            for c in range(d):
                tiles.append((br, d, u * group + c))
    return tiles


def _attn_kernel(slope_ref, q_ref, kp_ref, kc_ref, vp_ref, vc_ref, o_ref,
                 kwin, vwin, bias_scr, m_scr, l_scr, acc_scr):
    W = ATTN_ROWS
    n = BRANCH_SPAN
    step = pl.program_id(2)
    kwin[0:W, :] = kp_ref[...]
    kwin[W:2 * W, :] = kc_ref[...]
    vwin[0:W, :] = vp_ref[...]
    vwin[W:2 * W, :] = vc_ref[...]

    lane = lax.broadcasted_iota(jnp.int32, (n, LANES), 1)
    first_head = lane < HEAD_DIM_ATTN
    row = lax.broadcasted_iota(jnp.int32, (n, 2 * n), 0)
    col = lax.broadcasted_iota(jnp.int32, (n, 2 * n), 1)
    j = n + row - col
    valid = jnp.logical_and(j >= 0, j <= n)
    jf = j.astype(F32)
    for br, d in enumerate(DILATIONS):
        for hh in range(2):
            slope = slope_ref[:, hh * HEAD_DIM_ATTN:hh * HEAD_DIM_ATTN + 1]
            b = jnp.where(valid, -(slope * float(d)) * jf, NEG_INF)
            bias_scr[br, hh, 0] = b
            bias_scr[br, hh, 1] = jnp.where(col >= n, b, NEG_INF)

    for br, d, qs in _attn_tiles():
        lo = W + qs - n * d
        q = q_ref[pl.ds(qs, n, stride=d), :]
        kk = jnp.concatenate([kwin[pl.ds(lo, n, stride=d), :], kwin[pl.ds(W + qs, n, stride=d), :]],
                             axis=0).astype(BF16)
        vv = jnp.concatenate([vwin[pl.ds(lo, n, stride=d), :], vwin[pl.ds(W + qs, n, stride=d), :]],
                             axis=0).astype(BF16)
        if lo < W:
            sel = jnp.where(step == 0, 1, 0)
        else:
            sel = 0
        res = []
        for hh in range(2):
            qm = jnp.where(first_head if hh == 0 else jnp.logical_not(first_head), q, 0.0).astype(BF16)
            s = _dot_nt(qm, kk) + bias_scr[br, hh, sel]
            m = jnp.max(s, axis=-1, keepdims=True)
            p = jnp.exp(s - m)
            l = jnp.sum(p, axis=-1, keepdims=True)
            res.append((m, l, _dot(p.astype(BF16), vv)))
        rows = pl.ds(qs, n, stride=d)
        m_scr[br, rows, :] = jnp.where(first_head, res[0][0], res[1][0])
        l_scr[br, rows, :] = jnp.where(first_head, res[0][1], res[1][1])
        acc_scr[br, rows, :] = jnp.where(first_head, res[0][2], res[1][2])

    chunk = 256

    def combine(i, carry):
        r = pl.ds(pl.multiple_of(i * chunk, chunk), chunk)
        m0, m1, m2 = m_scr[0, r, :], m_scr[1, r, :], m_scr[2, r, :]
        mm = jnp.maximum(jnp.maximum(m0, m1), m2)
        w0, w1, w2 = jnp.exp(m0 - mm), jnp.exp(m1 - mm), jnp.exp(m2 - mm)
        num = w0 * acc_scr[0, r, :] + w1 * acc_scr[1, r, :] + w2 * acc_scr[2, r, :]
        den = w0 * l_scr[0, r, :] + w1 * l_scr[1, r, :] + w2 * l_scr[2, r, :]
        o_ref[r, :] = num / den
        return carry

    lax.fori_loop(0, W // chunk, combine, 0)


def _attention(q, k, v, B, S):
    W = ATTN_ROWS
    nb = S // W
    n_pairs = D_ATTN // LANES
    slopes = jnp.exp2(-8.0 * jnp.arange(1, N_HEADS_ATTN + 1, dtype=F32) / N_HEADS_ATTN)
    slope_l = jnp.repeat(slopes, HEAD_DIM_ATTN).reshape(n_pairs, 1, LANES)
    cur = pl.BlockSpec((W, LANES), lambda b, p, i: (b * nb + i, p))
    prev = pl.BlockSpec((W, LANES), lambda b, p, i: (b * nb + jnp.maximum(i - 1, 0), p))
    n = BRANCH_SPAN
    nbr = len(DILATIONS)
    return pl.pallas_call(
        _attn_kernel,
        grid=(B, n_pairs, nb),
        in_specs=[pl.BlockSpec((None, 1, LANES), lambda b, p, i: (p, 0, 0)), cur, prev, cur, prev, cur],
        out_specs=cur,
        out_shape=jax.ShapeDtypeStruct((B * S, D_ATTN), F32),
        scratch_shapes=[
            pltpu.VMEM((2 * W, LANES), F32), pltpu.VMEM((2 * W, LANES), F32),
            pltpu.VMEM((nbr, 2, 2, n, 2 * n), F32),
            pltpu.VMEM((nbr, W, LANES), F32), pltpu.VMEM((nbr, W, LANES), F32),
            pltpu.VMEM((nbr, W, LANES), F32),
        ],
        compiler_params=_vmem_limit(48),
        name="attention",
    )(slope_l, q, k, k, v, v)


def _mlstm_kernel(xm_ref, vm_ref, om_ref, gcol_ref, grow_ref, cw_ref, cb_ref, wq_ref, wk_ref,
                  bcol_ref, brow_ref, g_ref, o_ref, xs_ref, xtail_ref, c_ref, n_ref, m_ref):
    L = MLSTM_CHUNK
    dh = HEAD_DIM_MLSTM
    nh = N_HEADS_MLSTM

    @pl.when(pl.program_id(1) == 0)
    def _():
        xtail_ref[...] = jnp.zeros_like(xtail_ref)
        c_ref[...] = jnp.zeros_like(c_ref)
        n_ref[...] = jnp.zeros_like(n_ref)
        m_ref[...] = jnp.zeros_like(m_ref)

    x = xm_ref[...]
    xs_ref[0:SUBLANES, :] = xtail_ref[...]
    xs_ref[SUBLANES:SUBLANES + L, :] = x
    xtail_ref[...] = x[L - SUBLANES:L, :]
    xc = cb_ref[...] + x * cw_ref[CONV_K - 1:CONV_K, :]
    for back in range(1, CONV_K):
        xc = xc + xs_ref[SUBLANES - back:SUBLANES - back + L, :] * cw_ref[CONV_K - 1 - back:CONV_K - back, :]
    xc = xc * jax.nn.sigmoid(xc)

    gc = gcol_ref[...] + bcol_ref[...]
    gr = grow_ref[...] + brow_ref[...]
    ri = lax.broadcasted_iota(jnp.int32, (L, L), 0)
    ci = lax.broadcasted_iota(jnp.int32, (L, L), 1)
    causal = ri >= ci
    tril = jnp.where(causal, 1.0, 0.0).astype(BF16)
    triu = jnp.where(ri <= ci, 1.0, 0.0).astype(BF16)
    c1, c2, c3 = _split3(jax.nn.log_sigmoid(gc))
    bc = _dot(tril, c1) + _dot(tril, c2) + _dot(tril, c3)
    r1, r2, r3 = _split3(jax.nn.log_sigmoid(gr))
    brw = _dot(r1, triu) + _dot(r2, triu) + _dot(r3, triu)

    for h in range(nh):
        hs = slice(h * dh, (h + 1) * dh)
        b_col = bc[:, nh + h:nh + h + 1]
        a_col = gc[:, h:h + 1] - b_col
        a_row = gr[h:h + 1, :] - brw[nh + h:nh + h + 1, :]
        m_prev = m_ref[h:h + 1, 0:1]
        log_d = jnp.where(causal, b_col + a_row, NEG_INF)
        log_inter = b_col + m_prev
        m_t = jnp.maximum(log_inter, jnp.max(log_d, axis=-1, keepdims=True))
        d = jnp.exp(log_d - m_t)
        inter = jnp.exp(log_inter - m_t)

        xh = xc[:, hs].astype(BF16)
        qf = _dot(xh, wq_ref[h])
        kf = _dot(xh, wk_ref[h]) * (dh ** -0.5)
        qb = qf.astype(BF16)
        vb = vm_ref[:, hs].astype(BF16)
        s = _dot_nt(qb, kf.astype(BF16)) * d
        c_old = c_ref[h]
        n_old = n_ref[h:h + 1, :]
        num = _dot(s.astype(BF16), vb) + inter * _dot(qb, c_old.astype(BF16))
        den = jnp.sum(s, axis=-1, keepdims=True) + inter * jnp.sum(qf * n_old, axis=-1, keepdims=True)
        hh = num / jnp.maximum(jnp.abs(den), jnp.exp(-m_t))

        b_last = b_col[L - 1:L, :]
        log_w = b_last + a_col
        m_new = jnp.maximum(b_last + m_prev, jnp.max(log_w, axis=0, keepdims=True))
        kw = kf * jnp.exp(log_w - m_new)
        decay = jnp.exp(b_last + m_prev - m_new)
        c_ref[h] = decay * c_old + _dot_tn(kw.astype(BF16), vb)
        n_ref[h:h + 1, :] = decay * n_old + jnp.sum(kw, axis=0, keepdims=True)
        m_ref[h:h + 1, :] = jnp.broadcast_to(m_new, (1, LANES))

        hn = hh * lax.rsqrt(jnp.mean(hh * hh, axis=-1, keepdims=True) + EPS) * g_ref[:, hs]
        o_ref[:, hs] = jax.nn.sigmoid(om_ref[:, hs]) * hn


def _mlstm(xm, vm, om, gates, conv_w, conv_b, w_mq, w_mk, b_i, b_f, mh_g, B, S):
    L = MLSTM_CHUNK
    nc = S // L
    nh = N_HEADS_MLSTM
    ng = 2 * nh
    grow = gates[:, :ng].reshape(B, S, ng).transpose(0, 2, 1)
    bias = jnp.concatenate([b_i, b_f])
    bcol = jnp.pad(bias, (0, LANES - ng))[None, :]
    brow = bias[:, None]
    wq = w_mq.astype(BF16)
    wk = w_mk.astype(BF16)
    cb = conv_b[None, :]
    g = mh_g.reshape(1, D_MLSTM)
    rows = pl.BlockSpec((L, D_MLSTM), lambda b, c: (b * nc + c, 0))
    full = lambda a: pl.BlockSpec(a.shape, lambda b, c: (0,) * a.ndim)
    return pl.pallas_call(
        _mlstm_kernel,
        grid=(B, nc),
        in_specs=[rows, rows, rows, pl.BlockSpec((L, LANES), lambda b, c: (b * nc + c, 0)),
                  pl.BlockSpec((None, ng, L), lambda b, c: (b, 0, c)),
                  full(conv_w), full(cb), full(wq), full(wk), full(bcol), full(brow), full(g)],
        out_specs=rows,
        out_shape=jax.ShapeDtypeStruct((B * S, D_MLSTM), F32),
        scratch_shapes=[
            pltpu.VMEM((L + SUBLANES, D_MLSTM), F32), pltpu.VMEM((SUBLANES, D_MLSTM), F32),
            pltpu.VMEM((nh, HEAD_DIM_MLSTM, HEAD_DIM_MLSTM), F32),
            pltpu.VMEM((SUBLANES, LANES), F32), pltpu.VMEM((SUBLANES, LANES), F32),
        ],
        compiler_params=_vmem_limit(32),
        name="mlstm",
    )(xm, vm, om, gates, grow, conv_w, cb, wq, wk, bcol, brow, g)


def _out_route_kernel(attn_ref, ml_ref, x_ref, wo_ref, g2_ref, wrh_ref, wrl_ref, br_ref, earlier_ref,
                      x1_ref, h2_ref, rt_ref, gt_ref, base_ref, cnt_ref, carry_ref):
    tm = ROUTE_ROWS
    ne = N_EXPERTS

    @pl.when(pl.program_id(0) == 0)
    def _():
        carry_ref[...] = jnp.zeros_like(carry_ref)

    x1 = (x_ref[...] + _dot(attn_ref[...].astype(BF16), wo_ref[0:D_ATTN, :])
          + _dot(ml_ref[...].astype(BF16), wo_ref[D_ATTN:D_ATTN + D_MLSTM, :]))
    x1_ref[...] = x1
    h2 = x1 * lax.rsqrt(jnp.mean(x1 * x1, axis=-1, keepdims=True) + EPS) * g2_ref[...]
    hb = h2.astype(BF16)
    h2_ref[...] = hb
    hl = (h2 - hb.astype(F32)).astype(BF16)
    lt = (_dot_nt(wrh_ref[...], hb) + _dot_nt(wrh_ref[...], hl) + _dot_nt(wrl_ref[...], hb)) + br_ref[...]

    eidx = lax.broadcasted_iota(jnp.int32, (ne, tm), 0)
    vals, hots, idxs = [], [], []
    work = lt
    for _ in range(TOP_K):
        mx = jnp.max(work, axis=0, keepdims=True)
        idx = jnp.min(jnp.where(work == mx, eidx, ne), axis=0, keepdims=True)
        hot = eidx == idx
        vals.append(mx)
        hots.append(hot)
        idxs.append(idx)
        work = jnp.where(hot, NEG_INF, work)
    ex = [jnp.exp(v - vals[0]) for v in vals]
    tot = ex[0] + ex[1] + ex[2] + ex[3]

    onehot = jnp.zeros((ne, tm), F32)
    gt = jnp.zeros((ne, tm), F32)
    for kk in range(TOP_K):
        onehot = onehot + jnp.where(hots[kk], 1.0, 0.0)
        gt = gt + jnp.where(hots[kk], ex[kk] / tot, 0.0)
    carry = carry_ref[...]
    before = _dot(onehot.astype(BF16), earlier_ref[...]) + carry[:, 0:1]
    base_ref[...] = carry
    carry = carry + jnp.sum(onehot, axis=1, keepdims=True)
    carry_ref[...] = carry
    cnt_ref[...] = carry
    rt_ref[...] = jnp.where(onehot > 0.5, before, NOT_ROUTED).astype(jnp.int32)
    gt_ref[...] = gt


def _out_route(attn, ml, x2, w_out, g2, w_router, b_router):
    T = x2.shape[0]
    tm = ROUTE_ROWS
    ne = N_EXPERTS
    wo = w_out.astype(BF16)
    wrt = w_router.T
    wrh = wrt.astype(BF16)
    wrl = (wrt - wrh.astype(F32)).astype(BF16)
    br = b_router[:, None]
    g2r = g2[None, :]
    tok = jnp.arange(tm)
    earlier = (tok[:, None] < tok[None, :]).astype(BF16)
    row = lambda n: pl.BlockSpec((tm, n), lambda i: (i, 0))
    col = lambda n: pl.BlockSpec((n, tm), lambda i: (0, i))
    full = lambda a: pl.BlockSpec(a.shape, lambda i: (0,) * a.ndim)
    return pl.pallas_call(
        _out_route_kernel,
        grid=(T // tm,),
        in_specs=[row(D_ATTN), row(D_MLSTM), row(D_MODEL), full(wo), full(g2r), full(wrh), full(wrl), full(br),
                  full(earlier)],
        out_specs=[row(D_MODEL), row(D_MODEL), col(ne), col(ne),
                   pl.BlockSpec((ne, LANES), lambda i: (i, 0)), pl.BlockSpec((ne, LANES), lambda i: (0, 0))],
        out_shape=[jax.ShapeDtypeStruct((T, D_MODEL), F32), jax.ShapeDtypeStruct((T, D_MODEL), BF16),
                   jax.ShapeDtypeStruct((ne, T), jnp.int32), jax.ShapeDtypeStruct((ne, T), F32),
                   jax.ShapeDtypeStruct((T // tm * ne, LANES), F32), jax.ShapeDtypeStruct((ne, LANES), F32)],
        scratch_shapes=[pltpu.VMEM((ne, LANES), F32)],
        compiler_params=_vmem_limit(32),
        name="out_route",
    )(attn, ml, x2, wo, g2r, wrh, wrl, br, earlier)


def _row_copy(src_ref, dst_ref, sem, src_row, dst_row):
    return pltpu.make_async_copy(src_ref.at[pl.ds(src_row, 1)], dst_ref.at[pl.ds(dst_row, 1)], sem)


def _dispatch_kernel(runstart_ref, runlen_ref, lshift_ref, padstart_ref, npad_ref,
                     h_ref, rt_ref, xs_ref, cbuf, xbuf, zero_ref, stage_ref, sem, xsem, zsem):
    td = DISPATCH_ROWS
    ne = N_EXPERTS
    rb = RUN_ROWS
    half = rb // 2
    ta = BF16_TILE_ROWS
    grp = RUN_GROUP_ROWS // rb
    tile = pl.program_id(0)

    def window(i):
        off = runstart_ref[i] & (ta - 1)
        return off, off + runlen_ref[i]

    def first_chunk(tile_idx, e, action):
        i = tile_idx * ne + e
        off, total = window(i)
        dst = pl.multiple_of(runstart_ref[i] - off, ta)
        action(pltpu.make_async_copy(cbuf.at[pl.ds(e * rb, half)], xs_ref.at[pl.ds(dst, half)], sem))

        @pl.when(total > half)
        def _():
            action(pltpu.make_async_copy(cbuf.at[pl.ds(e * rb + half, half)],
                                         xs_ref.at[pl.ds(pl.multiple_of(dst + half, ta), half)], sem))

    @pl.when(tile > 0)
    def _():
        for e in range(ne):
            first_chunk(tile - 1, e, lambda cp: cp.wait())

    @pl.when(tile == 0)
    def _():
        stage_ref[...] = jnp.zeros_like(stage_ref)

    hb = h_ref[...]
    riota = lax.broadcasted_iota(jnp.int32, (rb, td), 0)
    rowt = lax.broadcasted_iota(jnp.int32, (ta, D_MODEL), 0)

    def select(e_slice, shift):
        return jnp.where(riota == rt_ref[e_slice, :] - shift, 1.0, 0.0)

    for g in range(ne // grp):
        experts = range(g * grp, (g + 1) * grp)
        parts = [select(slice(e, e + 1), lshift_ref[tile * ne + e] - window(tile * ne + e)[0]) for e in experts]
        pt = jnp.concatenate(parts, axis=0).astype(BF16)
        cbuf[g * RUN_GROUP_ROWS:(g + 1) * RUN_GROUP_ROWS, :] = _dot(pt, hb).astype(BF16)

    for e in range(ne):
        i = tile * ne + e
        off, total = window(i)
        head = pl.ds(e * rb, ta)
        old = stage_ref[e]
        cbuf[head, :] = jnp.where(rowt < off, old, cbuf[head, :])
        last = e * rb + jnp.minimum(total // ta * ta, rb - ta)
        keep = jnp.logical_and(runlen_ref[i] > 0, total < rb)
        stage_ref[e] = jnp.where(keep, cbuf[pl.ds(pl.multiple_of(last, ta), ta), :], old)

    for e in range(ne):
        first_chunk(tile, e, lambda cp: cp.start())

    def per_expert(e, carry):
        i = tile * ne + e
        off, total = window(i)

        def chunk(ch, c):
            pt = select(pl.ds(e, 1), lshift_ref[i] - off + ch * rb).astype(BF16)
            xbuf[...] = _dot(pt, hb).astype(BF16)
            dst = pl.multiple_of(runstart_ref[i] - off + ch * rb, ta)
            cp = pltpu.make_async_copy(xbuf, xs_ref.at[pl.ds(dst, rb)], xsem)
            cp.start()

            @pl.when(jnp.logical_and(ch == total // rb, total % rb != 0))
            def _():
                last = pl.multiple_of(total % rb // ta * ta, ta)
                stage_ref[e] = xbuf[pl.ds(last, ta), :]

            cp.wait()
            return c

        lax.fori_loop(1, (total + rb - 1) // rb, chunk, 0)
        return carry

    lax.fori_loop(0, ne, per_expert, 0)

    @pl.when(tile == pl.num_programs(0) - 1)
    def _():
        for e in range(ne):
            first_chunk(tile, e, lambda cp: cp.wait())

    @pl.when(tile == pl.num_programs(0) - 1)
    def _():
        zero_ref[...] = jnp.zeros_like(zero_ref)

        def fill(wait):
            def per_run(e, carry):
                start = padstart_ref[e]
                head = (-start) & (ta - 1)
                off = pl.multiple_of(start + head, ta)
                left = npad_ref[e] - head

                def whole(b, c):
                    dst = xs_ref.at[pl.ds(pl.multiple_of(off + b * td, ta), td)]
                    cp = pltpu.make_async_copy(zero_ref, dst, zsem)
                    cp.wait() if wait else cp.start()
                    return c

                n_whole = left // td
                lax.fori_loop(0, n_whole, whole, 0)
                off = pl.multiple_of(off + n_whole * td, ta)
                p = td // 2
                while p >= ta:
                    take = (left & p) != 0

                    @pl.when(take)
                    def _(off=off, p=p):
                        cp = pltpu.make_async_copy(zero_ref.at[pl.ds(0, p)], xs_ref.at[pl.ds(off, p)], zsem)
                        cp.wait() if wait else cp.start()

                    off = pl.multiple_of(off + jnp.where(take, p, 0), ta)
                    p //= 2
                return carry

            lax.fori_loop(0, N_EXPERTS + 1, per_run, 0)

        fill(wait=False)
        fill(wait=True)


def _dispatch(h2, rt, runstart, runlen, lshift, padstart, npad, n_slots):
    T = h2.shape[0]
    td = DISPATCH_ROWS
    ne = N_EXPERTS
    assert td == ROUTE_ROWS
    return pl.pallas_call(
        _dispatch_kernel,
        grid_spec=pltpu.PrefetchScalarGridSpec(
            num_scalar_prefetch=5,
            grid=(T // td,),
            in_specs=[pl.BlockSpec((td, D_MODEL), lambda i, *_: (i, 0)),
                      pl.BlockSpec((ne, td), lambda i, *_: (0, i))],
            out_specs=pl.BlockSpec(memory_space=pl.ANY),
            scratch_shapes=[pltpu.VMEM((ne * RUN_ROWS, D_MODEL), BF16), pltpu.VMEM((RUN_ROWS, D_MODEL), BF16),
                            pltpu.VMEM((td, D_MODEL), BF16), pltpu.VMEM((ne, BF16_TILE_ROWS, D_MODEL), BF16),
                            pltpu.SemaphoreType.DMA(()), pltpu.SemaphoreType.DMA(()),
                            pltpu.SemaphoreType.DMA(())],
        ),
        out_shape=jax.ShapeDtypeStruct((n_slots, D_MODEL), BF16),
        compiler_params=_vmem_limit(32),
        name="dispatch",
    )(runstart, runlen, lshift, padstart, npad, h2, rt)


def _expert_kernel(bexp_ref, first_ref, next_ref, nused_ref, xs_ref, wgu_hbm, bgu_ref, wd_hbm, bd_ref, y_ref,
                   wgu_st, wd_st, wgu_bf, wd_bf, wsem):
    tm = EXPERT_ROWS
    blk = pl.program_id(0)
    active = blk * tm < nused_ref[0]

    def weight_copies(e):
        return (pltpu.make_async_copy(wgu_hbm.at[e], wgu_st, wsem.at[0]),
                pltpu.make_async_copy(wd_hbm.at[e], wd_st, wsem.at[1]))

    @pl.when(blk == 0)
    def _():
        for cp in weight_copies(bexp_ref[0]):
            cp.start()

    @pl.when(jnp.logical_and(active, first_ref[blk] == 1))
    def _():
        for cp in weight_copies(bexp_ref[blk]):
            cp.wait()
        rc = CAST_ROWS

        def cast(i, carry):
            r = pl.ds(pl.multiple_of(i * rc, rc), rc)
            wgu_bf[r, :] = wgu_st[r, :].astype(BF16)
            wd_bf[r, :] = wd_st[r, :].astype(BF16)
            return carry

        lax.fori_loop(0, D_MODEL // rc, cast, 0)

        @pl.when(next_ref[blk] >= 0)
        def _():
            for cp in weight_copies(next_ref[blk]):
                cp.start()

    @pl.when(active)
    def _():
        xb = xs_ref[...]
        acc = jnp.zeros((tm, D_MODEL), F32)
        for f in range(D_FF // FF_CHUNK):
            lo = f * FF_CHUNK
            g = _dot(xb, wgu_bf[:, lo:lo + FF_CHUNK]) + bgu_ref[:, lo:lo + FF_CHUNK]
            u = _dot(xb, wgu_bf[:, D_FF + lo:D_FF + lo + FF_CHUNK]) + bgu_ref[:, D_FF + lo:D_FF + lo + FF_CHUNK]
            g = jnp.minimum(g, SWIGLU_LIMIT)
            u = jnp.clip(u, -SWIGLU_LIMIT, SWIGLU_LIMIT)
            act = g * jax.nn.sigmoid(SWIGLU_ALPHA * g) * (u + 1.0)
            acc = acc + _dot(act.astype(BF16), wd_bf[lo:lo + FF_CHUNK, :])
        y_ref[...] = (acc + bd_ref[...]).astype(y_ref.dtype)

    @pl.when(jnp.logical_not(active))
    def _():
        y_ref[...] = jnp.zeros_like(y_ref)


def _experts(xs, block_expert, first_block, next_expert, n_used, w_gate_up, b_gate_up, w_down, b_down):
    n_blocks = block_expert.shape[0]
    tm = EXPERT_ROWS
    assert D_FF == D_MODEL and xs.shape[0] == (n_blocks - 1) * tm
    bgu = b_gate_up[:, None, :]
    bd = b_down[:, None, :]
    last = n_blocks - 2
    return pl.pallas_call(
        _expert_kernel,
        grid_spec=pltpu.PrefetchScalarGridSpec(
            num_scalar_prefetch=4,
            grid=(n_blocks,),
            in_specs=[
                pl.BlockSpec((tm, D_MODEL), lambda i, be, fb, nx, nu: (jnp.minimum(i, last), 0)),
                pl.BlockSpec(memory_space=pl.ANY),
                pl.BlockSpec((None, 1, 2 * D_FF), lambda i, be, fb, nx, nu: (be[i], 0, 0)),
                pl.BlockSpec(memory_space=pl.ANY),
                pl.BlockSpec((None, 1, D_MODEL), lambda i, be, fb, nx, nu: (be[i], 0, 0)),
            ],
            out_specs=pl.BlockSpec((tm, D_MODEL), lambda i, be, fb, nx, nu: (i, 0)),
            scratch_shapes=[pltpu.VMEM((D_MODEL, 2 * D_FF), F32), pltpu.VMEM((D_FF, D_MODEL), F32),
                            pltpu.VMEM((D_MODEL, 2 * D_FF), BF16), pltpu.VMEM((D_FF, D_MODEL), BF16),
                            pltpu.SemaphoreType.DMA((2,))],
        ),
        out_shape=jax.ShapeDtypeStruct((n_blocks * tm, D_MODEL), BF16),
        compiler_params=_vmem_limit(48),
        name="experts",
    )(block_expert, first_block, next_expert, n_used, xs, w_gate_up, bgu, w_down, bd)


def _combine_kernel(fetch_ref, shift_ref, nch_ref, x1_ref, rt_ref, gt_ref, y_ref, o_ref,
                    buf, xbuf, sem, xsem):
    tc = ROUTE_ROWS
    ne = N_EXPERTS
    rb = RUN_ROWS
    grp = RUN_GROUP_ROWS // rb
    tile = pl.program_id(0)
    base = tile * ne
    slot = tile % 2

    def run_copy(tile_base, buf_slot, e):
        start = pl.multiple_of(fetch_ref[tile_base + e], BF16_TILE_ROWS)
        return pltpu.make_async_copy(y_ref.at[pl.ds(start, rb)], buf.at[buf_slot, pl.ds(e * rb, rb)],
                                     sem.at[buf_slot, e // grp])

    @pl.when(tile == 0)
    def _():
        for e in range(ne):
            run_copy(base, slot, e).start()

    @pl.when(tile + 1 < pl.num_programs(0))
    def _():
        for e in range(ne):
            run_copy(base + ne, 1 - slot, e).start()

    riota = lax.broadcasted_iota(jnp.int32, (rb, tc), 0)

    def select(e_slice, shift):
        return jnp.where(riota == rt_ref[e_slice, :] - shift, gt_ref[e_slice, :], 0.0)

    acc = x1_ref[...]
    for g in range(ne // grp):
        parts = []
        for e in range(g * grp, (g + 1) * grp):
            run_copy(base, slot, e).wait()
            parts.append(select(slice(e, e + 1), shift_ref[base + e]))
        pt = jnp.concatenate(parts, axis=0).astype(BF16)
        acc = acc + _dot_tn(pt, buf[slot, g * RUN_GROUP_ROWS:(g + 1) * RUN_GROUP_ROWS, :])
    o_ref[...] = acc

    def per_expert(e, carry):
        def chunk(ch, c):
            start = pl.multiple_of(fetch_ref[base + e] + ch * rb, BF16_TILE_ROWS)
            cp = pltpu.make_async_copy(y_ref.at[pl.ds(start, rb)], xbuf, xsem)
            cp.start()
            cp.wait()
            pt = select(pl.ds(e, 1), shift_ref[base + e] + ch * rb).astype(BF16)
            o_ref[...] += _dot_tn(pt, xbuf[...])
            return c

        lax.fori_loop(1, nch_ref[base + e], chunk, 0)
        return carry

    lax.fori_loop(0, ne, per_expert, 0)


def _combine(fetch, shift, nch, x1, rt, gt, y):
    T = x1.shape[0]
    tc = ROUTE_ROWS
    ne = N_EXPERTS
    return pl.pallas_call(
        _combine_kernel,
        grid_spec=pltpu.PrefetchScalarGridSpec(
            num_scalar_prefetch=3,
            grid=(T // tc,),
            in_specs=[pl.BlockSpec((tc, D_MODEL), lambda i, f, s, n: (i, 0)),
                      pl.BlockSpec((ne, tc), lambda i, f, s, n: (0, i)),
                      pl.BlockSpec((ne, tc), lambda i, f, s, n: (0, i)),
                      pl.BlockSpec(memory_space=pl.ANY)],
            out_specs=pl.BlockSpec((tc, D_MODEL), lambda i, f, s, n: (i, 0)),
            scratch_shapes=[pltpu.VMEM((2, ne * RUN_ROWS, D_MODEL), BF16), pltpu.VMEM((RUN_ROWS, D_MODEL), BF16),
                            pltpu.SemaphoreType.DMA((2, ne * RUN_ROWS // RUN_GROUP_ROWS)),
                            pltpu.SemaphoreType.DMA(())],
        ),
        out_shape=jax.ShapeDtypeStruct((T, D_MODEL), F32),
        compiler_params=_vmem_limit(32),
        name="combine",
    )(fetch, shift, nch, x1, rt, gt, y)


def _routing_tables(base, counts, T):
    tm = EXPERT_ROWS
    ne = N_EXPERTS
    n_tiles = T // ROUTE_ROWS
    n_blocks = -(-(T * TOP_K + ne * RUN_ROWS) // tm) + ne
    n_slots = n_blocks * tm
    sizes = counts[:, 0].astype(jnp.int32)
    psizes = (sizes + RUN_ROWS + tm - 1) // tm * tm
    pends = jnp.cumsum(psizes)
    pstarts = pends - psizes
    n_used = pends[-1:]
    padstart = jnp.concatenate([pstarts + sizes, n_used])
    npad = jnp.concatenate([psizes - sizes, n_slots - n_used])
    nb = n_blocks + 1
    blk = jnp.arange(nb, dtype=jnp.int32)
    block_expert = jnp.minimum(jnp.sum(pends[None, :] <= (blk * tm)[:, None], axis=1), ne - 1).astype(jnp.int32)
    first_block = jnp.concatenate([jnp.ones((1,), jnp.int32),
                                   (block_expert[1:] != block_expert[:-1]).astype(jnp.int32)])
    starts_group = jnp.logical_and(first_block == 1, blk * tm < n_used)
    pos = jnp.where(starts_group, blk, nb)
    later = jnp.concatenate([lax.cummin(pos[::-1])[::-1][1:], jnp.full((1,), nb, jnp.int32)])
    next_expert = jnp.where(later < nb, block_expert[jnp.minimum(later, nb - 1)], -1).astype(jnp.int32)
    tile_base = base[:, 0].astype(jnp.int32).reshape(n_tiles, ne)
    run_len = jnp.concatenate([tile_base[1:], sizes[None, :]], axis=0) - tile_base
    run_start = pstarts[None, :] + tile_base
    fetch = run_start // BF16_TILE_ROWS * BF16_TILE_ROWS
    shift = fetch - pstarts[None, :]
    nch = jnp.where(run_len > 0, (run_start - fetch + run_len + RUN_ROWS - 1) // RUN_ROWS, 0)
    flat = lambda a: a.reshape(n_tiles * ne).astype(jnp.int32)
    return (flat(run_start), flat(run_len), flat(tile_base), padstart, npad, n_slots,
            block_expert, first_block, next_expert, n_used, flat(fetch), flat(shift), flat(nch))


def kernel(x, norm1_g, w_in, q_norm_g, k_norm_g, conv_w, conv_b, w_mq, w_mk, b_igate, b_fgate, mh_norm_g,
           w_out, norm2_g, w_router, b_router, w_gate_up, b_gate_up, w_down, b_down):
    B, S, D = x.shape
    T = B * S
    assert D == D_MODEL and S % ATTN_ROWS == 0 and norm1_g.shape[0] == 1
    x2 = x.reshape(T, D)
    q, k, v, xm, vm, om, gates = _in_proj(x2, norm1_g[0], w_in[0], q_norm_g[0], k_norm_g[0])
    attn = _attention(q, k, v, B, S)
    ml = _mlstm(xm, vm, om, gates, conv_w[0], conv_b[0], w_mq[0], w_mk[0], b_igate[0], b_fgate[0],
                mh_norm_g[0], B, S)
    x1, h2, rt, gt, base, counts = _out_route(attn, ml, x2, w_out[0], norm2_g[0], w_router[0], b_router[0])
    (run_start, run_len, tile_base, padstart, npad, n_slots, block_expert, first_block, next_expert, n_used,
     fetch, shift, nch) = _routing_tables(base, counts, T)
    xs = _dispatch(h2, rt, run_start, run_len, tile_base, padstart, npad, n_slots)
    y = _experts(xs, block_expert, first_block, next_expert, n_used, w_gate_up[0], b_gate_up[0], w_down[0],
                 b_down[0])
    out = _combine(fetch, shift, nch, x1, rt, gt, y)
    return out.reshape(B, S, D)
```

```python
import functools

import jax
import jax.numpy as jnp
from jax import lax
from jax.experimental import pallas as pl
from jax.experimental.pallas import tpu as pltpu

F32 = jnp.float32
BF16 = jnp.bfloat16
NEG_INF = float("-inf")

D_MODEL = 1024
D_ATTN = 512
HEAD_DIM_ATTN = 64
N_HEADS_ATTN = 8
D_MLSTM = 512
HEAD_DIM_MLSTM = 128
N_HEADS_MLSTM = 4
CONV_K = 4
N_EXPERTS = 32
TOP_K = 4
D_FF = 1024
SWIGLU_LIMIT = 7.0
SWIGLU_ALPHA = 1.702
EPS = 1e-6
DILATIONS = (1, 4, 16)
BRANCH_SPAN = 128

LANES = 128
SUBLANES = 8

IN_PROJ_ROWS = 256
ATTN_ROWS = 2048
MLSTM_CHUNK = 256
ROUTE_ROWS = 256
ROUTE_TILES_PER_STEP = 2
EXPERT_ROWS = 256
FF_CHUNK = 512
CAST_ROWS = 128
DISPATCH_ROWS = 256
HEADNORM_LANES = 256
BF16_TILE_ROWS = 16
RUN_ROWS = 64
RUN_GROUP_ROWS = 256
NOT_ROUTED = -1.0e9


def _vmem_limit(mib):
    return pltpu.CompilerParams(vmem_limit_bytes=mib * 1024 * 1024)


def _split3(a):
    p1 = a.astype(BF16)
    r1 = a - p1.astype(F32)
    p2 = r1.astype(BF16)
    r2 = r1 - p2.astype(F32)
    return p1, p2, r2.astype(BF16)


def _dot(a, b):
    return jnp.dot(a, b, preferred_element_type=F32)


def _dot_nt(a, b):
    return lax.dot_general(a, b, (((1,), (1,)), ((), ())), preferred_element_type=F32)


def _dot_tn(a, b):
    return lax.dot_general(a, b, (((0,), (0,)), ((), ())), preferred_element_type=F32)


def _in_proj_kernel(x_ref, g1_ref, wqkv_ref, wm_ref, wgc_ref, qg_ref, kg_ref, hsum_ref,
                    q_ref, k_ref, v_ref, xm_ref, vm_ref, om_ref, gates_ref):
    x = x_ref[...]
    h = x * lax.rsqrt(jnp.mean(x * x, axis=-1, keepdims=True) + EPS) * g1_ref[...]
    hb = h.astype(BF16)
    hl = (h - hb.astype(F32)).astype(BF16)
    gw = HEADNORM_LANES

    def head_norm(z, g_ref):
        parts = []
        for c in range(D_ATTN // gw):
            zc = z[:, c * gw:(c + 1) * gw]
            ms = _dot((zc * zc).astype(BF16), hsum_ref[...]) * (1.0 / HEAD_DIM_ATTN)
            parts.append(zc * lax.rsqrt(ms + EPS))
        return jnp.concatenate(parts, axis=-1) * g_ref[...]

    zq = _dot(hb, wqkv_ref[:, 0:D_ATTN])
    q_ref[...] = head_norm(zq, qg_ref) * (HEAD_DIM_ATTN ** -0.5)
    zk = _dot(hb, wqkv_ref[:, D_ATTN:2 * D_ATTN])
    k_ref[...] = head_norm(zk, kg_ref)
    v_ref[...] = _dot(hb, wqkv_ref[:, 2 * D_ATTN:3 * D_ATTN])
    xm_ref[...] = _dot(hb, wm_ref[:, 0:D_MLSTM])
    vm_ref[...] = _dot(hb, wm_ref[:, D_MLSTM:2 * D_MLSTM])
    om_ref[...] = _dot(hb, wm_ref[:, 2 * D_MLSTM:3 * D_MLSTM])
    ng = 2 * N_HEADS_MLSTM
    gsum = _dot(hb, wgc_ref[...]) + _dot(hl, wgc_ref[...])
    gsum = gsum + pltpu.roll(gsum, LANES - ng, axis=1)
    lane = lax.broadcasted_iota(jnp.int32, gsum.shape, 1)
    gates_ref[...] = jnp.where(lane < ng, gsum, 0.0)


def _in_proj(x2, g1, w_in, q_g, k_g):
    T = x2.shape[0]
    tm = IN_PROJ_ROWS
    n_qkv = 3 * D_ATTN
    n_m = 3 * D_MLSTM
    ng = 2 * N_HEADS_MLSTM
    wqkv = w_in[:, :n_qkv].astype(BF16)
    wm = w_in[:, n_qkv:n_qkv + n_m].astype(BF16)
    wg = w_in[:, n_qkv + n_m:]
    wgh = wg.astype(BF16)
    wgl = (wg - wgh.astype(F32)).astype(BF16)
    wgc = jnp.pad(jnp.concatenate([wgh, wgl], axis=1), ((0, 0), (0, LANES - 2 * ng)))
    lane = jnp.arange(HEADNORM_LANES)
    hsum = (lane[:, None] // HEAD_DIM_ATTN == lane[None, :] // HEAD_DIM_ATTN).astype(BF16)
    qg = jnp.tile(q_g, N_HEADS_ATTN)[None, :]
    kg = jnp.tile(k_g, N_HEADS_ATTN)[None, :]
    row = lambda n: pl.BlockSpec((tm, n), lambda i: (i, 0))
    full = lambda a: pl.BlockSpec(a.shape, lambda i: (0,) * a.ndim)
    outs = [jax.ShapeDtypeStruct((T, D_ATTN), F32)] * 3 + [jax.ShapeDtypeStruct((T, D_MLSTM), F32)] * 3
    outs.append(jax.ShapeDtypeStruct((T, LANES), F32))
    g1r = g1[None, :]
    return pl.pallas_call(
        _in_proj_kernel,
        grid=(T // tm,),
        in_specs=[row(D_MODEL), full(g1r), full(wqkv), full(wm), full(wgc), full(qg), full(kg), full(hsum)],
        out_specs=[row(D_ATTN)] * 3 + [row(D_MLSTM)] * 3 + [row(LANES)],
        out_shape=outs,
        compiler_params=_vmem_limit(48),
        name="in_proj",
    )(x2, g1r, wqkv, wm, wgc, qg, kg, hsum)


def _attn_tiles():
    tiles = []
    for br, d in enumerate(DILATIONS):
        group = d * BRANCH_SPAN
        for u in range(ATTN_ROWS // group):
            for c in range(d):
                tiles.append((br, d, u * group + c))
    return tiles


def _attn_kernel(slope_ref, q_ref, kp_ref, kc_ref, vp_ref, vc_ref, o_ref,
                 kwin, vwin, bias_scr, m_scr, l_scr, acc_scr):
    W = ATTN_ROWS
    n = BRANCH_SPAN
    step = pl.program_id(2)
    kwin[0:W, :] = kp_ref[...]
    kwin[W:2 * W, :] = kc_ref[...]
    vwin[0:W, :] = vp_ref[...]
    vwin[W:2 * W, :] = vc_ref[...]

    lane = lax.broadcasted_iota(jnp.int32, (n, LANES), 1)
    first_head = lane < HEAD_DIM_ATTN
    row = lax.broadcasted_iota(jnp.int32, (n, 2 * n), 0)
    col = lax.broadcasted_iota(jnp.int32, (n, 2 * n), 1)
    j = n + row - col
    valid = jnp.logical_and(j >= 0, j <= n)
    jf = j.astype(F32)
    for br, d in enumerate(DILATIONS):
        for hh in range(2):
            slope = slope_ref[:, hh * HEAD_DIM_ATTN:hh * HEAD_DIM_ATTN + 1]
            b = jnp.where(valid, -(slope * float(d)) * jf, NEG_INF)
            bias_scr[br, hh, 0] = b
            bias_scr[br, hh, 1] = jnp.where(col >= n, b, NEG_INF)

    for br, d, qs in _attn_tiles():
        lo = W + qs - n * d
        q = q_ref[pl.ds(qs, n, stride=d), :]
        kk = jnp.concatenate([kwin[pl.ds(lo, n, stride=d), :], kwin[pl.ds(W + qs, n, stride=d), :]],
                             axis=0).astype(BF16)
        vv = jnp.concatenate([vwin[pl.ds(lo, n, stride=d), :], vwin[pl.ds(W + qs, n, stride=d), :]],
                             axis=0).astype(BF16)
        if lo < W:
            sel = jnp.where(step == 0, 1, 0)
        else:
            sel = 0
        res = []
        for hh in range(2):
            qm = jnp.where(first_head if hh == 0 else jnp.logical_not(first_head), q, 0.0).astype(BF16)
            s = _dot_nt(qm, kk) + bias_scr[br, hh, sel]
            m = jnp.max(s, axis=-1, keepdims=True)
            p = jnp.exp(s - m)
            l = jnp.sum(p, axis=-1, keepdims=True)
            res.append((m, l, _dot(p.astype(BF16), vv)))
        rows = pl.ds(qs, n, stride=d)
        m_scr[br, rows, :] = jnp.where(first_head, res[0][0], res[1][0])
        l_scr[br, rows, :] = jnp.where(first_head, res[0][1], res[1][1])
        acc_scr[br, rows, :] = jnp.where(first_head, res[0][2], res[1][2])

    chunk = 256

    def combine(i, carry):
        r = pl.ds(pl.multiple_of(i * chunk, chunk), chunk)
        m0, m1, m2 = m_scr[0, r, :], m_scr[1, r, :], m_scr[2, r, :]
        mm = jnp.maximum(jnp.maximum(m0, m1), m2)
        w0, w1, w2 = jnp.exp(m0 - mm), jnp.exp(m1 - mm), jnp.exp(m2 - mm)
        num = w0 * acc_scr[0, r, :] + w1 * acc_scr[1, r, :] + w2 * acc_scr[2, r, :]
        den = w0 * l_scr[0, r, :] + w1 * l_scr[1, r, :] + w2 * l_scr[2, r, :]
        o_ref[r, :] = num / den
        return carry

    lax.fori_loop(0, W // chunk, combine, 0)


def _attention(q, k, v, B, S):
    W = ATTN_ROWS
    nb = S // W
    n_pairs = D_ATTN // LANES
    slopes = jnp.exp2(-8.0 * jnp.arange(1, N_HEADS_ATTN + 1, dtype=F32) / N_HEADS_ATTN)
    slope_l = jnp.repeat(slopes, HEAD_DIM_ATTN).reshape(n_pairs, 1, LANES)
    cur = pl.BlockSpec((W, LANES), lambda b, p, i: (b * nb + i, p))
    prev = pl.BlockSpec((W, LANES), lambda b, p, i: (b * nb + jnp.maximum(i - 1, 0), p))
    n = BRANCH_SPAN
    nbr = len(DILATIONS)
    return pl.pallas_call(
        _attn_kernel,
        grid=(B, n_pairs, nb),
        in_specs=[pl.BlockSpec((None, 1, LANES), lambda b, p, i: (p, 0, 0)), cur, prev, cur, prev, cur],
        out_specs=cur,
        out_shape=jax.ShapeDtypeStruct((B * S, D_ATTN), F32),
        scratch_shapes=[
            pltpu.VMEM((2 * W, LANES), F32), pltpu.VMEM((2 * W, LANES), F32),
            pltpu.VMEM((nbr, 2, 2, n, 2 * n), F32),
            pltpu.VMEM((nbr, W, LANES), F32), pltpu.VMEM((nbr, W, LANES), F32),
            pltpu.VMEM((nbr, W, LANES), F32),
        ],
        compiler_params=_vmem_limit(48),
        name="attention",
    )(slope_l, q, k, k, v, v)


def _mlstm_kernel(xm_all, vm_all, om_all, gcol_all, grow_all, cw_ref, cb_ref, wq_ref, wk_ref,
                  bcol_ref, brow_ref, g_ref, o_all, xs_all, xtail_all, c_all, n_all, m_all):
    @pl.when(pl.program_id(0) == 0)
    def _():
        xtail_all[...] = jnp.zeros_like(xtail_all)
        c_all[...] = jnp.zeros_like(c_all)
        n_all[...] = jnp.zeros_like(n_all)
        m_all[...] = jnp.zeros_like(m_all)

    for b in range(xm_all.shape[0]):
        _mlstm_chunk(xm_all.at[b], vm_all.at[b], om_all.at[b], gcol_all.at[b], grow_all.at[b], cw_ref, cb_ref,
                     wq_ref, wk_ref, bcol_ref, brow_ref, g_ref, o_all.at[b], xs_all.at[b], xtail_all.at[b],
                     c_all.at[b], n_all.at[b], m_all.at[b])


def _mlstm_chunk(xm_ref, vm_ref, om_ref, gcol_ref, grow_ref, cw_ref, cb_ref, wq_ref, wk_ref,
                 bcol_ref, brow_ref, g_ref, o_ref, xs_ref, xtail_ref, c_ref, n_ref, m_ref):
    L = MLSTM_CHUNK
    dh = HEAD_DIM_MLSTM
    nh = N_HEADS_MLSTM

    x = xm_ref[...]
    xs_ref[0:SUBLANES, :] = xtail_ref[...]
    xs_ref[SUBLANES:SUBLANES + L, :] = x
    xtail_ref[...] = x[L - SUBLANES:L, :]
    xc = cb_ref[...] + x * cw_ref[CONV_K - 1:CONV_K, :]
    for back in range(1, CONV_K):
        xc = xc + xs_ref[SUBLANES - back:SUBLANES - back + L, :] * cw_ref[CONV_K - 1 - back:CONV_K - back, :]
    xc = xc * jax.nn.sigmoid(xc)

    gc = gcol_ref[...] + bcol_ref[...]
    gr = grow_ref[...] + brow_ref[...]
    ri = lax.broadcasted_iota(jnp.int32, (L, L), 0)
    ci = lax.broadcasted_iota(jnp.int32, (L, L), 1)
    causal = ri >= ci
    tril = jnp.where(causal, 1.0, 0.0).astype(BF16)
    triu = jnp.where(ri <= ci, 1.0, 0.0).astype(BF16)
    c1, c2, c3 = _split3(jax.nn.log_sigmoid(gc))
    bc = _dot(tril, c1) + _dot(tril, c2) + _dot(tril, c3)
    r1, r2, r3 = _split3(jax.nn.log_sigmoid(gr))
    brw = _dot(r1, triu) + _dot(r2, triu) + _dot(r3, triu)

    for h in range(nh):
        hs = slice(h * dh, (h + 1) * dh)
        b_col = bc[:, nh + h:nh + h + 1]
        a_col = gc[:, h:h + 1] - b_col
        a_row = gr[h:h + 1, :] - brw[nh + h:nh + h + 1, :]
        m_prev = m_ref[h:h + 1, 0:1]
        log_d = jnp.where(causal, b_col + a_row, NEG_INF)
        log_inter = b_col + m_prev
        m_t = jnp.maximum(log_inter, jnp.max(log_d, axis=-1, keepdims=True))
        d = jnp.exp(log_d - m_t)
        inter = jnp.exp(log_inter - m_t)

        xh = xc[:, hs].astype(BF16)
        qf = _dot(xh, wq_ref[h])
        kf = _dot(xh, wk_ref[h]) * (dh ** -0.5)
        qb = qf.astype(BF16)
        vb = vm_ref[:, hs].astype(BF16)
        s = _dot_nt(qb, kf.astype(BF16)) * d
        c_old = c_ref[h]
        n_old = n_ref[h:h + 1, :]
        num = _dot(s.astype(BF16), vb) + inter * _dot(qb, c_old.astype(BF16))
        den = jnp.sum(s, axis=-1, keepdims=True) + inter * jnp.sum(qf * n_old, axis=-1, keepdims=True)
        hh = num / jnp.maximum(jnp.abs(den), jnp.exp(-m_t))

        b_last = b_col[L - 1:L, :]
        log_w = b_last + a_col
        m_new = jnp.maximum(b_last + m_prev, jnp.max(log_w, axis=0, keepdims=True))
        kw = kf * jnp.exp(log_w - m_new)
        decay = jnp.exp(b_last + m_prev - m_new)
        c_ref[h] = decay * c_old + _dot_tn(kw.astype(BF16), vb)
        n_ref[h:h + 1, :] = decay * n_old + jnp.sum(kw, axis=0, keepdims=True)
        m_ref[h:h + 1, :] = jnp.broadcast_to(m_new, (1, LANES))

        hn = hh * lax.rsqrt(jnp.mean(hh * hh, axis=-1, keepdims=True) + EPS) * g_ref[:, hs]
        o_ref[:, hs] = jax.nn.sigmoid(om_ref[:, hs]) * hn


def _mlstm(xm, vm, om, gates, conv_w, conv_b, w_mq, w_mk, b_i, b_f, mh_g, B, S):
    L = MLSTM_CHUNK
    nc = S // L
    nh = N_HEADS_MLSTM
    ng = 2 * nh
    grow = gates[:, :ng].reshape(B, S, ng).transpose(0, 2, 1)
    bias = jnp.concatenate([b_i, b_f])
    bcol = jnp.pad(bias, (0, LANES - ng))[None, :]
    brow = bias[:, None]
    wq = w_mq.astype(BF16)
    wk = w_mk.astype(BF16)
    cb = conv_b[None, :]
    g = mh_g.reshape(1, D_MLSTM)
    rows = lambda n: pl.BlockSpec((B, L, n), lambda c: (0, c, 0))
    full = lambda a: pl.BlockSpec(a.shape, lambda c: (0,) * a.ndim)
    seq = lambda a: a.reshape(B, S, a.shape[-1])
    out = pl.pallas_call(
        _mlstm_kernel,
        grid=(nc,),
        in_specs=[rows(D_MLSTM), rows(D_MLSTM), rows(D_MLSTM), rows(LANES),
                  pl.BlockSpec((B, ng, L), lambda c: (0, 0, c)),
                  full(conv_w), full(cb), full(wq), full(wk), full(bcol), full(brow), full(g)],
        out_specs=rows(D_MLSTM),
        out_shape=jax.ShapeDtypeStruct((B, S, D_MLSTM), F32),
        scratch_shapes=[
            pltpu.VMEM((B, L + SUBLANES, D_MLSTM), F32), pltpu.VMEM((B, SUBLANES, D_MLSTM), F32),
            pltpu.VMEM((B, nh, HEAD_DIM_MLSTM, HEAD_DIM_MLSTM), F32),
            pltpu.VMEM((B, SUBLANES, LANES), F32), pltpu.VMEM((B, SUBLANES, LANES), F32),
        ],
        compiler_params=_vmem_limit(32),
        name="mlstm",
    )(seq(xm), seq(vm), seq(om), seq(gates), grow, conv_w, cb, wq, wk, bcol, brow, g)
    return out.reshape(B * S, D_MLSTM)


def _out_route_kernel(attn_ref, ml_ref, x_ref, wo_ref, g2_ref, wrh_ref, wrl_ref, br_ref, earlier_ref,
                      x1_ref, h2_ref, rt_ref, gt_ref, base_ref, cnt_ref, carry_ref):
    @pl.when(pl.program_id(0) == 0)
    def _():
        carry_ref[...] = jnp.zeros_like(carry_ref)

    carry = carry_ref[...]
    for j in range(ROUTE_TILES_PER_STEP):
        carry = _route_tile(j, carry, attn_ref, ml_ref, x_ref, wo_ref, g2_ref, wrh_ref, wrl_ref, br_ref,
                            earlier_ref, x1_ref, h2_ref, rt_ref, gt_ref, base_ref)
    carry_ref[...] = carry
    cnt_ref[...] = carry


def _route_tile(j, carry, attn_ref, ml_ref, x_ref, wo_ref, g2_ref, wrh_ref, wrl_ref, br_ref, earlier_ref,
                x1_ref, h2_ref, rt_ref, gt_ref, base_ref):
    tm = ROUTE_ROWS
    ne = N_EXPERTS
    rows = slice(j * tm, (j + 1) * tm)

    x1 = (x_ref[rows, :] + _dot(attn_ref[rows, :].astype(BF16), wo_ref[0:D_ATTN, :])
          + _dot(ml_ref[rows, :].astype(BF16), wo_ref[D_ATTN:D_ATTN + D_MLSTM, :]))
    x1_ref[rows, :] = x1
    h2 = x1 * lax.rsqrt(jnp.mean(x1 * x1, axis=-1, keepdims=True) + EPS) * g2_ref[...]
    hb = h2.astype(BF16)
    h2_ref[rows, :] = hb
    hl = (h2 - hb.astype(F32)).astype(BF16)
    lt = (_dot_nt(wrh_ref[...], hb) + _dot_nt(wrh_ref[...], hl) + _dot_nt(wrl_ref[...], hb)) + br_ref[...]

    eidx = lax.broadcasted_iota(jnp.int32, (ne, tm), 0)
    vals, hots, idxs = [], [], []
    work = lt
    for _ in range(TOP_K):
        mx = jnp.max(work, axis=0, keepdims=True)
        idx = jnp.min(jnp.where(work == mx, eidx, ne), axis=0, keepdims=True)
        hot = eidx == idx
        vals.append(mx)
        hots.append(hot)
        idxs.append(idx)
        work = jnp.where(hot, NEG_INF, work)
    ex = [jnp.exp(v - vals[0]) for v in vals]
    tot = ex[0] + ex[1] + ex[2] + ex[3]

    onehot = jnp.zeros((ne, tm), F32)
    gt = jnp.zeros((ne, tm), F32)
    for kk in range(TOP_K):
        onehot = onehot + jnp.where(hots[kk], 1.0, 0.0)
        gt = gt + jnp.where(hots[kk], ex[kk] / tot, 0.0)
    before = _dot(onehot.astype(BF16), earlier_ref[...]) + carry[:, 0:1]
    base_ref[j * ne:(j + 1) * ne, :] = carry
    rt_ref[:, rows] = jnp.where(onehot > 0.5, before, NOT_ROUTED).astype(jnp.int32)
    gt_ref[:, rows] = gt
    return carry + jnp.sum(onehot, axis=1, keepdims=True)


def _out_route(attn, ml, x2, w_out, g2, w_router, b_router):
    T = x2.shape[0]
    tm = ROUTE_ROWS
    ne = N_EXPERTS
    wo = w_out.astype(BF16)
    wrt = w_router.T
    wrh = wrt.astype(BF16)
    wrl = (wrt - wrh.astype(F32)).astype(BF16)
    br = b_router[:, None]
    g2r = g2[None, :]
    tok = jnp.arange(tm)
    earlier = (tok[:, None] < tok[None, :]).astype(BF16)
    ts = tm * ROUTE_TILES_PER_STEP
    row = lambda n: pl.BlockSpec((ts, n), lambda i: (i, 0))
    col = lambda n: pl.BlockSpec((n, ts), lambda i: (0, i))
    full = lambda a: pl.BlockSpec(a.shape, lambda i: (0,) * a.ndim)
    return pl.pallas_call(
        _out_route_kernel,
        grid=(T // ts,),
        in_specs=[row(D_ATTN), row(D_MLSTM), row(D_MODEL), full(wo), full(g2r), full(wrh), full(wrl), full(br),
                  full(earlier)],
        out_specs=[row(D_MODEL), row(D_MODEL), col(ne), col(ne),
                   pl.BlockSpec((ROUTE_TILES_PER_STEP * ne, LANES), lambda i: (i, 0)),
                   pl.BlockSpec((ne, LANES), lambda i: (0, 0))],
        out_shape=[jax.ShapeDtypeStruct((T, D_MODEL), F32), jax.ShapeDtypeStruct((T, D_MODEL), BF16),
                   jax.ShapeDtypeStruct((ne, T), jnp.int32), jax.ShapeDtypeStruct((ne, T), F32),
                   jax.ShapeDtypeStruct((T // tm * ne, LANES), F32), jax.ShapeDtypeStruct((ne, LANES), F32)],
        scratch_shapes=[pltpu.VMEM((ne, LANES), F32)],
        compiler_params=_vmem_limit(32),
        name="out_route",
    )(attn, ml, x2, wo, g2r, wrh, wrl, br, earlier)


def _row_copy(src_ref, dst_ref, sem, src_row, dst_row):
    return pltpu.make_async_copy(src_ref.at[pl.ds(src_row, 1)], dst_ref.at[pl.ds(dst_row, 1)], sem)


def _dispatch_kernel(runstart_ref, runlen_ref, lshift_ref, padstart_ref, npad_ref,
                     h_ref, rt_ref, xs_ref, cbuf, xbuf, zero_ref, stage_ref, sem, xsem, zsem):
    td = DISPATCH_ROWS
    ne = N_EXPERTS
    rb = RUN_ROWS
    half = rb // 2
    ta = BF16_TILE_ROWS
    grp = RUN_GROUP_ROWS // rb
    tile = pl.program_id(0)

    def window(i):
        off = runstart_ref[i] & (ta - 1)
        return off, off + runlen_ref[i]

    def first_chunk(tile_idx, e, action):
        i = tile_idx * ne + e
        off, total = window(i)
        dst = pl.multiple_of(runstart_ref[i] - off, ta)
        action(pltpu.make_async_copy(cbuf.at[pl.ds(e * rb, half)], xs_ref.at[pl.ds(dst, half)], sem))

        @pl.when(total > half)
        def _():
            action(pltpu.make_async_copy(cbuf.at[pl.ds(e * rb + half, half)],
                                         xs_ref.at[pl.ds(pl.multiple_of(dst + half, ta), half)], sem))

    @pl.when(tile > 0)
    def _():
        for e in range(ne):
            first_chunk(tile - 1, e, lambda cp: cp.wait())

    @pl.when(tile == 0)
    def _():
        stage_ref[...] = jnp.zeros_like(stage_ref)

    hb = h_ref[...]
    riota = lax.broadcasted_iota(jnp.int32, (rb, td), 0)
    rowt = lax.broadcasted_iota(jnp.int32, (ta, D_MODEL), 0)

    def select(e_slice, shift):
        return jnp.where(riota == rt_ref[e_slice, :] - shift, 1.0, 0.0)

    for g in range(ne // grp):
        experts = range(g * grp, (g + 1) * grp)
        parts = [select(slice(e, e + 1), lshift_ref[tile * ne + e] - window(tile * ne + e)[0]) for e in experts]
        pt = jnp.concatenate(parts, axis=0).astype(BF16)
        cbuf[g * RUN_GROUP_ROWS:(g + 1) * RUN_GROUP_ROWS, :] = _dot(pt, hb).astype(BF16)

    for e in range(ne):
        i = tile * ne + e
        off, total = window(i)
        head = pl.ds(e * rb, ta)
        old = stage_ref[e]
        cbuf[head, :] = jnp.where(rowt < off, old, cbuf[head, :])
        last = e * rb + jnp.minimum(total // ta * ta, rb - ta)
        keep = jnp.logical_and(runlen_ref[i] > 0, total < rb)
        stage_ref[e] = jnp.where(keep, cbuf[pl.ds(pl.multiple_of(last, ta), ta), :], old)

    for e in range(ne):
        first_chunk(tile, e, lambda cp: cp.start())

    def per_expert(e, carry):
        i = tile * ne + e
        off, total = window(i)

        def chunk(ch, c):
            pt = select(pl.ds(e, 1), lshift_ref[i] - off + ch * rb).astype(BF16)
            xbuf[...] = _dot(pt, hb).astype(BF16)
            dst = pl.multiple_of(runstart_ref[i] - off + ch * rb, ta)
            cp = pltpu.make_async_copy(xbuf, xs_ref.at[pl.ds(dst, rb)], xsem)
            cp.start()

            @pl.when(jnp.logical_and(ch == total // rb, total % rb != 0))
            def _():
                last = pl.multiple_of(total % rb // ta * ta, ta)
                stage_ref[e] = xbuf[pl.ds(last, ta), :]

            cp.wait()
            return c

        lax.fori_loop(1, (total + rb - 1) // rb, chunk, 0)
        return carry

    lax.fori_loop(0, ne, per_expert, 0)

    @pl.when(tile == pl.num_programs(0) - 1)
    def _():
        for e in range(ne):
            first_chunk(tile, e, lambda cp: cp.wait())

    @pl.when(tile == pl.num_programs(0) - 1)
    def _():
        zero_ref[...] = jnp.zeros_like(zero_ref)

        def fill(wait):
            def per_run(e, carry):
                start = padstart_ref[e]
                head = (-start) & (ta - 1)
                off = pl.multiple_of(start + head, ta)
                left = npad_ref[e] - head

                def whole(b, c):
                    dst = xs_ref.at[pl.ds(pl.multiple_of(off + b * td, ta), td)]
                    cp = pltpu.make_async_copy(zero_ref, dst, zsem)
                    cp.wait() if wait else cp.start()
                    return c

                n_whole = left // td
                lax.fori_loop(0, n_whole, whole, 0)
                off = pl.multiple_of(off + n_whole * td, ta)
                p = td // 2
                while p >= ta:
                    take = (left & p) != 0

                    @pl.when(take)
                    def _(off=off, p=p):
                        cp = pltpu.make_async_copy(zero_ref.at[pl.ds(0, p)], xs_ref.at[pl.ds(off, p)], zsem)
                        cp.wait() if wait else cp.start()

                    off = pl.multiple_of(off + jnp.where(take, p, 0), ta)
                    p //= 2
                return carry

            lax.fori_loop(0, N_EXPERTS + 1, per_run, 0)

        fill(wait=False)
        fill(wait=True)


def _dispatch(h2, rt, runstart, runlen, lshift, padstart, npad, n_slots):
    T = h2.shape[0]
    td = DISPATCH_ROWS
    ne = N_EXPERTS
    assert td == ROUTE_ROWS
    return pl.pallas_call(
        _dispatch_kernel,
        grid_spec=pltpu.PrefetchScalarGridSpec(
            num_scalar_prefetch=5,
            grid=(T // td,),
            in_specs=[pl.BlockSpec((td, D_MODEL), lambda i, *_: (i, 0)),
                      pl.BlockSpec((ne, td), lambda i, *_: (0, i))],
            out_specs=pl.BlockSpec(memory_space=pl.ANY),
            scratch_shapes=[pltpu.VMEM((ne * RUN_ROWS, D_MODEL), BF16), pltpu.VMEM((RUN_ROWS, D_MODEL), BF16),
                            pltpu.VMEM((td, D_MODEL), BF16), pltpu.VMEM((ne, BF16_TILE_ROWS, D_MODEL), BF16),
                            pltpu.SemaphoreType.DMA(()), pltpu.SemaphoreType.DMA(()),
                            pltpu.SemaphoreType.DMA(())],
        ),
        out_shape=jax.ShapeDtypeStruct((n_slots, D_MODEL), BF16),
        compiler_params=_vmem_limit(32),
        name="dispatch",
    )(runstart, runlen, lshift, padstart, npad, h2, rt)


def _expert_kernel(bexp_ref, first_ref, next_ref, nused_ref, xs_ref, wgu_hbm, bgu_ref, wd_hbm, bd_ref, y_ref,
                   wgu_st, wd_st, wgu_bf, wd_bf, wsem):
    tm = EXPERT_ROWS
    blk = pl.program_id(0)
    active = blk * tm < nused_ref[0]

    def weight_copies(e):
        return (pltpu.make_async_copy(wgu_hbm.at[e], wgu_st, wsem.at[0]),
                pltpu.make_async_copy(wd_hbm.at[e], wd_st, wsem.at[1]))

    @pl.when(blk == 0)
    def _():
        for cp in weight_copies(bexp_ref[0]):
            cp.start()

    @pl.when(jnp.logical_and(active, first_ref[blk] == 1))
    def _():
        for cp in weight_copies(bexp_ref[blk]):
            cp.wait()
        rc = CAST_ROWS

        def cast(i, carry):
            r = pl.ds(pl.multiple_of(i * rc, rc), rc)
            wgu_bf[r, :] = wgu_st[r, :].astype(BF16)
            wd_bf[r, :] = wd_st[r, :].astype(BF16)
            return carry

        lax.fori_loop(0, D_MODEL // rc, cast, 0)

        @pl.when(next_ref[blk] >= 0)
        def _():
            for cp in weight_copies(next_ref[blk]):
                cp.start()

    @pl.when(active)
    def _():
        xb = xs_ref[...]
        acc = jnp.zeros((tm, D_MODEL), F32)
        for f in range(D_FF // FF_CHUNK):
            lo = f * FF_CHUNK
            g = _dot(xb, wgu_bf[:, lo:lo + FF_CHUNK]) + bgu_ref[:, lo:lo + FF_CHUNK]
            u = _dot(xb, wgu_bf[:, D_FF + lo:D_FF + lo + FF_CHUNK]) + bgu_ref[:, D_FF + lo:D_FF + lo + FF_CHUNK]
            g = jnp.minimum(g, SWIGLU_LIMIT)
            u = jnp.clip(u, -SWIGLU_LIMIT, SWIGLU_LIMIT)
            act = g * jax.nn.sigmoid(SWIGLU_ALPHA * g) * (u + 1.0)
            acc = acc + _dot(act.astype(BF16), wd_bf[lo:lo + FF_CHUNK, :])
        y_ref[...] = (acc + bd_ref[...]).astype(y_ref.dtype)

    @pl.when(jnp.logical_not(active))
    def _():
        y_ref[...] = jnp.zeros_like(y_ref)


def _experts(xs, block_expert, first_block, next_expert, n_used, w_gate_up, b_gate_up, w_down, b_down):
    n_blocks = block_expert.shape[0]
    tm = EXPERT_ROWS
    assert D_FF == D_MODEL and xs.shape[0] == (n_blocks - 1) * tm
    bgu = b_gate_up[:, None, :]
    bd = b_down[:, None, :]
    last = n_blocks - 2
    return pl.pallas_call(
        _expert_kernel,
        grid_spec=pltpu.PrefetchScalarGridSpec(
            num_scalar_prefetch=4,
            grid=(n_blocks,),
            in_specs=[
                pl.BlockSpec((tm, D_MODEL), lambda i, be, fb, nx, nu: (jnp.minimum(i, last), 0)),
                pl.BlockSpec(memory_space=pl.ANY),
                pl.BlockSpec((None, 1, 2 * D_FF), lambda i, be, fb, nx, nu: (be[i], 0, 0)),
                pl.BlockSpec(memory_space=pl.ANY),
                pl.BlockSpec((None, 1, D_MODEL), lambda i, be, fb, nx, nu: (be[i], 0, 0)),
            ],
            out_specs=pl.BlockSpec((tm, D_MODEL), lambda i, be, fb, nx, nu: (i, 0)),
            scratch_shapes=[pltpu.VMEM((D_MODEL, 2 * D_FF), F32), pltpu.VMEM((D_FF, D_MODEL), F32),
                            pltpu.VMEM((D_MODEL, 2 * D_FF), BF16), pltpu.VMEM((D_FF, D_MODEL), BF16),
                            pltpu.SemaphoreType.DMA((2,))],
        ),
        out_shape=jax.ShapeDtypeStruct((n_blocks * tm, D_MODEL), BF16),
        compiler_params=_vmem_limit(48),
        name="experts",
    )(block_expert, first_block, next_expert, n_used, xs, w_gate_up, bgu, w_down, bd)


def _combine_kernel(fetch_ref, shift_ref, nch_ref, x1_ref, rt_ref, gt_ref, y_ref, o_ref,
                    buf, xbuf, sem, xsem):
    tc = ROUTE_ROWS
    ne = N_EXPERTS
    rb = RUN_ROWS
    grp = RUN_GROUP_ROWS // rb
    tile = pl.program_id(0)
    base = tile * ne
    slot = tile % 2

    def run_copy(tile_base, buf_slot, e):
        start = pl.multiple_of(fetch_ref[tile_base + e], BF16_TILE_ROWS)
        return pltpu.make_async_copy(y_ref.at[pl.ds(start, rb)], buf.at[buf_slot, pl.ds(e * rb, rb)],
                                     sem.at[buf_slot, e // grp])

    @pl.when(tile == 0)
    def _():
        for e in range(ne):
            run_copy(base, slot, e).start()

    @pl.when(tile + 1 < pl.num_programs(0))
    def _():
        for e in range(ne):
            run_copy(base + ne, 1 - slot, e).start()

    riota = lax.broadcasted_iota(jnp.int32, (rb, tc), 0)

    def select(e_slice, shift):
        return jnp.where(riota == rt_ref[e_slice, :] - shift, gt_ref[e_slice, :], 0.0)

    acc = x1_ref[...]
    for g in range(ne // grp):
        parts = []
        for e in range(g * grp, (g + 1) * grp):
            run_copy(base, slot, e).wait()
            parts.append(select(slice(e, e + 1), shift_ref[base + e]))
        pt = jnp.concatenate(parts, axis=0).astype(BF16)
        acc = acc + _dot_tn(pt, buf[slot, g * RUN_GROUP_ROWS:(g + 1) * RUN_GROUP_ROWS, :])
    o_ref[...] = acc

    def per_expert(e, carry):
        def chunk(ch, c):
            start = pl.multiple_of(fetch_ref[base + e] + ch * rb, BF16_TILE_ROWS)
            cp = pltpu.make_async_copy(y_ref.at[pl.ds(start, rb)], xbuf, xsem)
            cp.start()
            cp.wait()
            pt = select(pl.ds(e, 1), shift_ref[base + e] + ch * rb).astype(BF16)
            o_ref[...] += _dot_tn(pt, xbuf[...])
            return c

        lax.fori_loop(1, nch_ref[base + e], chunk, 0)
        return carry

    lax.fori_loop(0, ne, per_expert, 0)


def _combine(fetch, shift, nch, x1, rt, gt, y):
    T = x1.shape[0]
    tc = ROUTE_ROWS
    ne = N_EXPERTS
    return pl.pallas_call(
        _combine_kernel,
        grid_spec=pltpu.PrefetchScalarGridSpec(
            num_scalar_prefetch=3,
            grid=(T // tc,),
            in_specs=[pl.BlockSpec((tc, D_MODEL), lambda i, f, s, n: (i, 0)),
                      pl.BlockSpec((ne, tc), lambda i, f, s, n: (0, i)),
                      pl.BlockSpec((ne, tc), lambda i, f, s, n: (0, i)),
                      pl.BlockSpec(memory_space=pl.ANY)],
            out_specs=pl.BlockSpec((tc, D_MODEL), lambda i, f, s, n: (i, 0)),
            scratch_shapes=[pltpu.VMEM((2, ne * RUN_ROWS, D_MODEL), BF16), pltpu.VMEM((RUN_ROWS, D_MODEL), BF16),
                            pltpu.SemaphoreType.DMA((2, ne * RUN_ROWS // RUN_GROUP_ROWS)),
                            pltpu.SemaphoreType.DMA(())],
        ),
        out_shape=jax.ShapeDtypeStruct((T, D_MODEL), F32),
        compiler_params=_vmem_limit(32),
        name="combine",
    )(fetch, shift, nch, x1, rt, gt, y)


def _routing_tables(base, counts, T):
    tm = EXPERT_ROWS
    ne = N_EXPERTS
    n_tiles = T // ROUTE_ROWS
    n_blocks = -(-(T * TOP_K + ne * RUN_ROWS) // tm) + ne
    n_slots = n_blocks * tm
    sizes = counts[:, 0].astype(jnp.int32)
    psizes = (sizes + RUN_ROWS + tm - 1) // tm * tm
    pends = jnp.cumsum(psizes)
    pstarts = pends - psizes
    n_used = pends[-1:]
    padstart = jnp.concatenate([pstarts + sizes, n_used])
    npad = jnp.concatenate([psizes - sizes, n_slots - n_used])
    nb = n_blocks + 1
    blk = jnp.arange(nb, dtype=jnp.int32)
    block_expert = jnp.minimum(jnp.sum(pends[None, :] <= (blk * tm)[:, None], axis=1), ne - 1).astype(jnp.int32)
    first_block = jnp.concatenate([jnp.ones((1,), jnp.int32),
                                   (block_expert[1:] != block_expert[:-1]).astype(jnp.int32)])
    starts_group = jnp.logical_and(first_block == 1, blk * tm < n_used)
    pos = jnp.where(starts_group, blk, nb)
    later = jnp.concatenate([lax.cummin(pos[::-1])[::-1][1:], jnp.full((1,), nb, jnp.int32)])
    next_expert = jnp.where(later < nb, block_expert[jnp.minimum(later, nb - 1)], -1).astype(jnp.int32)
    tile_base = base[:, 0].astype(jnp.int32).reshape(n_tiles, ne)
    run_len = jnp.concatenate([tile_base[1:], sizes[None, :]], axis=0) - tile_base
    run_start = pstarts[None, :] + tile_base
    fetch = run_start // BF16_TILE_ROWS * BF16_TILE_ROWS
    shift = fetch - pstarts[None, :]
    nch = jnp.where(run_len > 0, (run_start - fetch + run_len + RUN_ROWS - 1) // RUN_ROWS, 0)
    flat = lambda a: a.reshape(n_tiles * ne).astype(jnp.int32)
    return (flat(run_start), flat(run_len), flat(tile_base), padstart, npad, n_slots,
            block_expert, first_block, next_expert, n_used, flat(fetch), flat(shift), flat(nch))


def kernel(x, norm1_g, w_in, q_norm_g, k_norm_g, conv_w, conv_b, w_mq, w_mk, b_igate, b_fgate, mh_norm_g,
           w_out, norm2_g, w_router, b_router, w_gate_up, b_gate_up, w_down, b_down):
    B, S, D = x.shape
    T = B * S
    assert D == D_MODEL and S % ATTN_ROWS == 0 and norm1_g.shape[0] == 1
    x2 = x.reshape(T, D)
    q, k, v, xm, vm, om, gates = _in_proj(x2, norm1_g[0], w_in[0], q_norm_g[0], k_norm_g[0])
    attn = _attention(q, k, v, B, S)
    ml = _mlstm(xm, vm, om, gates, conv_w[0], conv_b[0], w_mq[0], w_mk[0], b_igate[0], b_fgate[0],
                mh_norm_g[0], B, S)
    x1, h2, rt, gt, base, counts = _out_route(attn, ml, x2, w_out[0], norm2_g[0], w_router[0], b_router[0])
    (run_start, run_len, tile_base, padstart, npad, n_slots, block_expert, first_block, next_expert, n_used,
     fetch, shift, nch) = _routing_tables(base, counts, T)
    xs = _dispatch(h2, rt, run_start, run_len, tile_base, padstart, npad, n_slots)
    y = _experts(xs, block_expert, first_block, next_expert, n_used, w_gate_up[0], b_gate_up[0], w_down[0],
                 b_down[0])
    out = _combine(fetch, shift, nch, x1, rt, gt, y)
    return out.reshape(B, S, D)
```

```python
import functools

import jax
import jax.numpy as jnp
from jax import lax
from jax.experimental import pallas as pl
from jax.experimental.pallas import tpu as pltpu

F32 = jnp.float32
BF16 = jnp.bfloat16
NEG_INF = float("-inf")

D_MODEL = 1024
D_ATTN = 512
HEAD_DIM_ATTN = 64
N_HEADS_ATTN = 8
D_MLSTM = 512
HEAD_DIM_MLSTM = 128
N_HEADS_MLSTM = 4
CONV_K = 4
N_EXPERTS = 32
TOP_K = 4
D_FF = 1024
SWIGLU_LIMIT = 7.0
SWIGLU_ALPHA = 1.702
EPS = 1e-6
DILATIONS = (1, 4, 16)
BRANCH_SPAN = 128

LANES = 128
SUBLANES = 8

IN_PROJ_ROWS = 256
ATTN_ROWS = 2048
MLSTM_CHUNK = 256
ROUTE_ROWS = 256
ROUTE_TILES_PER_STEP = 2
EXPERT_ROWS = 256
FF_CHUNK = 512
CAST_ROWS = 128
DISPATCH_ROWS = 256
HEADNORM_LANES = 256
BF16_TILE_ROWS = 16
RUN_ROWS = 64
RUN_GROUP_ROWS = 256
NOT_ROUTED = -1.0e9


def _vmem_limit(mib):
    return pltpu.CompilerParams(vmem_limit_bytes=mib * 1024 * 1024)


def _split3(a):
    p1 = a.astype(BF16)
    r1 = a - p1.astype(F32)
    p2 = r1.astype(BF16)
    r2 = r1 - p2.astype(F32)
    return p1, p2, r2.astype(BF16)


def _dot(a, b):
    return jnp.dot(a, b, preferred_element_type=F32)


def _dot_nt(a, b):
    return lax.dot_general(a, b, (((1,), (1,)), ((), ())), preferred_element_type=F32)


def _dot_tn(a, b):
    return lax.dot_general(a, b, (((0,), (0,)), ((), ())), preferred_element_type=F32)


def _in_proj_kernel(x_ref, g1_ref, wqkv_ref, wm_ref, wgc_ref, qg_ref, kg_ref, hsum_ref,
                    q_ref, k_ref, v_ref, xm_ref, vm_ref, om_ref, gates_ref):
    x = x_ref[...]
    h = x * lax.rsqrt(jnp.mean(x * x, axis=-1, keepdims=True) + EPS) * g1_ref[...]
    hb = h.astype(BF16)
    hl = (h - hb.astype(F32)).astype(BF16)
    gw = HEADNORM_LANES

    def head_norm(z, g_ref):
        parts = []
        for c in range(D_ATTN // gw):
            zc = z[:, c * gw:(c + 1) * gw]
            ms = _dot((zc * zc).astype(BF16), hsum_ref[...]) * (1.0 / HEAD_DIM_ATTN)
            parts.append(zc * lax.rsqrt(ms + EPS))
        return jnp.concatenate(parts, axis=-1) * g_ref[...]

    zq = _dot(hb, wqkv_ref[:, 0:D_ATTN])
    q_ref[...] = head_norm(zq, qg_ref) * (HEAD_DIM_ATTN ** -0.5)
    zk = _dot(hb, wqkv_ref[:, D_ATTN:2 * D_ATTN])
    k_ref[...] = head_norm(zk, kg_ref)
    v_ref[...] = _dot(hb, wqkv_ref[:, 2 * D_ATTN:3 * D_ATTN])
    xm_ref[...] = _dot(hb, wm_ref[:, 0:D_MLSTM])
    vm_ref[...] = _dot(hb, wm_ref[:, D_MLSTM:2 * D_MLSTM])
    om_ref[...] = _dot(hb, wm_ref[:, 2 * D_MLSTM:3 * D_MLSTM])
    ng = 2 * N_HEADS_MLSTM
    gsum = _dot(hb, wgc_ref[...]) + _dot(hl, wgc_ref[...])
    gsum = gsum + pltpu.roll(gsum, LANES - ng, axis=1)
    lane = lax.broadcasted_iota(jnp.int32, gsum.shape, 1)
    gates_ref[...] = jnp.where(lane < ng, gsum, 0.0)


def _in_proj(x2, g1, w_in, q_g, k_g):
    T = x2.shape[0]
    tm = IN_PROJ_ROWS
    n_qkv = 3 * D_ATTN
    n_m = 3 * D_MLSTM
    ng = 2 * N_HEADS_MLSTM
    wqkv = w_in[:, :n_qkv].astype(BF16)
    wm = w_in[:, n_qkv:n_qkv + n_m].astype(BF16)
    wg = w_in[:, n_qkv + n_m:]
    wgh = wg.astype(BF16)
    wgl = (wg - wgh.astype(F32)).astype(BF16)
    wgc = jnp.pad(jnp.concatenate([wgh, wgl], axis=1), ((0, 0), (0, LANES - 2 * ng)))
    lane = jnp.arange(HEADNORM_LANES)
    hsum = (lane[:, None] // HEAD_DIM_ATTN == lane[None, :] // HEAD_DIM_ATTN).astype(BF16)
    qg = jnp.tile(q_g, N_HEADS_ATTN)[None, :]
    kg = jnp.tile(k_g, N_HEADS_ATTN)[None, :]
    row = lambda n: pl.BlockSpec((tm, n), lambda i: (i, 0))
    full = lambda a: pl.BlockSpec(a.shape, lambda i: (0,) * a.ndim)
    outs = [jax.ShapeDtypeStruct((T, D_ATTN), F32)] * 3 + [jax.ShapeDtypeStruct((T, D_MLSTM), F32)] * 3
    outs.append(jax.ShapeDtypeStruct((T, LANES), F32))
    g1r = g1[None, :]
    return pl.pallas_call(
        _in_proj_kernel,
        grid=(T // tm,),
        in_specs=[row(D_MODEL), full(g1r), full(wqkv), full(wm), full(wgc), full(qg), full(kg), full(hsum)],
        out_specs=[row(D_ATTN)] * 3 + [row(D_MLSTM)] * 3 + [row(LANES)],
        out_shape=outs,
        compiler_params=_vmem_limit(48),
        name="in_proj",
    )(x2, g1r, wqkv, wm, wgc, qg, kg, hsum)


def _attn_tiles():
    tiles = []
    for br, d in enumerate(DILATIONS):
        group = d * BRANCH_SPAN
        for u in range(ATTN_ROWS // group):
            for c in range(d):
                tiles.append((br, d, u * group + c))
    return tiles


def _attn_kernel(slope_ref, q_ref, kp_ref, kc_ref, vp_ref, vc_ref, o_ref,
                 kwin, vwin, bias_scr, m_scr, l_scr, acc_scr):
    W = ATTN_ROWS
    n = BRANCH_SPAN
    step = pl.program_id(2)
    kwin[0:W, :] = kp_ref[...]
    kwin[W:2 * W, :] = kc_ref[...]
    vwin[0:W, :] = vp_ref[...]
    vwin[W:2 * W, :] = vc_ref[...]

    lane = lax.broadcasted_iota(jnp.int32, (n, LANES), 1)
    first_head = lane < HEAD_DIM_ATTN
    row = lax.broadcasted_iota(jnp.int32, (n, 2 * n), 0)
    col = lax.broadcasted_iota(jnp.int32, (n, 2 * n), 1)
    j = n + row - col
    valid = jnp.logical_and(j >= 0, j <= n)
    jf = j.astype(F32)
    for br, d in enumerate(DILATIONS):
        for hh in range(2):
            slope = slope_ref[:, hh * HEAD_DIM_ATTN:hh * HEAD_DIM_ATTN + 1]
            b = jnp.where(valid, -(slope * float(d)) * jf, NEG_INF)
            bias_scr[br, hh, 0] = b
            bias_scr[br, hh, 1] = jnp.where(col >= n, b, NEG_INF)

    for br, d, qs in _attn_tiles():
        lo = W + qs - n * d
        q = q_ref[pl.ds(qs, n, stride=d), :]
        kk = jnp.concatenate([kwin[pl.ds(lo, n, stride=d), :], kwin[pl.ds(W + qs, n, stride=d), :]],
                             axis=0).astype(BF16)
        vv = jnp.concatenate([vwin[pl.ds(lo, n, stride=d), :], vwin[pl.ds(W + qs, n, stride=d), :]],
                             axis=0).astype(BF16)
        if lo < W:
            sel = jnp.where(step == 0, 1, 0)
        else:
            sel = 0
        res = []
        for hh in range(2):
            qm = jnp.where(first_head if hh == 0 else jnp.logical_not(first_head), q, 0.0).astype(BF16)
            s = _dot_nt(qm, kk) + bias_scr[br, hh, sel]
            m = jnp.max(s, axis=-1, keepdims=True)
            p = jnp.exp(s - m)
            l = jnp.sum(p, axis=-1, keepdims=True)
            res.append((m, l, _dot(p.astype(BF16), vv)))
        rows = pl.ds(qs, n, stride=d)
        m_scr[br, rows, :] = jnp.where(first_head, res[0][0], res[1][0])
        l_scr[br, rows, :] = jnp.where(first_head, res[0][1], res[1][1])
        acc_scr[br, rows, :] = jnp.where(first_head, res[0][2], res[1][2])

    chunk = 256

    def combine(i, carry):
        r = pl.ds(pl.multiple_of(i * chunk, chunk), chunk)
        m0, m1, m2 = m_scr[0, r, :], m_scr[1, r, :], m_scr[2, r, :]
        mm = jnp.maximum(jnp.maximum(m0, m1), m2)
        w0, w1, w2 = jnp.exp(m0 - mm), jnp.exp(m1 - mm), jnp.exp(m2 - mm)
        num = w0 * acc_scr[0, r, :] + w1 * acc_scr[1, r, :] + w2 * acc_scr[2, r, :]
        den = w0 * l_scr[0, r, :] + w1 * l_scr[1, r, :] + w2 * l_scr[2, r, :]
        o_ref[r, :] = num / den
        return carry

    lax.fori_loop(0, W // chunk, combine, 0)


def _attention(q, k, v, B, S):
    W = ATTN_ROWS
    nb = S // W
    n_pairs = D_ATTN // LANES
    slopes = jnp.exp2(-8.0 * jnp.arange(1, N_HEADS_ATTN + 1, dtype=F32) / N_HEADS_ATTN)
    slope_l = jnp.repeat(slopes, HEAD_DIM_ATTN).reshape(n_pairs, 1, LANES)
    cur = pl.BlockSpec((W, LANES), lambda b, p, i: (b * nb + i, p))
    prev = pl.BlockSpec((W, LANES), lambda b, p, i: (b * nb + jnp.maximum(i - 1, 0), p))
    n = BRANCH_SPAN
    nbr = len(DILATIONS)
    return pl.pallas_call(
        _attn_kernel,
        grid=(B, n_pairs, nb),
        in_specs=[pl.BlockSpec((None, 1, LANES), lambda b, p, i: (p, 0, 0)), cur, prev, cur, prev, cur],
        out_specs=cur,
        out_shape=jax.ShapeDtypeStruct((B * S, D_ATTN), F32),
        scratch_shapes=[
            pltpu.VMEM((2 * W, LANES), F32), pltpu.VMEM((2 * W, LANES), F32),
            pltpu.VMEM((nbr, 2, 2, n, 2 * n), F32),
            pltpu.VMEM((nbr, W, LANES), F32), pltpu.VMEM((nbr, W, LANES), F32),
            pltpu.VMEM((nbr, W, LANES), F32),
        ],
        compiler_params=_vmem_limit(48),
        name="attention",
    )(slope_l, q, k, k, v, v)


def _mlstm_kernel(xm_ref, vm_ref, om_ref, gcol_ref, grow_ref, cw_ref, cb_ref, wq_ref, wk_ref,
                  bcol_ref, brow_ref, g_ref, o_ref, xs_ref, xtail_ref, c_ref, n_ref, m_ref):
    L = MLSTM_CHUNK
    dh = HEAD_DIM_MLSTM
    nh = N_HEADS_MLSTM

    @pl.when(pl.program_id(1) == 0)
    def _():
        xtail_ref[...] = jnp.zeros_like(xtail_ref)
        c_ref[...] = jnp.zeros_like(c_ref)
        n_ref[...] = jnp.zeros_like(n_ref)
        m_ref[...] = jnp.zeros_like(m_ref)

    x = xm_ref[...]
    xs_ref[0:SUBLANES, :] = xtail_ref[...]
    xs_ref[SUBLANES:SUBLANES + L, :] = x
    xtail_ref[...] = x[L - SUBLANES:L, :]
    xc = cb_ref[...] + x * cw_ref[CONV_K - 1:CONV_K, :]
    for back in range(1, CONV_K):
        xc = xc + xs_ref[SUBLANES - back:SUBLANES - back + L, :] * cw_ref[CONV_K - 1 - back:CONV_K - back, :]
    xc = xc * jax.nn.sigmoid(xc)

    gc = gcol_ref[...] + bcol_ref[...]
    gr = grow_ref[...] + brow_ref[...]
    ri = lax.broadcasted_iota(jnp.int32, (L, L), 0)
    ci = lax.broadcasted_iota(jnp.int32, (L, L), 1)
    causal = ri >= ci
    tril = jnp.where(causal, 1.0, 0.0).astype(BF16)
    triu = jnp.where(ri <= ci, 1.0, 0.0).astype(BF16)
    c1, c2, c3 = _split3(jax.nn.log_sigmoid(gc))
    bc = _dot(tril, c1) + _dot(tril, c2) + _dot(tril, c3)
    r1, r2, r3 = _split3(jax.nn.log_sigmoid(gr))
    brw = _dot(r1, triu) + _dot(r2, triu) + _dot(r3, triu)

    for h in range(nh):
        hs = slice(h * dh, (h + 1) * dh)
        b_col = bc[:, nh + h:nh + h + 1]
        a_col = gc[:, h:h + 1] - b_col
        a_row = gr[h:h + 1, :] - brw[nh + h:nh + h + 1, :]
        m_prev = m_ref[h:h + 1, 0:1]
        log_d = jnp.where(causal, b_col + a_row, NEG_INF)
        log_inter = b_col + m_prev
        m_t = jnp.maximum(log_inter, jnp.max(log_d, axis=-1, keepdims=True))
        d = jnp.exp(log_d - m_t)
        inter = jnp.exp(log_inter - m_t)

        xh = xc[:, hs].astype(BF16)
        qf = _dot(xh, wq_ref[h])
        kf = _dot(xh, wk_ref[h]) * (dh ** -0.5)
        qb = qf.astype(BF16)
        vb = vm_ref[:, hs].astype(BF16)
        s = _dot_nt(qb, kf.astype(BF16)) * d
        c_old = c_ref[h]
        n_old = n_ref[h:h + 1, :]
        num = _dot(s.astype(BF16), vb) + inter * _dot(qb, c_old.astype(BF16))
        den = jnp.sum(s, axis=-1, keepdims=True) + inter * jnp.sum(qf * n_old, axis=-1, keepdims=True)
        hh = num / jnp.maximum(jnp.abs(den), jnp.exp(-m_t))

        b_last = b_col[L - 1:L, :]
        log_w = b_last + a_col
        m_new = jnp.maximum(b_last + m_prev, jnp.max(log_w, axis=0, keepdims=True))
        kw = kf * jnp.exp(log_w - m_new)
        decay = jnp.exp(b_last + m_prev - m_new)
        c_ref[h] = decay * c_old + _dot_tn(kw.astype(BF16), vb)
        n_ref[h:h + 1, :] = decay * n_old + jnp.sum(kw, axis=0, keepdims=True)
        m_ref[h:h + 1, :] = jnp.broadcast_to(m_new, (1, LANES))

        hn = hh * lax.rsqrt(jnp.mean(hh * hh, axis=-1, keepdims=True) + EPS) * g_ref[:, hs]
        o_ref[:, hs] = jax.nn.sigmoid(om_ref[:, hs]) * hn


def _mlstm(xm, vm, om, gates, conv_w, conv_b, w_mq, w_mk, b_i, b_f, mh_g, B, S):
    L = MLSTM_CHUNK
    nc = S // L
    nh = N_HEADS_MLSTM
    ng = 2 * nh
    grow = gates[:, :ng].reshape(B, S, ng).transpose(0, 2, 1)
    bias = jnp.concatenate([b_i, b_f])
    bcol = jnp.pad(bias, (0, LANES - ng))[None, :]
    brow = bias[:, None]
    wq = w_mq.astype(BF16)
    wk = w_mk.astype(BF16)
    cb = conv_b[None, :]
    g = mh_g.reshape(1, D_MLSTM)
    rows = pl.BlockSpec((L, D_MLSTM), lambda b, c: (b * nc + c, 0))
    full = lambda a: pl.BlockSpec(a.shape, lambda b, c: (0,) * a.ndim)
    return pl.pallas_call(
        _mlstm_kernel,
        grid=(B, nc),
        in_specs=[rows, rows, rows, pl.BlockSpec((L, LANES), lambda b, c: (b * nc + c, 0)),
                  pl.BlockSpec((None, ng, L), lambda b, c: (b, 0, c)),
                  full(conv_w), full(cb), full(wq), full(wk), full(bcol), full(brow), full(g)],
        out_specs=rows,
        out_shape=jax.ShapeDtypeStruct((B * S, D_MLSTM), F32),
        scratch_shapes=[
            pltpu.VMEM((L + SUBLANES, D_MLSTM), F32), pltpu.VMEM((SUBLANES, D_MLSTM), F32),
            pltpu.VMEM((nh, HEAD_DIM_MLSTM, HEAD_DIM_MLSTM), F32),
            pltpu.VMEM((SUBLANES, LANES), F32), pltpu.VMEM((SUBLANES, LANES), F32),
        ],
        compiler_params=_vmem_limit(32),
        name="mlstm",
    )(xm, vm, om, gates, grow, conv_w, cb, wq, wk, bcol, brow, g)


def _out_route_kernel(attn_ref, ml_ref, x_ref, wo_ref, g2_ref, wrh_ref, wrl_ref, br_ref, earlier_ref,
                      x1_ref, h2_ref, rt_ref, gt_ref, base_ref, cnt_ref, carry_ref):
    @pl.when(pl.program_id(0) == 0)
    def _():
        carry_ref[...] = jnp.zeros_like(carry_ref)

    carry = carry_ref[...]
    for j in range(ROUTE_TILES_PER_STEP):
        carry = _route_tile(j, carry, attn_ref, ml_ref, x_ref, wo_ref, g2_ref, wrh_ref, wrl_ref, br_ref,
                            earlier_ref, x1_ref, h2_ref, rt_ref, gt_ref, base_ref)
    carry_ref[...] = carry
    cnt_ref[...] = carry


def _route_tile(j, carry, attn_ref, ml_ref, x_ref, wo_ref, g2_ref, wrh_ref, wrl_ref, br_ref, earlier_ref,
                x1_ref, h2_ref, rt_ref, gt_ref, base_ref):
    tm = ROUTE_ROWS
    ne = N_EXPERTS
    rows = slice(j * tm, (j + 1) * tm)

    x1 = (x_ref[rows, :] + _dot(attn_ref[rows, :].astype(BF16), wo_ref[0:D_ATTN, :])
          + _dot(ml_ref[rows, :].astype(BF16), wo_ref[D_ATTN:D_ATTN + D_MLSTM, :]))
    x1_ref[rows, :] = x1
    h2 = x1 * lax.rsqrt(jnp.mean(x1 * x1, axis=-1, keepdims=True) + EPS) * g2_ref[...]
    hb = h2.astype(BF16)
    h2_ref[rows, :] = hb
    hl = (h2 - hb.astype(F32)).astype(BF16)
    lt = (_dot_nt(wrh_ref[...], hb) + _dot_nt(wrh_ref[...], hl) + _dot_nt(wrl_ref[...], hb)) + br_ref[...]

    eidx = lax.broadcasted_iota(jnp.int32, (ne, tm), 0)
    vals, hots, idxs = [], [], []
    work = lt
    for _ in range(TOP_K):
        mx = jnp.max(work, axis=0, keepdims=True)
        idx = jnp.min(jnp.where(work == mx, eidx, ne), axis=0, keepdims=True)
        hot = eidx == idx
        vals.append(mx)
        hots.append(hot)
        idxs.append(idx)
        work = jnp.where(hot, NEG_INF, work)
    ex = [jnp.exp(v - vals[0]) for v in vals]
    tot = ex[0] + ex[1] + ex[2] + ex[3]

    onehot = jnp.zeros((ne, tm), F32)
    gt = jnp.zeros((ne, tm), F32)
    for kk in range(TOP_K):
        onehot = onehot + jnp.where(hots[kk], 1.0, 0.0)
        gt = gt + jnp.where(hots[kk], ex[kk] / tot, 0.0)
    before = _dot(onehot.astype(BF16), earlier_ref[...]) + carry[:, 0:1]
    base_ref[j * ne:(j + 1) * ne, :] = carry
    rt_ref[:, rows] = jnp.where(onehot > 0.5, before, NOT_ROUTED).astype(jnp.int32)
    gt_ref[:, rows] = gt
    return carry + jnp.sum(onehot, axis=1, keepdims=True)


def _out_route(attn, ml, x2, w_out, g2, w_router, b_router):
    T = x2.shape[0]
    tm = ROUTE_ROWS
    ne = N_EXPERTS
    wo = w_out.astype(BF16)
    wrt = w_router.T
    wrh = wrt.astype(BF16)
    wrl = (wrt - wrh.astype(F32)).astype(BF16)
    br = b_router[:, None]
    g2r = g2[None, :]
    tok = jnp.arange(tm)
    earlier = (tok[:, None] < tok[None, :]).astype(BF16)
    ts = tm * ROUTE_TILES_PER_STEP
    row = lambda n: pl.BlockSpec((ts, n), lambda i: (i, 0))
    col = lambda n: pl.BlockSpec((n, ts), lambda i: (0, i))
    full = lambda a: pl.BlockSpec(a.shape, lambda i: (0,) * a.ndim)
    return pl.pallas_call(
        _out_route_kernel,
        grid=(T // ts,),
        in_specs=[row(D_ATTN), row(D_MLSTM), row(D_MODEL), full(wo), full(g2r), full(wrh), full(wrl), full(br),
                  full(earlier)],
        out_specs=[row(D_MODEL), row(D_MODEL), col(ne), col(ne),
                   pl.BlockSpec((ROUTE_TILES_PER_STEP * ne, LANES), lambda i: (i, 0)),
                   pl.BlockSpec((ne, LANES), lambda i: (0, 0))],
        out_shape=[jax.ShapeDtypeStruct((T, D_MODEL), F32), jax.ShapeDtypeStruct((T, D_MODEL), BF16),
                   jax.ShapeDtypeStruct((ne, T), jnp.int32), jax.ShapeDtypeStruct((ne, T), F32),
                   jax.ShapeDtypeStruct((T // tm * ne, LANES), F32), jax.ShapeDtypeStruct((ne, LANES), F32)],
        scratch_shapes=[pltpu.VMEM((ne, LANES), F32)],
        compiler_params=_vmem_limit(32),
        name="out_route",
    )(attn, ml, x2, wo, g2r, wrh, wrl, br, earlier)


def _row_copy(src_ref, dst_ref, sem, src_row, dst_row):
    return pltpu.make_async_copy(src_ref.at[pl.ds(src_row, 1)], dst_ref.at[pl.ds(dst_row, 1)], sem)


def _dispatch_kernel(runstart_ref, runlen_ref, lshift_ref, padstart_ref, npad_ref,
                     h_ref, rt_ref, xs_ref, cbuf, xbuf, zero_ref, stage_ref, sem, xsem, zsem):
    td = DISPATCH_ROWS
    ne = N_EXPERTS
    rb = RUN_ROWS
    half = rb // 2
    ta = BF16_TILE_ROWS
    grp = RUN_GROUP_ROWS // rb
    tile = pl.program_id(0)

    def window(i):
        off = runstart_ref[i] & (ta - 1)
        return off, off + runlen_ref[i]

    def first_chunk(tile_idx, e, action):
        i = tile_idx * ne + e
        off, total = window(i)
        dst = pl.multiple_of(runstart_ref[i] - off, ta)
        action(pltpu.make_async_copy(cbuf.at[pl.ds(e * rb, half)], xs_ref.at[pl.ds(dst, half)], sem))

        @pl.when(total > half)
        def _():
            action(pltpu.make_async_copy(cbuf.at[pl.ds(e * rb + half, half)],
                                         xs_ref.at[pl.ds(pl.multiple_of(dst + half, ta), half)], sem))

    @pl.when(tile > 0)
    def _():
        for e in range(ne):
            first_chunk(tile - 1, e, lambda cp: cp.wait())

    @pl.when(tile == 0)
    def _():
        stage_ref[...] = jnp.zeros_like(stage_ref)

    hb = h_ref[...]
    riota = lax.broadcasted_iota(jnp.int32, (rb, td), 0)
    rowt = lax.broadcasted_iota(jnp.int32, (ta, D_MODEL), 0)

    def select(e_slice, shift):
        return jnp.where(riota == rt_ref[e_slice, :] - shift, 1.0, 0.0)

    for g in range(ne // grp):
        experts = range(g * grp, (g + 1) * grp)
        parts = [select(slice(e, e + 1), lshift_ref[tile * ne + e] - window(tile * ne + e)[0]) for e in experts]
        pt = jnp.concatenate(parts, axis=0).astype(BF16)
        cbuf[g * RUN_GROUP_ROWS:(g + 1) * RUN_GROUP_ROWS, :] = _dot(pt, hb).astype(BF16)

    for e in range(ne):
        i = tile * ne + e
        off, total = window(i)
        head = pl.ds(e * rb, ta)
        old = stage_ref[e]
        cbuf[head, :] = jnp.where(rowt < off, old, cbuf[head, :])
        last = e * rb + jnp.minimum(total // ta * ta, rb - ta)
        keep = jnp.logical_and(runlen_ref[i] > 0, total < rb)
        stage_ref[e] = jnp.where(keep, cbuf[pl.ds(pl.multiple_of(last, ta), ta), :], old)

    for e in range(ne):
        first_chunk(tile, e, lambda cp: cp.start())

    def per_expert(e, carry):
        i = tile * ne + e
        off, total = window(i)

        def chunk(ch, c):
            pt = select(pl.ds(e, 1), lshift_ref[i] - off + ch * rb).astype(BF16)
            xbuf[...] = _dot(pt, hb).astype(BF16)
            dst = pl.multiple_of(runstart_ref[i] - off + ch * rb, ta)
            cp = pltpu.make_async_copy(xbuf, xs_ref.at[pl.ds(dst, rb)], xsem)
            cp.start()

            @pl.when(jnp.logical_and(ch == total // rb, total % rb != 0))
            def _():
                last = pl.multiple_of(total % rb // ta * ta, ta)
                stage_ref[e] = xbuf[pl.ds(last, ta), :]

            cp.wait()
            return c

        lax.fori_loop(1, (total + rb - 1) // rb, chunk, 0)
        return carry

    lax.fori_loop(0, ne, per_expert, 0)

    @pl.when(tile == pl.num_programs(0) - 1)
    def _():
        for e in range(ne):
            first_chunk(tile, e, lambda cp: cp.wait())

    @pl.when(tile == pl.num_programs(0) - 1)
    def _():
        zero_ref[...] = jnp.zeros_like(zero_ref)

        def fill(wait):
            def per_run(e, carry):
                start = padstart_ref[e]
                head = (-start) & (ta - 1)
                off = pl.multiple_of(start + head, ta)
                left = npad_ref[e] - head

                def whole(b, c):
                    dst = xs_ref.at[pl.ds(pl.multiple_of(off + b * td, ta), td)]
                    cp = pltpu.make_async_copy(zero_ref, dst, zsem)
                    cp.wait() if wait else cp.start()
                    return c

                n_whole = left // td
                lax.fori_loop(0, n_whole, whole, 0)
                off = pl.multiple_of(off + n_whole * td, ta)
                p = td // 2
                while p >= ta:
                    take = (left & p) != 0

                    @pl.when(take)
                    def _(off=off, p=p):
                        cp = pltpu.make_async_copy(zero_ref.at[pl.ds(0, p)], xs_ref.at[pl.ds(off, p)], zsem)
                        cp.wait() if wait else cp.start()

                    off = pl.multiple_of(off + jnp.where(take, p, 0), ta)
                    p //= 2
                return carry

            lax.fori_loop(0, N_EXPERTS + 1, per_run, 0)

        fill(wait=False)
        fill(wait=True)


def _dispatch(h2, rt, runstart, runlen, lshift, padstart, npad, n_slots):
    T = h2.shape[0]
    td = DISPATCH_ROWS
    ne = N_EXPERTS
    assert td == ROUTE_ROWS
    return pl.pallas_call(
        _dispatch_kernel,
        grid_spec=pltpu.PrefetchScalarGridSpec(
            num_scalar_prefetch=5,
            grid=(T // td,),
            in_specs=[pl.BlockSpec((td, D_MODEL), lambda i, *_: (i, 0)),
                      pl.BlockSpec((ne, td), lambda i, *_: (0, i))],
            out_specs=pl.BlockSpec(memory_space=pl.ANY),
            scratch_shapes=[pltpu.VMEM((ne * RUN_ROWS, D_MODEL), BF16), pltpu.VMEM((RUN_ROWS, D_MODEL), BF16),
                            pltpu.VMEM((td, D_MODEL), BF16), pltpu.VMEM((ne, BF16_TILE_ROWS, D_MODEL), BF16),
                            pltpu.SemaphoreType.DMA(()), pltpu.SemaphoreType.DMA(()),
                            pltpu.SemaphoreType.DMA(())],
        ),
        out_shape=jax.ShapeDtypeStruct((n_slots, D_MODEL), BF16),
        compiler_params=_vmem_limit(32),
        name="dispatch",
    )(runstart, runlen, lshift, padstart, npad, h2, rt)


def _expert_kernel(pstart_ref, nblk_ref, ntail_ref, xs_hbm, wgu_ref, bgu_ref, wd_ref, bd_ref, y_hbm,
                   wgu_bf, wd_bf, xbuf, ybuf, xsem, ysem):
    tm = EXPERT_ROWS
    e = pl.program_id(0)
    nblk = nblk_ref[e]
    row0 = pstart_ref[e]

    def rows(j):
        return pl.ds(pl.multiple_of(row0 + j * tm, tm), tm)

    def x_copy(j, slot):
        return pltpu.make_async_copy(xs_hbm.at[rows(j)], xbuf.at[slot], xsem.at[slot])

    def y_copy(j, slot):
        return pltpu.make_async_copy(ybuf.at[slot], y_hbm.at[rows(j)], ysem.at[slot])

    @pl.when(nblk > 0)
    def _():
        x_copy(0, 0).start()
        rc = CAST_ROWS

        def cast(i, carry):
            r = pl.ds(pl.multiple_of(i * rc, rc), rc)
            wgu_bf[r, :] = wgu_ref[r, :].astype(BF16)
            wd_bf[r, :] = wd_ref[r, :].astype(BF16)
            return carry

        lax.fori_loop(0, D_MODEL // rc, cast, 0)

        def block(j, carry):
            slot = j % 2
            x_copy(j, slot).wait()

            @pl.when(j + 1 < nblk)
            def _():
                x_copy(j + 1, 1 - slot).start()

            @pl.when(j >= 2)
            def _():
                y_copy(j - 2, slot).wait()

            xb = xbuf[slot]
            acc = jnp.zeros((tm, D_MODEL), F32)
            for f in range(D_FF // FF_CHUNK):
                lo = f * FF_CHUNK
                g = _dot(xb, wgu_bf[:, lo:lo + FF_CHUNK]) + bgu_ref[:, lo:lo + FF_CHUNK]
                u = (_dot(xb, wgu_bf[:, D_FF + lo:D_FF + lo + FF_CHUNK])
                     + bgu_ref[:, D_FF + lo:D_FF + lo + FF_CHUNK])
                g = jnp.minimum(g, SWIGLU_LIMIT)
                u = jnp.clip(u, -SWIGLU_LIMIT, SWIGLU_LIMIT)
                act = g * jax.nn.sigmoid(SWIGLU_ALPHA * g) * (u + 1.0)
                acc = acc + _dot(act.astype(BF16), wd_bf[lo:lo + FF_CHUNK, :])
            ybuf[slot] = (acc + bd_ref[...]).astype(BF16)
            y_copy(j, slot).start()
            return carry

        lax.fori_loop(0, nblk, block, 0)

        @pl.when(nblk >= 2)
        def _():
            y_copy(nblk - 2, nblk % 2).wait()

        y_copy(nblk - 1, (nblk - 1) % 2).wait()

    @pl.when(e == pl.num_programs(0) - 1)
    def _():
        ybuf[0] = jnp.zeros((tm, D_MODEL), BF16)
        used = row0 + nblk * tm

        def tail(action):
            def one(b, carry):
                dst = y_hbm.at[pl.ds(pl.multiple_of(used + b * tm, tm), tm)]
                action(pltpu.make_async_copy(ybuf.at[0], dst, ysem.at[0]))
                return carry

            lax.fori_loop(0, ntail_ref[0], one, 0)

        tail(lambda cp: cp.start())
        tail(lambda cp: cp.wait())


def _experts(xs, pstarts, nblk, ntail, w_gate_up, b_gate_up, w_down, b_down):
    tm = EXPERT_ROWS
    ne = N_EXPERTS
    assert D_FF == D_MODEL
    bgu = b_gate_up[:, None, :]
    bd = b_down[:, None, :]
    per_expert = lambda *shape: pl.BlockSpec((None,) + shape, lambda e, *_: (e, 0, 0))
    return pl.pallas_call(
        _expert_kernel,
        grid_spec=pltpu.PrefetchScalarGridSpec(
            num_scalar_prefetch=3,
            grid=(ne,),
            in_specs=[
                pl.BlockSpec(memory_space=pl.ANY),
                per_expert(D_MODEL, 2 * D_FF), per_expert(1, 2 * D_FF),
                per_expert(D_FF, D_MODEL), per_expert(1, D_MODEL),
            ],
            out_specs=pl.BlockSpec(memory_space=pl.ANY),
            scratch_shapes=[pltpu.VMEM((D_MODEL, 2 * D_FF), BF16), pltpu.VMEM((D_FF, D_MODEL), BF16),
                            pltpu.VMEM((2, tm, D_MODEL), BF16), pltpu.VMEM((2, tm, D_MODEL), BF16),
                            pltpu.SemaphoreType.DMA((2,)), pltpu.SemaphoreType.DMA((2,))],
        ),
        out_shape=jax.ShapeDtypeStruct((xs.shape[0] + tm, D_MODEL), BF16),
        compiler_params=_vmem_limit(48),
        name="experts",
    )(pstarts, nblk, ntail, xs, w_gate_up, bgu, w_down, bd)


def _combine_kernel(fetch_ref, shift_ref, nch_ref, x1_ref, rt_ref, gt_ref, y_ref, o_ref,
                    buf, xbuf, sem, xsem):
    tc = ROUTE_ROWS
    ne = N_EXPERTS
    rb = RUN_ROWS
    grp = RUN_GROUP_ROWS // rb
    tile = pl.program_id(0)
    base = tile * ne
    slot = tile % 2

    def run_copy(tile_base, buf_slot, e):
        start = pl.multiple_of(fetch_ref[tile_base + e], BF16_TILE_ROWS)
        return pltpu.make_async_copy(y_ref.at[pl.ds(start, rb)], buf.at[buf_slot, pl.ds(e * rb, rb)],
                                     sem.at[buf_slot, e // grp])

    @pl.when(tile == 0)
    def _():
        for e in range(ne):
            run_copy(base, slot, e).start()

    @pl.when(tile + 1 < pl.num_programs(0))
    def _():
        for e in range(ne):
            run_copy(base + ne, 1 - slot, e).start()

    riota = lax.broadcasted_iota(jnp.int32, (rb, tc), 0)

    def select(e_slice, shift):
        return jnp.where(riota == rt_ref[e_slice, :] - shift, gt_ref[e_slice, :], 0.0)

    acc = x1_ref[...]
    for g in range(ne // grp):
        parts = []
        for e in range(g * grp, (g + 1) * grp):
            run_copy(base, slot, e).wait()
            parts.append(select(slice(e, e + 1), shift_ref[base + e]))
        pt = jnp.concatenate(parts, axis=0).astype(BF16)
        acc = acc + _dot_tn(pt, buf[slot, g * RUN_GROUP_ROWS:(g + 1) * RUN_GROUP_ROWS, :])
    o_ref[...] = acc

    def per_expert(e, carry):
        def chunk(ch, c):
            start = pl.multiple_of(fetch_ref[base + e] + ch * rb, BF16_TILE_ROWS)
            cp = pltpu.make_async_copy(y_ref.at[pl.ds(start, rb)], xbuf, xsem)
            cp.start()
            cp.wait()
            pt = select(pl.ds(e, 1), shift_ref[base + e] + ch * rb).astype(BF16)
            o_ref[...] += _dot_tn(pt, xbuf[...])
            return c

        lax.fori_loop(1, nch_ref[base + e], chunk, 0)
        return carry

    lax.fori_loop(0, ne, per_expert, 0)


def _combine(fetch, shift, nch, x1, rt, gt, y):
    T = x1.shape[0]
    tc = ROUTE_ROWS
    ne = N_EXPERTS
    return pl.pallas_call(
        _combine_kernel,
        grid_spec=pltpu.PrefetchScalarGridSpec(
            num_scalar_prefetch=3,
            grid=(T // tc,),
            in_specs=[pl.BlockSpec((tc, D_MODEL), lambda i, f, s, n: (i, 0)),
                      pl.BlockSpec((ne, tc), lambda i, f, s, n: (0, i)),
                      pl.BlockSpec((ne, tc), lambda i, f, s, n: (0, i)),
                      pl.BlockSpec(memory_space=pl.ANY)],
            out_specs=pl.BlockSpec((tc, D_MODEL), lambda i, f, s, n: (i, 0)),
            scratch_shapes=[pltpu.VMEM((2, ne * RUN_ROWS, D_MODEL), BF16), pltpu.VMEM((RUN_ROWS, D_MODEL), BF16),
                            pltpu.SemaphoreType.DMA((2, ne * RUN_ROWS // RUN_GROUP_ROWS)),
                            pltpu.SemaphoreType.DMA(())],
        ),
        out_shape=jax.ShapeDtypeStruct((T, D_MODEL), F32),
        compiler_params=_vmem_limit(32),
        name="combine",
    )(fetch, shift, nch, x1, rt, gt, y)


def _routing_tables(base, counts, T):
    tm = EXPERT_ROWS
    ne = N_EXPERTS
    n_tiles = T // ROUTE_ROWS
    n_blocks = -(-(T * TOP_K + ne * RUN_ROWS) // tm) + ne
    n_slots = n_blocks * tm
    sizes = counts[:, 0].astype(jnp.int32)
    psizes = (sizes + RUN_ROWS + tm - 1) // tm * tm
    pends = jnp.cumsum(psizes)
    pstarts = pends - psizes
    n_used = pends[-1:]
    padstart = jnp.concatenate([pstarts + sizes, n_used])
    npad = jnp.concatenate([psizes - sizes, n_slots - n_used])
    nblk = psizes // tm
    ntail = (n_slots + tm - n_used) // tm
    tile_base = base[:, 0].astype(jnp.int32).reshape(n_tiles, ne)
    run_len = jnp.concatenate([tile_base[1:], sizes[None, :]], axis=0) - tile_base
    run_start = pstarts[None, :] + tile_base
    fetch = run_start // BF16_TILE_ROWS * BF16_TILE_ROWS
    shift = fetch - pstarts[None, :]
    nch = jnp.where(run_len > 0, (run_start - fetch + run_len + RUN_ROWS - 1) // RUN_ROWS, 0)
    flat = lambda a: a.reshape(n_tiles * ne).astype(jnp.int32)
    return (flat(run_start), flat(run_len), flat(tile_base), padstart, npad, n_slots,
            pstarts, nblk, ntail, flat(fetch), flat(shift), flat(nch))


def kernel(x, norm1_g, w_in, q_norm_g, k_norm_g, conv_w, conv_b, w_mq, w_mk, b_igate, b_fgate, mh_norm_g,
           w_out, norm2_g, w_router, b_router, w_gate_up, b_gate_up, w_down, b_down):
    B, S, D = x.shape
    T = B * S
    assert D == D_MODEL and S % ATTN_ROWS == 0 and norm1_g.shape[0] == 1
    x2 = x.reshape(T, D)
    q, k, v, xm, vm, om, gates = _in_proj(x2, norm1_g[0], w_in[0], q_norm_g[0], k_norm_g[0])
    attn = _attention(q, k, v, B, S)
    ml = _mlstm(xm, vm, om, gates, conv_w[0], conv_b[0], w_mq[0], w_mk[0], b_igate[0], b_fgate[0],
                mh_norm_g[0], B, S)
    x1, h2, rt, gt, base, counts = _out_route(attn, ml, x2, w_out[0], norm2_g[0], w_router[0], b_router[0])
    (run_start, run_len, tile_base, padstart, npad, n_slots, pstarts, nblk, ntail,
     fetch, shift, nch) = _routing_tables(base, counts, T)
    xs = _dispatch(h2, rt, run_start, run_len, tile_base, padstart, npad, n_slots)
    y = _experts(xs, pstarts, nblk, ntail, w_gate_up[0], b_gate_up[0], w_down[0], b_down[0])
    out = _combine(fetch, shift, nch, x1, rt, gt, y)
    return out.reshape(B, S, D)
```

```python
import functools

import jax
import jax.numpy as jnp
from jax import lax
from jax.experimental import pallas as pl
from jax.experimental.pallas import tpu as pltpu

F32 = jnp.float32
BF16 = jnp.bfloat16
NEG_INF = float("-inf")

D_MODEL = 1024
D_ATTN = 512
HEAD_DIM_ATTN = 64
N_HEADS_ATTN = 8
D_MLSTM = 512
HEAD_DIM_MLSTM = 128
N_HEADS_MLSTM = 4
CONV_K = 4
N_EXPERTS = 32
TOP_K = 4
D_FF = 1024
SWIGLU_LIMIT = 7.0
SWIGLU_ALPHA = 1.702
EPS = 1e-6
DILATIONS = (1, 4, 16)
BRANCH_SPAN = 128

LANES = 128
SUBLANES = 8

IN_PROJ_ROWS = 512
ATTN_ROWS = 2048
MLSTM_CHUNK = 256
ROUTE_ROWS = 256
ROUTE_TILES_PER_STEP = 2
EXPERT_ROWS = 256
FF_CHUNK = 512
CAST_ROWS = 128
DISPATCH_ROWS = 256
HEADNORM_LANES = 256
BF16_TILE_ROWS = 16
RUN_ROWS = 64
RUN_GROUP_ROWS = 256
NOT_ROUTED = -1.0e9


def _vmem_limit(mib):
    return pltpu.CompilerParams(vmem_limit_bytes=mib * 1024 * 1024)


def _split3(a):
    p1 = a.astype(BF16)
    r1 = a - p1.astype(F32)
    p2 = r1.astype(BF16)
    r2 = r1 - p2.astype(F32)
    return p1, p2, r2.astype(BF16)


def _dot(a, b):
    return jnp.dot(a, b, preferred_element_type=F32)


def _dot_nt(a, b):
    return lax.dot_general(a, b, (((1,), (1,)), ((), ())), preferred_element_type=F32)


def _dot_tn(a, b):
    return lax.dot_general(a, b, (((0,), (0,)), ((), ())), preferred_element_type=F32)


def _in_proj_kernel(x_ref, g1_ref, wqkv_ref, wm_ref, wgc_ref, qg_ref, kg_ref, hsum_ref,
                    q_ref, k_ref, v_ref, xm_ref, vm_ref, om_ref, gates_ref):
    x = x_ref[...]
    h = x * lax.rsqrt(jnp.mean(x * x, axis=-1, keepdims=True) + EPS) * g1_ref[...]
    hb = h.astype(BF16)
    hl = (h - hb.astype(F32)).astype(BF16)
    gw = HEADNORM_LANES

    def head_norm(z, g_ref):
        parts = []
        for c in range(D_ATTN // gw):
            zc = z[:, c * gw:(c + 1) * gw]
            ms = _dot((zc * zc).astype(BF16), hsum_ref[...]) * (1.0 / HEAD_DIM_ATTN)
            parts.append(zc * lax.rsqrt(ms + EPS))
        return jnp.concatenate(parts, axis=-1) * g_ref[...]

    zq = _dot(hb, wqkv_ref[:, 0:D_ATTN])
    q_ref[...] = head_norm(zq, qg_ref) * (HEAD_DIM_ATTN ** -0.5)
    zk = _dot(hb, wqkv_ref[:, D_ATTN:2 * D_ATTN])
    k_ref[...] = head_norm(zk, kg_ref)
    v_ref[...] = _dot(hb, wqkv_ref[:, 2 * D_ATTN:3 * D_ATTN])
    xm_ref[...] = _dot(hb, wm_ref[:, 0:D_MLSTM])
    vm_ref[...] = _dot(hb, wm_ref[:, D_MLSTM:2 * D_MLSTM])
    om_ref[...] = _dot(hb, wm_ref[:, 2 * D_MLSTM:3 * D_MLSTM])
    ng = 2 * N_HEADS_MLSTM
    rows = hb.shape[0]
    both = _dot(jnp.concatenate([hb, hl], axis=0), wgc_ref[...])
    gsum = both[:rows] + both[rows:]
    gsum = gsum + pltpu.roll(gsum, LANES - ng, axis=1)
    lane = lax.broadcasted_iota(jnp.int32, gsum.shape, 1)
    gates_ref[...] = jnp.where(lane < ng, gsum, 0.0)


def _in_proj(x2, g1, w_in, q_g, k_g):
    T = x2.shape[0]
    tm = IN_PROJ_ROWS
    n_qkv = 3 * D_ATTN
    n_m = 3 * D_MLSTM
    ng = 2 * N_HEADS_MLSTM
    wqkv = w_in[:, :n_qkv].astype(BF16)
    wm = w_in[:, n_qkv:n_qkv + n_m].astype(BF16)
    wg = w_in[:, n_qkv + n_m:]
    wgh = wg.astype(BF16)
    wgl = (wg - wgh.astype(F32)).astype(BF16)
    wgc = jnp.pad(jnp.concatenate([wgh, wgl], axis=1), ((0, 0), (0, LANES - 2 * ng)))
    lane = jnp.arange(HEADNORM_LANES)
    hsum = (lane[:, None] // HEAD_DIM_ATTN == lane[None, :] // HEAD_DIM_ATTN).astype(BF16)
    qg = jnp.tile(q_g, N_HEADS_ATTN)[None, :]
    kg = jnp.tile(k_g, N_HEADS_ATTN)[None, :]
    row = lambda n: pl.BlockSpec((tm, n), lambda i: (i, 0))
    full = lambda a: pl.BlockSpec(a.shape, lambda i: (0,) * a.ndim)
    outs = [jax.ShapeDtypeStruct((T, D_ATTN), F32)] * 3 + [jax.ShapeDtypeStruct((T, D_MLSTM), F32)] * 3
    outs.append(jax.ShapeDtypeStruct((T, LANES), F32))
    g1r = g1[None, :]
    return pl.pallas_call(
        _in_proj_kernel,
        grid=(T // tm,),
        in_specs=[row(D_MODEL), full(g1r), full(wqkv), full(wm), full(wgc), full(qg), full(kg), full(hsum)],
        out_specs=[row(D_ATTN)] * 3 + [row(D_MLSTM)] * 3 + [row(LANES)],
        out_shape=outs,
        compiler_params=_vmem_limit(48),
        name="in_proj",
    )(x2, g1r, wqkv, wm, wgc, qg, kg, hsum)


def _attn_tiles():
    tiles = []
    for br, d in enumerate(DILATIONS):
        group = d * BRANCH_SPAN
        for u in range(ATTN_ROWS // group):
            for c in range(d):
                tiles.append((br, d, u * group + c))
    return tiles


def _attn_kernel(slope_ref, q_ref, kp_ref, kc_ref, vp_ref, vc_ref, o_ref,
                 kwin, vwin, bias_scr, m_scr, l_scr, acc_scr):
    W = ATTN_ROWS
    n = BRANCH_SPAN
    step = pl.program_id(2)
    kwin[0:W, :] = kp_ref[...]
    kwin[W:2 * W, :] = kc_ref[...]
    vwin[0:W, :] = vp_ref[...]
    vwin[W:2 * W, :] = vc_ref[...]

    lane = lax.broadcasted_iota(jnp.int32, (n, LANES), 1)
    first_head = lane < HEAD_DIM_ATTN
    row = lax.broadcasted_iota(jnp.int32, (n, 2 * n), 0)
    col = lax.broadcasted_iota(jnp.int32, (n, 2 * n), 1)
    j = n + row - col
    valid = jnp.logical_and(j >= 0, j <= n)
    jf = j.astype(F32)
    for br, d in enumerate(DILATIONS):
        for hh in range(2):
            slope = slope_ref[:, hh * HEAD_DIM_ATTN:hh * HEAD_DIM_ATTN + 1]
            b = jnp.where(valid, -(slope * float(d)) * jf, NEG_INF)
            bias_scr[br, hh, 0] = b
            bias_scr[br, hh, 1] = jnp.where(col >= n, b, NEG_INF)

    for br, d, qs in _attn_tiles():
        lo = W + qs - n * d
        q = q_ref[pl.ds(qs, n, stride=d), :]
        kk = jnp.concatenate([kwin[pl.ds(lo, n, stride=d), :], kwin[pl.ds(W + qs, n, stride=d), :]],
                             axis=0).astype(BF16)
        vv = jnp.concatenate([vwin[pl.ds(lo, n, stride=d), :], vwin[pl.ds(W + qs, n, stride=d), :]],
                             axis=0).astype(BF16)
        if lo < W:
            sel = jnp.where(step == 0, 1, 0)
        else:
            sel = 0
        res = []
        for hh in range(2):
            qm = jnp.where(first_head if hh == 0 else jnp.logical_not(first_head), q, 0.0).astype(BF16)
            s = _dot_nt(qm, kk) + bias_scr[br, hh, sel]
            m = jnp.max(s, axis=-1, keepdims=True)
            p = jnp.exp(s - m)
            l = jnp.sum(p, axis=-1, keepdims=True)
            res.append((m, l, _dot(p.astype(BF16), vv)))
        rows = pl.ds(qs, n, stride=d)
        m_scr[br, rows, :] = jnp.where(first_head, res[0][0], res[1][0])
        l_scr[br, rows, :] = jnp.where(first_head, res[0][1], res[1][1])
        acc_scr[br, rows, :] = jnp.where(first_head, res[0][2], res[1][2])

    chunk = 256

    def combine(i, carry):
        r = pl.ds(pl.multiple_of(i * chunk, chunk), chunk)
        m0, m1, m2 = m_scr[0, r, :], m_scr[1, r, :], m_scr[2, r, :]
        mm = jnp.maximum(jnp.maximum(m0, m1), m2)
        w0, w1, w2 = jnp.exp(m0 - mm), jnp.exp(m1 - mm), jnp.exp(m2 - mm)
        num = w0 * acc_scr[0, r, :] + w1 * acc_scr[1, r, :] + w2 * acc_scr[2, r, :]
        den = w0 * l_scr[0, r, :] + w1 * l_scr[1, r, :] + w2 * l_scr[2, r, :]
        o_ref[r, :] = num / den
        return carry

    lax.fori_loop(0, W // chunk, combine, 0)


def _attention(q, k, v, B, S):
    W = ATTN_ROWS
    nb = S // W
    n_pairs = D_ATTN // LANES
    slopes = jnp.exp2(-8.0 * jnp.arange(1, N_HEADS_ATTN + 1, dtype=F32) / N_HEADS_ATTN)
    slope_l = jnp.repeat(slopes, HEAD_DIM_ATTN).reshape(n_pairs, 1, LANES)
    cur = pl.BlockSpec((W, LANES), lambda b, p, i: (b * nb + i, p))
    prev = pl.BlockSpec((W, LANES), lambda b, p, i: (b * nb + jnp.maximum(i - 1, 0), p))
    n = BRANCH_SPAN
    nbr = len(DILATIONS)
    return pl.pallas_call(
        _attn_kernel,
        grid=(B, n_pairs, nb),
        in_specs=[pl.BlockSpec((None, 1, LANES), lambda b, p, i: (p, 0, 0)), cur, prev, cur, prev, cur],
        out_specs=cur,
        out_shape=jax.ShapeDtypeStruct((B * S, D_ATTN), F32),
        scratch_shapes=[
            pltpu.VMEM((2 * W, LANES), F32), pltpu.VMEM((2 * W, LANES), F32),
            pltpu.VMEM((nbr, 2, 2, n, 2 * n), F32),
            pltpu.VMEM((nbr, W, LANES), F32), pltpu.VMEM((nbr, W, LANES), F32),
            pltpu.VMEM((nbr, W, LANES), F32),
        ],
        compiler_params=_vmem_limit(48),
        name="attention",
    )(slope_l, q, k, k, v, v)


def _mlstm_kernel(xm_ref, vm_ref, om_ref, gcol_ref, grow_ref, cw_ref, cb_ref, wq_ref, wk_ref,
                  bcol_ref, brow_ref, g_ref, o_ref, xs_ref, xtail_ref, c_ref, n_ref, m_ref):
    L = MLSTM_CHUNK
    dh = HEAD_DIM_MLSTM
    nh = N_HEADS_MLSTM

    @pl.when(pl.program_id(1) == 0)
    def _():
        xtail_ref[...] = jnp.zeros_like(xtail_ref)
        c_ref[...] = jnp.zeros_like(c_ref)
        n_ref[...] = jnp.zeros_like(n_ref)
        m_ref[...] = jnp.zeros_like(m_ref)

    x = xm_ref[...]
    xs_ref[0:SUBLANES, :] = xtail_ref[...]
    xs_ref[SUBLANES:SUBLANES + L, :] = x
    xtail_ref[...] = x[L - SUBLANES:L, :]
    xc = cb_ref[...] + x * cw_ref[CONV_K - 1:CONV_K, :]
    for back in range(1, CONV_K):
        xc = xc + xs_ref[SUBLANES - back:SUBLANES - back + L, :] * cw_ref[CONV_K - 1 - back:CONV_K - back, :]
    xc = xc * jax.nn.sigmoid(xc)

    gc = gcol_ref[...] + bcol_ref[...]
    gr = grow_ref[...] + brow_ref[...]
    ri = lax.broadcasted_iota(jnp.int32, (L, L), 0)
    ci = lax.broadcasted_iota(jnp.int32, (L, L), 1)
    causal = ri >= ci
    tril = jnp.where(causal, 1.0, 0.0).astype(BF16)
    triu = jnp.where(ri <= ci, 1.0, 0.0).astype(BF16)
    c1, c2, c3 = _split3(jax.nn.log_sigmoid(gc))
    bc = _dot(tril, c1) + _dot(tril, c2) + _dot(tril, c3)
    r1, r2, r3 = _split3(jax.nn.log_sigmoid(gr))
    brw = _dot(r1, triu) + _dot(r2, triu) + _dot(r3, triu)

    for h in range(nh):
        hs = slice(h * dh, (h + 1) * dh)
        b_col = bc[:, nh + h:nh + h + 1]
        a_col = gc[:, h:h + 1] - b_col
        a_row = gr[h:h + 1, :] - brw[nh + h:nh + h + 1, :]
        m_prev = m_ref[h:h + 1, 0:1]
        log_d = jnp.where(causal, b_col + a_row, NEG_INF)
        log_inter = b_col + m_prev
        m_t = jnp.maximum(log_inter, jnp.max(log_d, axis=-1, keepdims=True))
        d = jnp.exp(log_d - m_t)
        inter = jnp.exp(log_inter - m_t)

        xh = xc[:, hs].astype(BF16)
        qf = _dot(xh, wq_ref[h])
        kf = _dot(xh, wk_ref[h]) * (dh ** -0.5)
        qb = qf.astype(BF16)
        vb = vm_ref[:, hs].astype(BF16)
        s = _dot_nt(qb, kf.astype(BF16)) * d
        c_old = c_ref[h]
        n_old = n_ref[h:h + 1, :]
        num = _dot(s.astype(BF16), vb) + inter * _dot(qb, c_old.astype(BF16))
        den = jnp.sum(s, axis=-1, keepdims=True) + inter * jnp.sum(qf * n_old, axis=-1, keepdims=True)
        hh = num / jnp.maximum(jnp.abs(den), jnp.exp(-m_t))

        b_last = b_col[L - 1:L, :]
        log_w = b_last + a_col
        m_new = jnp.maximum(b_last + m_prev, jnp.max(log_w, axis=0, keepdims=True))
        kw = kf * jnp.exp(log_w - m_new)
        decay = jnp.exp(b_last + m_prev - m_new)
        c_ref[h] = decay * c_old + _dot_tn(kw.astype(BF16), vb)
        n_ref[h:h + 1, :] = decay * n_old + jnp.sum(kw, axis=0, keepdims=True)
        m_ref[h:h + 1, :] = jnp.broadcast_to(m_new, (1, LANES))

        hn = hh * lax.rsqrt(jnp.mean(hh * hh, axis=-1, keepdims=True) + EPS) * g_ref[:, hs]
        o_ref[:, hs] = jax.nn.sigmoid(om_ref[:, hs]) * hn


def _mlstm(xm, vm, om, gates, conv_w, conv_b, w_mq, w_mk, b_i, b_f, mh_g, B, S):
    L = MLSTM_CHUNK
    nc = S // L
    nh = N_HEADS_MLSTM
    ng = 2 * nh
    grow = gates[:, :ng].reshape(B, S, ng).transpose(0, 2, 1)
    bias = jnp.concatenate([b_i, b_f])
    bcol = jnp.pad(bias, (0, LANES - ng))[None, :]
    brow = bias[:, None]
    wq = w_mq.astype(BF16)
    wk = w_mk.astype(BF16)
    cb = conv_b[None, :]
    g = mh_g.reshape(1, D_MLSTM)
    rows = pl.BlockSpec((L, D_MLSTM), lambda b, c: (b * nc + c, 0))
    full = lambda a: pl.BlockSpec(a.shape, lambda b, c: (0,) * a.ndim)
    return pl.pallas_call(
        _mlstm_kernel,
        grid=(B, nc),
        in_specs=[rows, rows, rows, pl.BlockSpec((L, LANES), lambda b, c: (b * nc + c, 0)),
                  pl.BlockSpec((None, ng, L), lambda b, c: (b, 0, c)),
                  full(conv_w), full(cb), full(wq), full(wk), full(bcol), full(brow), full(g)],
        out_specs=rows,
        out_shape=jax.ShapeDtypeStruct((B * S, D_MLSTM), F32),
        scratch_shapes=[
            pltpu.VMEM((L + SUBLANES, D_MLSTM), F32), pltpu.VMEM((SUBLANES, D_MLSTM), F32),
            pltpu.VMEM((nh, HEAD_DIM_MLSTM, HEAD_DIM_MLSTM), F32),
            pltpu.VMEM((SUBLANES, LANES), F32), pltpu.VMEM((SUBLANES, LANES), F32),
        ],
        compiler_params=_vmem_limit(32),
        name="mlstm",
    )(xm, vm, om, gates, grow, conv_w, cb, wq, wk, bcol, brow, g)


def _out_route_kernel(attn_ref, ml_ref, x_ref, wo_ref, g2_ref, wr_ref, br_ref, earlier_ref,
                      x1_ref, h2_ref, rt_ref, gt_ref, base_ref, cnt_ref, carry_ref):
    @pl.when(pl.program_id(0) == 0)
    def _():
        carry_ref[...] = jnp.zeros_like(carry_ref)

    carry = carry_ref[...]
    for j in range(ROUTE_TILES_PER_STEP):
        carry = _route_tile(j, carry, attn_ref, ml_ref, x_ref, wo_ref, g2_ref, wr_ref, br_ref,
                            earlier_ref, x1_ref, h2_ref, rt_ref, gt_ref, base_ref)
    carry_ref[...] = carry
    cnt_ref[...] = carry


def _route_tile(j, carry, attn_ref, ml_ref, x_ref, wo_ref, g2_ref, wr_ref, br_ref, earlier_ref,
                x1_ref, h2_ref, rt_ref, gt_ref, base_ref):
    tm = ROUTE_ROWS
    ne = N_EXPERTS
    rows = slice(j * tm, (j + 1) * tm)

    x1 = (x_ref[rows, :] + _dot(attn_ref[rows, :].astype(BF16), wo_ref[0:D_ATTN, :])
          + _dot(ml_ref[rows, :].astype(BF16), wo_ref[D_ATTN:D_ATTN + D_MLSTM, :]))
    x1_ref[rows, :] = x1
    h2 = x1 * lax.rsqrt(jnp.mean(x1 * x1, axis=-1, keepdims=True) + EPS) * g2_ref[...]
    hb = h2.astype(BF16)
    h2_ref[rows, :] = hb
    hl = (h2 - hb.astype(F32)).astype(BF16)
    hi_pass = _dot_nt(wr_ref[...], hb)
    lt = (hi_pass[:ne] + hi_pass[ne:] + _dot_nt(wr_ref[0:ne, :], hl)) + br_ref[...]

    eidx = lax.broadcasted_iota(jnp.int32, (ne, tm), 0)
    vals, hots, idxs = [], [], []
    work = lt
    for _ in range(TOP_K):
        mx = jnp.max(work, axis=0, keepdims=True)
        idx = jnp.min(jnp.where(work == mx, eidx, ne), axis=0, keepdims=True)
        hot = eidx == idx
        vals.append(mx)
        hots.append(hot)
        idxs.append(idx)
        work = jnp.where(hot, NEG_INF, work)
    ex = [jnp.exp(v - vals[0]) for v in vals]
    tot = ex[0] + ex[1] + ex[2] + ex[3]

    onehot = jnp.zeros((ne, tm), F32)
    gt = jnp.zeros((ne, tm), F32)
    for kk in range(TOP_K):
        onehot = onehot + jnp.where(hots[kk], 1.0, 0.0)
        gt = gt + jnp.where(hots[kk], ex[kk] / tot, 0.0)
    before = _dot(onehot.astype(BF16), earlier_ref[...]) + carry[:, 0:1]
    base_ref[j * ne:(j + 1) * ne, :] = carry
    rt_ref[:, rows] = jnp.where(onehot > 0.5, before, NOT_ROUTED).astype(jnp.int32)
    gt_ref[:, rows] = gt
    return carry + jnp.sum(onehot, axis=1, keepdims=True)


def _out_route(attn, ml, x2, w_out, g2, w_router, b_router):
    T = x2.shape[0]
    tm = ROUTE_ROWS
    ne = N_EXPERTS
    wo = w_out.astype(BF16)
    wrt = w_router.T
    wrh = wrt.astype(BF16)
    wr = jnp.concatenate([wrh, (wrt - wrh.astype(F32)).astype(BF16)], axis=0)
    br = b_router[:, None]
    g2r = g2[None, :]
    tok = jnp.arange(tm)
    earlier = (tok[:, None] < tok[None, :]).astype(BF16)
    ts = tm * ROUTE_TILES_PER_STEP
    row = lambda n: pl.BlockSpec((ts, n), lambda i: (i, 0))
    col = lambda n: pl.BlockSpec((n, ts), lambda i: (0, i))
    full = lambda a: pl.BlockSpec(a.shape, lambda i: (0,) * a.ndim)
    return pl.pallas_call(
        _out_route_kernel,
        grid=(T // ts,),
        in_specs=[row(D_ATTN), row(D_MLSTM), row(D_MODEL), full(wo), full(g2r), full(wr), full(br),
                  full(earlier)],
        out_specs=[row(D_MODEL), row(D_MODEL), col(ne), col(ne),
                   pl.BlockSpec((ROUTE_TILES_PER_STEP * ne, LANES), lambda i: (i, 0)),
                   pl.BlockSpec((ne, LANES), lambda i: (0, 0))],
        out_shape=[jax.ShapeDtypeStruct((T, D_MODEL), F32), jax.ShapeDtypeStruct((T, D_MODEL), BF16),
                   jax.ShapeDtypeStruct((ne, T), jnp.int32), jax.ShapeDtypeStruct((ne, T), F32),
                   jax.ShapeDtypeStruct((T // tm * ne, LANES), F32), jax.ShapeDtypeStruct((ne, LANES), F32)],
        scratch_shapes=[pltpu.VMEM((ne, LANES), F32)],
        compiler_params=_vmem_limit(32),
        name="out_route",
    )(attn, ml, x2, wo, g2r, wr, br, earlier)


def _row_copy(src_ref, dst_ref, sem, src_row, dst_row):
    return pltpu.make_async_copy(src_ref.at[pl.ds(src_row, 1)], dst_ref.at[pl.ds(dst_row, 1)], sem)


def _dispatch_kernel(runstart_ref, runlen_ref, lshift_ref, padstart_ref, npad_ref,
                     h_ref, rt_ref, xs_ref, cbuf, xbuf, zero_ref, stage_ref, sem, xsem, zsem):
    td = DISPATCH_ROWS
    ne = N_EXPERTS
    rb = RUN_ROWS
    half = rb // 2
    ta = BF16_TILE_ROWS
    grp = RUN_GROUP_ROWS // rb
    tile = pl.program_id(0)

    def window(i):
        off = runstart_ref[i] & (ta - 1)
        return off, off + runlen_ref[i]

    def first_chunk(tile_idx, e, action):
        i = tile_idx * ne + e
        off, total = window(i)
        dst = pl.multiple_of(runstart_ref[i] - off, ta)
        action(pltpu.make_async_copy(cbuf.at[pl.ds(e * rb, half)], xs_ref.at[pl.ds(dst, half)], sem))

        @pl.when(total > half)
        def _():
            action(pltpu.make_async_copy(cbuf.at[pl.ds(e * rb + half, half)],
                                         xs_ref.at[pl.ds(pl.multiple_of(dst + half, ta), half)], sem))

    @pl.when(tile > 0)
    def _():
        for e in range(ne):
            first_chunk(tile - 1, e, lambda cp: cp.wait())

    @pl.when(tile == 0)
    def _():
        stage_ref[...] = jnp.zeros_like(stage_ref)

    hb = h_ref[...]
    riota = lax.broadcasted_iota(jnp.int32, (rb, td), 0)
    rowt = lax.broadcasted_iota(jnp.int32, (ta, D_MODEL), 0)

    def select(e_slice, shift):
        return jnp.where(riota == rt_ref[e_slice, :] - shift, 1.0, 0.0)

    for g in range(ne // grp):
        experts = range(g * grp, (g + 1) * grp)
        parts = [select(slice(e, e + 1), lshift_ref[tile * ne + e] - window(tile * ne + e)[0]) for e in experts]
        pt = jnp.concatenate(parts, axis=0).astype(BF16)
        cbuf[g * RUN_GROUP_ROWS:(g + 1) * RUN_GROUP_ROWS, :] = _dot(pt, hb).astype(BF16)

    for e in range(ne):
        i = tile * ne + e
        off, total = window(i)
        head = pl.ds(e * rb, ta)
        old = stage_ref[e]
        cbuf[head, :] = jnp.where(rowt < off, old, cbuf[head, :])
        last = e * rb + jnp.minimum(total // ta * ta, rb - ta)
        keep = jnp.logical_and(runlen_ref[i] > 0, total < rb)
        stage_ref[e] = jnp.where(keep, cbuf[pl.ds(pl.multiple_of(last, ta), ta), :], old)

    for e in range(ne):
        first_chunk(tile, e, lambda cp: cp.start())

    def per_expert(e, carry):
        i = tile * ne + e
        off, total = window(i)

        def chunk(ch, c):
            pt = select(pl.ds(e, 1), lshift_ref[i] - off + ch * rb).astype(BF16)
            xbuf[...] = _dot(pt, hb).astype(BF16)
            dst = pl.multiple_of(runstart_ref[i] - off + ch * rb, ta)
            cp = pltpu.make_async_copy(xbuf, xs_ref.at[pl.ds(dst, rb)], xsem)
            cp.start()

            @pl.when(jnp.logical_and(ch == total // rb, total % rb != 0))
            def _():
                last = pl.multiple_of(total % rb // ta * ta, ta)
                stage_ref[e] = xbuf[pl.ds(last, ta), :]

            cp.wait()
            return c

        lax.fori_loop(1, (total + rb - 1) // rb, chunk, 0)
        return carry

    lax.fori_loop(0, ne, per_expert, 0)

    @pl.when(tile == pl.num_programs(0) - 1)
    def _():
        for e in range(ne):
            first_chunk(tile, e, lambda cp: cp.wait())

    @pl.when(tile == pl.num_programs(0) - 1)
    def _():
        zero_ref[...] = jnp.zeros_like(zero_ref)

        def fill(wait):
            def per_run(e, carry):
                start = padstart_ref[e]
                head = (-start) & (ta - 1)
                off = pl.multiple_of(start + head, ta)
                left = npad_ref[e] - head

                def whole(b, c):
                    dst = xs_ref.at[pl.ds(pl.multiple_of(off + b * td, ta), td)]
                    cp = pltpu.make_async_copy(zero_ref, dst, zsem)
                    cp.wait() if wait else cp.start()
                    return c

                n_whole = left // td
                lax.fori_loop(0, n_whole, whole, 0)
                off = pl.multiple_of(off + n_whole * td, ta)
                p = td // 2
                while p >= ta:
                    take = (left & p) != 0

                    @pl.when(take)
                    def _(off=off, p=p):
                        cp = pltpu.make_async_copy(zero_ref.at[pl.ds(0, p)], xs_ref.at[pl.ds(off, p)], zsem)
                        cp.wait() if wait else cp.start()

                    off = pl.multiple_of(off + jnp.where(take, p, 0), ta)
                    p //= 2
                return carry

            lax.fori_loop(0, N_EXPERTS + 1, per_run, 0)

        fill(wait=False)
        fill(wait=True)


def _dispatch(h2, rt, runstart, runlen, lshift, padstart, npad, n_slots):
    T = h2.shape[0]
    td = DISPATCH_ROWS
    ne = N_EXPERTS
    assert td == ROUTE_ROWS
    return pl.pallas_call(
        _dispatch_kernel,
        grid_spec=pltpu.PrefetchScalarGridSpec(
            num_scalar_prefetch=5,
            grid=(T // td,),
            in_specs=[pl.BlockSpec((td, D_MODEL), lambda i, *_: (i, 0)),
                      pl.BlockSpec((ne, td), lambda i, *_: (0, i))],
            out_specs=pl.BlockSpec(memory_space=pl.ANY),
            scratch_shapes=[pltpu.VMEM((ne * RUN_ROWS, D_MODEL), BF16), pltpu.VMEM((RUN_ROWS, D_MODEL), BF16),
                            pltpu.VMEM((td, D_MODEL), BF16), pltpu.VMEM((ne, BF16_TILE_ROWS, D_MODEL), BF16),
                            pltpu.SemaphoreType.DMA(()), pltpu.SemaphoreType.DMA(()),
                            pltpu.SemaphoreType.DMA(())],
        ),
        out_shape=jax.ShapeDtypeStruct((n_slots, D_MODEL), BF16),
        compiler_params=_vmem_limit(32),
        name="dispatch",
    )(runstart, runlen, lshift, padstart, npad, h2, rt)


def _expert_kernel(bexp_ref, first_ref, next_ref, nused_ref, xs_ref, wgu_hbm, bgu_ref, wd_hbm, bd_ref, y_ref,
                   wgu_st, wd_st, wgu_bf, wd_bf, wsem):
    tm = EXPERT_ROWS
    blk = pl.program_id(0)
    active = blk * tm < nused_ref[0]

    def weight_copies(e):
        return (pltpu.make_async_copy(wgu_hbm.at[e], wgu_st, wsem.at[0]),
                pltpu.make_async_copy(wd_hbm.at[e], wd_st, wsem.at[1]))

    @pl.when(blk == 0)
    def _():
        for cp in weight_copies(bexp_ref[0]):
            cp.start()

    @pl.when(jnp.logical_and(active, first_ref[blk] == 1))
    def _():
        for cp in weight_copies(bexp_ref[blk]):
            cp.wait()
        rc = CAST_ROWS

        def cast(i, carry):
            r = pl.ds(pl.multiple_of(i * rc, rc), rc)
            wgu_bf[r, :] = wgu_st[r, :].astype(BF16)
            wd_bf[r, :] = wd_st[r, :].astype(BF16)
            return carry

        lax.fori_loop(0, D_MODEL // rc, cast, 0)

        @pl.when(next_ref[blk] >= 0)
        def _():
            for cp in weight_copies(next_ref[blk]):
                cp.start()

    @pl.when(active)
    def _():
        xb = xs_ref[...]
        acc = jnp.zeros((tm, D_MODEL), F32)
        for f in range(D_FF // FF_CHUNK):
            lo = f * FF_CHUNK
            g = _dot(xb, wgu_bf[:, lo:lo + FF_CHUNK]) + bgu_ref[:, lo:lo + FF_CHUNK]
            u = _dot(xb, wgu_bf[:, D_FF + lo:D_FF + lo + FF_CHUNK]) + bgu_ref[:, D_FF + lo:D_FF + lo + FF_CHUNK]
            g = jnp.minimum(g, SWIGLU_LIMIT)
            u = jnp.clip(u, -SWIGLU_LIMIT, SWIGLU_LIMIT)
            act = g * jax.nn.sigmoid(SWIGLU_ALPHA * g) * (u + 1.0)
            acc = acc + _dot(act.astype(BF16), wd_bf[lo:lo + FF_CHUNK, :])
        y_ref[...] = (acc + bd_ref[...]).astype(y_ref.dtype)

    @pl.when(jnp.logical_not(active))
    def _():
        y_ref[...] = jnp.zeros_like(y_ref)


def _experts(xs, block_expert, first_block, next_expert, n_used, w_gate_up, b_gate_up, w_down, b_down):
    n_blocks = block_expert.shape[0]
    tm = EXPERT_ROWS
    assert D_FF == D_MODEL and xs.shape[0] == (n_blocks - 1) * tm
    bgu = b_gate_up[:, None, :]
    bd = b_down[:, None, :]
    last = n_blocks - 2
    return pl.pallas_call(
        _expert_kernel,
        grid_spec=pltpu.PrefetchScalarGridSpec(
            num_scalar_prefetch=4,
            grid=(n_blocks,),
            in_specs=[
                pl.BlockSpec((tm, D_MODEL), lambda i, be, fb, nx, nu: (jnp.minimum(i, last), 0)),
                pl.BlockSpec(memory_space=pl.ANY),
                pl.BlockSpec((None, 1, 2 * D_FF), lambda i, be, fb, nx, nu: (be[i], 0, 0)),
                pl.BlockSpec(memory_space=pl.ANY),
                pl.BlockSpec((None, 1, D_MODEL), lambda i, be, fb, nx, nu: (be[i], 0, 0)),
            ],
            out_specs=pl.BlockSpec((tm, D_MODEL), lambda i, be, fb, nx, nu: (i, 0)),
            scratch_shapes=[pltpu.VMEM((D_MODEL, 2 * D_FF), F32), pltpu.VMEM((D_FF, D_MODEL), F32),
                            pltpu.VMEM((D_MODEL, 2 * D_FF), BF16), pltpu.VMEM((D_FF, D_MODEL), BF16),
                            pltpu.SemaphoreType.DMA((2,))],
        ),
        out_shape=jax.ShapeDtypeStruct((n_blocks * tm, D_MODEL), BF16),
        compiler_params=_vmem_limit(48),
        name="experts",
    )(block_expert, first_block, next_expert, n_used, xs, w_gate_up, bgu, w_down, bd)


def _combine_kernel(fetch_ref, shift_ref, nch_ref, x1_ref, rt_ref, gt_ref, y_ref, o_ref,
                    buf, xbuf, sem, xsem):
    tc = ROUTE_ROWS
    ne = N_EXPERTS
    rb = RUN_ROWS
    grp = RUN_GROUP_ROWS // rb
    tile = pl.program_id(0)
    base = tile * ne
    slot = tile % 2

    def run_copy(tile_base, buf_slot, e):
        start = pl.multiple_of(fetch_ref[tile_base + e], BF16_TILE_ROWS)
        return pltpu.make_async_copy(y_ref.at[pl.ds(start, rb)], buf.at[buf_slot, pl.ds(e * rb, rb)],
                                     sem.at[buf_slot, e // grp])

    @pl.when(tile == 0)
    def _():
        for e in range(ne):
            run_copy(base, slot, e).start()

    @pl.when(tile + 1 < pl.num_programs(0))
    def _():
        for e in range(ne):
            run_copy(base + ne, 1 - slot, e).start()

    riota = lax.broadcasted_iota(jnp.int32, (rb, tc), 0)

    def select(e_slice, shift):
        return jnp.where(riota == rt_ref[e_slice, :] - shift, gt_ref[e_slice, :], 0.0)

    acc = x1_ref[...]
    for g in range(ne // grp):
        parts = []
        for e in range(g * grp, (g + 1) * grp):
            run_copy(base, slot, e).wait()
            parts.append(select(slice(e, e + 1), shift_ref[base + e]))
        pt = jnp.concatenate(parts, axis=0).astype(BF16)
        acc = acc + _dot_tn(pt, buf[slot, g * RUN_GROUP_ROWS:(g + 1) * RUN_GROUP_ROWS, :])
    o_ref[...] = acc

    def per_expert(e, carry):
        def chunk(ch, c):
            start = pl.multiple_of(fetch_ref[base + e] + ch * rb, BF16_TILE_ROWS)
            cp = pltpu.make_async_copy(y_ref.at[pl.ds(start, rb)], xbuf, xsem)
            cp.start()
            cp.wait()
            pt = select(pl.ds(e, 1), shift_ref[base + e] + ch * rb).astype(BF16)
            o_ref[...] += _dot_tn(pt, xbuf[...])
            return c

        lax.fori_loop(1, nch_ref[base + e], chunk, 0)
        return carry

    lax.fori_loop(0, ne, per_expert, 0)


def _combine(fetch, shift, nch, x1, rt, gt, y):
    T = x1.shape[0]
    tc = ROUTE_ROWS
    ne = N_EXPERTS
    return pl.pallas_call(
        _combine_kernel,
        grid_spec=pltpu.PrefetchScalarGridSpec(
            num_scalar_prefetch=3,
            grid=(T // tc,),
            in_specs=[pl.BlockSpec((tc, D_MODEL), lambda i, f, s, n: (i, 0)),
                      pl.BlockSpec((ne, tc), lambda i, f, s, n: (0, i)),
                      pl.BlockSpec((ne, tc), lambda i, f, s, n: (0, i)),
                      pl.BlockSpec(memory_space=pl.ANY)],
            out_specs=pl.BlockSpec((tc, D_MODEL), lambda i, f, s, n: (i, 0)),
            scratch_shapes=[pltpu.VMEM((2, ne * RUN_ROWS, D_MODEL), BF16), pltpu.VMEM((RUN_ROWS, D_MODEL), BF16),
                            pltpu.SemaphoreType.DMA((2, ne * RUN_ROWS // RUN_GROUP_ROWS)),
                            pltpu.SemaphoreType.DMA(())],
        ),
        out_shape=jax.ShapeDtypeStruct((T, D_MODEL), F32),
        compiler_params=_vmem_limit(32),
        name="combine",
    )(fetch, shift, nch, x1, rt, gt, y)


def _routing_tables(base, counts, T):
    tm = EXPERT_ROWS
    ne = N_EXPERTS
    n_tiles = T // ROUTE_ROWS
    n_blocks = -(-(T * TOP_K + ne * RUN_ROWS) // tm) + ne
    n_slots = n_blocks * tm
    sizes = counts[:, 0].astype(jnp.int32)
    psizes = (sizes + RUN_ROWS + tm - 1) // tm * tm
    pends = jnp.cumsum(psizes)
    pstarts = pends - psizes
    n_used = pends[-1:]
    padstart = jnp.concatenate([pstarts + sizes, n_used])
    npad = jnp.concatenate([psizes - sizes, n_slots - n_used])
    nb = n_blocks + 1
    blk = jnp.arange(nb, dtype=jnp.int32)
    block_expert = jnp.minimum(jnp.sum(pends[None, :] <= (blk * tm)[:, None], axis=1), ne - 1).astype(jnp.int32)
    first_block = jnp.concatenate([jnp.ones((1,), jnp.int32),
                                   (block_expert[1:] != block_expert[:-1]).astype(jnp.int32)])
    starts_group = jnp.logical_and(first_block == 1, blk * tm < n_used)
    pos = jnp.where(starts_group, blk, nb)
    later = jnp.concatenate([lax.cummin(pos[::-1])[::-1][1:], jnp.full((1,), nb, jnp.int32)])
    next_expert = jnp.where(later < nb, block_expert[jnp.minimum(later, nb - 1)], -1).astype(jnp.int32)
    tile_base = base[:, 0].astype(jnp.int32).reshape(n_tiles, ne)
    run_len = jnp.concatenate([tile_base[1:], sizes[None, :]], axis=0) - tile_base
    run_start = pstarts[None, :] + tile_base
    fetch = run_start // BF16_TILE_ROWS * BF16_TILE_ROWS
    shift = fetch - pstarts[None, :]
    nch = jnp.where(run_len > 0, (run_start - fetch + run_len + RUN_ROWS - 1) // RUN_ROWS, 0)
    flat = lambda a: a.reshape(n_tiles * ne).astype(jnp.int32)
    return (flat(run_start), flat(run_len), flat(tile_base), padstart, npad, n_slots,
            block_expert, first_block, next_expert, n_used, flat(fetch), flat(shift), flat(nch))


def kernel(x, norm1_g, w_in, q_norm_g, k_norm_g, conv_w, conv_b, w_mq, w_mk, b_igate, b_fgate, mh_norm_g,
           w_out, norm2_g, w_router, b_router, w_gate_up, b_gate_up, w_down, b_down):
    B, S, D = x.shape
    T = B * S
    assert D == D_MODEL and S % ATTN_ROWS == 0 and norm1_g.shape[0] == 1
    x2 = x.reshape(T, D)
    q, k, v, xm, vm, om, gates = _in_proj(x2, norm1_g[0], w_in[0], q_norm_g[0], k_norm_g[0])
    attn = _attention(q, k, v, B, S)
    ml = _mlstm(xm, vm, om, gates, conv_w[0], conv_b[0], w_mq[0], w_mk[0], b_igate[0], b_fgate[0],
                mh_norm_g[0], B, S)
    x1, h2, rt, gt, base, counts = _out_route(attn, ml, x2, w_out[0], norm2_g[0], w_router[0], b_router[0])
    (run_start, run_len, tile_base, padstart, npad, n_slots, block_expert, first_block, next_expert, n_used,
     fetch, shift, nch) = _routing_tables(base, counts, T)
    xs = _dispatch(h2, rt, run_start, run_len, tile_base, padstart, npad, n_slots)
    y = _experts(xs, block_expert, first_block, next_expert, n_used, w_gate_up[0], b_gate_up[0], w_down[0],
                 b_down[0])
    out = _combine(fetch, shift, nch, x1, rt, gt, y)
    return out.reshape(B, S, D)
```

```python
import functools

import jax
import jax.numpy as jnp
from jax import lax
from jax.experimental import pallas as pl
from jax.experimental.pallas import tpu as pltpu

F32 = jnp.float32
BF16 = jnp.bfloat16
NEG_INF = float("-inf")

D_MODEL = 1024
D_ATTN = 512
HEAD_DIM_ATTN = 64
N_HEADS_ATTN = 8
D_MLSTM = 512
HEAD_DIM_MLSTM = 128
N_HEADS_MLSTM = 4
CONV_K = 4
N_EXPERTS = 32
TOP_K = 4
D_FF = 1024
SWIGLU_LIMIT = 7.0
SWIGLU_ALPHA = 1.702
EPS = 1e-6
DILATIONS = (1, 4, 16)
BRANCH_SPAN = 128

LANES = 128
SUBLANES = 8

IN_PROJ_ROWS = 512
ATTN_ROWS = 2048
MLSTM_CHUNK = 256
ROUTE_ROWS = 256
ROUTE_TILES_PER_STEP = 2
EXPERT_ROWS = 256
FF_CHUNK = 1024
CAST_ROWS = 128
DISPATCH_ROWS = 256
HEADNORM_LANES = 256
BF16_TILE_ROWS = 16
RUN_ROWS = 64
RUN_GROUP_ROWS = 256
NOT_ROUTED = -1.0e9


def _vmem_limit(mib):
    return pltpu.CompilerParams(vmem_limit_bytes=mib * 1024 * 1024)


def _split3(a):
    p1 = a.astype(BF16)
    r1 = a - p1.astype(F32)
    p2 = r1.astype(BF16)
    r2 = r1 - p2.astype(F32)
    return p1, p2, r2.astype(BF16)


def _dot(a, b):
    return jnp.dot(a, b, preferred_element_type=F32)


def _dot_nt(a, b):
    return lax.dot_general(a, b, (((1,), (1,)), ((), ())), preferred_element_type=F32)


def _dot_tn(a, b):
    return lax.dot_general(a, b, (((0,), (0,)), ((), ())), preferred_element_type=F32)


def _in_proj_kernel(x_ref, g1_ref, wqkv_ref, wm_ref, wgc_ref, qg_ref, kg_ref, hsum_ref,
                    q_ref, k_ref, v_ref, xm_ref, vm_ref, om_ref, gates_ref):
    x = x_ref[...]
    h = x * lax.rsqrt(jnp.mean(x * x, axis=-1, keepdims=True) + EPS) * g1_ref[...]
    hb = h.astype(BF16)
    hl = (h - hb.astype(F32)).astype(BF16)
    gw = HEADNORM_LANES

    def head_norm(z, g_ref):
        parts = []
        for c in range(D_ATTN // gw):
            zc = z[:, c * gw:(c + 1) * gw]
            ms = _dot((zc * zc).astype(BF16), hsum_ref[...]) * (1.0 / HEAD_DIM_ATTN)
            parts.append(zc * lax.rsqrt(ms + EPS))
        return jnp.concatenate(parts, axis=-1) * g_ref[...]

    zq = _dot(hb, wqkv_ref[:, 0:D_ATTN])
    q_ref[...] = head_norm(zq, qg_ref) * (HEAD_DIM_ATTN ** -0.5)
    zk = _dot(hb, wqkv_ref[:, D_ATTN:2 * D_ATTN])
    k_ref[...] = head_norm(zk, kg_ref)
    v_ref[...] = _dot(hb, wqkv_ref[:, 2 * D_ATTN:3 * D_ATTN])
    xm_ref[...] = _dot(hb, wm_ref[:, 0:D_MLSTM])
    vm_ref[...] = _dot(hb, wm_ref[:, D_MLSTM:2 * D_MLSTM])
    om_ref[...] = _dot(hb, wm_ref[:, 2 * D_MLSTM:3 * D_MLSTM])
    ng = 2 * N_HEADS_MLSTM
    rows = hb.shape[0]
    both = _dot(jnp.concatenate([hb, hl], axis=0), wgc_ref[...])
    gsum = both[:rows] + both[rows:]
    gsum = gsum + pltpu.roll(gsum, LANES - ng, axis=1)
    lane = lax.broadcasted_iota(jnp.int32, gsum.shape, 1)
    gates_ref[...] = jnp.where(lane < ng, gsum, 0.0)


def _in_proj(x2, g1, w_in, q_g, k_g):
    T = x2.shape[0]
    tm = IN_PROJ_ROWS
    n_qkv = 3 * D_ATTN
    n_m = 3 * D_MLSTM
    ng = 2 * N_HEADS_MLSTM
    wqkv = w_in[:, :n_qkv].astype(BF16)
    wm = w_in[:, n_qkv:n_qkv + n_m].astype(BF16)
    wg = w_in[:, n_qkv + n_m:]
    wgh = wg.astype(BF16)
    wgl = (wg - wgh.astype(F32)).astype(BF16)
    wgc = jnp.pad(jnp.concatenate([wgh, wgl], axis=1), ((0, 0), (0, LANES - 2 * ng)))
    lane = jnp.arange(HEADNORM_LANES)
    hsum = (lane[:, None] // HEAD_DIM_ATTN == lane[None, :] // HEAD_DIM_ATTN).astype(BF16)
    qg = jnp.tile(q_g, N_HEADS_ATTN)[None, :]
    kg = jnp.tile(k_g, N_HEADS_ATTN)[None, :]
    row = lambda n: pl.BlockSpec((tm, n), lambda i: (i, 0))
    full = lambda a: pl.BlockSpec(a.shape, lambda i: (0,) * a.ndim)
    outs = [jax.ShapeDtypeStruct((T, D_ATTN), F32)] * 3 + [jax.ShapeDtypeStruct((T, D_MLSTM), F32)] * 3
    outs.append(jax.ShapeDtypeStruct((T, LANES), F32))
    g1r = g1[None, :]
    return pl.pallas_call(
        _in_proj_kernel,
        grid=(T // tm,),
        in_specs=[row(D_MODEL), full(g1r), full(wqkv), full(wm), full(wgc), full(qg), full(kg), full(hsum)],
        out_specs=[row(D_ATTN)] * 3 + [row(D_MLSTM)] * 3 + [row(LANES)],
        out_shape=outs,
        compiler_params=_vmem_limit(48),
        name="in_proj",
    )(x2, g1r, wqkv, wm, wgc, qg, kg, hsum)


def _attn_tiles():
    tiles = []
    for br, d in enumerate(DILATIONS):
        group = d * BRANCH_SPAN
        for u in range(ATTN_ROWS // group):
            for c in range(d):
                tiles.append((br, d, u * group + c))
    return tiles


def _attn_kernel(slope_ref, q_ref, kp_ref, kc_ref, vp_ref, vc_ref, o_ref,
                 kwin, vwin, bias_scr, m_scr, l_scr, acc_scr):
    W = ATTN_ROWS
    n = BRANCH_SPAN
    step = pl.program_id(2)
    kwin[0:W, :] = kp_ref[...]
    kwin[W:2 * W, :] = kc_ref[...]
    vwin[0:W, :] = vp_ref[...]
    vwin[W:2 * W, :] = vc_ref[...]

    lane = lax.broadcasted_iota(jnp.int32, (n, LANES), 1)
    first_head = lane < HEAD_DIM_ATTN

    @pl.when(step == 0)
    def _():
        row = lax.broadcasted_iota(jnp.int32, (n, 2 * n), 0)
        col = lax.broadcasted_iota(jnp.int32, (n, 2 * n), 1)
        j = n + row - col
        valid = jnp.logical_and(j >= 0, j <= n)
        jf = j.astype(F32)
        for br, d in enumerate(DILATIONS):
            for hh in range(2):
                slope = slope_ref[:, hh * HEAD_DIM_ATTN:hh * HEAD_DIM_ATTN + 1]
                b = jnp.where(valid, -(slope * float(d)) * jf, NEG_INF)
                bias_scr[br, hh, 0] = b
                bias_scr[br, hh, 1] = jnp.where(col >= n, b, NEG_INF)

    for br, d, qs in _attn_tiles():
        lo = W + qs - n * d
        q = q_ref[pl.ds(qs, n, stride=d), :]
        kk = jnp.concatenate([kwin[pl.ds(lo, n, stride=d), :], kwin[pl.ds(W + qs, n, stride=d), :]],
                             axis=0).astype(BF16)
        vv = jnp.concatenate([vwin[pl.ds(lo, n, stride=d), :], vwin[pl.ds(W + qs, n, stride=d), :]],
                             axis=0).astype(BF16)
        if lo < W:
            sel = jnp.where(step == 0, 1, 0)
        else:
            sel = 0
        res = []
        for hh in range(2):
            qm = jnp.where(first_head if hh == 0 else jnp.logical_not(first_head), q, 0.0).astype(BF16)
            s = _dot_nt(qm, kk) + bias_scr[br, hh, sel]
            m = jnp.max(s, axis=-1, keepdims=True)
            p = jnp.exp(s - m)
            l = jnp.sum(p, axis=-1, keepdims=True)
            res.append((m, l, _dot(p.astype(BF16), vv)))
        rows = pl.ds(qs, n, stride=d)
        m_scr[br, rows, :] = jnp.where(first_head, res[0][0], res[1][0])
        l_scr[br, rows, :] = jnp.where(first_head, res[0][1], res[1][1])
        acc_scr[br, rows, :] = jnp.where(first_head, res[0][2], res[1][2])

    chunk = 256

    def combine(i, carry):
        r = pl.ds(pl.multiple_of(i * chunk, chunk), chunk)
        m0, m1, m2 = m_scr[0, r, :], m_scr[1, r, :], m_scr[2, r, :]
        mm = jnp.maximum(jnp.maximum(m0, m1), m2)
        w0, w1, w2 = jnp.exp(m0 - mm), jnp.exp(m1 - mm), jnp.exp(m2 - mm)
        num = w0 * acc_scr[0, r, :] + w1 * acc_scr[1, r, :] + w2 * acc_scr[2, r, :]
        den = w0 * l_scr[0, r, :] + w1 * l_scr[1, r, :] + w2 * l_scr[2, r, :]
        o_ref[r, :] = num / den
        return carry

    lax.fori_loop(0, W // chunk, combine, 0)


def _attention(q, k, v, B, S):
    W = ATTN_ROWS
    nb = S // W
    n_pairs = D_ATTN // LANES
    slopes = jnp.exp2(-8.0 * jnp.arange(1, N_HEADS_ATTN + 1, dtype=F32) / N_HEADS_ATTN)
    slope_l = jnp.repeat(slopes, HEAD_DIM_ATTN).reshape(n_pairs, 1, LANES)
    cur = pl.BlockSpec((W, LANES), lambda b, p, i: (b * nb + i, p))
    prev = pl.BlockSpec((W, LANES), lambda b, p, i: (b * nb + jnp.maximum(i - 1, 0), p))
    n = BRANCH_SPAN
    nbr = len(DILATIONS)
    return pl.pallas_call(
        _attn_kernel,
        grid=(B, n_pairs, nb),
        in_specs=[pl.BlockSpec((None, 1, LANES), lambda b, p, i: (p, 0, 0)), cur, prev, cur, prev, cur],
        out_specs=cur,
        out_shape=jax.ShapeDtypeStruct((B * S, D_ATTN), F32),
        scratch_shapes=[
            pltpu.VMEM((2 * W, LANES), F32), pltpu.VMEM((2 * W, LANES), F32),
            pltpu.VMEM((nbr, 2, 2, n, 2 * n), F32),
            pltpu.VMEM((nbr, W, LANES), F32), pltpu.VMEM((nbr, W, LANES), F32),
            pltpu.VMEM((nbr, W, LANES), F32),
        ],
        compiler_params=_vmem_limit(48),
        name="attention",
    )(slope_l, q, k, k, v, v)


def _mlstm_kernel(xm_ref, vm_ref, om_ref, gcol_ref, grow_ref, cw_ref, cb_ref, wq_ref, wk_ref,
                  bcol_ref, brow_ref, g_ref, o_ref, xs_ref, xtail_ref, c_ref, n_ref, m_ref):
    L = MLSTM_CHUNK
    dh = HEAD_DIM_MLSTM
    nh = N_HEADS_MLSTM

    @pl.when(pl.program_id(1) == 0)
    def _():
        xtail_ref[...] = jnp.zeros_like(xtail_ref)
        c_ref[...] = jnp.zeros_like(c_ref)
        n_ref[...] = jnp.zeros_like(n_ref)
        m_ref[...] = jnp.zeros_like(m_ref)

    x = xm_ref[...]
    xs_ref[0:SUBLANES, :] = xtail_ref[...]
    xs_ref[SUBLANES:SUBLANES + L, :] = x
    xtail_ref[...] = x[L - SUBLANES:L, :]
    xc = cb_ref[...] + x * cw_ref[CONV_K - 1:CONV_K, :]
    for back in range(1, CONV_K):
        xc = xc + xs_ref[SUBLANES - back:SUBLANES - back + L, :] * cw_ref[CONV_K - 1 - back:CONV_K - back, :]
    xc = xc * jax.nn.sigmoid(xc)

    gc = gcol_ref[...] + bcol_ref[...]
    gr = grow_ref[...] + brow_ref[...]
    ri = lax.broadcasted_iota(jnp.int32, (L, L), 0)
    ci = lax.broadcasted_iota(jnp.int32, (L, L), 1)
    causal = ri >= ci
    tril = jnp.where(causal, 1.0, 0.0).astype(BF16)
    triu = jnp.where(ri <= ci, 1.0, 0.0).astype(BF16)
    c1, c2, c3 = _split3(jax.nn.log_sigmoid(gc))
    bc = _dot(tril, c1) + _dot(tril, c2) + _dot(tril, c3)
    r1, r2, r3 = _split3(jax.nn.log_sigmoid(gr))
    brw = _dot(r1, triu) + _dot(r2, triu) + _dot(r3, triu)

    for h in range(nh):
        hs = slice(h * dh, (h + 1) * dh)
        b_col = bc[:, nh + h:nh + h + 1]
        a_col = gc[:, h:h + 1] - b_col
        a_row = gr[h:h + 1, :] - brw[nh + h:nh + h + 1, :]
        m_prev = m_ref[h:h + 1, 0:1]
        log_d = jnp.where(causal, b_col + a_row, NEG_INF)
        log_inter = b_col + m_prev
        m_t = jnp.maximum(log_inter, jnp.max(log_d, axis=-1, keepdims=True))
        d = jnp.exp(log_d - m_t)
        inter = jnp.exp(log_inter - m_t)

        xh = xc[:, hs].astype(BF16)
        qf = _dot(xh, wq_ref[h])
        kf = _dot(xh, wk_ref[h]) * (dh ** -0.5)
        qb = qf.astype(BF16)
        vb = vm_ref[:, hs].astype(BF16)
        s = _dot_nt(qb, kf.astype(BF16)) * d
        c_old = c_ref[h]
        n_old = n_ref[h:h + 1, :]
        num = _dot(s.astype(BF16), vb) + inter * _dot(qb, c_old.astype(BF16))
        den = jnp.sum(s, axis=-1, keepdims=True) + inter * jnp.sum(qf * n_old, axis=-1, keepdims=True)
        hh = num / jnp.maximum(jnp.abs(den), jnp.exp(-m_t))

        b_last = b_col[L - 1:L, :]
        log_w = b_last + a_col
        m_new = jnp.maximum(b_last + m_prev, jnp.max(log_w, axis=0, keepdims=True))
        kw = kf * jnp.exp(log_w - m_new)
        decay = jnp.exp(b_last + m_prev - m_new)
        c_ref[h] = decay * c_old + _dot_tn(kw.astype(BF16), vb)
        n_ref[h:h + 1, :] = decay * n_old + jnp.sum(kw, axis=0, keepdims=True)
        m_ref[h:h + 1, :] = jnp.broadcast_to(m_new, (1, LANES))

        hn = hh * lax.rsqrt(jnp.mean(hh * hh, axis=-1, keepdims=True) + EPS) * g_ref[:, hs]
        o_ref[:, hs] = jax.nn.sigmoid(om_ref[:, hs]) * hn


def _mlstm(xm, vm, om, gates, conv_w, conv_b, w_mq, w_mk, b_i, b_f, mh_g, B, S):
    L = MLSTM_CHUNK
    nc = S // L
    nh = N_HEADS_MLSTM
    ng = 2 * nh
    grow = gates[:, :ng].reshape(B, S, ng).transpose(0, 2, 1)
    bias = jnp.concatenate([b_i, b_f])
    bcol = jnp.pad(bias, (0, LANES - ng))[None, :]
    brow = bias[:, None]
    wq = w_mq.astype(BF16)
    wk = w_mk.astype(BF16)
    cb = conv_b[None, :]
    g = mh_g.reshape(1, D_MLSTM)
    rows = pl.BlockSpec((L, D_MLSTM), lambda b, c: (b * nc + c, 0))
    full = lambda a: pl.BlockSpec(a.shape, lambda b, c: (0,) * a.ndim)
    return pl.pallas_call(
        _mlstm_kernel,
        grid=(B, nc),
        in_specs=[rows, rows, rows, pl.BlockSpec((L, LANES), lambda b, c: (b * nc + c, 0)),
                  pl.BlockSpec((None, ng, L), lambda b, c: (b, 0, c)),
                  full(conv_w), full(cb), full(wq), full(wk), full(bcol), full(brow), full(g)],
        out_specs=rows,
        out_shape=jax.ShapeDtypeStruct((B * S, D_MLSTM), F32),
        scratch_shapes=[
            pltpu.VMEM((L + SUBLANES, D_MLSTM), F32), pltpu.VMEM((SUBLANES, D_MLSTM), F32),
            pltpu.VMEM((nh, HEAD_DIM_MLSTM, HEAD_DIM_MLSTM), F32),
            pltpu.VMEM((SUBLANES, LANES), F32), pltpu.VMEM((SUBLANES, LANES), F32),
        ],
        compiler_params=_vmem_limit(32),
        name="mlstm",
    )(xm, vm, om, gates, grow, conv_w, cb, wq, wk, bcol, brow, g)


def _out_route_kernel(attn_ref, ml_ref, x_ref, wo_ref, g2_ref, wr_ref, br_ref, earlier_ref,
                      x1_ref, h2_ref, rt_ref, gt_ref, base_ref, cnt_ref, carry_ref):
    @pl.when(pl.program_id(0) == 0)
    def _():
        carry_ref[...] = jnp.zeros_like(carry_ref)

    carry = carry_ref[...]
    for j in range(ROUTE_TILES_PER_STEP):
        carry = _route_tile(j, carry, attn_ref, ml_ref, x_ref, wo_ref, g2_ref, wr_ref, br_ref,
                            earlier_ref, x1_ref, h2_ref, rt_ref, gt_ref, base_ref)
    carry_ref[...] = carry
    cnt_ref[...] = carry


def _route_tile(j, carry, attn_ref, ml_ref, x_ref, wo_ref, g2_ref, wr_ref, br_ref, earlier_ref,
                x1_ref, h2_ref, rt_ref, gt_ref, base_ref):
    tm = ROUTE_ROWS
    ne = N_EXPERTS
    rows = slice(j * tm, (j + 1) * tm)

    x1 = (x_ref[rows, :] + _dot(attn_ref[rows, :].astype(BF16), wo_ref[0:D_ATTN, :])
          + _dot(ml_ref[rows, :].astype(BF16), wo_ref[D_ATTN:D_ATTN + D_MLSTM, :]))
    x1_ref[rows, :] = x1
    h2 = x1 * lax.rsqrt(jnp.mean(x1 * x1, axis=-1, keepdims=True) + EPS) * g2_ref[...]
    hb = h2.astype(BF16)
    h2_ref[rows, :] = hb
    hl = (h2 - hb.astype(F32)).astype(BF16)
    hi_pass = _dot_nt(wr_ref[...], hb)
    lt = (hi_pass[:ne] + hi_pass[ne:] + _dot_nt(wr_ref[0:ne, :], hl)) + br_ref[...]

    eidx = lax.broadcasted_iota(jnp.int32, (ne, tm), 0)
    vals, hots, idxs = [], [], []
    work = lt
    for _ in range(TOP_K):
        mx = jnp.max(work, axis=0, keepdims=True)
        idx = jnp.min(jnp.where(work == mx, eidx, ne), axis=0, keepdims=True)
        hot = eidx == idx
        vals.append(mx)
        hots.append(hot)
        idxs.append(idx)
        work = jnp.where(hot, NEG_INF, work)
    ex = [jnp.exp(v - vals[0]) for v in vals]
    tot = ex[0] + ex[1] + ex[2] + ex[3]

    onehot = jnp.zeros((ne, tm), F32)
    gt = jnp.zeros((ne, tm), F32)
    for kk in range(TOP_K):
        onehot = onehot + jnp.where(hots[kk], 1.0, 0.0)
        gt = gt + jnp.where(hots[kk], ex[kk] / tot, 0.0)
    before = _dot(onehot.astype(BF16), earlier_ref[...]) + carry[:, 0:1]
    base_ref[j * ne:(j + 1) * ne, :] = carry
    rt_ref[:, rows] = jnp.where(onehot > 0.5, before, NOT_ROUTED).astype(jnp.int32)
    gt_ref[:, rows] = gt
    return carry + jnp.sum(onehot, axis=1, keepdims=True)


def _out_route(attn, ml, x2, w_out, g2, w_router, b_router):
    T = x2.shape[0]
    tm = ROUTE_ROWS
    ne = N_EXPERTS
    wo = w_out.astype(BF16)
    wrt = w_router.T
    wrh = wrt.astype(BF16)
    wr = jnp.concatenate([wrh, (wrt - wrh.astype(F32)).astype(BF16)], axis=0)
    br = b_router[:, None]
    g2r = g2[None, :]
    tok = jnp.arange(tm)
    earlier = (tok[:, None] < tok[None, :]).astype(BF16)
    ts = tm * ROUTE_TILES_PER_STEP
    row = lambda n: pl.BlockSpec((ts, n), lambda i: (i, 0))
    col = lambda n: pl.BlockSpec((n, ts), lambda i: (0, i))
    full = lambda a: pl.BlockSpec(a.shape, lambda i: (0,) * a.ndim)
    return pl.pallas_call(
        _out_route_kernel,
        grid=(T // ts,),
        in_specs=[row(D_ATTN), row(D_MLSTM), row(D_MODEL), full(wo), full(g2r), full(wr), full(br),
                  full(earlier)],
        out_specs=[row(D_MODEL), row(D_MODEL), col(ne), col(ne),
                   pl.BlockSpec((ROUTE_TILES_PER_STEP * ne, LANES), lambda i: (i, 0)),
                   pl.BlockSpec((ne, LANES), lambda i: (0, 0))],
        out_shape=[jax.ShapeDtypeStruct((T, D_MODEL), F32), jax.ShapeDtypeStruct((T, D_MODEL), BF16),
                   jax.ShapeDtypeStruct((ne, T), jnp.int32), jax.ShapeDtypeStruct((ne, T), F32),
                   jax.ShapeDtypeStruct((T // tm * ne, LANES), F32), jax.ShapeDtypeStruct((ne, LANES), F32)],
        scratch_shapes=[pltpu.VMEM((ne, LANES), F32)],
        compiler_params=_vmem_limit(32),
        name="out_route",
    )(attn, ml, x2, wo, g2r, wr, br, earlier)


def _row_copy(src_ref, dst_ref, sem, src_row, dst_row):
    return pltpu.make_async_copy(src_ref.at[pl.ds(src_row, 1)], dst_ref.at[pl.ds(dst_row, 1)], sem)


def _dispatch_kernel(runstart_ref, runlen_ref, lshift_ref, padstart_ref, npad_ref,
                     h_ref, rt_ref, xs_ref, cbuf, xbuf, zero_ref, stage_ref, sem, xsem, zsem):
    td = DISPATCH_ROWS
    ne = N_EXPERTS
    rb = RUN_ROWS
    half = rb // 2
    ta = BF16_TILE_ROWS
    grp = RUN_GROUP_ROWS // rb
    tile = pl.program_id(0)

    def window(i):
        off = runstart_ref[i] & (ta - 1)
        return off, off + runlen_ref[i]

    def first_chunk(tile_idx, e, action):
        i = tile_idx * ne + e
        off, total = window(i)
        dst = pl.multiple_of(runstart_ref[i] - off, ta)
        action(pltpu.make_async_copy(cbuf.at[pl.ds(e * rb, half)], xs_ref.at[pl.ds(dst, half)], sem))

        @pl.when(total > half)
        def _():
            action(pltpu.make_async_copy(cbuf.at[pl.ds(e * rb + half, half)],
                                         xs_ref.at[pl.ds(pl.multiple_of(dst + half, ta), half)], sem))

    @pl.when(tile > 0)
    def _():
        for e in range(ne):
            first_chunk(tile - 1, e, lambda cp: cp.wait())

    @pl.when(tile == 0)
    def _():
        stage_ref[...] = jnp.zeros_like(stage_ref)

    hb = h_ref[...]
    riota = lax.broadcasted_iota(jnp.int32, (rb, td), 0)
    rowt = lax.broadcasted_iota(jnp.int32, (ta, D_MODEL), 0)

    def select(e_slice, shift):
        return jnp.where(riota == rt_ref[e_slice, :] - shift, 1.0, 0.0)

    for g in range(ne // grp):
        experts = range(g * grp, (g + 1) * grp)
        parts = [select(slice(e, e + 1), lshift_ref[tile * ne + e] - window(tile * ne + e)[0]) for e in experts]
        pt = jnp.concatenate(parts, axis=0).astype(BF16)
        cbuf[g * RUN_GROUP_ROWS:(g + 1) * RUN_GROUP_ROWS, :] = _dot(pt, hb).astype(BF16)

    for e in range(ne):
        i = tile * ne + e
        off, total = window(i)
        head = pl.ds(e * rb, ta)
        old = stage_ref[e]
        cbuf[head, :] = jnp.where(rowt < off, old, cbuf[head, :])
        last = e * rb + jnp.minimum(total // ta * ta, rb - ta)
        keep = jnp.logical_and(runlen_ref[i] > 0, total < rb)
        stage_ref[e] = jnp.where(keep, cbuf[pl.ds(pl.multiple_of(last, ta), ta), :], old)

    for e in range(ne):
        first_chunk(tile, e, lambda cp: cp.start())

    def per_expert(e, carry):
        i = tile * ne + e
        off, total = window(i)

        def chunk(ch, c):
            pt = select(pl.ds(e, 1), lshift_ref[i] - off + ch * rb).astype(BF16)
            xbuf[...] = _dot(pt, hb).astype(BF16)
            dst = pl.multiple_of(runstart_ref[i] - off + ch * rb, ta)
            cp = pltpu.make_async_copy(xbuf, xs_ref.at[pl.ds(dst, rb)], xsem)
            cp.start()

            @pl.when(jnp.logical_and(ch == total // rb, total % rb != 0))
            def _():
                last = pl.multiple_of(total % rb // ta * ta, ta)
                stage_ref[e] = xbuf[pl.ds(last, ta), :]

            cp.wait()
            return c

        lax.fori_loop(1, (total + rb - 1) // rb, chunk, 0)
        return carry

    lax.fori_loop(0, ne, per_expert, 0)

    @pl.when(tile == pl.num_programs(0) - 1)
    def _():
        for e in range(ne):
            first_chunk(tile, e, lambda cp: cp.wait())

    @pl.when(tile == pl.num_programs(0) - 1)
    def _():
        zero_ref[...] = jnp.zeros_like(zero_ref)

        def fill(wait):
            def per_run(e, carry):
                start = padstart_ref[e]
                head = (-start) & (ta - 1)
                off = pl.multiple_of(start + head, ta)
                left = npad_ref[e] - head

                def whole(b, c):
                    dst = xs_ref.at[pl.ds(pl.multiple_of(off + b * td, ta), td)]
                    cp = pltpu.make_async_copy(zero_ref, dst, zsem)
                    cp.wait() if wait else cp.start()
                    return c

                n_whole = left // td
                lax.fori_loop(0, n_whole, whole, 0)
                off = pl.multiple_of(off + n_whole * td, ta)
                p = td // 2
                while p >= ta:
                    take = (left & p) != 0

                    @pl.when(take)
                    def _(off=off, p=p):
                        cp = pltpu.make_async_copy(zero_ref.at[pl.ds(0, p)], xs_ref.at[pl.ds(off, p)], zsem)
                        cp.wait() if wait else cp.start()

                    off = pl.multiple_of(off + jnp.where(take, p, 0), ta)
                    p //= 2
                return carry

            lax.fori_loop(0, N_EXPERTS + 1, per_run, 0)

        fill(wait=False)
        fill(wait=True)


def _dispatch(h2, rt, runstart, runlen, lshift, padstart, npad, n_slots):
    T = h2.shape[0]
    td = DISPATCH_ROWS
    ne = N_EXPERTS
    assert td == ROUTE_ROWS
    return pl.pallas_call(
        _dispatch_kernel,
        grid_spec=pltpu.PrefetchScalarGridSpec(
            num_scalar_prefetch=5,
            grid=(T // td,),
            in_specs=[pl.BlockSpec((td, D_MODEL), lambda i, *_: (i, 0)),
                      pl.BlockSpec((ne, td), lambda i, *_: (0, i))],
            out_specs=pl.BlockSpec(memory_space=pl.ANY),
            scratch_shapes=[pltpu.VMEM((ne * RUN_ROWS, D_MODEL), BF16), pltpu.VMEM((RUN_ROWS, D_MODEL), BF16),
                            pltpu.VMEM((td, D_MODEL), BF16), pltpu.VMEM((ne, BF16_TILE_ROWS, D_MODEL), BF16),
                            pltpu.SemaphoreType.DMA(()), pltpu.SemaphoreType.DMA(()),
                            pltpu.SemaphoreType.DMA(())],
        ),
        out_shape=jax.ShapeDtypeStruct((n_slots, D_MODEL), BF16),
        compiler_params=_vmem_limit(32),
        name="dispatch",
    )(runstart, runlen, lshift, padstart, npad, h2, rt)


def _expert_kernel(bexp_ref, first_ref, next_ref, nused_ref, xs_ref, wgu_hbm, bgu_ref, wd_hbm, bd_ref, y_ref,
                   wgu_st, wd_st, wgu_bf, wd_bf, wsem):
    tm = EXPERT_ROWS
    blk = pl.program_id(0)
    active = blk * tm < nused_ref[0]

    def weight_copies(e):
        return (pltpu.make_async_copy(wgu_hbm.at[e], wgu_st, wsem.at[0]),
                pltpu.make_async_copy(wd_hbm.at[e], wd_st, wsem.at[1]))

    @pl.when(blk == 0)
    def _():
        for cp in weight_copies(bexp_ref[0]):
            cp.start()

    @pl.when(jnp.logical_and(active, first_ref[blk] == 1))
    def _():
        for cp in weight_copies(bexp_ref[blk]):
            cp.wait()
        rc = CAST_ROWS

        def cast(i, carry):
            r = pl.ds(pl.multiple_of(i * rc, rc), rc)
            wgu_bf[r, :] = wgu_st[r, :].astype(BF16)
            wd_bf[r, :] = wd_st[r, :].astype(BF16)
            return carry

        lax.fori_loop(0, D_MODEL // rc, cast, 0)

        @pl.when(next_ref[blk] >= 0)
        def _():
            for cp in weight_copies(next_ref[blk]):
                cp.start()

    @pl.when(active)
    def _():
        xb = xs_ref[...]
        acc = jnp.zeros((tm, D_MODEL), F32)
        for f in range(D_FF // FF_CHUNK):
            lo = f * FF_CHUNK
            g = _dot(xb, wgu_bf[:, lo:lo + FF_CHUNK]) + bgu_ref[:, lo:lo + FF_CHUNK]
            u = _dot(xb, wgu_bf[:, D_FF + lo:D_FF + lo + FF_CHUNK]) + bgu_ref[:, D_FF + lo:D_FF + lo + FF_CHUNK]
            g = jnp.minimum(g, SWIGLU_LIMIT)
            u = jnp.clip(u, -SWIGLU_LIMIT, SWIGLU_LIMIT)
            act = g * jax.nn.sigmoid(SWIGLU_ALPHA * g) * (u + 1.0)
            acc = acc + _dot(act.astype(BF16), wd_bf[lo:lo + FF_CHUNK, :])
        y_ref[...] = (acc + bd_ref[...]).astype(y_ref.dtype)

    @pl.when(jnp.logical_not(active))
    def _():
        y_ref[...] = jnp.zeros_like(y_ref)


def _experts(xs, block_expert, first_block, next_expert, n_used, w_gate_up, b_gate_up, w_down, b_down):
    n_blocks = block_expert.shape[0]
    tm = EXPERT_ROWS
    assert D_FF == D_MODEL and xs.shape[0] == (n_blocks - 1) * tm
    bgu = b_gate_up[:, None, :]
    bd = b_down[:, None, :]
    last = n_blocks - 2
    return pl.pallas_call(
        _expert_kernel,
        grid_spec=pltpu.PrefetchScalarGridSpec(
            num_scalar_prefetch=4,
            grid=(n_blocks,),
            in_specs=[
                pl.BlockSpec((tm, D_MODEL), lambda i, be, fb, nx, nu: (jnp.minimum(i, last), 0)),
                pl.BlockSpec(memory_space=pl.ANY),
                pl.BlockSpec((None, 1, 2 * D_FF), lambda i, be, fb, nx, nu: (be[i], 0, 0)),
                pl.BlockSpec(memory_space=pl.ANY),
                pl.BlockSpec((None, 1, D_MODEL), lambda i, be, fb, nx, nu: (be[i], 0, 0)),
            ],
            out_specs=pl.BlockSpec((tm, D_MODEL), lambda i, be, fb, nx, nu: (i, 0)),
            scratch_shapes=[pltpu.VMEM((D_MODEL, 2 * D_FF), F32), pltpu.VMEM((D_FF, D_MODEL), F32),
                            pltpu.VMEM((D_MODEL, 2 * D_FF), BF16), pltpu.VMEM((D_FF, D_MODEL), BF16),
                            pltpu.SemaphoreType.DMA((2,))],
        ),
        out_shape=jax.ShapeDtypeStruct((n_blocks * tm, D_MODEL), BF16),
        compiler_params=_vmem_limit(48),
        name="experts",
    )(block_expert, first_block, next_expert, n_used, xs, w_gate_up, bgu, w_down, bd)


def _combine_kernel(fetch_ref, shift_ref, nch_ref, x1_ref, rt_ref, gt_ref, y_ref, o_ref,
                    buf, xbuf, sem, xsem):
    tc = ROUTE_ROWS
    ne = N_EXPERTS
    rb = RUN_ROWS
    grp = RUN_GROUP_ROWS // rb
    tile = pl.program_id(0)
    base = tile * ne
    slot = tile % 2

    def run_copy(tile_base, buf_slot, e):
        start = pl.multiple_of(fetch_ref[tile_base + e], BF16_TILE_ROWS)
        return pltpu.make_async_copy(y_ref.at[pl.ds(start, rb)], buf.at[buf_slot, pl.ds(e * rb, rb)],
                                     sem.at[buf_slot, e // grp])

    @pl.when(tile == 0)
    def _():
        for e in range(ne):
            run_copy(base, slot, e).start()

    @pl.when(tile + 1 < pl.num_programs(0))
    def _():
        for e in range(ne):
            run_copy(base + ne, 1 - slot, e).start()

    riota = lax.broadcasted_iota(jnp.int32, (rb, tc), 0)

    def select(e_slice, shift):
        return jnp.where(riota == rt_ref[e_slice, :] - shift, gt_ref[e_slice, :], 0.0)

    acc = x1_ref[...]
    for g in range(ne // grp):
        parts = []
        for e in range(g * grp, (g + 1) * grp):
            run_copy(base, slot, e).wait()
            parts.append(select(slice(e, e + 1), shift_ref[base + e]))
        pt = jnp.concatenate(parts, axis=0).astype(BF16)
        acc = acc + _dot_tn(pt, buf[slot, g * RUN_GROUP_ROWS:(g + 1) * RUN_GROUP_ROWS, :])
    o_ref[...] = acc

    def per_expert(e, carry):
        def chunk(ch, c):
            start = pl.multiple_of(fetch_ref[base + e] + ch * rb, BF16_TILE_ROWS)
            cp = pltpu.make_async_copy(y_ref.at[pl.ds(start, rb)], xbuf, xsem)
            cp.start()
            cp.wait()
            pt = select(pl.ds(e, 1), shift_ref[base + e] + ch * rb).astype(BF16)
            o_ref[...] += _dot_tn(pt, xbuf[...])
            return c

        lax.fori_loop(1, nch_ref[base + e], chunk, 0)
        return carry

    lax.fori_loop(0, ne, per_expert, 0)


def _combine(fetch, shift, nch, x1, rt, gt, y):
    T = x1.shape[0]
    tc = ROUTE_ROWS
    ne = N_EXPERTS
    return pl.pallas_call(
        _combine_kernel,
        grid_spec=pltpu.PrefetchScalarGridSpec(
            num_scalar_prefetch=3,
            grid=(T // tc,),
            in_specs=[pl.BlockSpec((tc, D_MODEL), lambda i, f, s, n: (i, 0)),
                      pl.BlockSpec((ne, tc), lambda i, f, s, n: (0, i)),
                      pl.BlockSpec((ne, tc), lambda i, f, s, n: (0, i)),
                      pl.BlockSpec(memory_space=pl.ANY)],
            out_specs=pl.BlockSpec((tc, D_MODEL), lambda i, f, s, n: (i, 0)),
            scratch_shapes=[pltpu.VMEM((2, ne * RUN_ROWS, D_MODEL), BF16), pltpu.VMEM((RUN_ROWS, D_MODEL), BF16),
                            pltpu.SemaphoreType.DMA((2, ne * RUN_ROWS // RUN_GROUP_ROWS)),
                            pltpu.SemaphoreType.DMA(())],
        ),
        out_shape=jax.ShapeDtypeStruct((T, D_MODEL), F32),
        compiler_params=_vmem_limit(32),
        name="combine",
    )(fetch, shift, nch, x1, rt, gt, y)


def _routing_tables(base, counts, T):
    tm = EXPERT_ROWS
    ne = N_EXPERTS
    n_tiles = T // ROUTE_ROWS
    n_blocks = -(-(T * TOP_K + ne * RUN_ROWS) // tm) + ne
    n_slots = n_blocks * tm
    sizes = counts[:, 0].astype(jnp.int32)
    psizes = (sizes + RUN_ROWS + tm - 1) // tm * tm
    pends = jnp.cumsum(psizes)
    pstarts = pends - psizes
    n_used = pends[-1:]
    padstart = jnp.concatenate([pstarts + sizes, n_used])
    npad = jnp.concatenate([psizes - sizes, n_slots - n_used])
    nb = n_blocks + 1
    blk = jnp.arange(nb, dtype=jnp.int32)
    block_expert = jnp.minimum(jnp.sum(pends[None, :] <= (blk * tm)[:, None], axis=1), ne - 1).astype(jnp.int32)
    first_block = jnp.concatenate([jnp.ones((1,), jnp.int32),
                                   (block_expert[1:] != block_expert[:-1]).astype(jnp.int32)])
    starts_group = jnp.logical_and(first_block == 1, blk * tm < n_used)
    pos = jnp.where(starts_group, blk, nb)
    later = jnp.concatenate([lax.cummin(pos[::-1])[::-1][1:], jnp.full((1,), nb, jnp.int32)])
    next_expert = jnp.where(later < nb, block_expert[jnp.minimum(later, nb - 1)], -1).astype(jnp.int32)
    tile_base = base[:, 0].astype(jnp.int32).reshape(n_tiles, ne)
    run_len = jnp.concatenate([tile_base[1:], sizes[None, :]], axis=0) - tile_base
    run_start = pstarts[None, :] + tile_base
    fetch = run_start // BF16_TILE_ROWS * BF16_TILE_ROWS
    shift = fetch - pstarts[None, :]
    nch = jnp.where(run_len > 0, (run_start - fetch + run_len + RUN_ROWS - 1) // RUN_ROWS, 0)
    flat = lambda a: a.reshape(n_tiles * ne).astype(jnp.int32)
    return (flat(run_start), flat(run_len), flat(tile_base), padstart, npad, n_slots,
            block_expert, first_block, next_expert, n_used, flat(fetch), flat(shift), flat(nch))


def kernel(x, norm1_g, w_in, q_norm_g, k_norm_g, conv_w, conv_b, w_mq, w_mk, b_igate, b_fgate, mh_norm_g,
           w_out, norm2_g, w_router, b_router, w_gate_up, b_gate_up, w_down, b_down):
    B, S, D = x.shape
    T = B * S
    assert D == D_MODEL and S % ATTN_ROWS == 0 and norm1_g.shape[0] == 1
    x2 = x.reshape(T, D)
    q, k, v, xm, vm, om, gates = _in_proj(x2, norm1_g[0], w_in[0], q_norm_g[0], k_norm_g[0])
    attn = _attention(q, k, v, B, S)
    ml = _mlstm(xm, vm, om, gates, conv_w[0], conv_b[0], w_mq[0], w_mk[0], b_igate[0], b_fgate[0],
                mh_norm_g[0], B, S)
    x1, h2, rt, gt, base, counts = _out_route(attn, ml, x2, w_out[0], norm2_g[0], w_router[0], b_router[0])
    (run_start, run_len, tile_base, padstart, npad, n_slots, block_expert, first_block, next_expert, n_used,
     fetch, shift, nch) = _routing_tables(base, counts, T)
    xs = _dispatch(h2, rt, run_start, run_len, tile_base, padstart, npad, n_slots)
    y = _experts(xs, block_expert, first_block, next_expert, n_used, w_gate_up[0], b_gate_up[0], w_down[0],
                 b_down[0])
    out = _combine(fetch, shift, nch, x1, rt, gt, y)
    return out.reshape(B, S, D)
```

```python
import jax
import jax.numpy as jnp
from jax import lax
from jax.experimental import pallas as pl
from jax.experimental.pallas import tpu as pltpu

F32 = jnp.float32
BF16 = jnp.bfloat16
NEG_INF = float("-inf")

D_MODEL = 1024
D_ATTN = 512
HEAD_DIM_ATTN = 64
N_HEADS_ATTN = 8
D_MLSTM = 512
HEAD_DIM_MLSTM = 128
N_HEADS_MLSTM = 4
CONV_K = 4
N_EXPERTS = 32
TOP_K = 4
D_FF = 1024
SWIGLU_LIMIT = 7.0
SWIGLU_ALPHA = 1.702
EPS = 1e-6
DILATIONS = (1, 4, 16)
BRANCH_SPAN = 128

LANES = 128
SUBLANES = 8

IN_PROJ_ROWS = 512
ATTN_ROWS = 2048
MLSTM_CHUNK = 256
ROUTE_ROWS = 256
ROUTE_TILES_PER_STEP = 2
EXPERT_ROWS = 256
FF_CHUNK = 1024
CAST_ROWS = 128
DISPATCH_ROWS = 256
HEADNORM_LANES = 256
BF16_TILE_ROWS = 16
RUN_ROWS = 64
RUN_GROUP_ROWS = 256
NOT_ROUTED = -1.0e9


def _vmem_limit(mib):
    return pltpu.CompilerParams(vmem_limit_bytes=mib * 1024 * 1024)


def _split3(a):
    p1 = a.astype(BF16)
    r1 = a - p1.astype(F32)
    p2 = r1.astype(BF16)
    r2 = r1 - p2.astype(F32)
    return p1, p2, r2.astype(BF16)


def _dot(a, b):
    return jnp.dot(a, b, preferred_element_type=F32)


def _dot_nt(a, b):
    return lax.dot_general(a, b, (((1,), (1,)), ((), ())), preferred_element_type=F32)


def _dot_tn(a, b):
    return lax.dot_general(a, b, (((0,), (0,)), ((), ())), preferred_element_type=F32)


def _in_proj_kernel(x_ref, g1_ref, wqkv_ref, wm_ref, wgc_ref, qg_ref, kg_ref, hsum_ref,
                    q_ref, k_ref, v_ref, xm_ref, vm_ref, om_ref, gates_ref):
    x = x_ref[...]
    h = x * lax.rsqrt(jnp.mean(x * x, axis=-1, keepdims=True) + EPS) * g1_ref[...]
    hb = h.astype(BF16)
    hl = (h - hb.astype(F32)).astype(BF16)
    gw = HEADNORM_LANES

    def head_norm(z, g_ref):
        parts = []
        for c in range(D_ATTN // gw):
            zc = z[:, c * gw:(c + 1) * gw]
            ms = _dot((zc * zc).astype(BF16), hsum_ref[...]) * (1.0 / HEAD_DIM_ATTN)
            parts.append(zc * lax.rsqrt(ms + EPS))
        return jnp.concatenate(parts, axis=-1) * g_ref[...]

    zq = _dot(hb, wqkv_ref[:, 0:D_ATTN])
    q_ref[...] = head_norm(zq, qg_ref) * (HEAD_DIM_ATTN ** -0.5)
    zk = _dot(hb, wqkv_ref[:, D_ATTN:2 * D_ATTN])
    k_ref[...] = head_norm(zk, kg_ref)
    v_ref[...] = _dot(hb, wqkv_ref[:, 2 * D_ATTN:3 * D_ATTN])
    xm_ref[...] = _dot(hb, wm_ref[:, 0:D_MLSTM])
    vm_ref[...] = _dot(hb, wm_ref[:, D_MLSTM:2 * D_MLSTM])
    om_ref[...] = _dot(hb, wm_ref[:, 2 * D_MLSTM:3 * D_MLSTM])
    ng = 2 * N_HEADS_MLSTM
    rows = hb.shape[0]
    both = _dot(jnp.concatenate([hb, hl], axis=0), wgc_ref[...])
    gsum = both[:rows] + both[rows:]
    gsum = gsum + pltpu.roll(gsum, LANES - ng, axis=1)
    lane = lax.broadcasted_iota(jnp.int32, gsum.shape, 1)
    gates_ref[...] = jnp.where(lane < ng, gsum, 0.0)


def _in_proj(x2, g1, w_in, q_g, k_g):
    T = x2.shape[0]
    tm = IN_PROJ_ROWS
    n_qkv = 3 * D_ATTN
    n_m = 3 * D_MLSTM
    ng = 2 * N_HEADS_MLSTM
    wqkv = w_in[:, :n_qkv].astype(BF16)
    wm = w_in[:, n_qkv:n_qkv + n_m].astype(BF16)
    wg = w_in[:, n_qkv + n_m:]
    wgh = wg.astype(BF16)
    wgl = (wg - wgh.astype(F32)).astype(BF16)
    wgc = jnp.pad(jnp.concatenate([wgh, wgl], axis=1), ((0, 0), (0, LANES - 2 * ng)))
    lane = jnp.arange(HEADNORM_LANES)
    hsum = (lane[:, None] // HEAD_DIM_ATTN == lane[None, :] // HEAD_DIM_ATTN).astype(BF16)
    qg = jnp.tile(q_g, N_HEADS_ATTN)[None, :]
    kg = jnp.tile(k_g, N_HEADS_ATTN)[None, :]
    row = lambda n: pl.BlockSpec((tm, n), lambda i: (i, 0))
    full = lambda a: pl.BlockSpec(a.shape, lambda i: (0,) * a.ndim)
    outs = [jax.ShapeDtypeStruct((T, D_ATTN), F32)] * 3 + [jax.ShapeDtypeStruct((T, D_MLSTM), F32)] * 3
    outs.append(jax.ShapeDtypeStruct((T, LANES), F32))
    g1r = g1[None, :]
    return pl.pallas_call(
        _in_proj_kernel,
        grid=(T // tm,),
        in_specs=[row(D_MODEL), full(g1r), full(wqkv), full(wm), full(wgc), full(qg), full(kg), full(hsum)],
        out_specs=[row(D_ATTN)] * 3 + [row(D_MLSTM)] * 3 + [row(LANES)],
        out_shape=outs,
        compiler_params=_vmem_limit(48),
        name="in_proj",
    )(x2, g1r, wqkv, wm, wgc, qg, kg, hsum)


def _attn_tiles():
    tiles = []
    for br, d in enumerate(DILATIONS):
        group = d * BRANCH_SPAN
        for u in range(ATTN_ROWS // group):
            for c in range(d):
                tiles.append((br, d, u * group + c))
    return tiles


def _attn_kernel(slope_ref, q_ref, kp_ref, kc_ref, vp_ref, vc_ref, o_ref,
                 kwin, vwin, bias_scr, m_scr, l_scr, acc_scr):
    W = ATTN_ROWS
    n = BRANCH_SPAN
    step = pl.program_id(2)
    kwin[0:W, :] = kp_ref[...]
    kwin[W:2 * W, :] = kc_ref[...]
    vwin[0:W, :] = vp_ref[...]
    vwin[W:2 * W, :] = vc_ref[...]

    lane = lax.broadcasted_iota(jnp.int32, (n, LANES), 1)
    first_head = lane < HEAD_DIM_ATTN

    @pl.when(step == 0)
    def _():
        row = lax.broadcasted_iota(jnp.int32, (n, 2 * n), 0)
        col = lax.broadcasted_iota(jnp.int32, (n, 2 * n), 1)
        j = n + row - col
        valid = jnp.logical_and(j >= 0, j <= n)
        jf = j.astype(F32)
        for br, d in enumerate(DILATIONS):
            for hh in range(2):
                slope = slope_ref[:, hh * HEAD_DIM_ATTN:hh * HEAD_DIM_ATTN + 1]
                b = jnp.where(valid, -(slope * float(d)) * jf, NEG_INF)
                bias_scr[br, hh, 0] = b
                bias_scr[br, hh, 1] = jnp.where(col >= n, b, NEG_INF)

    for br, d, qs in _attn_tiles():
        lo = W + qs - n * d
        q = q_ref[pl.ds(qs, n, stride=d), :]
        kk = jnp.concatenate([kwin[pl.ds(lo, n, stride=d), :], kwin[pl.ds(W + qs, n, stride=d), :]],
                             axis=0).astype(BF16)
        vv = jnp.concatenate([vwin[pl.ds(lo, n, stride=d), :], vwin[pl.ds(W + qs, n, stride=d), :]],
                             axis=0).astype(BF16)
        if lo < W:
            sel = jnp.where(step == 0, 1, 0)
        else:
            sel = 0
        res = []
        for hh in range(2):
            qm = jnp.where(first_head if hh == 0 else jnp.logical_not(first_head), q, 0.0).astype(BF16)
            s = _dot_nt(qm, kk) + bias_scr[br, hh, sel]
            m = jnp.max(s, axis=-1, keepdims=True)
            p = jnp.exp(s - m)
            l = jnp.sum(p, axis=-1, keepdims=True)
            res.append((m, l, _dot(p.astype(BF16), vv)))
        rows = pl.ds(qs, n, stride=d)
        m_scr[br, rows, :] = jnp.where(first_head, res[0][0], res[1][0])
        l_scr[br, rows, :] = jnp.where(first_head, res[0][1], res[1][1])
        acc_scr[br, rows, :] = jnp.where(first_head, res[0][2], res[1][2])

    chunk = 256

    def combine(i, carry):
        r = pl.ds(pl.multiple_of(i * chunk, chunk), chunk)
        m0, m1, m2 = m_scr[0, r, :], m_scr[1, r, :], m_scr[2, r, :]
        mm = jnp.maximum(jnp.maximum(m0, m1), m2)
        w0, w1, w2 = jnp.exp(m0 - mm), jnp.exp(m1 - mm), jnp.exp(m2 - mm)
        num = w0 * acc_scr[0, r, :] + w1 * acc_scr[1, r, :] + w2 * acc_scr[2, r, :]
        den = w0 * l_scr[0, r, :] + w1 * l_scr[1, r, :] + w2 * l_scr[2, r, :]
        o_ref[r, :] = num / den
        return carry

    lax.fori_loop(0, W // chunk, combine, 0)


def _attention(q, k, v, B, S):
    W = ATTN_ROWS
    nb = S // W
    n_pairs = D_ATTN // LANES
    slopes = jnp.exp2(-8.0 * jnp.arange(1, N_HEADS_ATTN + 1, dtype=F32) / N_HEADS_ATTN)
    slope_l = jnp.repeat(slopes, HEAD_DIM_ATTN).reshape(n_pairs, 1, LANES)
    cur = pl.BlockSpec((W, LANES), lambda b, p, i: (b * nb + i, p))
    prev = pl.BlockSpec((W, LANES), lambda b, p, i: (b * nb + jnp.maximum(i - 1, 0), p))
    n = BRANCH_SPAN
    nbr = len(DILATIONS)
    return pl.pallas_call(
        _attn_kernel,
        grid=(B, n_pairs, nb),
        in_specs=[pl.BlockSpec((None, 1, LANES), lambda b, p, i: (p, 0, 0)), cur, prev, cur, prev, cur],
        out_specs=cur,
        out_shape=jax.ShapeDtypeStruct((B * S, D_ATTN), F32),
        scratch_shapes=[
            pltpu.VMEM((2 * W, LANES), F32), pltpu.VMEM((2 * W, LANES), F32),
            pltpu.VMEM((nbr, 2, 2, n, 2 * n), F32),
            pltpu.VMEM((nbr, W, LANES), F32), pltpu.VMEM((nbr, W, LANES), F32),
            pltpu.VMEM((nbr, W, LANES), F32),
        ],
        compiler_params=_vmem_limit(48),
        name="attention",
    )(slope_l, q, k, k, v, v)


def _mlstm_kernel(xm_ref, vm_ref, om_ref, gcol_ref, grow_ref, cw_ref, cb_ref, wq_ref, wk_ref,
                  bcol_ref, brow_ref, g_ref, o_ref, xs_ref, xtail_ref, c_ref, n_ref, m_ref):
    L = MLSTM_CHUNK
    dh = HEAD_DIM_MLSTM
    nh = N_HEADS_MLSTM

    @pl.when(pl.program_id(1) == 0)
    def _():
        xtail_ref[...] = jnp.zeros_like(xtail_ref)
        c_ref[...] = jnp.zeros_like(c_ref)
        n_ref[...] = jnp.zeros_like(n_ref)
        m_ref[...] = jnp.zeros_like(m_ref)

    x = xm_ref[...]
    xs_ref[0:SUBLANES, :] = xtail_ref[...]
    xs_ref[SUBLANES:SUBLANES + L, :] = x
    xtail_ref[...] = x[L - SUBLANES:L, :]
    xc = cb_ref[...] + x * cw_ref[CONV_K - 1:CONV_K, :]
    for back in range(1, CONV_K):
        xc = xc + xs_ref[SUBLANES - back:SUBLANES - back + L, :] * cw_ref[CONV_K - 1 - back:CONV_K - back, :]
    xc = xc * jax.nn.sigmoid(xc)

    gc = gcol_ref[...] + bcol_ref[...]
    gr = grow_ref[...] + brow_ref[...]
    ri = lax.broadcasted_iota(jnp.int32, (L, L), 0)
    ci = lax.broadcasted_iota(jnp.int32, (L, L), 1)
    causal = ri >= ci
    tril = jnp.where(causal, 1.0, 0.0).astype(BF16)
    triu = jnp.where(ri <= ci, 1.0, 0.0).astype(BF16)
    c1, c2, c3 = _split3(jax.nn.log_sigmoid(gc))
    bc = _dot(tril, c1) + _dot(tril, c2) + _dot(tril, c3)
    r1, r2, r3 = _split3(jax.nn.log_sigmoid(gr))
    brw = _dot(r1, triu) + _dot(r2, triu) + _dot(r3, triu)

    for h in range(nh):
        hs = slice(h * dh, (h + 1) * dh)
        b_col = bc[:, nh + h:nh + h + 1]
        a_col = gc[:, h:h + 1] - b_col
        a_row = gr[h:h + 1, :] - brw[nh + h:nh + h + 1, :]
        m_prev = m_ref[h:h + 1, 0:1]
        log_d = jnp.where(causal, b_col + a_row, NEG_INF)
        log_inter = b_col + m_prev
        m_t = jnp.maximum(log_inter, jnp.max(log_d, axis=-1, keepdims=True))
        d = jnp.exp(log_d - m_t)
        inter = jnp.exp(log_inter - m_t)

        xh = xc[:, hs].astype(BF16)
        qf = _dot(xh, wq_ref[h])
        kf = _dot(xh, wk_ref[h]) * (dh ** -0.5)
        qb = qf.astype(BF16)
        vb = vm_ref[:, hs].astype(BF16)
        s = _dot_nt(qb, kf.astype(BF16)) * d
        c_old = c_ref[h]
        n_old = n_ref[h:h + 1, :]
        num = _dot(s.astype(BF16), vb) + inter * _dot(qb, c_old.astype(BF16))
        den = jnp.sum(s, axis=-1, keepdims=True) + inter * jnp.sum(qf * n_old, axis=-1, keepdims=True)
        hh = num / jnp.maximum(jnp.abs(den), jnp.exp(-m_t))

        b_last = b_col[L - 1:L, :]
        log_w = b_last + a_col
        m_new = jnp.maximum(b_last + m_prev, jnp.max(log_w, axis=0, keepdims=True))
        kw = kf * jnp.exp(log_w - m_new)
        decay = jnp.exp(b_last + m_prev - m_new)
        c_ref[h] = decay * c_old + _dot_tn(kw.astype(BF16), vb)
        n_ref[h:h + 1, :] = decay * n_old + jnp.sum(kw, axis=0, keepdims=True)
        m_ref[h:h + 1, :] = jnp.broadcast_to(m_new, (1, LANES))

        hn = hh * lax.rsqrt(jnp.mean(hh * hh, axis=-1, keepdims=True) + EPS) * g_ref[:, hs]
        o_ref[:, hs] = jax.nn.sigmoid(om_ref[:, hs]) * hn


def _mlstm(xm, vm, om, gates, conv_w, conv_b, w_mq, w_mk, b_i, b_f, mh_g, B, S):
    L = MLSTM_CHUNK
    nc = S // L
    nh = N_HEADS_MLSTM
    ng = 2 * nh
    grow = gates[:, :ng].reshape(B, S, ng).transpose(0, 2, 1)
    bias = jnp.concatenate([b_i, b_f])
    bcol = jnp.pad(bias, (0, LANES - ng))[None, :]
    brow = bias[:, None]
    wq = w_mq.astype(BF16)
    wk = w_mk.astype(BF16)
    cb = conv_b[None, :]
    g = mh_g.reshape(1, D_MLSTM)
    rows = pl.BlockSpec((L, D_MLSTM), lambda b, c: (b * nc + c, 0))
    full = lambda a: pl.BlockSpec(a.shape, lambda b, c: (0,) * a.ndim)
    return pl.pallas_call(
        _mlstm_kernel,
        grid=(B, nc),
        in_specs=[rows, rows, rows, pl.BlockSpec((L, LANES), lambda b, c: (b * nc + c, 0)),
                  pl.BlockSpec((None, ng, L), lambda b, c: (b, 0, c)),
                  full(conv_w), full(cb), full(wq), full(wk), full(bcol), full(brow), full(g)],
        out_specs=rows,
        out_shape=jax.ShapeDtypeStruct((B * S, D_MLSTM), F32),
        scratch_shapes=[
            pltpu.VMEM((L + SUBLANES, D_MLSTM), F32), pltpu.VMEM((SUBLANES, D_MLSTM), F32),
            pltpu.VMEM((nh, HEAD_DIM_MLSTM, HEAD_DIM_MLSTM), F32),
            pltpu.VMEM((SUBLANES, LANES), F32), pltpu.VMEM((SUBLANES, LANES), F32),
        ],
        compiler_params=_vmem_limit(32),
        name="mlstm",
    )(xm, vm, om, gates, grow, conv_w, cb, wq, wk, bcol, brow, g)


def _out_route_kernel(attn_ref, ml_ref, x_ref, wo_ref, g2_ref, wr_ref, br_ref, earlier_ref,
                      x1_ref, h2_ref, rt_ref, gt_ref, base_ref, cnt_ref, carry_ref):
    @pl.when(pl.program_id(0) == 0)
    def _():
        carry_ref[...] = jnp.zeros_like(carry_ref)

    carry = carry_ref[...]
    for j in range(ROUTE_TILES_PER_STEP):
        carry = _route_tile(j, carry, attn_ref, ml_ref, x_ref, wo_ref, g2_ref, wr_ref, br_ref,
                            earlier_ref, x1_ref, h2_ref, rt_ref, gt_ref, base_ref)
    carry_ref[...] = carry
    cnt_ref[...] = carry


def _route_tile(j, carry, attn_ref, ml_ref, x_ref, wo_ref, g2_ref, wr_ref, br_ref, earlier_ref,
                x1_ref, h2_ref, rt_ref, gt_ref, base_ref):
    tm = ROUTE_ROWS
    ne = N_EXPERTS
    rows = slice(j * tm, (j + 1) * tm)

    x1 = (x_ref[rows, :] + _dot(attn_ref[rows, :].astype(BF16), wo_ref[0:D_ATTN, :])
          + _dot(ml_ref[rows, :].astype(BF16), wo_ref[D_ATTN:D_ATTN + D_MLSTM, :]))
    x1_ref[rows, :] = x1
    h2 = x1 * lax.rsqrt(jnp.mean(x1 * x1, axis=-1, keepdims=True) + EPS) * g2_ref[...]
    hb = h2.astype(BF16)
    h2_ref[rows, :] = hb
    hl = (h2 - hb.astype(F32)).astype(BF16)
    hi_pass = _dot_nt(wr_ref[...], hb)
    lt = (hi_pass[:ne] + hi_pass[ne:] + _dot_nt(wr_ref[0:ne, :], hl)) + br_ref[...]

    eidx = lax.broadcasted_iota(jnp.int32, (ne, tm), 0)
    vals, hots = [], []
    work = lt
    for _ in range(TOP_K):
        mx = jnp.max(work, axis=0, keepdims=True)
        idx = jnp.min(jnp.where(work == mx, eidx, ne), axis=0, keepdims=True)
        hot = eidx == idx
        vals.append(mx)
        hots.append(hot)
        work = jnp.where(hot, NEG_INF, work)
    ex = [jnp.exp(v - vals[0]) for v in vals]
    tot = ex[0] + ex[1] + ex[2] + ex[3]

    onehot = jnp.zeros((ne, tm), F32)
    gt = jnp.zeros((ne, tm), F32)
    for kk in range(TOP_K):
        onehot = onehot + jnp.where(hots[kk], 1.0, 0.0)
        gt = gt + jnp.where(hots[kk], ex[kk] / tot, 0.0)
    before = _dot(onehot.astype(BF16), earlier_ref[...]) + carry[:, 0:1]
    base_ref[j * ne:(j + 1) * ne, :] = carry
    rt_ref[:, rows] = jnp.where(onehot > 0.5, before, NOT_ROUTED).astype(jnp.int32)
    gt_ref[:, rows] = gt
    return carry + jnp.sum(onehot, axis=1, keepdims=True)


def _out_route(attn, ml, x2, w_out, g2, w_router, b_router):
    T = x2.shape[0]
    tm = ROUTE_ROWS
    ne = N_EXPERTS
    wo = w_out.astype(BF16)
    wrt = w_router.T
    wrh = wrt.astype(BF16)
    wr = jnp.concatenate([wrh, (wrt - wrh.astype(F32)).astype(BF16)], axis=0)
    br = b_router[:, None]
    g2r = g2[None, :]
    tok = jnp.arange(tm)
    earlier = (tok[:, None] < tok[None, :]).astype(BF16)
    ts = tm * ROUTE_TILES_PER_STEP
    row = lambda n: pl.BlockSpec((ts, n), lambda i: (i, 0))
    col = lambda n: pl.BlockSpec((n, ts), lambda i: (0, i))
    full = lambda a: pl.BlockSpec(a.shape, lambda i: (0,) * a.ndim)
    return pl.pallas_call(
        _out_route_kernel,
        grid=(T // ts,),
        in_specs=[row(D_ATTN), row(D_MLSTM), row(D_MODEL), full(wo), full(g2r), full(wr), full(br),
                  full(earlier)],
        out_specs=[row(D_MODEL), row(D_MODEL), col(ne), col(ne),
                   pl.BlockSpec((ROUTE_TILES_PER_STEP * ne, LANES), lambda i: (i, 0)),
                   pl.BlockSpec((ne, LANES), lambda i: (0, 0))],
        out_shape=[jax.ShapeDtypeStruct((T, D_MODEL), F32), jax.ShapeDtypeStruct((T, D_MODEL), BF16),
                   jax.ShapeDtypeStruct((ne, T), jnp.int32), jax.ShapeDtypeStruct((ne, T), F32),
                   jax.ShapeDtypeStruct((T // tm * ne, LANES), F32), jax.ShapeDtypeStruct((ne, LANES), F32)],
        scratch_shapes=[pltpu.VMEM((ne, LANES), F32)],
        compiler_params=_vmem_limit(32),
        name="out_route",
    )(attn, ml, x2, wo, g2r, wr, br, earlier)


def _dispatch_kernel(runstart_ref, runlen_ref, lshift_ref, padstart_ref, npad_ref,
                     h_ref, rt_ref, xs_ref, cbuf, xbuf, zero_ref, stage_ref, sem, xsem, zsem):
    td = DISPATCH_ROWS
    ne = N_EXPERTS
    rb = RUN_ROWS
    half = rb // 2
    ta = BF16_TILE_ROWS
    grp = RUN_GROUP_ROWS // rb
    tile = pl.program_id(0)

    def window(i):
        off = runstart_ref[i] & (ta - 1)
        return off, off + runlen_ref[i]

    def first_chunk(tile_idx, e, action):
        i = tile_idx * ne + e
        off, total = window(i)
        dst = pl.multiple_of(runstart_ref[i] - off, ta)
        action(pltpu.make_async_copy(cbuf.at[pl.ds(e * rb, half)], xs_ref.at[pl.ds(dst, half)], sem))

        @pl.when(total > half)
        def _():
            action(pltpu.make_async_copy(cbuf.at[pl.ds(e * rb + half, half)],
                                         xs_ref.at[pl.ds(pl.multiple_of(dst + half, ta), half)], sem))

    @pl.when(tile > 0)
    def _():
        for e in range(ne):
            first_chunk(tile - 1, e, lambda cp: cp.wait())

    @pl.when(tile == 0)
    def _():
        stage_ref[...] = jnp.zeros_like(stage_ref)

    hb = h_ref[...]
    riota = lax.broadcasted_iota(jnp.int32, (rb, td), 0)
    rowt = lax.broadcasted_iota(jnp.int32, (ta, D_MODEL), 0)

    def select(e_slice, shift):
        return jnp.where(riota == rt_ref[e_slice, :] - shift, 1.0, 0.0)

    for g in range(ne // grp):
        experts = range(g * grp, (g + 1) * grp)
        parts = [select(slice(e, e + 1), lshift_ref[tile * ne + e] - window(tile * ne + e)[0]) for e in experts]
        pt = jnp.concatenate(parts, axis=0).astype(BF16)
        cbuf[g * RUN_GROUP_ROWS:(g + 1) * RUN_GROUP_ROWS, :] = _dot(pt, hb).astype(BF16)

    for e in range(ne):
        i = tile * ne + e
        off, total = window(i)
        head = pl.ds(e * rb, ta)
        old = stage_ref[e]
        cbuf[head, :] = jnp.where(rowt < off, old, cbuf[head, :])
        last = e * rb + jnp.minimum(total // ta * ta, rb - ta)
        keep = jnp.logical_and(runlen_ref[i] > 0, total < rb)
        stage_ref[e] = jnp.where(keep, cbuf[pl.ds(pl.multiple_of(last, ta), ta), :], old)

    for e in range(ne):
        first_chunk(tile, e, lambda cp: cp.start())

    def per_expert(e, carry):
        i = tile * ne + e
        off, total = window(i)

        def chunk(ch, c):
            pt = select(pl.ds(e, 1), lshift_ref[i] - off + ch * rb).astype(BF16)
            xbuf[...] = _dot(pt, hb).astype(BF16)
            dst = pl.multiple_of(runstart_ref[i] - off + ch * rb, ta)
            cp = pltpu.make_async_copy(xbuf, xs_ref.at[pl.ds(dst, rb)], xsem)
            cp.start()

            @pl.when(jnp.logical_and(ch == total // rb, total % rb != 0))
            def _():
                last = pl.multiple_of(total % rb // ta * ta, ta)
                stage_ref[e] = xbuf[pl.ds(last, ta), :]

            cp.wait()
            return c

        lax.fori_loop(1, (total + rb - 1) // rb, chunk, 0)
        return carry

    lax.fori_loop(0, ne, per_expert, 0)

    @pl.when(tile == pl.num_programs(0) - 1)
    def _():
        for e in range(ne):
            first_chunk(tile, e, lambda cp: cp.wait())

    @pl.when(tile == pl.num_programs(0) - 1)
    def _():
        zero_ref[...] = jnp.zeros_like(zero_ref)

        def fill(wait):
            def per_run(e, carry):
                start = padstart_ref[e]
                head = (-start) & (ta - 1)
                off = pl.multiple_of(start + head, ta)
                left = npad_ref[e] - head

                def whole(b, c):
                    dst = xs_ref.at[pl.ds(pl.multiple_of(off + b * td, ta), td)]
                    cp = pltpu.make_async_copy(zero_ref, dst, zsem)
                    cp.wait() if wait else cp.start()
                    return c

                n_whole = left // td
                lax.fori_loop(0, n_whole, whole, 0)
                off = pl.multiple_of(off + n_whole * td, ta)
                p = td // 2
                while p >= ta:
                    take = (left & p) != 0

                    @pl.when(take)
                    def _(off=off, p=p):
                        cp = pltpu.make_async_copy(zero_ref.at[pl.ds(0, p)], xs_ref.at[pl.ds(off, p)], zsem)
                        cp.wait() if wait else cp.start()

                    off = pl.multiple_of(off + jnp.where(take, p, 0), ta)
                    p //= 2
                return carry

            lax.fori_loop(0, N_EXPERTS + 1, per_run, 0)

        fill(wait=False)
        fill(wait=True)


def _dispatch(h2, rt, runstart, runlen, lshift, padstart, npad, n_slots):
    T = h2.shape[0]
    td = DISPATCH_ROWS
    ne = N_EXPERTS
    assert td == ROUTE_ROWS
    return pl.pallas_call(
        _dispatch_kernel,
        grid_spec=pltpu.PrefetchScalarGridSpec(
            num_scalar_prefetch=5,
            grid=(T // td,),
            in_specs=[pl.BlockSpec((td, D_MODEL), lambda i, *_: (i, 0)),
                      pl.BlockSpec((ne, td), lambda i, *_: (0, i))],
            out_specs=pl.BlockSpec(memory_space=pl.ANY),
            scratch_shapes=[pltpu.VMEM((ne * RUN_ROWS, D_MODEL), BF16), pltpu.VMEM((RUN_ROWS, D_MODEL), BF16),
                            pltpu.VMEM((td, D_MODEL), BF16), pltpu.VMEM((ne, BF16_TILE_ROWS, D_MODEL), BF16),
                            pltpu.SemaphoreType.DMA(()), pltpu.SemaphoreType.DMA(()),
                            pltpu.SemaphoreType.DMA(())],
        ),
        out_shape=jax.ShapeDtypeStruct((n_slots, D_MODEL), BF16),
        compiler_params=_vmem_limit(32),
        name="dispatch",
    )(runstart, runlen, lshift, padstart, npad, h2, rt)


def _expert_kernel(bexp_ref, first_ref, next_ref, nused_ref, xs_ref, wgu_hbm, bgu_ref, wd_hbm, bd_ref, y_ref,
                   wgu_st, wd_st, wgu_bf, wd_bf, wsem):
    tm = EXPERT_ROWS
    blk = pl.program_id(0)
    active = blk * tm < nused_ref[0]

    def weight_copies(e):
        return (pltpu.make_async_copy(wgu_hbm.at[e], wgu_st, wsem.at[0]),
                pltpu.make_async_copy(wd_hbm.at[e], wd_st, wsem.at[1]))

    @pl.when(blk == 0)
    def _():
        for cp in weight_copies(bexp_ref[0]):
            cp.start()

    @pl.when(jnp.logical_and(active, first_ref[blk] == 1))
    def _():
        for cp in weight_copies(bexp_ref[blk]):
            cp.wait()
        rc = CAST_ROWS

        def cast(i, carry):
            r = pl.ds(pl.multiple_of(i * rc, rc), rc)
            wgu_bf[r, :] = wgu_st[r, :].astype(BF16)
            wd_bf[r, :] = wd_st[r, :].astype(BF16)
            return carry

        lax.fori_loop(0, D_MODEL // rc, cast, 0)

        @pl.when(next_ref[blk] >= 0)
        def _():
            for cp in weight_copies(next_ref[blk]):
                cp.start()

    @pl.when(active)
    def _():
        xb = xs_ref[...]
        acc = jnp.zeros((tm, D_MODEL), F32)
        for f in range(D_FF // FF_CHUNK):
            lo = f * FF_CHUNK
            g = _dot(xb, wgu_bf[:, lo:lo + FF_CHUNK]) + bgu_ref[:, lo:lo + FF_CHUNK]
            u = _dot(xb, wgu_bf[:, D_FF + lo:D_FF + lo + FF_CHUNK]) + bgu_ref[:, D_FF + lo:D_FF + lo + FF_CHUNK]
            g = jnp.minimum(g, SWIGLU_LIMIT)
            u = jnp.clip(u, -SWIGLU_LIMIT, SWIGLU_LIMIT)
            act = g * jax.nn.sigmoid(SWIGLU_ALPHA * g) * (u + 1.0)
            acc = acc + _dot(act.astype(BF16), wd_bf[lo:lo + FF_CHUNK, :])
        y_ref[...] = (acc + bd_ref[...]).astype(y_ref.dtype)

    @pl.when(jnp.logical_not(active))
    def _():
        y_ref[...] = jnp.zeros_like(y_ref)


def _experts(xs, block_expert, first_block, next_expert, n_used, w_gate_up, b_gate_up, w_down, b_down):
    n_blocks = block_expert.shape[0]
    tm = EXPERT_ROWS
    assert D_FF == D_MODEL and xs.shape[0] == (n_blocks - 1) * tm
    bgu = b_gate_up[:, None, :]
    bd = b_down[:, None, :]
    last = n_blocks - 2
    return pl.pallas_call(
        _expert_kernel,
        grid_spec=pltpu.PrefetchScalarGridSpec(
            num_scalar_prefetch=4,
            grid=(n_blocks,),
            in_specs=[
                pl.BlockSpec((tm, D_MODEL), lambda i, be, fb, nx, nu: (jnp.minimum(i, last), 0)),
                pl.BlockSpec(memory_space=pl.ANY),
                pl.BlockSpec((None, 1, 2 * D_FF), lambda i, be, fb, nx, nu: (be[i], 0, 0)),
                pl.BlockSpec(memory_space=pl.ANY),
                pl.BlockSpec((None, 1, D_MODEL), lambda i, be, fb, nx, nu: (be[i], 0, 0)),
            ],
            out_specs=pl.BlockSpec((tm, D_MODEL), lambda i, be, fb, nx, nu: (i, 0)),
            scratch_shapes=[pltpu.VMEM((D_MODEL, 2 * D_FF), F32), pltpu.VMEM((D_FF, D_MODEL), F32),
                            pltpu.VMEM((D_MODEL, 2 * D_FF), BF16), pltpu.VMEM((D_FF, D_MODEL), BF16),
                            pltpu.SemaphoreType.DMA((2,))],
        ),
        out_shape=jax.ShapeDtypeStruct((n_blocks * tm, D_MODEL), BF16),
        compiler_params=_vmem_limit(48),
        name="experts",
    )(block_expert, first_block, next_expert, n_used, xs, w_gate_up, bgu, w_down, bd)


def _combine_kernel(fetch_ref, shift_ref, nch_ref, x1_ref, rt_ref, gt_ref, y_ref, o_ref,
                    buf, xbuf, sem, xsem):
    tc = ROUTE_ROWS
    ne = N_EXPERTS
    rb = RUN_ROWS
    grp = RUN_GROUP_ROWS // rb
    tile = pl.program_id(0)
    base = tile * ne
    slot = tile % 2

    def run_copy(tile_base, buf_slot, e):
        start = pl.multiple_of(fetch_ref[tile_base + e], BF16_TILE_ROWS)
        return pltpu.make_async_copy(y_ref.at[pl.ds(start, rb)], buf.at[buf_slot, pl.ds(e * rb, rb)],
                                     sem.at[buf_slot, e // grp])

    @pl.when(tile == 0)
    def _():
        for e in range(ne):
            run_copy(base, slot, e).start()

    @pl.when(tile + 1 < pl.num_programs(0))
    def _():
        for e in range(ne):
            run_copy(base + ne, 1 - slot, e).start()

    riota = lax.broadcasted_iota(jnp.int32, (rb, tc), 0)

    def select(e_slice, shift):
        return jnp.where(riota == rt_ref[e_slice, :] - shift, gt_ref[e_slice, :], 0.0)

    acc = x1_ref[...]
    for g in range(ne // grp):
        parts = []
        for e in range(g * grp, (g + 1) * grp):
            run_copy(base, slot, e).wait()
            parts.append(select(slice(e, e + 1), shift_ref[base + e]))
        pt = jnp.concatenate(parts, axis=0).astype(BF16)
        acc = acc + _dot_tn(pt, buf[slot, g * RUN_GROUP_ROWS:(g + 1) * RUN_GROUP_ROWS, :])
    o_ref[...] = acc

    def per_expert(e, carry):
        def chunk(ch, c):
            start = pl.multiple_of(fetch_ref[base + e] + ch * rb, BF16_TILE_ROWS)
            cp = pltpu.make_async_copy(y_ref.at[pl.ds(start, rb)], xbuf, xsem)
            cp.start()
            cp.wait()
            pt = select(pl.ds(e, 1), shift_ref[base + e] + ch * rb).astype(BF16)
            o_ref[...] += _dot_tn(pt, xbuf[...])
            return c

        lax.fori_loop(1, nch_ref[base + e], chunk, 0)
        return carry

    lax.fori_loop(0, ne, per_expert, 0)


def _combine(fetch, shift, nch, x1, rt, gt, y):
    T = x1.shape[0]
    tc = ROUTE_ROWS
    ne = N_EXPERTS
    return pl.pallas_call(
        _combine_kernel,
        grid_spec=pltpu.PrefetchScalarGridSpec(
            num_scalar_prefetch=3,
            grid=(T // tc,),
            in_specs=[pl.BlockSpec((tc, D_MODEL), lambda i, f, s, n: (i, 0)),
                      pl.BlockSpec((ne, tc), lambda i, f, s, n: (0, i)),
                      pl.BlockSpec((ne, tc), lambda i, f, s, n: (0, i)),
                      pl.BlockSpec(memory_space=pl.ANY)],
            out_specs=pl.BlockSpec((tc, D_MODEL), lambda i, f, s, n: (i, 0)),
            scratch_shapes=[pltpu.VMEM((2, ne * RUN_ROWS, D_MODEL), BF16), pltpu.VMEM((RUN_ROWS, D_MODEL), BF16),
                            pltpu.SemaphoreType.DMA((2, ne * RUN_ROWS // RUN_GROUP_ROWS)),
                            pltpu.SemaphoreType.DMA(())],
        ),
        out_shape=jax.ShapeDtypeStruct((T, D_MODEL), F32),
        compiler_params=_vmem_limit(32),
        name="combine",
    )(fetch, shift, nch, x1, rt, gt, y)


def _routing_tables(base, counts, T):
    tm = EXPERT_ROWS
    ne = N_EXPERTS
    n_tiles = T // ROUTE_ROWS
    n_blocks = -(-(T * TOP_K + ne * RUN_ROWS) // tm) + ne
    n_slots = n_blocks * tm
    sizes = counts[:, 0].astype(jnp.int32)
    psizes = (sizes + RUN_ROWS + tm - 1) // tm * tm
    pends = jnp.cumsum(psizes)
    pstarts = pends - psizes
    n_used = pends[-1:]
    padstart = jnp.concatenate([pstarts + sizes, n_used])
    npad = jnp.concatenate([psizes - sizes, n_slots - n_used])
    nb = n_blocks + 1
    blk = jnp.arange(nb, dtype=jnp.int32)
    block_expert = jnp.minimum(jnp.sum(pends[None, :] <= (blk * tm)[:, None], axis=1), ne - 1).astype(jnp.int32)
    first_block = jnp.concatenate([jnp.ones((1,), jnp.int32),
                                   (block_expert[1:] != block_expert[:-1]).astype(jnp.int32)])
    starts_group = jnp.logical_and(first_block == 1, blk * tm < n_used)
    pos = jnp.where(starts_group, blk, nb)
    later = jnp.concatenate([lax.cummin(pos[::-1])[::-1][1:], jnp.full((1,), nb, jnp.int32)])
    next_expert = jnp.where(later < nb, block_expert[jnp.minimum(later, nb - 1)], -1).astype(jnp.int32)
    tile_base = base[:, 0].astype(jnp.int32).reshape(n_tiles, ne)
    run_len = jnp.concatenate([tile_base[1:], sizes[None, :]], axis=0) - tile_base
    run_start = pstarts[None, :] + tile_base
    fetch = run_start // BF16_TILE_ROWS * BF16_TILE_ROWS
    shift = fetch - pstarts[None, :]
    nch = jnp.where(run_len > 0, (run_start - fetch + run_len + RUN_ROWS - 1) // RUN_ROWS, 0)
    flat = lambda a: a.reshape(n_tiles * ne).astype(jnp.int32)
    return (flat(run_start), flat(run_len), flat(tile_base), padstart, npad, n_slots,
            block_expert, first_block, next_expert, n_used, flat(fetch), flat(shift), flat(nch))


def kernel(x, norm1_g, w_in, q_norm_g, k_norm_g, conv_w, conv_b, w_mq, w_mk, b_igate, b_fgate, mh_norm_g,
           w_out, norm2_g, w_router, b_router, w_gate_up, b_gate_up, w_down, b_down):
    B, S, D = x.shape
    T = B * S
    assert D == D_MODEL and S % ATTN_ROWS == 0 and norm1_g.shape[0] == 1
    x2 = x.reshape(T, D)
    q, k, v, xm, vm, om, gates = _in_proj(x2, norm1_g[0], w_in[0], q_norm_g[0], k_norm_g[0])
    attn = _attention(q, k, v, B, S)
    ml = _mlstm(xm, vm, om, gates, conv_w[0], conv_b[0], w_mq[0], w_mk[0], b_igate[0], b_fgate[0],
                mh_norm_g[0], B, S)
    x1, h2, rt, gt, base, counts = _out_route(attn, ml, x2, w_out[0], norm2_g[0], w_router[0], b_router[0])
    (run_start, run_len, tile_base, padstart, npad, n_slots, block_expert, first_block, next_expert, n_used,
     fetch, shift, nch) = _routing_tables(base, counts, T)
    xs = _dispatch(h2, rt, run_start, run_len, tile_base, padstart, npad, n_slots)
    y = _experts(xs, block_expert, first_block, next_expert, n_used, w_gate_up[0], b_gate_up[0], w_down[0],
                 b_down[0])
    out = _combine(fetch, shift, nch, x1, rt, gt, y)
    return out.reshape(B, S, D)
```

```python
import jax
import jax.numpy as jnp
from jax import lax
from jax.experimental import pallas as pl
from jax.experimental.pallas import tpu as pltpu

F32 = jnp.float32
BF16 = jnp.bfloat16
NEG_INF = float("-inf")

D_MODEL = 1024
D_ATTN = 512
HEAD_DIM_ATTN = 64
N_HEADS_ATTN = 8
D_MLSTM = 512
HEAD_DIM_MLSTM = 128
N_HEADS_MLSTM = 4
CONV_K = 4
N_EXPERTS = 32
TOP_K = 4
D_FF = 1024
SWIGLU_LIMIT = 7.0
SWIGLU_ALPHA = 1.702
EPS = 1e-6
LOG2_E = 1.4426950408889634
DILATIONS = (1, 4, 16)
BRANCH_SPAN = 128

LANES = 128
SUBLANES = 8

IN_PROJ_ROWS = 512
ATTN_ROWS = 2048
MLSTM_CHUNK = 256
ROUTE_ROWS = 256
ROUTE_TILES_PER_STEP = 2
EXPERT_ROWS = 256
FF_CHUNK = 1024
CAST_ROWS = 128
DISPATCH_ROWS = 256
HEADNORM_LANES = 256
BF16_TILE_ROWS = 16
RUN_ROWS = 64
RUN_GROUP_ROWS = 256
NOT_ROUTED = -1.0e9


def _vmem_limit(mib):
    return pltpu.CompilerParams(vmem_limit_bytes=mib * 1024 * 1024)


def _split3(a):
    p1 = a.astype(BF16)
    r1 = a - p1.astype(F32)
    p2 = r1.astype(BF16)
    r2 = r1 - p2.astype(F32)
    return p1, p2, r2.astype(BF16)


def _dot(a, b):
    return jnp.dot(a, b, preferred_element_type=F32)


def _dot_nt(a, b):
    return lax.dot_general(a, b, (((1,), (1,)), ((), ())), preferred_element_type=F32)


def _dot_tn(a, b):
    return lax.dot_general(a, b, (((0,), (0,)), ((), ())), preferred_element_type=F32)


def _in_proj_kernel(x_ref, g1_ref, wqkv_ref, wm_ref, wgc_ref, qg_ref, kg_ref, hsum_ref,
                    q_ref, k_ref, v_ref, xm_ref, vm_ref, om_ref, gates_ref):
    x = x_ref[...]
    h = x * lax.rsqrt(jnp.mean(x * x, axis=-1, keepdims=True) + EPS) * g1_ref[...]
    hb = h.astype(BF16)
    hl = (h - hb.astype(F32)).astype(BF16)
    gw = HEADNORM_LANES

    def head_norm(z, g_ref):
        parts = []
        for c in range(D_ATTN // gw):
            zc = z[:, c * gw:(c + 1) * gw]
            ms = _dot((zc * zc).astype(BF16), hsum_ref[...]) * (1.0 / HEAD_DIM_ATTN)
            parts.append(zc * lax.rsqrt(ms + EPS))
        return jnp.concatenate(parts, axis=-1) * g_ref[...]

    zq = _dot(hb, wqkv_ref[:, 0:D_ATTN])
    q_ref[...] = head_norm(zq, qg_ref) * (HEAD_DIM_ATTN ** -0.5 * LOG2_E)
    zk = _dot(hb, wqkv_ref[:, D_ATTN:2 * D_ATTN])
    k_ref[...] = head_norm(zk, kg_ref)
    v_ref[...] = _dot(hb, wqkv_ref[:, 2 * D_ATTN:3 * D_ATTN])
    xm_ref[...] = _dot(hb, wm_ref[:, 0:D_MLSTM])
    vm_ref[...] = _dot(hb, wm_ref[:, D_MLSTM:2 * D_MLSTM])
    om_ref[...] = _dot(hb, wm_ref[:, 2 * D_MLSTM:3 * D_MLSTM])
    ng = 2 * N_HEADS_MLSTM
    rows = hb.shape[0]
    both = _dot(jnp.concatenate([hb, hl], axis=0), wgc_ref[...])
    gsum = both[:rows] + both[rows:]
    gsum = gsum + pltpu.roll(gsum, LANES - ng, axis=1)
    lane = lax.broadcasted_iota(jnp.int32, gsum.shape, 1)
    gates_ref[...] = jnp.where(lane < ng, gsum, 0.0)


def _in_proj(x2, g1, w_in, q_g, k_g):
    T = x2.shape[0]
    tm = IN_PROJ_ROWS
    n_qkv = 3 * D_ATTN
    n_m = 3 * D_MLSTM
    ng = 2 * N_HEADS_MLSTM
    wqkv = w_in[:, :n_qkv].astype(BF16)
    wm = w_in[:, n_qkv:n_qkv + n_m].astype(BF16)
    wg = w_in[:, n_qkv + n_m:]
    wgh = wg.astype(BF16)
    wgl = (wg - wgh.astype(F32)).astype(BF16)
    wgc = jnp.pad(jnp.concatenate([wgh, wgl], axis=1), ((0, 0), (0, LANES - 2 * ng)))
    lane = jnp.arange(HEADNORM_LANES)
    hsum = (lane[:, None] // HEAD_DIM_ATTN == lane[None, :] // HEAD_DIM_ATTN).astype(BF16)
    qg = jnp.tile(q_g, N_HEADS_ATTN)[None, :]
    kg = jnp.tile(k_g, N_HEADS_ATTN)[None, :]
    row = lambda n: pl.BlockSpec((tm, n), lambda i: (i, 0))
    full = lambda a: pl.BlockSpec(a.shape, lambda i: (0,) * a.ndim)
    outs = [jax.ShapeDtypeStruct((T, D_ATTN), F32)] * 3 + [jax.ShapeDtypeStruct((T, D_MLSTM), F32)] * 3
    outs.append(jax.ShapeDtypeStruct((T, LANES), F32))
    g1r = g1[None, :]
    return pl.pallas_call(
        _in_proj_kernel,
        grid=(T // tm,),
        in_specs=[row(D_MODEL), full(g1r), full(wqkv), full(wm), full(wgc), full(qg), full(kg), full(hsum)],
        out_specs=[row(D_ATTN)] * 3 + [row(D_MLSTM)] * 3 + [row(LANES)],
        out_shape=outs,
        compiler_params=_vmem_limit(48),
        name="in_proj",
    )(x2, g1r, wqkv, wm, wgc, qg, kg, hsum)


def _attn_tiles():
    tiles = []
    for br, d in enumerate(DILATIONS):
        group = d * BRANCH_SPAN
        for u in range(ATTN_ROWS // group):
            for c in range(d):
                tiles.append((br, d, u * group + c))
    return tiles


def _attn_kernel(slope_ref, q_ref, kp_ref, kc_ref, vp_ref, vc_ref, o_ref,
                 kwin, vwin, bias_scr, m_scr, l_scr, acc_scr):
    W = ATTN_ROWS
    n = BRANCH_SPAN
    step = pl.program_id(2)
    kwin[0:W, :] = kp_ref[...]
    kwin[W:2 * W, :] = kc_ref[...]
    vwin[0:W, :] = vp_ref[...]
    vwin[W:2 * W, :] = vc_ref[...]

    lane = lax.broadcasted_iota(jnp.int32, (n, LANES), 1)
    first_head = lane < HEAD_DIM_ATTN

    @pl.when(step == 0)
    def _():
        row = lax.broadcasted_iota(jnp.int32, (n, 2 * n), 0)
        col = lax.broadcasted_iota(jnp.int32, (n, 2 * n), 1)
        j = n + row - col
        valid = jnp.logical_and(j >= 0, j <= n)
        jf = j.astype(F32)
        for br, d in enumerate(DILATIONS):
            for hh in range(2):
                slope = slope_ref[:, hh * HEAD_DIM_ATTN:hh * HEAD_DIM_ATTN + 1]
                b = jnp.where(valid, -(slope * (float(d) * LOG2_E)) * jf, NEG_INF)
                bias_scr[br, hh, 0] = b
                bias_scr[br, hh, 1] = jnp.where(col >= n, b, NEG_INF)

    for br, d, qs in _attn_tiles():
        lo = W + qs - n * d
        q = q_ref[pl.ds(qs, n, stride=d), :]
        kk = jnp.concatenate([kwin[pl.ds(lo, n, stride=d), :], kwin[pl.ds(W + qs, n, stride=d), :]],
                             axis=0).astype(BF16)
        vv = jnp.concatenate([vwin[pl.ds(lo, n, stride=d), :], vwin[pl.ds(W + qs, n, stride=d), :]],
                             axis=0).astype(BF16)
        if lo < W:
            sel = jnp.where(step == 0, 1, 0)
        else:
            sel = 0
        qb = q.astype(BF16)
        zero = jnp.zeros_like(qb)
        q2 = jnp.concatenate([jnp.where(first_head, qb, zero), jnp.where(first_head, zero, qb)], axis=0)
        s2 = _dot_nt(q2, kk)
        ms, ls, ps = [], [], []
        for hh in range(2):
            s = s2[hh * n:(hh + 1) * n] + bias_scr[br, hh, sel]
            m = jnp.max(s, axis=-1, keepdims=True)
            p = jnp.exp2(s - m)
            ms.append(m)
            ls.append(jnp.sum(p, axis=-1, keepdims=True))
            ps.append(p.astype(BF16))
        pv = _dot(jnp.concatenate(ps, axis=0), vv)
        rows = pl.ds(qs, n, stride=d)
        m_scr[br, rows, :] = jnp.where(first_head, ms[0], ms[1])
        l_scr[br, rows, :] = jnp.where(first_head, ls[0], ls[1])
        acc_scr[br, rows, :] = jnp.where(first_head, pv[:n], pv[n:])

    chunk = 256

    def combine(i, carry):
        r = pl.ds(pl.multiple_of(i * chunk, chunk), chunk)
        m0, m1, m2 = m_scr[0, r, :], m_scr[1, r, :], m_scr[2, r, :]
        mm = jnp.maximum(jnp.maximum(m0, m1), m2)
        w0, w1, w2 = jnp.exp2(m0 - mm), jnp.exp2(m1 - mm), jnp.exp2(m2 - mm)
        num = w0 * acc_scr[0, r, :] + w1 * acc_scr[1, r, :] + w2 * acc_scr[2, r, :]
        den = w0 * l_scr[0, r, :] + w1 * l_scr[1, r, :] + w2 * l_scr[2, r, :]
        o_ref[r, :] = num / den
        return carry

    lax.fori_loop(0, W // chunk, combine, 0)


def _attention(q, k, v, B, S):
    W = ATTN_ROWS
    nb = S // W
    n_pairs = D_ATTN // LANES
    slopes = jnp.exp2(-8.0 * jnp.arange(1, N_HEADS_ATTN + 1, dtype=F32) / N_HEADS_ATTN)
    slope_l = jnp.repeat(slopes, HEAD_DIM_ATTN).reshape(n_pairs, 1, LANES)
    cur = pl.BlockSpec((W, LANES), lambda b, p, i: (b * nb + i, p))
    prev = pl.BlockSpec((W, LANES), lambda b, p, i: (b * nb + jnp.maximum(i - 1, 0), p))
    n = BRANCH_SPAN
    nbr = len(DILATIONS)
    return pl.pallas_call(
        _attn_kernel,
        grid=(B, n_pairs, nb),
        in_specs=[pl.BlockSpec((None, 1, LANES), lambda b, p, i: (p, 0, 0)), cur, prev, cur, prev, cur],
        out_specs=cur,
        out_shape=jax.ShapeDtypeStruct((B * S, D_ATTN), F32),
        scratch_shapes=[
            pltpu.VMEM((2 * W, LANES), F32), pltpu.VMEM((2 * W, LANES), F32),
            pltpu.VMEM((nbr, 2, 2, n, 2 * n), F32),
            pltpu.VMEM((nbr, W, LANES), F32), pltpu.VMEM((nbr, W, LANES), F32),
            pltpu.VMEM((nbr, W, LANES), F32),
        ],
        compiler_params=_vmem_limit(48),
        name="attention",
    )(slope_l, q, k, k, v, v)


def _mlstm_kernel(xm_ref, vm_ref, om_ref, gcol_ref, grow_ref, cw_ref, cb_ref, wq_ref, wk_ref,
                  bcol_ref, brow_ref, g_ref, o_ref, xs_ref, xtail_ref, c_ref, n_ref, m_ref):
    L = MLSTM_CHUNK
    dh = HEAD_DIM_MLSTM
    nh = N_HEADS_MLSTM

    @pl.when(pl.program_id(1) == 0)
    def _():
        xtail_ref[...] = jnp.zeros_like(xtail_ref)
        c_ref[...] = jnp.zeros_like(c_ref)
        n_ref[...] = jnp.zeros_like(n_ref)
        m_ref[...] = jnp.zeros_like(m_ref)

    x = xm_ref[...]
    xs_ref[0:SUBLANES, :] = xtail_ref[...]
    xs_ref[SUBLANES:SUBLANES + L, :] = x
    xtail_ref[...] = x[L - SUBLANES:L, :]
    xc = cb_ref[...] + x * cw_ref[CONV_K - 1:CONV_K, :]
    for back in range(1, CONV_K):
        xc = xc + xs_ref[SUBLANES - back:SUBLANES - back + L, :] * cw_ref[CONV_K - 1 - back:CONV_K - back, :]
    xc = xc * jax.nn.sigmoid(xc)

    gc = gcol_ref[...] + bcol_ref[...]
    gr = grow_ref[...] + brow_ref[...]
    ri = lax.broadcasted_iota(jnp.int32, (L, L), 0)
    ci = lax.broadcasted_iota(jnp.int32, (L, L), 1)
    causal = ri >= ci
    tril = jnp.where(causal, 1.0, 0.0).astype(BF16)
    triu = jnp.where(ri <= ci, 1.0, 0.0).astype(BF16)
    c1, c2, c3 = _split3(jax.nn.log_sigmoid(gc))
    bc = _dot(tril, c1) + _dot(tril, c2) + _dot(tril, c3)
    r1, r2, r3 = _split3(jax.nn.log_sigmoid(gr))
    brw = _dot(r1, triu) + _dot(r2, triu) + _dot(r3, triu)

    for h in range(nh):
        hs = slice(h * dh, (h + 1) * dh)
        b_col = bc[:, nh + h:nh + h + 1]
        a_col = gc[:, h:h + 1] - b_col
        a_row = gr[h:h + 1, :] - brw[nh + h:nh + h + 1, :]
        m_prev = m_ref[h:h + 1, 0:1]
        log_d = jnp.where(causal, b_col + a_row, NEG_INF)
        log_inter = b_col + m_prev
        m_t = jnp.maximum(log_inter, jnp.max(log_d, axis=-1, keepdims=True))
        d = jnp.exp(log_d - m_t)
        inter = jnp.exp(log_inter - m_t)

        xh = xc[:, hs].astype(BF16)
        qf = _dot(xh, wq_ref[h])
        kf = _dot(xh, wk_ref[h]) * (dh ** -0.5)
        qb = qf.astype(BF16)
        vb = vm_ref[:, hs].astype(BF16)
        s = _dot_nt(qb, kf.astype(BF16)) * d
        c_old = c_ref[h]
        n_old = n_ref[h:h + 1, :]
        num = _dot(s.astype(BF16), vb) + inter * _dot(qb, c_old.astype(BF16))
        den = jnp.sum(s, axis=-1, keepdims=True) + inter * jnp.sum(qf * n_old, axis=-1, keepdims=True)
        hh = num / jnp.maximum(jnp.abs(den), jnp.exp(-m_t))

        b_last = b_col[L - 1:L, :]
        log_w = b_last + a_col
        m_new = jnp.maximum(b_last + m_prev, jnp.max(log_w, axis=0, keepdims=True))
        kw = kf * jnp.exp(log_w - m_new)
        decay = jnp.exp(b_last + m_prev - m_new)
        c_ref[h] = decay * c_old + _dot_tn(kw.astype(BF16), vb)
        n_ref[h:h + 1, :] = decay * n_old + jnp.sum(kw, axis=0, keepdims=True)
        m_ref[h:h + 1, :] = jnp.broadcast_to(m_new, (1, LANES))

        hn = hh * lax.rsqrt(jnp.mean(hh * hh, axis=-1, keepdims=True) + EPS) * g_ref[:, hs]
        o_ref[:, hs] = jax.nn.sigmoid(om_ref[:, hs]) * hn


def _mlstm(xm, vm, om, gates, conv_w, conv_b, w_mq, w_mk, b_i, b_f, mh_g, B, S):
    L = MLSTM_CHUNK
    nc = S // L
    nh = N_HEADS_MLSTM
    ng = 2 * nh
    grow = gates[:, :ng].reshape(B, S, ng).transpose(0, 2, 1)
    bias = jnp.concatenate([b_i, b_f])
    bcol = jnp.pad(bias, (0, LANES - ng))[None, :]
    brow = bias[:, None]
    wq = w_mq.astype(BF16)
    wk = w_mk.astype(BF16)
    cb = conv_b[None, :]
    g = mh_g.reshape(1, D_MLSTM)
    rows = pl.BlockSpec((L, D_MLSTM), lambda b, c: (b * nc + c, 0))
    full = lambda a: pl.BlockSpec(a.shape, lambda b, c: (0,) * a.ndim)
    return pl.pallas_call(
        _mlstm_kernel,
        grid=(B, nc),
        in_specs=[rows, rows, rows, pl.BlockSpec((L, LANES), lambda b, c: (b * nc + c, 0)),
                  pl.BlockSpec((None, ng, L), lambda b, c: (b, 0, c)),
                  full(conv_w), full(cb), full(wq), full(wk), full(bcol), full(brow), full(g)],
        out_specs=rows,
        out_shape=jax.ShapeDtypeStruct((B * S, D_MLSTM), F32),
        scratch_shapes=[
            pltpu.VMEM((L + SUBLANES, D_MLSTM), F32), pltpu.VMEM((SUBLANES, D_MLSTM), F32),
            pltpu.VMEM((nh, HEAD_DIM_MLSTM, HEAD_DIM_MLSTM), F32),
            pltpu.VMEM((SUBLANES, LANES), F32), pltpu.VMEM((SUBLANES, LANES), F32),
        ],
        compiler_params=_vmem_limit(32),
        name="mlstm",
    )(xm, vm, om, gates, grow, conv_w, cb, wq, wk, bcol, brow, g)


def _out_route_kernel(attn_ref, ml_ref, x_ref, wo_ref, g2_ref, wr_ref, br_ref, earlier_ref,
                      x1_ref, h2_ref, rt_ref, gt_ref, base_ref, cnt_ref, carry_ref):
    @pl.when(pl.program_id(0) == 0)
    def _():
        carry_ref[...] = jnp.zeros_like(carry_ref)

    carry = carry_ref[...]
    for j in range(ROUTE_TILES_PER_STEP):
        carry = _route_tile(j, carry, attn_ref, ml_ref, x_ref, wo_ref, g2_ref, wr_ref, br_ref,
                            earlier_ref, x1_ref, h2_ref, rt_ref, gt_ref, base_ref)
    carry_ref[...] = carry
    cnt_ref[...] = carry


def _route_tile(j, carry, attn_ref, ml_ref, x_ref, wo_ref, g2_ref, wr_ref, br_ref, earlier_ref,
                x1_ref, h2_ref, rt_ref, gt_ref, base_ref):
    tm = ROUTE_ROWS
    ne = N_EXPERTS
    rows = slice(j * tm, (j + 1) * tm)

    x1 = (x_ref[rows, :] + _dot(attn_ref[rows, :].astype(BF16), wo_ref[0:D_ATTN, :])
          + _dot(ml_ref[rows, :].astype(BF16), wo_ref[D_ATTN:D_ATTN + D_MLSTM, :]))
    x1_ref[rows, :] = x1
    h2 = x1 * lax.rsqrt(jnp.mean(x1 * x1, axis=-1, keepdims=True) + EPS) * g2_ref[...]
    hb = h2.astype(BF16)
    h2_ref[rows, :] = hb
    hl = (h2 - hb.astype(F32)).astype(BF16)
    hi_pass = _dot_nt(wr_ref[...], hb)
    lt = (hi_pass[:ne] + hi_pass[ne:] + _dot_nt(wr_ref[0:ne, :], hl)) + br_ref[...]

    eidx = lax.broadcasted_iota(jnp.int32, (ne, tm), 0)
    vals, hots = [], []
    work = lt
    for _ in range(TOP_K):
        mx = jnp.max(work, axis=0, keepdims=True)
        idx = jnp.min(jnp.where(work == mx, eidx, ne), axis=0, keepdims=True)
        hot = eidx == idx
        vals.append(mx)
        hots.append(hot)
        work = jnp.where(hot, NEG_INF, work)
    ex = [jnp.exp(v - vals[0]) for v in vals]
    tot = ex[0] + ex[1] + ex[2] + ex[3]

    onehot = jnp.zeros((ne, tm), F32)
    gt = jnp.zeros((ne, tm), F32)
    for kk in range(TOP_K):
        onehot = onehot + jnp.where(hots[kk], 1.0, 0.0)
        gt = gt + jnp.where(hots[kk], ex[kk] / tot, 0.0)
    before = _dot(onehot.astype(BF16), earlier_ref[...]) + carry[:, 0:1]
    base_ref[j * ne:(j + 1) * ne, :] = carry
    rt_ref[:, rows] = jnp.where(onehot > 0.5, before, NOT_ROUTED).astype(jnp.int32)
    gt_ref[:, rows] = gt
    return carry + jnp.sum(onehot, axis=1, keepdims=True)


def _out_route(attn, ml, x2, w_out, g2, w_router, b_router):
    T = x2.shape[0]
    tm = ROUTE_ROWS
    ne = N_EXPERTS
    wo = w_out.astype(BF16)
    wrt = w_router.T
    wrh = wrt.astype(BF16)
    wr = jnp.concatenate([wrh, (wrt - wrh.astype(F32)).astype(BF16)], axis=0)
    br = b_router[:, None]
    g2r = g2[None, :]
    tok = jnp.arange(tm)
    earlier = (tok[:, None] < tok[None, :]).astype(BF16)
    ts = tm * ROUTE_TILES_PER_STEP
    row = lambda n: pl.BlockSpec((ts, n), lambda i: (i, 0))
    col = lambda n: pl.BlockSpec((n, ts), lambda i: (0, i))
    full = lambda a: pl.BlockSpec(a.shape, lambda i: (0,) * a.ndim)
    return pl.pallas_call(
        _out_route_kernel,
        grid=(T // ts,),
        in_specs=[row(D_ATTN), row(D_MLSTM), row(D_MODEL), full(wo), full(g2r), full(wr), full(br),
                  full(earlier)],
        out_specs=[row(D_MODEL), row(D_MODEL), col(ne), col(ne),
                   pl.BlockSpec((ROUTE_TILES_PER_STEP * ne, LANES), lambda i: (i, 0)),
                   pl.BlockSpec((ne, LANES), lambda i: (0, 0))],
        out_shape=[jax.ShapeDtypeStruct((T, D_MODEL), F32), jax.ShapeDtypeStruct((T, D_MODEL), BF16),
                   jax.ShapeDtypeStruct((ne, T), jnp.int32), jax.ShapeDtypeStruct((ne, T), F32),
                   jax.ShapeDtypeStruct((T // tm * ne, LANES), F32), jax.ShapeDtypeStruct((ne, LANES), F32)],
        scratch_shapes=[pltpu.VMEM((ne, LANES), F32)],
        compiler_params=_vmem_limit(32),
        name="out_route",
    )(attn, ml, x2, wo, g2r, wr, br, earlier)


def _dispatch_kernel(runstart_ref, runlen_ref, lshift_ref, padstart_ref, npad_ref,
                     h_ref, rt_ref, xs_ref, cbuf, xbuf, zero_ref, stage_ref, sem, xsem, zsem):
    td = DISPATCH_ROWS
    ne = N_EXPERTS
    rb = RUN_ROWS
    half = rb // 2
    ta = BF16_TILE_ROWS
    grp = RUN_GROUP_ROWS // rb
    tile = pl.program_id(0)

    def window(i):
        off = runstart_ref[i] & (ta - 1)
        return off, off + runlen_ref[i]

    def first_chunk(tile_idx, e, action):
        i = tile_idx * ne + e
        off, total = window(i)
        dst = pl.multiple_of(runstart_ref[i] - off, ta)
        action(pltpu.make_async_copy(cbuf.at[pl.ds(e * rb, half)], xs_ref.at[pl.ds(dst, half)], sem))

        @pl.when(total > half)
        def _():
            action(pltpu.make_async_copy(cbuf.at[pl.ds(e * rb + half, half)],
                                         xs_ref.at[pl.ds(pl.multiple_of(dst + half, ta), half)], sem))

    @pl.when(tile > 0)
    def _():
        for e in range(ne):
            first_chunk(tile - 1, e, lambda cp: cp.wait())

    @pl.when(tile == 0)
    def _():
        stage_ref[...] = jnp.zeros_like(stage_ref)

    hb = h_ref[...]
    riota = lax.broadcasted_iota(jnp.int32, (rb, td), 0)
    rowt = lax.broadcasted_iota(jnp.int32, (ta, D_MODEL), 0)

    def select(e_slice, shift):
        return jnp.where(riota == rt_ref[e_slice, :] - shift, 1.0, 0.0)

    for g in range(ne // grp):
        experts = range(g * grp, (g + 1) * grp)
        parts = [select(slice(e, e + 1), lshift_ref[tile * ne + e] - window(tile * ne + e)[0]) for e in experts]
        pt = jnp.concatenate(parts, axis=0).astype(BF16)
        cbuf[g * RUN_GROUP_ROWS:(g + 1) * RUN_GROUP_ROWS, :] = _dot(pt, hb).astype(BF16)

    for e in range(ne):
        i = tile * ne + e
        off, total = window(i)
        head = pl.ds(e * rb, ta)
        old = stage_ref[e]
        cbuf[head, :] = jnp.where(rowt < off, old, cbuf[head, :])
        last = e * rb + jnp.minimum(total // ta * ta, rb - ta)
        keep = jnp.logical_and(runlen_ref[i] > 0, total < rb)
        stage_ref[e] = jnp.where(keep, cbuf[pl.ds(pl.multiple_of(last, ta), ta), :], old)

    for e in range(ne):
        first_chunk(tile, e, lambda cp: cp.start())

    def per_expert(e, carry):
        i = tile * ne + e
        off, total = window(i)

        def chunk(ch, c):
            pt = select(pl.ds(e, 1), lshift_ref[i] - off + ch * rb).astype(BF16)
            xbuf[...] = _dot(pt, hb).astype(BF16)
            dst = pl.multiple_of(runstart_ref[i] - off + ch * rb, ta)
            cp = pltpu.make_async_copy(xbuf, xs_ref.at[pl.ds(dst, rb)], xsem)
            cp.start()

            @pl.when(jnp.logical_and(ch == total // rb, total % rb != 0))
            def _():
                last = pl.multiple_of(total % rb // ta * ta, ta)
                stage_ref[e] = xbuf[pl.ds(last, ta), :]

            cp.wait()
            return c

        lax.fori_loop(1, (total + rb - 1) // rb, chunk, 0)
        return carry

    lax.fori_loop(0, ne, per_expert, 0)

    @pl.when(tile == pl.num_programs(0) - 1)
    def _():
        for e in range(ne):
            first_chunk(tile, e, lambda cp: cp.wait())

    @pl.when(tile == pl.num_programs(0) - 1)
    def _():
        zero_ref[...] = jnp.zeros_like(zero_ref)

        def fill(wait):
            def per_run(e, carry):
                start = padstart_ref[e]
                head = (-start) & (ta - 1)
                off = pl.multiple_of(start + head, ta)
                left = npad_ref[e] - head

                def whole(b, c):
                    dst = xs_ref.at[pl.ds(pl.multiple_of(off + b * td, ta), td)]
                    cp = pltpu.make_async_copy(zero_ref, dst, zsem)
                    cp.wait() if wait else cp.start()
                    return c

                n_whole = left // td
                lax.fori_loop(0, n_whole, whole, 0)
                off = pl.multiple_of(off + n_whole * td, ta)
                p = td // 2
                while p >= ta:
                    take = (left & p) != 0

                    @pl.when(take)
                    def _(off=off, p=p):
                        cp = pltpu.make_async_copy(zero_ref.at[pl.ds(0, p)], xs_ref.at[pl.ds(off, p)], zsem)
                        cp.wait() if wait else cp.start()

                    off = pl.multiple_of(off + jnp.where(take, p, 0), ta)
                    p //= 2
                return carry

            lax.fori_loop(0, N_EXPERTS + 1, per_run, 0)

        fill(wait=False)
        fill(wait=True)


def _dispatch(h2, rt, runstart, runlen, lshift, padstart, npad, n_slots):
    T = h2.shape[0]
    td = DISPATCH_ROWS
    ne = N_EXPERTS
    assert td == ROUTE_ROWS
    return pl.pallas_call(
        _dispatch_kernel,
        grid_spec=pltpu.PrefetchScalarGridSpec(
            num_scalar_prefetch=5,
            grid=(T // td,),
            in_specs=[pl.BlockSpec((td, D_MODEL), lambda i, *_: (i, 0)),
                      pl.BlockSpec((ne, td), lambda i, *_: (0, i))],
            out_specs=pl.BlockSpec(memory_space=pl.ANY),
            scratch_shapes=[pltpu.VMEM((ne * RUN_ROWS, D_MODEL), BF16), pltpu.VMEM((RUN_ROWS, D_MODEL), BF16),
                            pltpu.VMEM((td, D_MODEL), BF16), pltpu.VMEM((ne, BF16_TILE_ROWS, D_MODEL), BF16),
                            pltpu.SemaphoreType.DMA(()), pltpu.SemaphoreType.DMA(()),
                            pltpu.SemaphoreType.DMA(())],
        ),
        out_shape=jax.ShapeDtypeStruct((n_slots, D_MODEL), BF16),
        compiler_params=_vmem_limit(32),
        name="dispatch",
    )(runstart, runlen, lshift, padstart, npad, h2, rt)


def _expert_kernel(bexp_ref, first_ref, next_ref, nused_ref, xs_ref, wgu_hbm, bgu_ref, wd_hbm, bd_ref, y_ref,
                   wgu_st, wd_st, wgu_bf, wd_bf, wsem):
    tm = EXPERT_ROWS
    blk = pl.program_id(0)
    active = blk * tm < nused_ref[0]

    def weight_copies(e):
        return (pltpu.make_async_copy(wgu_hbm.at[e], wgu_st, wsem.at[0]),
                pltpu.make_async_copy(wd_hbm.at[e], wd_st, wsem.at[1]))

    @pl.when(blk == 0)
    def _():
        for cp in weight_copies(bexp_ref[0]):
            cp.start()

    @pl.when(jnp.logical_and(active, first_ref[blk] == 1))
    def _():
        for cp in weight_copies(bexp_ref[blk]):
            cp.wait()
        rc = CAST_ROWS

        def cast(i, carry):
            r = pl.ds(pl.multiple_of(i * rc, rc), rc)
            wgu_bf[r, :] = wgu_st[r, :].astype(BF16)
            wd_bf[r, :] = wd_st[r, :].astype(BF16)
            return carry

        lax.fori_loop(0, D_MODEL // rc, cast, 0)

        @pl.when(next_ref[blk] >= 0)
        def _():
            for cp in weight_copies(next_ref[blk]):
                cp.start()

    @pl.when(active)
    def _():
        xb = xs_ref[...]
        acc = jnp.zeros((tm, D_MODEL), F32)
        for f in range(D_FF // FF_CHUNK):
            lo = f * FF_CHUNK
            g = _dot(xb, wgu_bf[:, lo:lo + FF_CHUNK]) + bgu_ref[:, lo:lo + FF_CHUNK]
            u = _dot(xb, wgu_bf[:, D_FF + lo:D_FF + lo + FF_CHUNK]) + bgu_ref[:, D_FF + lo:D_FF + lo + FF_CHUNK]
            g = jnp.minimum(g, SWIGLU_LIMIT)
            u = jnp.clip(u, -SWIGLU_LIMIT, SWIGLU_LIMIT)
            act = g * jax.nn.sigmoid(SWIGLU_ALPHA * g) * (u + 1.0)
            acc = acc + _dot(act.astype(BF16), wd_bf[lo:lo + FF_CHUNK, :])
        y_ref[...] = (acc + bd_ref[...]).astype(y_ref.dtype)

    @pl.when(jnp.logical_not(active))
    def _():
        y_ref[...] = jnp.zeros_like(y_ref)


def _experts(xs, block_expert, first_block, next_expert, n_used, w_gate_up, b_gate_up, w_down, b_down):
    n_blocks = block_expert.shape[0]
    tm = EXPERT_ROWS
    assert D_FF == D_MODEL and xs.shape[0] == (n_blocks - 1) * tm
    bgu = b_gate_up[:, None, :]
    bd = b_down[:, None, :]
    last = n_blocks - 2
    return pl.pallas_call(
        _expert_kernel,
        grid_spec=pltpu.PrefetchScalarGridSpec(
            num_scalar_prefetch=4,
            grid=(n_blocks,),
            in_specs=[
                pl.BlockSpec((tm, D_MODEL), lambda i, be, fb, nx, nu: (jnp.minimum(i, last), 0)),
                pl.BlockSpec(memory_space=pl.ANY),
                pl.BlockSpec((None, 1, 2 * D_FF), lambda i, be, fb, nx, nu: (be[i], 0, 0)),
                pl.BlockSpec(memory_space=pl.ANY),
                pl.BlockSpec((None, 1, D_MODEL), lambda i, be, fb, nx, nu: (be[i], 0, 0)),
            ],
            out_specs=pl.BlockSpec((tm, D_MODEL), lambda i, be, fb, nx, nu: (i, 0)),
            scratch_shapes=[pltpu.VMEM((D_MODEL, 2 * D_FF), F32), pltpu.VMEM((D_FF, D_MODEL), F32),
                            pltpu.VMEM((D_MODEL, 2 * D_FF), BF16), pltpu.VMEM((D_FF, D_MODEL), BF16),
                            pltpu.SemaphoreType.DMA((2,))],
        ),
        out_shape=jax.ShapeDtypeStruct((n_blocks * tm, D_MODEL), BF16),
        compiler_params=_vmem_limit(48),
        name="experts",
    )(block_expert, first_block, next_expert, n_used, xs, w_gate_up, bgu, w_down, bd)


def _combine_kernel(fetch_ref, shift_ref, nch_ref, x1_ref, rt_ref, gt_ref, y_ref, o_ref,
                    buf, xbuf, sem, xsem):
    tc = ROUTE_ROWS
    ne = N_EXPERTS
    rb = RUN_ROWS
    grp = RUN_GROUP_ROWS // rb
    tile = pl.program_id(0)
    base = tile * ne
    slot = tile % 2

    def run_copy(tile_base, buf_slot, e):
        start = pl.multiple_of(fetch_ref[tile_base + e], BF16_TILE_ROWS)
        return pltpu.make_async_copy(y_ref.at[pl.ds(start, rb)], buf.at[buf_slot, pl.ds(e * rb, rb)],
                                     sem.at[buf_slot, e // grp])

    @pl.when(tile == 0)
    def _():
        for e in range(ne):
            run_copy(base, slot, e).start()

    @pl.when(tile + 1 < pl.num_programs(0))
    def _():
        for e in range(ne):
            run_copy(base + ne, 1 - slot, e).start()

    riota = lax.broadcasted_iota(jnp.int32, (rb, tc), 0)

    def select(e_slice, shift):
        return jnp.where(riota == rt_ref[e_slice, :] - shift, gt_ref[e_slice, :], 0.0)

    acc = x1_ref[...]
    for g in range(ne // grp):
        parts = []
        for e in range(g * grp, (g + 1) * grp):
            run_copy(base, slot, e).wait()
            parts.append(select(slice(e, e + 1), shift_ref[base + e]))
        pt = jnp.concatenate(parts, axis=0).astype(BF16)
        acc = acc + _dot_tn(pt, buf[slot, g * RUN_GROUP_ROWS:(g + 1) * RUN_GROUP_ROWS, :])
    o_ref[...] = acc

    def per_expert(e, carry):
        def chunk(ch, c):
            start = pl.multiple_of(fetch_ref[base + e] + ch * rb, BF16_TILE_ROWS)
            cp = pltpu.make_async_copy(y_ref.at[pl.ds(start, rb)], xbuf, xsem)
            cp.start()
            cp.wait()
            pt = select(pl.ds(e, 1), shift_ref[base + e] + ch * rb).astype(BF16)
            o_ref[...] += _dot_tn(pt, xbuf[...])
            return c

        lax.fori_loop(1, nch_ref[base + e], chunk, 0)
        return carry

    lax.fori_loop(0, ne, per_expert, 0)


def _combine(fetch, shift, nch, x1, rt, gt, y):
    T = x1.shape[0]
    tc = ROUTE_ROWS
    ne = N_EXPERTS
    return pl.pallas_call(
        _combine_kernel,
        grid_spec=pltpu.PrefetchScalarGridSpec(
            num_scalar_prefetch=3,
            grid=(T // tc,),
            in_specs=[pl.BlockSpec((tc, D_MODEL), lambda i, f, s, n: (i, 0)),
                      pl.BlockSpec((ne, tc), lambda i, f, s, n: (0, i)),
                      pl.BlockSpec((ne, tc), lambda i, f, s, n: (0, i)),
                      pl.BlockSpec(memory_space=pl.ANY)],
            out_specs=pl.BlockSpec((tc, D_MODEL), lambda i, f, s, n: (i, 0)),
            scratch_shapes=[pltpu.VMEM((2, ne * RUN_ROWS, D_MODEL), BF16), pltpu.VMEM((RUN_ROWS, D_MODEL), BF16),
                            pltpu.SemaphoreType.DMA((2, ne * RUN_ROWS // RUN_GROUP_ROWS)),
                            pltpu.SemaphoreType.DMA(())],
        ),
        out_shape=jax.ShapeDtypeStruct((T, D_MODEL), F32),
        compiler_params=_vmem_limit(32),
        name="combine",
    )(fetch, shift, nch, x1, rt, gt, y)


def _routing_tables(base, counts, T):
    tm = EXPERT_ROWS
    ne = N_EXPERTS
    n_tiles = T // ROUTE_ROWS
    n_blocks = -(-(T * TOP_K + ne * RUN_ROWS) // tm) + ne
    n_slots = n_blocks * tm
    sizes = counts[:, 0].astype(jnp.int32)
    psizes = (sizes + RUN_ROWS + tm - 1) // tm * tm
    pends = jnp.cumsum(psizes)
    pstarts = pends - psizes
    n_used = pends[-1:]
    padstart = jnp.concatenate([pstarts + sizes, n_used])
    npad = jnp.concatenate([psizes - sizes, n_slots - n_used])
    nb = n_blocks + 1
    blk = jnp.arange(nb, dtype=jnp.int32)
    block_expert = jnp.minimum(jnp.sum(pends[None, :] <= (blk * tm)[:, None], axis=1), ne - 1).astype(jnp.int32)
    first_block = jnp.concatenate([jnp.ones((1,), jnp.int32),
                                   (block_expert[1:] != block_expert[:-1]).astype(jnp.int32)])
    starts_group = jnp.logical_and(first_block == 1, blk * tm < n_used)
    pos = jnp.where(starts_group, blk, nb)
    later = jnp.concatenate([lax.cummin(pos[::-1])[::-1][1:], jnp.full((1,), nb, jnp.int32)])
    next_expert = jnp.where(later < nb, block_expert[jnp.minimum(later, nb - 1)], -1).astype(jnp.int32)
    tile_base = base[:, 0].astype(jnp.int32).reshape(n_tiles, ne)
    run_len = jnp.concatenate([tile_base[1:], sizes[None, :]], axis=0) - tile_base
    run_start = pstarts[None, :] + tile_base
    fetch = run_start // BF16_TILE_ROWS * BF16_TILE_ROWS
    shift = fetch - pstarts[None, :]
    nch = jnp.where(run_len > 0, (run_start - fetch + run_len + RUN_ROWS - 1) // RUN_ROWS, 0)
    flat = lambda a: a.reshape(n_tiles * ne).astype(jnp.int32)
    return (flat(run_start), flat(run_len), flat(tile_base), padstart, npad, n_slots,
            block_expert, first_block, next_expert, n_used, flat(fetch), flat(shift), flat(nch))


def kernel(x, norm1_g, w_in, q_norm_g, k_norm_g, conv_w, conv_b, w_mq, w_mk, b_igate, b_fgate, mh_norm_g,
           w_out, norm2_g, w_router, b_router, w_gate_up, b_gate_up, w_down, b_down):
    B, S, D = x.shape
    T = B * S
    assert D == D_MODEL and S % ATTN_ROWS == 0 and norm1_g.shape[0] == 1
    x2 = x.reshape(T, D)
    q, k, v, xm, vm, om, gates = _in_proj(x2, norm1_g[0], w_in[0], q_norm_g[0], k_norm_g[0])
    attn = _attention(q, k, v, B, S)
    ml = _mlstm(xm, vm, om, gates, conv_w[0], conv_b[0], w_mq[0], w_mk[0], b_igate[0], b_fgate[0],
                mh_norm_g[0], B, S)
    x1, h2, rt, gt, base, counts = _out_route(attn, ml, x2, w_out[0], norm2_g[0], w_router[0], b_router[0])
    (run_start, run_len, tile_base, padstart, npad, n_slots, block_expert, first_block, next_expert, n_used,
     fetch, shift, nch) = _routing_tables(base, counts, T)
    xs = _dispatch(h2, rt, run_start, run_len, tile_base, padstart, npad, n_slots)
    y = _experts(xs, block_expert, first_block, next_expert, n_used, w_gate_up[0], b_gate_up[0], w_down[0],
                 b_down[0])
    out = _combine(fetch, shift, nch, x1, rt, gt, y)
    return out.reshape(B, S, D)
```

```python
import jax
import jax.numpy as jnp
from jax import lax
from jax.experimental import pallas as pl
from jax.experimental.pallas import tpu as pltpu

F32 = jnp.float32
BF16 = jnp.bfloat16
NEG_INF = float("-inf")

D_MODEL = 1024
D_ATTN = 512
HEAD_DIM_ATTN = 64
N_HEADS_ATTN = 8
D_MLSTM = 512
HEAD_DIM_MLSTM = 128
N_HEADS_MLSTM = 4
CONV_K = 4
N_EXPERTS = 32
TOP_K = 4
D_FF = 1024
SWIGLU_LIMIT = 7.0
SWIGLU_ALPHA = 1.702
EPS = 1e-6
LOG2_E = 1.4426950408889634
DILATIONS = (1, 4, 16)
BRANCH_SPAN = 128

LANES = 128
SUBLANES = 8

IN_PROJ_ROWS = 512
ATTN_ROWS = 2048
MLSTM_CHUNK = 256
ROUTE_ROWS = 256
ROUTE_TILES_PER_STEP = 2
EXPERT_ROWS = 256
FF_CHUNK = 1024
CAST_ROWS = 128
DISPATCH_ROWS = 256
HEADNORM_LANES = 256
BF16_TILE_ROWS = 16
RUN_ROWS = 64
RUN_GROUP_ROWS = 256
NOT_ROUTED = -1.0e9


def _vmem_limit(mib):
    return pltpu.CompilerParams(vmem_limit_bytes=mib * 1024 * 1024)


def _split3(a):
    p1 = a.astype(BF16)
    r1 = a - p1.astype(F32)
    p2 = r1.astype(BF16)
    r2 = r1 - p2.astype(F32)
    return p1, p2, r2.astype(BF16)


def _dot(a, b):
    return jnp.dot(a, b, preferred_element_type=F32)


def _dot_nt(a, b):
    return lax.dot_general(a, b, (((1,), (1,)), ((), ())), preferred_element_type=F32)


def _dot_tn(a, b):
    return lax.dot_general(a, b, (((0,), (0,)), ((), ())), preferred_element_type=F32)


def _in_proj_kernel(x_ref, g1_ref, wqkv_ref, wm_ref, wgc_ref, qg_ref, kg_ref, hsum_ref,
                    q_ref, k_ref, v_ref, xm_ref, vm_ref, om_ref, gates_ref):
    x = x_ref[...]
    h = x * lax.rsqrt(jnp.mean(x * x, axis=-1, keepdims=True) + EPS) * g1_ref[...]
    hb = h.astype(BF16)
    hl = (h - hb.astype(F32)).astype(BF16)
    gw = HEADNORM_LANES

    def head_norm(z, g_ref):
        parts = []
        for c in range(D_ATTN // gw):
            zc = z[:, c * gw:(c + 1) * gw]
            ms = _dot((zc * zc).astype(BF16), hsum_ref[...]) * (1.0 / HEAD_DIM_ATTN)
            parts.append(zc * lax.rsqrt(ms + EPS))
        return jnp.concatenate(parts, axis=-1) * g_ref[...]

    zq = _dot(hb, wqkv_ref[:, 0:D_ATTN])
    q_ref[...] = head_norm(zq, qg_ref) * (HEAD_DIM_ATTN ** -0.5 * LOG2_E)
    zk = _dot(hb, wqkv_ref[:, D_ATTN:2 * D_ATTN])
    k_ref[...] = head_norm(zk, kg_ref)
    v_ref[...] = _dot(hb, wqkv_ref[:, 2 * D_ATTN:3 * D_ATTN])
    xm_ref[...] = _dot(hb, wm_ref[:, 0:D_MLSTM])
    vm_ref[...] = _dot(hb, wm_ref[:, D_MLSTM:2 * D_MLSTM])
    om_ref[...] = _dot(hb, wm_ref[:, 2 * D_MLSTM:3 * D_MLSTM])
    ng = 2 * N_HEADS_MLSTM
    rows = hb.shape[0]
    both = _dot(jnp.concatenate([hb, hl], axis=0), wgc_ref[...])
    gsum = both[:rows] + both[rows:]
    gsum = gsum + pltpu.roll(gsum, LANES - ng, axis=1)
    lane = lax.broadcasted_iota(jnp.int32, gsum.shape, 1)
    gates_ref[...] = jnp.where(lane < ng, gsum, 0.0)


def _in_proj(x2, g1, w_in, q_g, k_g):
    T = x2.shape[0]
    tm = IN_PROJ_ROWS
    n_qkv = 3 * D_ATTN
    n_m = 3 * D_MLSTM
    ng = 2 * N_HEADS_MLSTM
    wqkv = w_in[:, :n_qkv].astype(BF16)
    wm = w_in[:, n_qkv:n_qkv + n_m].astype(BF16)
    wg = w_in[:, n_qkv + n_m:]
    wgh = wg.astype(BF16)
    wgl = (wg - wgh.astype(F32)).astype(BF16)
    wgc = jnp.pad(jnp.concatenate([wgh, wgl], axis=1), ((0, 0), (0, LANES - 2 * ng)))
    lane = jnp.arange(HEADNORM_LANES)
    hsum = (lane[:, None] // HEAD_DIM_ATTN == lane[None, :] // HEAD_DIM_ATTN).astype(BF16)
    qg = jnp.tile(q_g, N_HEADS_ATTN)[None, :]
    kg = jnp.tile(k_g, N_HEADS_ATTN)[None, :]
    row = lambda n: pl.BlockSpec((tm, n), lambda i: (i, 0))
    full = lambda a: pl.BlockSpec(a.shape, lambda i: (0,) * a.ndim)
    outs = [jax.ShapeDtypeStruct((T, D_ATTN), F32)] * 3 + [jax.ShapeDtypeStruct((T, D_MLSTM), F32)] * 3
    outs.append(jax.ShapeDtypeStruct((T, LANES), F32))
    g1r = g1[None, :]
    return pl.pallas_call(
        _in_proj_kernel,
        grid=(T // tm,),
        in_specs=[row(D_MODEL), full(g1r), full(wqkv), full(wm), full(wgc), full(qg), full(kg), full(hsum)],
        out_specs=[row(D_ATTN)] * 3 + [row(D_MLSTM)] * 3 + [row(LANES)],
        out_shape=outs,
        compiler_params=_vmem_limit(48),
        name="in_proj",
    )(x2, g1r, wqkv, wm, wgc, qg, kg, hsum)


def _attn_tiles():
    tiles = []
    for br, d in enumerate(DILATIONS):
        group = d * BRANCH_SPAN
        for u in range(ATTN_ROWS // group):
            for c in range(d):
                tiles.append((br, d, u * group + c))
    return tiles


def _attn_kernel(slope_ref, q_ref, kp_ref, kc_ref, vp_ref, vc_ref, o_ref,
                 kwin, vwin, bias_scr, m_scr, l_scr, acc_scr):
    W = ATTN_ROWS
    n = BRANCH_SPAN
    step = pl.program_id(2)
    kwin[0:W, :] = kp_ref[...]
    kwin[W:2 * W, :] = kc_ref[...]
    vwin[0:W, :] = vp_ref[...]
    vwin[W:2 * W, :] = vc_ref[...]

    lane = lax.broadcasted_iota(jnp.int32, (n, LANES), 1)
    first_head = lane < HEAD_DIM_ATTN

    @pl.when(step == 0)
    def _():
        row = lax.broadcasted_iota(jnp.int32, (n, 2 * n), 0)
        col = lax.broadcasted_iota(jnp.int32, (n, 2 * n), 1)
        j = n + row - col
        valid = jnp.logical_and(j >= 0, j <= n)
        jf = j.astype(F32)
        for br, d in enumerate(DILATIONS):
            for hh in range(2):
                slope = slope_ref[:, hh * HEAD_DIM_ATTN:hh * HEAD_DIM_ATTN + 1]
                b = jnp.where(valid, -(slope * (float(d) * LOG2_E)) * jf, NEG_INF)
                bias_scr[br, hh, 0] = b
                bias_scr[br, hh, 1] = jnp.where(col >= n, b, NEG_INF)

    for br, d, qs in _attn_tiles():
        lo = W + qs - n * d
        q = q_ref[pl.ds(qs, n, stride=d), :]
        kk = jnp.concatenate([kwin[pl.ds(lo, n, stride=d), :], kwin[pl.ds(W + qs, n, stride=d), :]],
                             axis=0).astype(BF16)
        vv = jnp.concatenate([vwin[pl.ds(lo, n, stride=d), :], vwin[pl.ds(W + qs, n, stride=d), :]],
                             axis=0).astype(BF16)
        if lo < W:
            sel = jnp.where(step == 0, 1, 0)
        else:
            sel = 0
        qb = q.astype(BF16)
        zero = jnp.zeros_like(qb)
        q2 = jnp.concatenate([jnp.where(first_head, qb, zero), jnp.where(first_head, zero, qb)], axis=0)
        s2 = _dot_nt(q2, kk)
        ms, ls, ps = [], [], []
        for hh in range(2):
            s = s2[hh * n:(hh + 1) * n] + bias_scr[br, hh, sel]
            m = jnp.max(s, axis=-1, keepdims=True)
            p = jnp.exp2(s - m)
            ms.append(m)
            ls.append(jnp.sum(p, axis=-1, keepdims=True))
            ps.append(p.astype(BF16))
        pv = _dot(jnp.concatenate(ps, axis=0), vv)
        rows = pl.ds(qs, n, stride=d)
        m_scr[br, rows, :] = jnp.where(first_head, ms[0], ms[1])
        l_scr[br, rows, :] = jnp.where(first_head, ls[0], ls[1])
        acc_scr[br, rows, :] = jnp.where(first_head, pv[:n], pv[n:])

    chunk = 256

    def combine(i, carry):
        r = pl.ds(pl.multiple_of(i * chunk, chunk), chunk)
        m0, m1, m2 = m_scr[0, r, :], m_scr[1, r, :], m_scr[2, r, :]
        mm = jnp.maximum(jnp.maximum(m0, m1), m2)
        w0, w1, w2 = jnp.exp2(m0 - mm), jnp.exp2(m1 - mm), jnp.exp2(m2 - mm)
        num = w0 * acc_scr[0, r, :] + w1 * acc_scr[1, r, :] + w2 * acc_scr[2, r, :]
        den = w0 * l_scr[0, r, :] + w1 * l_scr[1, r, :] + w2 * l_scr[2, r, :]
        o_ref[r, :] = num / den
        return carry

    lax.fori_loop(0, W // chunk, combine, 0)


def _attention(q, k, v, B, S):
    W = ATTN_ROWS
    nb = S // W
    n_pairs = D_ATTN // LANES
    slopes = jnp.exp2(-8.0 * jnp.arange(1, N_HEADS_ATTN + 1, dtype=F32) / N_HEADS_ATTN)
    slope_l = jnp.repeat(slopes, HEAD_DIM_ATTN).reshape(n_pairs, 1, LANES)
    cur = pl.BlockSpec((W, LANES), lambda b, p, i: (b * nb + i, p))
    prev = pl.BlockSpec((W, LANES), lambda b, p, i: (b * nb + jnp.maximum(i - 1, 0), p))
    n = BRANCH_SPAN
    nbr = len(DILATIONS)
    return pl.pallas_call(
        _attn_kernel,
        grid=(B, n_pairs, nb),
        in_specs=[pl.BlockSpec((None, 1, LANES), lambda b, p, i: (p, 0, 0)), cur, prev, cur, prev, cur],
        out_specs=cur,
        out_shape=jax.ShapeDtypeStruct((B * S, D_ATTN), F32),
        scratch_shapes=[
            pltpu.VMEM((2 * W, LANES), F32), pltpu.VMEM((2 * W, LANES), F32),
            pltpu.VMEM((nbr, 2, 2, n, 2 * n), F32),
            pltpu.VMEM((nbr, W, LANES), F32), pltpu.VMEM((nbr, W, LANES), F32),
            pltpu.VMEM((nbr, W, LANES), F32),
        ],
        compiler_params=_vmem_limit(48),
        name="attention",
    )(slope_l, q, k, k, v, v)


def _mlstm_kernel(xm_ref, vm_ref, om_ref, gcol_ref, grow_ref, cw_ref, cb_ref, wq_ref, wk_ref,
                  bcol_ref, brow_ref, g_ref, o_ref, xs_ref, xtail_ref, c_ref, n_ref, m_ref):
    L = MLSTM_CHUNK
    dh = HEAD_DIM_MLSTM
    nh = N_HEADS_MLSTM

    @pl.when(pl.program_id(1) == 0)
    def _():
        xtail_ref[...] = jnp.zeros_like(xtail_ref)
        c_ref[...] = jnp.zeros_like(c_ref)
        n_ref[...] = jnp.zeros_like(n_ref)
        m_ref[...] = jnp.zeros_like(m_ref)

    x = xm_ref[...]
    xs_ref[0:SUBLANES, :] = xtail_ref[...]
    xs_ref[SUBLANES:SUBLANES + L, :] = x
    xtail_ref[...] = x[L - SUBLANES:L, :]
    xc = cb_ref[...] + x * cw_ref[CONV_K - 1:CONV_K, :]
    for back in range(1, CONV_K):
        xc = xc + xs_ref[SUBLANES - back:SUBLANES - back + L, :] * cw_ref[CONV_K - 1 - back:CONV_K - back, :]
    xc = xc * jax.nn.sigmoid(xc)

    gc = gcol_ref[...] + bcol_ref[...]
    gr = grow_ref[...] + brow_ref[...]
    ri = lax.broadcasted_iota(jnp.int32, (L, L), 0)
    ci = lax.broadcasted_iota(jnp.int32, (L, L), 1)
    causal = ri >= ci
    tril = jnp.where(causal, 1.0, 0.0).astype(BF16)
    triu = jnp.where(ri <= ci, 1.0, 0.0).astype(BF16)
    c1, c2, c3 = _split3(jax.nn.log_sigmoid(gc))
    bc = _dot(tril, c1) + _dot(tril, c2) + _dot(tril, c3)
    r1, r2, r3 = _split3(jax.nn.log_sigmoid(gr))
    brw = _dot(r1, triu) + _dot(r2, triu) + _dot(r3, triu)

    for h in range(nh):
        hs = slice(h * dh, (h + 1) * dh)
        b_col = bc[:, nh + h:nh + h + 1]
        a_col = gc[:, h:h + 1] - b_col
        a_row = gr[h:h + 1, :] - brw[nh + h:nh + h + 1, :]
        m_prev = m_ref[h:h + 1, 0:1]
        log_d = jnp.where(causal, b_col + a_row, NEG_INF)
        log_inter = b_col + m_prev
        m_t = jnp.maximum(log_inter, jnp.max(log_d, axis=-1, keepdims=True))
        d = jnp.exp(log_d - m_t)
        inter = jnp.exp(log_inter - m_t)

        xh = xc[:, hs].astype(BF16)
        qf = _dot(xh, wq_ref[h])
        kf = _dot(xh, wk_ref[h]) * (dh ** -0.5)
        qb = qf.astype(BF16)
        vb = vm_ref[:, hs].astype(BF16)
        s = _dot_nt(qb, kf.astype(BF16)) * d
        c_old = c_ref[h]
        n_old = n_ref[h:h + 1, :]
        num = _dot(s.astype(BF16), vb) + inter * _dot(qb, c_old.astype(BF16))
        den = jnp.sum(s, axis=-1, keepdims=True) + inter * jnp.sum(qf * n_old, axis=-1, keepdims=True)
        hh = num / jnp.maximum(jnp.abs(den), jnp.exp(-m_t))

        b_last = b_col[L - 1:L, :]
        log_w = b_last + a_col
        m_new = jnp.maximum(b_last + m_prev, jnp.max(log_w, axis=0, keepdims=True))
        kw = kf * jnp.exp(log_w - m_new)
        decay = jnp.exp(b_last + m_prev - m_new)
        c_ref[h] = decay * c_old + _dot_tn(kw.astype(BF16), vb)
        n_ref[h:h + 1, :] = decay * n_old + jnp.sum(kw, axis=0, keepdims=True)
        m_ref[h:h + 1, :] = jnp.broadcast_to(m_new, (1, LANES))

        hn = hh * lax.rsqrt(jnp.mean(hh * hh, axis=-1, keepdims=True) + EPS) * g_ref[:, hs]
        o_ref[:, hs] = jax.nn.sigmoid(om_ref[:, hs]) * hn


def _mlstm(xm, vm, om, gates, conv_w, conv_b, w_mq, w_mk, b_i, b_f, mh_g, B, S):
    L = MLSTM_CHUNK
    nc = S // L
    nh = N_HEADS_MLSTM
    ng = 2 * nh
    grow = gates[:, :ng].reshape(B, S, ng).transpose(0, 2, 1)
    bias = jnp.concatenate([b_i, b_f])
    bcol = jnp.pad(bias, (0, LANES - ng))[None, :]
    brow = bias[:, None]
    wq = w_mq.astype(BF16)
    wk = w_mk.astype(BF16)
    cb = conv_b[None, :]
    g = mh_g.reshape(1, D_MLSTM)
    rows = pl.BlockSpec((L, D_MLSTM), lambda b, c: (b * nc + c, 0))
    full = lambda a: pl.BlockSpec(a.shape, lambda b, c: (0,) * a.ndim)
    return pl.pallas_call(
        _mlstm_kernel,
        grid=(B, nc),
        in_specs=[rows, rows, rows, pl.BlockSpec((L, LANES), lambda b, c: (b * nc + c, 0)),
                  pl.BlockSpec((None, ng, L), lambda b, c: (b, 0, c)),
                  full(conv_w), full(cb), full(wq), full(wk), full(bcol), full(brow), full(g)],
        out_specs=rows,
        out_shape=jax.ShapeDtypeStruct((B * S, D_MLSTM), F32),
        scratch_shapes=[
            pltpu.VMEM((L + SUBLANES, D_MLSTM), F32), pltpu.VMEM((SUBLANES, D_MLSTM), F32),
            pltpu.VMEM((nh, HEAD_DIM_MLSTM, HEAD_DIM_MLSTM), F32),
            pltpu.VMEM((SUBLANES, LANES), F32), pltpu.VMEM((SUBLANES, LANES), F32),
        ],
        compiler_params=_vmem_limit(32),
        name="mlstm",
    )(xm, vm, om, gates, grow, conv_w, cb, wq, wk, bcol, brow, g)


def _out_route_kernel(attn_ref, ml_ref, x_ref, wo_ref, g2_ref, wr_ref, br_ref, earlier_ref,
                      x1_ref, h2_ref, rt_ref, gt_ref, base_ref, cnt_ref, carry_ref):
    @pl.when(pl.program_id(0) == 0)
    def _():
        carry_ref[...] = jnp.zeros_like(carry_ref)

    carry = carry_ref[...]
    for j in range(ROUTE_TILES_PER_STEP):
        carry = _route_tile(j, carry, attn_ref, ml_ref, x_ref, wo_ref, g2_ref, wr_ref, br_ref,
                            earlier_ref, x1_ref, h2_ref, rt_ref, gt_ref, base_ref)
    carry_ref[...] = carry
    cnt_ref[...] = carry


def _route_tile(j, carry, attn_ref, ml_ref, x_ref, wo_ref, g2_ref, wr_ref, br_ref, earlier_ref,
                x1_ref, h2_ref, rt_ref, gt_ref, base_ref):
    tm = ROUTE_ROWS
    ne = N_EXPERTS
    rows = slice(j * tm, (j + 1) * tm)

    x1 = (x_ref[rows, :] + _dot(attn_ref[rows, :].astype(BF16), wo_ref[0:D_ATTN, :])
          + _dot(ml_ref[rows, :].astype(BF16), wo_ref[D_ATTN:D_ATTN + D_MLSTM, :]))
    x1_ref[rows, :] = x1
    h2 = x1 * lax.rsqrt(jnp.mean(x1 * x1, axis=-1, keepdims=True) + EPS) * g2_ref[...]
    hb = h2.astype(BF16)
    h2_ref[rows, :] = hb
    hl = (h2 - hb.astype(F32)).astype(BF16)
    hi_pass = _dot_nt(wr_ref[...], hb)
    lt = (hi_pass[:ne] + hi_pass[ne:] + _dot_nt(wr_ref[0:ne, :], hl)) + br_ref[...]

    eidx = lax.broadcasted_iota(jnp.int32, (ne, tm), 0)
    vals, hots = [], []
    work = lt
    for _ in range(TOP_K):
        mx = jnp.max(work, axis=0, keepdims=True)
        idx = jnp.min(jnp.where(work == mx, eidx, ne), axis=0, keepdims=True)
        hot = eidx == idx
        vals.append(mx)
        hots.append(hot)
        work = jnp.where(hot, NEG_INF, work)
    ex = [jnp.exp(v - vals[0]) for v in vals]
    tot = ex[0] + ex[1] + ex[2] + ex[3]

    onehot = jnp.zeros((ne, tm), F32)
    gt = jnp.zeros((ne, tm), F32)
    for kk in range(TOP_K):
        onehot = onehot + jnp.where(hots[kk], 1.0, 0.0)
        gt = gt + jnp.where(hots[kk], ex[kk] / tot, 0.0)
    before = _dot(onehot.astype(BF16), earlier_ref[...]) + carry[:, 0:1]
    base_ref[j * ne:(j + 1) * ne, :] = carry
    rt_ref[:, rows] = jnp.where(onehot > 0.5, before, NOT_ROUTED).astype(jnp.int32)
    gt_ref[:, rows] = gt
    return carry + jnp.sum(onehot, axis=1, keepdims=True)


def _out_route(attn, ml, x2, w_out, g2, w_router, b_router):
    T = x2.shape[0]
    tm = ROUTE_ROWS
    ne = N_EXPERTS
    wo = w_out.astype(BF16)
    wrt = w_router.T
    wrh = wrt.astype(BF16)
    wr = jnp.concatenate([wrh, (wrt - wrh.astype(F32)).astype(BF16)], axis=0)
    br = b_router[:, None]
    g2r = g2[None, :]
    tok = jnp.arange(tm)
    earlier = (tok[:, None] < tok[None, :]).astype(BF16)
    ts = tm * ROUTE_TILES_PER_STEP
    row = lambda n: pl.BlockSpec((ts, n), lambda i: (i, 0))
    col = lambda n: pl.BlockSpec((n, ts), lambda i: (0, i))
    full = lambda a: pl.BlockSpec(a.shape, lambda i: (0,) * a.ndim)
    return pl.pallas_call(
        _out_route_kernel,
        grid=(T // ts,),
        in_specs=[row(D_ATTN), row(D_MLSTM), row(D_MODEL), full(wo), full(g2r), full(wr), full(br),
                  full(earlier)],
        out_specs=[row(D_MODEL), row(D_MODEL), col(ne), col(ne),
                   pl.BlockSpec((ROUTE_TILES_PER_STEP * ne, LANES), lambda i: (i, 0)),
                   pl.BlockSpec((ne, LANES), lambda i: (0, 0))],
        out_shape=[jax.ShapeDtypeStruct((T, D_MODEL), F32), jax.ShapeDtypeStruct((T, D_MODEL), BF16),
                   jax.ShapeDtypeStruct((ne, T), jnp.int32), jax.ShapeDtypeStruct((ne, T), F32),
                   jax.ShapeDtypeStruct((T // tm * ne, LANES), F32), jax.ShapeDtypeStruct((ne, LANES), F32)],
        scratch_shapes=[pltpu.VMEM((ne, LANES), F32)],
        compiler_params=_vmem_limit(32),
        name="out_route",
    )(attn, ml, x2, wo, g2r, wr, br, earlier)


def _dispatch_kernel(runstart_ref, runlen_ref, lshift_ref, padstart_ref, npad_ref,
                     h_ref, rt_ref, xs_ref, cbuf, xbuf, zero_ref, stage_ref, sem, xsem, zsem):
    td = DISPATCH_ROWS
    ne = N_EXPERTS
    rb = RUN_ROWS
    ta = BF16_TILE_ROWS
    grp = RUN_GROUP_ROWS // rb
    tile = pl.program_id(0)

    def window(i):
        off = runstart_ref[i] & (ta - 1)
        return off, off + runlen_ref[i]

    def first_chunk(tile_idx, e, action):
        i = tile_idx * ne + e
        off, _ = window(i)
        dst = pl.multiple_of(runstart_ref[i] - off, ta)
        action(pltpu.make_async_copy(cbuf.at[pl.ds(e * rb, rb)], xs_ref.at[pl.ds(dst, rb)], sem))

    @pl.when(tile > 0)
    def _():
        for e in range(ne):
            first_chunk(tile - 1, e, lambda cp: cp.wait())

    @pl.when(tile == 0)
    def _():
        stage_ref[...] = jnp.zeros_like(stage_ref)

    hb = h_ref[...]
    riota = lax.broadcasted_iota(jnp.int32, (rb, td), 0)
    rowt = lax.broadcasted_iota(jnp.int32, (ta, D_MODEL), 0)

    def select(e_slice, shift):
        return jnp.where(riota == rt_ref[e_slice, :] - shift, 1.0, 0.0)

    for g in range(ne // grp):
        experts = range(g * grp, (g + 1) * grp)
        parts = [select(slice(e, e + 1), lshift_ref[tile * ne + e] - window(tile * ne + e)[0]) for e in experts]
        pt = jnp.concatenate(parts, axis=0).astype(BF16)
        cbuf[g * RUN_GROUP_ROWS:(g + 1) * RUN_GROUP_ROWS, :] = _dot(pt, hb).astype(BF16)

    for e in range(ne):
        i = tile * ne + e
        off, total = window(i)
        head = pl.ds(e * rb, ta)
        old = stage_ref[e]
        cbuf[head, :] = jnp.where(rowt < off, old, cbuf[head, :])
        last = e * rb + jnp.minimum(total // ta * ta, rb - ta)
        keep = jnp.logical_and(runlen_ref[i] > 0, total < rb)
        stage_ref[e] = jnp.where(keep, cbuf[pl.ds(pl.multiple_of(last, ta), ta), :], old)

    for e in range(ne):
        first_chunk(tile, e, lambda cp: cp.start())

    def per_expert(e, carry):
        i = tile * ne + e
        off, total = window(i)

        def chunk(ch, c):
            pt = select(pl.ds(e, 1), lshift_ref[i] - off + ch * rb).astype(BF16)
            xbuf[...] = _dot(pt, hb).astype(BF16)
            dst = pl.multiple_of(runstart_ref[i] - off + ch * rb, ta)
            cp = pltpu.make_async_copy(xbuf, xs_ref.at[pl.ds(dst, rb)], xsem)
            cp.start()

            @pl.when(jnp.logical_and(ch == total // rb, total % rb != 0))
            def _():
                last = pl.multiple_of(total % rb // ta * ta, ta)
                stage_ref[e] = xbuf[pl.ds(last, ta), :]

            cp.wait()
            return c

        lax.fori_loop(1, (total + rb - 1) // rb, chunk, 0)
        return carry

    lax.fori_loop(0, ne, per_expert, 0)

    @pl.when(tile == pl.num_programs(0) - 1)
    def _():
        for e in range(ne):
            first_chunk(tile, e, lambda cp: cp.wait())

    @pl.when(tile == pl.num_programs(0) - 1)
    def _():
        zero_ref[...] = jnp.zeros_like(zero_ref)

        def fill(wait):
            def per_run(e, carry):
                start = padstart_ref[e]
                head = (-start) & (ta - 1)
                off = pl.multiple_of(start + head, ta)
                left = npad_ref[e] - head

                def whole(b, c):
                    dst = xs_ref.at[pl.ds(pl.multiple_of(off + b * td, ta), td)]
                    cp = pltpu.make_async_copy(zero_ref, dst, zsem)
                    cp.wait() if wait else cp.start()
                    return c

                n_whole = left // td
                lax.fori_loop(0, n_whole, whole, 0)
                off = pl.multiple_of(off + n_whole * td, ta)
                p = td // 2
                while p >= ta:
                    take = (left & p) != 0

                    @pl.when(take)
                    def _(off=off, p=p):
                        cp = pltpu.make_async_copy(zero_ref.at[pl.ds(0, p)], xs_ref.at[pl.ds(off, p)], zsem)
                        cp.wait() if wait else cp.start()

                    off = pl.multiple_of(off + jnp.where(take, p, 0), ta)
                    p //= 2
                return carry

            lax.fori_loop(0, N_EXPERTS + 1, per_run, 0)

        fill(wait=False)
        fill(wait=True)


def _dispatch(h2, rt, runstart, runlen, lshift, padstart, npad, n_slots):
    T = h2.shape[0]
    td = DISPATCH_ROWS
    ne = N_EXPERTS
    assert td == ROUTE_ROWS
    return pl.pallas_call(
        _dispatch_kernel,
        grid_spec=pltpu.PrefetchScalarGridSpec(
            num_scalar_prefetch=5,
            grid=(T // td,),
            in_specs=[pl.BlockSpec((td, D_MODEL), lambda i, *_: (i, 0)),
                      pl.BlockSpec((ne, td), lambda i, *_: (0, i))],
            out_specs=pl.BlockSpec(memory_space=pl.ANY),
            scratch_shapes=[pltpu.VMEM((ne * RUN_ROWS, D_MODEL), BF16), pltpu.VMEM((RUN_ROWS, D_MODEL), BF16),
                            pltpu.VMEM((td, D_MODEL), BF16), pltpu.VMEM((ne, BF16_TILE_ROWS, D_MODEL), BF16),
                            pltpu.SemaphoreType.DMA(()), pltpu.SemaphoreType.DMA(()),
                            pltpu.SemaphoreType.DMA(())],
        ),
        out_shape=jax.ShapeDtypeStruct((n_slots, D_MODEL), BF16),
        compiler_params=_vmem_limit(32),
        name="dispatch",
    )(runstart, runlen, lshift, padstart, npad, h2, rt)


def _expert_kernel(bexp_ref, first_ref, next_ref, nused_ref, xs_ref, wgu_hbm, bgu_ref, wd_hbm, bd_ref, y_ref,
                   wgu_st, wd_st, wgu_bf, wd_bf, wsem):
    tm = EXPERT_ROWS
    blk = pl.program_id(0)
    active = blk * tm < nused_ref[0]

    def weight_copies(e):
        return (pltpu.make_async_copy(wgu_hbm.at[e], wgu_st, wsem.at[0]),
                pltpu.make_async_copy(wd_hbm.at[e], wd_st, wsem.at[1]))

    @pl.when(blk == 0)
    def _():
        for cp in weight_copies(bexp_ref[0]):
            cp.start()

    @pl.when(jnp.logical_and(active, first_ref[blk] == 1))
    def _():
        for cp in weight_copies(bexp_ref[blk]):
            cp.wait()
        rc = CAST_ROWS

        def cast(i, carry):
            r = pl.ds(pl.multiple_of(i * rc, rc), rc)
            wgu_bf[r, :] = wgu_st[r, :].astype(BF16)
            wd_bf[r, :] = wd_st[r, :].astype(BF16)
            return carry

        lax.fori_loop(0, D_MODEL // rc, cast, 0)

        @pl.when(next_ref[blk] >= 0)
        def _():
            for cp in weight_copies(next_ref[blk]):
                cp.start()

    @pl.when(active)
    def _():
        xb = xs_ref[...]
        acc = jnp.zeros((tm, D_MODEL), F32)
        for f in range(D_FF // FF_CHUNK):
            lo = f * FF_CHUNK
            g = _dot(xb, wgu_bf[:, lo:lo + FF_CHUNK]) + bgu_ref[:, lo:lo + FF_CHUNK]
            u = _dot(xb, wgu_bf[:, D_FF + lo:D_FF + lo + FF_CHUNK]) + bgu_ref[:, D_FF + lo:D_FF + lo + FF_CHUNK]
            g = jnp.minimum(g, SWIGLU_LIMIT)
            u = jnp.clip(u, -SWIGLU_LIMIT, SWIGLU_LIMIT)
            act = g * jax.nn.sigmoid(SWIGLU_ALPHA * g) * (u + 1.0)
            acc = acc + _dot(act.astype(BF16), wd_bf[lo:lo + FF_CHUNK, :])
        y_ref[...] = (acc + bd_ref[...]).astype(y_ref.dtype)

    @pl.when(jnp.logical_not(active))
    def _():
        y_ref[...] = jnp.zeros_like(y_ref)


def _experts(xs, block_expert, first_block, next_expert, n_used, w_gate_up, b_gate_up, w_down, b_down):
    n_blocks = block_expert.shape[0]
    tm = EXPERT_ROWS
    assert D_FF == D_MODEL and xs.shape[0] == (n_blocks - 1) * tm
    bgu = b_gate_up[:, None, :]
    bd = b_down[:, None, :]
    last = n_blocks - 2
    return pl.pallas_call(
        _expert_kernel,
        grid_spec=pltpu.PrefetchScalarGridSpec(
            num_scalar_prefetch=4,
            grid=(n_blocks,),
            in_specs=[
                pl.BlockSpec((tm, D_MODEL), lambda i, be, fb, nx, nu: (jnp.minimum(i, last), 0)),
                pl.BlockSpec(memory_space=pl.ANY),
                pl.BlockSpec((None, 1, 2 * D_FF), lambda i, be, fb, nx, nu: (be[i], 0, 0)),
                pl.BlockSpec(memory_space=pl.ANY),
                pl.BlockSpec((None, 1, D_MODEL), lambda i, be, fb, nx, nu: (be[i], 0, 0)),
            ],
            out_specs=pl.BlockSpec((tm, D_MODEL), lambda i, be, fb, nx, nu: (i, 0)),
            scratch_shapes=[pltpu.VMEM((D_MODEL, 2 * D_FF), F32), pltpu.VMEM((D_FF, D_MODEL), F32),
                            pltpu.VMEM((D_MODEL, 2 * D_FF), BF16), pltpu.VMEM((D_FF, D_MODEL), BF16),
                            pltpu.SemaphoreType.DMA((2,))],
        ),
        out_shape=jax.ShapeDtypeStruct((n_blocks * tm, D_MODEL), BF16),
        compiler_params=_vmem_limit(48),
        name="experts",
    )(block_expert, first_block, next_expert, n_used, xs, w_gate_up, bgu, w_down, bd)


def _combine_kernel(fetch_ref, shift_ref, nch_ref, x1_ref, rt_ref, gt_ref, y_ref, o_ref,
                    buf, xbuf, sem, xsem):
    tc = ROUTE_ROWS
    ne = N_EXPERTS
    rb = RUN_ROWS
    grp = RUN_GROUP_ROWS // rb
    tile = pl.program_id(0)
    base = tile * ne
    slot = tile % 2

    def run_copy(tile_base, buf_slot, e):
        start = pl.multiple_of(fetch_ref[tile_base + e], BF16_TILE_ROWS)
        return pltpu.make_async_copy(y_ref.at[pl.ds(start, rb)], buf.at[buf_slot, pl.ds(e * rb, rb)],
                                     sem.at[buf_slot, e // grp])

    @pl.when(tile == 0)
    def _():
        for e in range(ne):
            run_copy(base, slot, e).start()

    @pl.when(tile + 1 < pl.num_programs(0))
    def _():
        for e in range(ne):
            run_copy(base + ne, 1 - slot, e).start()

    riota = lax.broadcasted_iota(jnp.int32, (rb, tc), 0)

    def select(e_slice, shift):
        return jnp.where(riota == rt_ref[e_slice, :] - shift, gt_ref[e_slice, :], 0.0)

    parts = []
    for e in range(ne):
        run_copy(base, slot, e).wait()
        parts.append(select(slice(e, e + 1), shift_ref[base + e]).astype(BF16))
    o_ref[...] = x1_ref[...] + _dot_tn(jnp.concatenate(parts, axis=0), buf[slot])

    def per_expert(e, carry):
        def chunk(ch, c):
            start = pl.multiple_of(fetch_ref[base + e] + ch * rb, BF16_TILE_ROWS)
            cp = pltpu.make_async_copy(y_ref.at[pl.ds(start, rb)], xbuf, xsem)
            cp.start()
            cp.wait()
            pt = select(pl.ds(e, 1), shift_ref[base + e] + ch * rb).astype(BF16)
            o_ref[...] += _dot_tn(pt, xbuf[...])
            return c

        lax.fori_loop(1, nch_ref[base + e], chunk, 0)
        return carry

    lax.fori_loop(0, ne, per_expert, 0)


def _combine(fetch, shift, nch, x1, rt, gt, y):
    T = x1.shape[0]
    tc = ROUTE_ROWS
    ne = N_EXPERTS
    return pl.pallas_call(
        _combine_kernel,
        grid_spec=pltpu.PrefetchScalarGridSpec(
            num_scalar_prefetch=3,
            grid=(T // tc,),
            in_specs=[pl.BlockSpec((tc, D_MODEL), lambda i, f, s, n: (i, 0)),
                      pl.BlockSpec((ne, tc), lambda i, f, s, n: (0, i)),
                      pl.BlockSpec((ne, tc), lambda i, f, s, n: (0, i)),
                      pl.BlockSpec(memory_space=pl.ANY)],
            out_specs=pl.BlockSpec((tc, D_MODEL), lambda i, f, s, n: (i, 0)),
            scratch_shapes=[pltpu.VMEM((2, ne * RUN_ROWS, D_MODEL), BF16), pltpu.VMEM((RUN_ROWS, D_MODEL), BF16),
                            pltpu.SemaphoreType.DMA((2, ne * RUN_ROWS // RUN_GROUP_ROWS)),
                            pltpu.SemaphoreType.DMA(())],
        ),
        out_shape=jax.ShapeDtypeStruct((T, D_MODEL), F32),
        compiler_params=_vmem_limit(32),
        name="combine",
    )(fetch, shift, nch, x1, rt, gt, y)


def _routing_tables(base, counts, T):
    tm = EXPERT_ROWS
    ne = N_EXPERTS
    n_tiles = T // ROUTE_ROWS
    n_blocks = -(-(T * TOP_K + ne * RUN_ROWS) // tm) + ne
    n_slots = n_blocks * tm
    sizes = counts[:, 0].astype(jnp.int32)
    psizes = (sizes + RUN_ROWS + tm - 1) // tm * tm
    pends = jnp.cumsum(psizes)
    pstarts = pends - psizes
    n_used = pends[-1:]
    padstart = jnp.concatenate([pstarts + sizes, n_used])
    npad = jnp.concatenate([psizes - sizes, n_slots - n_used])
    nb = n_blocks + 1
    blk = jnp.arange(nb, dtype=jnp.int32)
    block_expert = jnp.minimum(jnp.sum(pends[None, :] <= (blk * tm)[:, None], axis=1), ne - 1).astype(jnp.int32)
    first_block = jnp.concatenate([jnp.ones((1,), jnp.int32),
                                   (block_expert[1:] != block_expert[:-1]).astype(jnp.int32)])
    starts_group = jnp.logical_and(first_block == 1, blk * tm < n_used)
    pos = jnp.where(starts_group, blk, nb)
    later = jnp.concatenate([lax.cummin(pos[::-1])[::-1][1:], jnp.full((1,), nb, jnp.int32)])
    next_expert = jnp.where(later < nb, block_expert[jnp.minimum(later, nb - 1)], -1).astype(jnp.int32)
    tile_base = base[:, 0].astype(jnp.int32).reshape(n_tiles, ne)
    run_len = jnp.concatenate([tile_base[1:], sizes[None, :]], axis=0) - tile_base
    run_start = pstarts[None, :] + tile_base
    fetch = run_start // BF16_TILE_ROWS * BF16_TILE_ROWS
    shift = fetch - pstarts[None, :]
    nch = jnp.where(run_len > 0, (run_start - fetch + run_len + RUN_ROWS - 1) // RUN_ROWS, 0)
    flat = lambda a: a.reshape(n_tiles * ne).astype(jnp.int32)
    return (flat(run_start), flat(run_len), flat(tile_base), padstart, npad, n_slots,
            block_expert, first_block, next_expert, n_used, flat(fetch), flat(shift), flat(nch))


def kernel(x, norm1_g, w_in, q_norm_g, k_norm_g, conv_w, conv_b, w_mq, w_mk, b_igate, b_fgate, mh_norm_g,
           w_out, norm2_g, w_router, b_router, w_gate_up, b_gate_up, w_down, b_down):
    B, S, D = x.shape
    T = B * S
    assert D == D_MODEL and S % ATTN_ROWS == 0 and norm1_g.shape[0] == 1
    x2 = x.reshape(T, D)
    q, k, v, xm, vm, om, gates = _in_proj(x2, norm1_g[0], w_in[0], q_norm_g[0], k_norm_g[0])
    attn = _attention(q, k, v, B, S)
    ml = _mlstm(xm, vm, om, gates, conv_w[0], conv_b[0], w_mq[0], w_mk[0], b_igate[0], b_fgate[0],
                mh_norm_g[0], B, S)
    x1, h2, rt, gt, base, counts = _out_route(attn, ml, x2, w_out[0], norm2_g[0], w_router[0], b_router[0])
    (run_start, run_len, tile_base, padstart, npad, n_slots, block_expert, first_block, next_expert, n_used,
     fetch, shift, nch) = _routing_tables(base, counts, T)
    xs = _dispatch(h2, rt, run_start, run_len, tile_base, padstart, npad, n_slots)
    y = _experts(xs, block_expert, first_block, next_expert, n_used, w_gate_up[0], b_gate_up[0], w_down[0],
                 b_down[0])
    out = _combine(fetch, shift, nch, x1, rt, gt, y)
    return out.reshape(B, S, D)
```

```python
import jax
import jax.numpy as jnp
from jax import lax
from jax.experimental import pallas as pl
from jax.experimental.pallas import tpu as pltpu

F32 = jnp.float32
BF16 = jnp.bfloat16
NEG_INF = float("-inf")

D_MODEL = 1024
D_ATTN = 512
HEAD_DIM_ATTN = 64
N_HEADS_ATTN = 8
D_MLSTM = 512
HEAD_DIM_MLSTM = 128
N_HEADS_MLSTM = 4
CONV_K = 4
N_EXPERTS = 32
TOP_K = 4
D_FF = 1024
SWIGLU_LIMIT = 7.0
SWIGLU_ALPHA = 1.702
EPS = 1e-6
LOG2_E = 1.4426950408889634
DILATIONS = (1, 4, 16)
BRANCH_SPAN = 128

LANES = 128
SUBLANES = 8

IN_PROJ_ROWS = 512
ATTN_ROWS = 2048
MLSTM_CHUNK = 256
ROUTE_ROWS = 256
ROUTE_TILES_PER_STEP = 2
EXPERT_ROWS = 256
FF_CHUNK = 1024
CAST_ROWS = 128
DISPATCH_ROWS = 256
HEADNORM_LANES = 256
BF16_TILE_ROWS = 16
RUN_ROWS = 64
RUN_GROUP_ROWS = 256
NOT_ROUTED = -1.0e9


def _vmem_limit(mib):
    return pltpu.CompilerParams(vmem_limit_bytes=mib * 1024 * 1024)


def _split3(a):
    p1 = a.astype(BF16)
    r1 = a - p1.astype(F32)
    p2 = r1.astype(BF16)
    r2 = r1 - p2.astype(F32)
    return p1, p2, r2.astype(BF16)


def _dot(a, b):
    return jnp.dot(a, b, preferred_element_type=F32)


def _dot_nt(a, b):
    return lax.dot_general(a, b, (((1,), (1,)), ((), ())), preferred_element_type=F32)


def _dot_tn(a, b):
    return lax.dot_general(a, b, (((0,), (0,)), ((), ())), preferred_element_type=F32)


def _in_proj_kernel(x_ref, g1_ref, wqkv_ref, wm_ref, wgc_ref, qg_ref, kg_ref, hsum_ref,
                    q_ref, k_ref, v_ref, xm_ref, vm_ref, om_ref, gates_ref):
    x = x_ref[...]
    h = x * lax.rsqrt(jnp.mean(x * x, axis=-1, keepdims=True) + EPS) * g1_ref[...]
    hb = h.astype(BF16)
    hl = (h - hb.astype(F32)).astype(BF16)
    gw = HEADNORM_LANES

    def head_norm(z, g_ref):
        parts = []
        for c in range(D_ATTN // gw):
            zc = z[:, c * gw:(c + 1) * gw]
            ms = _dot((zc * zc).astype(BF16), hsum_ref[...]) * (1.0 / HEAD_DIM_ATTN)
            parts.append(zc * lax.rsqrt(ms + EPS))
        return jnp.concatenate(parts, axis=-1) * g_ref[...]

    zq = _dot(hb, wqkv_ref[:, 0:D_ATTN])
    q_ref[...] = head_norm(zq, qg_ref) * (HEAD_DIM_ATTN ** -0.5 * LOG2_E)
    zk = _dot(hb, wqkv_ref[:, D_ATTN:2 * D_ATTN])
    k_ref[...] = head_norm(zk, kg_ref)
    v_ref[...] = _dot(hb, wqkv_ref[:, 2 * D_ATTN:3 * D_ATTN])
    xm_ref[...] = _dot(hb, wm_ref[:, 0:D_MLSTM])
    vm_ref[...] = _dot(hb, wm_ref[:, D_MLSTM:2 * D_MLSTM])
    om_ref[...] = _dot(hb, wm_ref[:, 2 * D_MLSTM:3 * D_MLSTM])
    ng = 2 * N_HEADS_MLSTM
    rows = hb.shape[0]
    both = _dot(jnp.concatenate([hb, hl], axis=0), wgc_ref[...])
    gsum = both[:rows] + both[rows:]
    gsum = gsum + pltpu.roll(gsum, LANES - ng, axis=1)
    lane = lax.broadcasted_iota(jnp.int32, gsum.shape, 1)
    gates_ref[...] = jnp.where(lane < ng, gsum, 0.0)


def _in_proj(x2, g1, w_in, q_g, k_g):
    T = x2.shape[0]
    tm = IN_PROJ_ROWS
    n_qkv = 3 * D_ATTN
    n_m = 3 * D_MLSTM
    ng = 2 * N_HEADS_MLSTM
    wqkv = w_in[:, :n_qkv].astype(BF16)
    wm = w_in[:, n_qkv:n_qkv + n_m].astype(BF16)
    wg = w_in[:, n_qkv + n_m:]
    wgh = wg.astype(BF16)
    wgl = (wg - wgh.astype(F32)).astype(BF16)
    wgc = jnp.pad(jnp.concatenate([wgh, wgl], axis=1), ((0, 0), (0, LANES - 2 * ng)))
    lane = jnp.arange(HEADNORM_LANES)
    hsum = (lane[:, None] // HEAD_DIM_ATTN == lane[None, :] // HEAD_DIM_ATTN).astype(BF16)
    qg = jnp.tile(q_g, N_HEADS_ATTN)[None, :]
    kg = jnp.tile(k_g, N_HEADS_ATTN)[None, :]
    row = lambda n: pl.BlockSpec((tm, n), lambda i: (i, 0))
    full = lambda a: pl.BlockSpec(a.shape, lambda i: (0,) * a.ndim)
    outs = [jax.ShapeDtypeStruct((T, D_ATTN), F32)] * 3 + [jax.ShapeDtypeStruct((T, D_MLSTM), F32)] * 3
    outs.append(jax.ShapeDtypeStruct((T, LANES), F32))
    g1r = g1[None, :]
    return pl.pallas_call(
        _in_proj_kernel,
        grid=(T // tm,),
        in_specs=[row(D_MODEL), full(g1r), full(wqkv), full(wm), full(wgc), full(qg), full(kg), full(hsum)],
        out_specs=[row(D_ATTN)] * 3 + [row(D_MLSTM)] * 3 + [row(LANES)],
        out_shape=outs,
        compiler_params=_vmem_limit(48),
        name="in_proj",
    )(x2, g1r, wqkv, wm, wgc, qg, kg, hsum)


def _attn_tiles():
    tiles = []
    for br, d in enumerate(DILATIONS):
        group = d * BRANCH_SPAN
        for u in range(ATTN_ROWS // group):
            for c in range(d):
                tiles.append((br, d, u * group + c))
    return tiles


def _attn_kernel(slope_ref, q_ref, kp_ref, kc_ref, vp_ref, vc_ref, o_ref,
                 kwin, vwin, bias_scr, m_scr, l_scr, acc_scr):
    W = ATTN_ROWS
    n = BRANCH_SPAN
    step = pl.program_id(2)
    kwin[0:W, :] = kp_ref[...]
    kwin[W:2 * W, :] = kc_ref[...]
    vwin[0:W, :] = vp_ref[...]
    vwin[W:2 * W, :] = vc_ref[...]

    lane = lax.broadcasted_iota(jnp.int32, (n, LANES), 1)
    first_head = lane < HEAD_DIM_ATTN

    @pl.when(step == 0)
    def _():
        row = lax.broadcasted_iota(jnp.int32, (n, 2 * n), 0)
        col = lax.broadcasted_iota(jnp.int32, (n, 2 * n), 1)
        j = n + row - col
        valid = jnp.logical_and(j >= 0, j <= n)
        jf = j.astype(F32)
        for br, d in enumerate(DILATIONS):
            for hh in range(2):
                slope = slope_ref[:, hh * HEAD_DIM_ATTN:hh * HEAD_DIM_ATTN + 1]
                b = jnp.where(valid, -(slope * (float(d) * LOG2_E)) * jf, NEG_INF)
                bias_scr[br, hh, 0] = b
                bias_scr[br, hh, 1] = jnp.where(col >= n, b, NEG_INF)

    for br, d, qs in _attn_tiles():
        lo = W + qs - n * d
        q = q_ref[pl.ds(qs, n, stride=d), :]
        kk = jnp.concatenate([kwin[pl.ds(lo, n, stride=d), :], kwin[pl.ds(W + qs, n, stride=d), :]],
                             axis=0).astype(BF16)
        vv = jnp.concatenate([vwin[pl.ds(lo, n, stride=d), :], vwin[pl.ds(W + qs, n, stride=d), :]],
                             axis=0).astype(BF16)
        if lo < W:
            sel = jnp.where(step == 0, 1, 0)
        else:
            sel = 0
        qb = q.astype(BF16)
        zero = jnp.zeros_like(qb)
        q2 = jnp.concatenate([jnp.where(first_head, qb, zero), jnp.where(first_head, zero, qb)], axis=0)
        s2 = _dot_nt(q2, kk)
        ms, ls, ps = [], [], []
        for hh in range(2):
            s = s2[hh * n:(hh + 1) * n] + bias_scr[br, hh, sel]
            m = jnp.max(s, axis=-1, keepdims=True)
            p = jnp.exp2(s - m)
            ms.append(m)
            ls.append(jnp.sum(p, axis=-1, keepdims=True))
            ps.append(p.astype(BF16))
        pv = _dot(jnp.concatenate(ps, axis=0), vv)
        rows = pl.ds(qs, n, stride=d)
        m_scr[br, rows, :] = jnp.where(first_head, ms[0], ms[1])
        l_scr[br, rows, :] = jnp.where(first_head, ls[0], ls[1])
        acc_scr[br, rows, :] = jnp.where(first_head, pv[:n], pv[n:])

    chunk = 256

    def combine(i, carry):
        r = pl.ds(pl.multiple_of(i * chunk, chunk), chunk)
        m0, m1, m2 = m_scr[0, r, :], m_scr[1, r, :], m_scr[2, r, :]
        mm = jnp.maximum(jnp.maximum(m0, m1), m2)
        w0, w1, w2 = jnp.exp2(m0 - mm), jnp.exp2(m1 - mm), jnp.exp2(m2 - mm)
        num = w0 * acc_scr[0, r, :] + w1 * acc_scr[1, r, :] + w2 * acc_scr[2, r, :]
        den = w0 * l_scr[0, r, :] + w1 * l_scr[1, r, :] + w2 * l_scr[2, r, :]
        o_ref[r, :] = num / den
        return carry

    lax.fori_loop(0, W // chunk, combine, 0)


def _attention(q, k, v, B, S):
    W = ATTN_ROWS
    nb = S // W
    n_pairs = D_ATTN // LANES
    slopes = jnp.exp2(-8.0 * jnp.arange(1, N_HEADS_ATTN + 1, dtype=F32) / N_HEADS_ATTN)
    slope_l = jnp.repeat(slopes, HEAD_DIM_ATTN).reshape(n_pairs, 1, LANES)
    cur = pl.BlockSpec((W, LANES), lambda b, p, i: (b * nb + i, p))
    prev = pl.BlockSpec((W, LANES), lambda b, p, i: (b * nb + jnp.maximum(i - 1, 0), p))
    n = BRANCH_SPAN
    nbr = len(DILATIONS)
    return pl.pallas_call(
        _attn_kernel,
        grid=(B, n_pairs, nb),
        in_specs=[pl.BlockSpec((None, 1, LANES), lambda b, p, i: (p, 0, 0)), cur, prev, cur, prev, cur],
        out_specs=cur,
        out_shape=jax.ShapeDtypeStruct((B * S, D_ATTN), F32),
        scratch_shapes=[
            pltpu.VMEM((2 * W, LANES), F32), pltpu.VMEM((2 * W, LANES), F32),
            pltpu.VMEM((nbr, 2, 2, n, 2 * n), F32),
            pltpu.VMEM((nbr, W, LANES), F32), pltpu.VMEM((nbr, W, LANES), F32),
            pltpu.VMEM((nbr, W, LANES), F32),
        ],
        compiler_params=_vmem_limit(48),
        name="attention",
    )(slope_l, q, k, k, v, v)


def _mlstm_kernel(xm_ref, vm_ref, om_ref, gcol_ref, grow_ref, cw_ref, cb_ref, wq_ref, wk_ref,
                  bcol_ref, brow_ref, g_ref, o_ref, xs_ref, xtail_ref, c_ref, n_ref, m_ref):
    L = MLSTM_CHUNK
    dh = HEAD_DIM_MLSTM
    nh = N_HEADS_MLSTM

    @pl.when(pl.program_id(1) == 0)
    def _():
        xtail_ref[...] = jnp.zeros_like(xtail_ref)
        c_ref[...] = jnp.zeros_like(c_ref)
        n_ref[...] = jnp.zeros_like(n_ref)
        m_ref[...] = jnp.zeros_like(m_ref)

    x = xm_ref[...]
    xs_ref[0:SUBLANES, :] = xtail_ref[...]
    xs_ref[SUBLANES:SUBLANES + L, :] = x
    xtail_ref[...] = x[L - SUBLANES:L, :]
    xc = cb_ref[...] + x * cw_ref[CONV_K - 1:CONV_K, :]
    for back in range(1, CONV_K):
        xc = xc + xs_ref[SUBLANES - back:SUBLANES - back + L, :] * cw_ref[CONV_K - 1 - back:CONV_K - back, :]
    xc = xc * jax.nn.sigmoid(xc)

    gc = gcol_ref[...] + bcol_ref[...]
    gr = grow_ref[...] + brow_ref[...]
    ri = lax.broadcasted_iota(jnp.int32, (L, L), 0)
    ci = lax.broadcasted_iota(jnp.int32, (L, L), 1)
    causal = ri >= ci
    tril = jnp.where(causal, 1.0, 0.0).astype(BF16)
    triu = jnp.where(ri <= ci, 1.0, 0.0).astype(BF16)
    c1, c2, c3 = _split3(jax.nn.log_sigmoid(gc))
    bc = _dot(tril, c1) + _dot(tril, c2) + _dot(tril, c3)
    r1, r2, r3 = _split3(jax.nn.log_sigmoid(gr))
    brw = _dot(r1, triu) + _dot(r2, triu) + _dot(r3, triu)

    for h in range(nh):
        hs = slice(h * dh, (h + 1) * dh)
        b_col = bc[:, nh + h:nh + h + 1]
        a_col = gc[:, h:h + 1] - b_col
        a_row = gr[h:h + 1, :] - brw[nh + h:nh + h + 1, :]
        m_prev = m_ref[h:h + 1, 0:1]
        log_d = jnp.where(causal, b_col + a_row, NEG_INF)
        log_inter = b_col + m_prev
        m_t = jnp.maximum(log_inter, jnp.max(log_d, axis=-1, keepdims=True))
        d = jnp.exp(log_d - m_t)
        inter = jnp.exp(log_inter - m_t)

        xh = xc[:, hs].astype(BF16)
        qf = _dot(xh, wq_ref[h])
        kf = _dot(xh, wk_ref[h]) * (dh ** -0.5)
        qb = qf.astype(BF16)
        vb = vm_ref[:, hs].astype(BF16)
        s = _dot_nt(qb, kf.astype(BF16)) * d
        c_old = c_ref[h]
        n_old = n_ref[h:h + 1, :]
        num = _dot(s.astype(BF16), vb) + inter * _dot(qb, c_old.astype(BF16))
        den = jnp.sum(s, axis=-1, keepdims=True) + inter * jnp.sum(qf * n_old, axis=-1, keepdims=True)
        hh = num / jnp.maximum(jnp.abs(den), jnp.exp(-m_t))

        b_last = b_col[L - 1:L, :]
        log_w = b_last + a_col
        m_new = jnp.maximum(b_last + m_prev, jnp.max(log_w, axis=0, keepdims=True))
        kw = kf * jnp.exp(log_w - m_new)
        decay = jnp.exp(b_last + m_prev - m_new)
        c_ref[h] = decay * c_old + _dot_tn(kw.astype(BF16), vb)
        n_ref[h:h + 1, :] = decay * n_old + jnp.sum(kw, axis=0, keepdims=True)
        m_ref[h:h + 1, :] = jnp.broadcast_to(m_new, (1, LANES))

        hn = hh * lax.rsqrt(jnp.mean(hh * hh, axis=-1, keepdims=True) + EPS) * g_ref[:, hs]
        o_ref[:, hs] = jax.nn.sigmoid(om_ref[:, hs]) * hn


def _mlstm(xm, vm, om, gates, conv_w, conv_b, w_mq, w_mk, b_i, b_f, mh_g, B, S):
    L = MLSTM_CHUNK
    nc = S // L
    nh = N_HEADS_MLSTM
    ng = 2 * nh
    grow = gates[:, :ng].reshape(B, S, ng).transpose(0, 2, 1)
    bias = jnp.concatenate([b_i, b_f])
    bcol = jnp.pad(bias, (0, LANES - ng))[None, :]
    brow = bias[:, None]
    wq = w_mq.astype(BF16)
    wk = w_mk.astype(BF16)
    cb = conv_b[None, :]
    g = mh_g.reshape(1, D_MLSTM)
    rows = pl.BlockSpec((L, D_MLSTM), lambda b, c: (b * nc + c, 0))
    full = lambda a: pl.BlockSpec(a.shape, lambda b, c: (0,) * a.ndim)
    return pl.pallas_call(
        _mlstm_kernel,
        grid=(B, nc),
        in_specs=[rows, rows, rows, pl.BlockSpec((L, LANES), lambda b, c: (b * nc + c, 0)),
                  pl.BlockSpec((None, ng, L), lambda b, c: (b, 0, c)),
                  full(conv_w), full(cb), full(wq), full(wk), full(bcol), full(brow), full(g)],
        out_specs=rows,
        out_shape=jax.ShapeDtypeStruct((B * S, D_MLSTM), F32),
        scratch_shapes=[
            pltpu.VMEM((L + SUBLANES, D_MLSTM), F32), pltpu.VMEM((SUBLANES, D_MLSTM), F32),
            pltpu.VMEM((nh, HEAD_DIM_MLSTM, HEAD_DIM_MLSTM), F32),
            pltpu.VMEM((SUBLANES, LANES), F32), pltpu.VMEM((SUBLANES, LANES), F32),
        ],
        compiler_params=_vmem_limit(32),
        name="mlstm",
    )(xm, vm, om, gates, grow, conv_w, cb, wq, wk, bcol, brow, g)


def _out_route_kernel(attn_ref, ml_ref, x_ref, wo_ref, g2_ref, wr_ref, br_ref, earlier_ref,
                      x1_ref, h2_ref, rt_ref, gt_ref, base_ref, cnt_ref, carry_ref):
    @pl.when(pl.program_id(0) == 0)
    def _():
        carry_ref[...] = jnp.zeros_like(carry_ref)

    carry = carry_ref[...]
    for j in range(ROUTE_TILES_PER_STEP):
        carry = _route_tile(j, carry, attn_ref, ml_ref, x_ref, wo_ref, g2_ref, wr_ref, br_ref,
                            earlier_ref, x1_ref, h2_ref, rt_ref, gt_ref, base_ref)
    carry_ref[...] = carry
    cnt_ref[...] = carry


def _route_tile(j, carry, attn_ref, ml_ref, x_ref, wo_ref, g2_ref, wr_ref, br_ref, earlier_ref,
                x1_ref, h2_ref, rt_ref, gt_ref, base_ref):
    tm = ROUTE_ROWS
    ne = N_EXPERTS
    rows = slice(j * tm, (j + 1) * tm)

    x1 = (x_ref[rows, :] + _dot(attn_ref[rows, :].astype(BF16), wo_ref[0:D_ATTN, :])
          + _dot(ml_ref[rows, :].astype(BF16), wo_ref[D_ATTN:D_ATTN + D_MLSTM, :]))
    x1_ref[rows, :] = x1
    h2 = x1 * lax.rsqrt(jnp.mean(x1 * x1, axis=-1, keepdims=True) + EPS) * g2_ref[...]
    hb = h2.astype(BF16)
    h2_ref[rows, :] = hb
    hl = (h2 - hb.astype(F32)).astype(BF16)
    hi_pass = _dot_nt(wr_ref[...], hb)
    lt = (hi_pass[:ne] + hi_pass[ne:] + _dot_nt(wr_ref[0:ne, :], hl)) + br_ref[...]

    eidx = lax.broadcasted_iota(jnp.int32, (ne, tm), 0)
    vals, hots = [], []
    work = lt
    for _ in range(TOP_K):
        mx = jnp.max(work, axis=0, keepdims=True)
        idx = jnp.min(jnp.where(work == mx, eidx, ne), axis=0, keepdims=True)
        hot = eidx == idx
        vals.append(mx)
        hots.append(hot)
        work = jnp.where(hot, NEG_INF, work)
    ex = [jnp.exp(v - vals[0]) for v in vals]
    tot = ex[0] + ex[1] + ex[2] + ex[3]

    onehot = jnp.zeros((ne, tm), F32)
    gt = jnp.zeros((ne, tm), F32)
    for kk in range(TOP_K):
        onehot = onehot + jnp.where(hots[kk], 1.0, 0.0)
        gt = gt + jnp.where(hots[kk], ex[kk] / tot, 0.0)
    before = _dot(onehot.astype(BF16), earlier_ref[...]) + carry[:, 0:1]
    base_ref[j * ne:(j + 1) * ne, :] = carry
    rt_ref[:, rows] = jnp.where(onehot > 0.5, before, NOT_ROUTED).astype(jnp.int32)
    gt_ref[:, rows] = gt
    return carry + jnp.sum(onehot, axis=1, keepdims=True)


def _out_route(attn, ml, x2, w_out, g2, w_router, b_router):
    T = x2.shape[0]
    tm = ROUTE_ROWS
    ne = N_EXPERTS
    wo = w_out.astype(BF16)
    wrt = w_router.T
    wrh = wrt.astype(BF16)
    wr = jnp.concatenate([wrh, (wrt - wrh.astype(F32)).astype(BF16)], axis=0)
    br = b_router[:, None]
    g2r = g2[None, :]
    tok = jnp.arange(tm)
    earlier = (tok[:, None] < tok[None, :]).astype(BF16)
    ts = tm * ROUTE_TILES_PER_STEP
    row = lambda n: pl.BlockSpec((ts, n), lambda i: (i, 0))
    col = lambda n: pl.BlockSpec((n, ts), lambda i: (0, i))
    full = lambda a: pl.BlockSpec(a.shape, lambda i: (0,) * a.ndim)
    return pl.pallas_call(
        _out_route_kernel,
        grid=(T // ts,),
        in_specs=[row(D_ATTN), row(D_MLSTM), row(D_MODEL), full(wo), full(g2r), full(wr), full(br),
                  full(earlier)],
        out_specs=[row(D_MODEL), row(D_MODEL), col(ne), col(ne),
                   pl.BlockSpec((ROUTE_TILES_PER_STEP * ne, LANES), lambda i: (i, 0)),
                   pl.BlockSpec((ne, LANES), lambda i: (0, 0))],
        out_shape=[jax.ShapeDtypeStruct((T, D_MODEL), F32), jax.ShapeDtypeStruct((T, D_MODEL), BF16),
                   jax.ShapeDtypeStruct((ne, T), jnp.int32), jax.ShapeDtypeStruct((ne, T), F32),
                   jax.ShapeDtypeStruct((T // tm * ne, LANES), F32), jax.ShapeDtypeStruct((ne, LANES), F32)],
        scratch_shapes=[pltpu.VMEM((ne, LANES), F32)],
        compiler_params=_vmem_limit(32),
        name="out_route",
    )(attn, ml, x2, wo, g2r, wr, br, earlier)


def _dispatch_kernel(runstart_ref, runlen_ref, lshift_ref, padstart_ref, npad_ref,
                     h_ref, rt_ref, xs_ref, cbufs, xbuf, zero_ref, stage_ref, sem, xsem, zsem):
    td = DISPATCH_ROWS
    ne = N_EXPERTS
    rb = RUN_ROWS
    ta = BF16_TILE_ROWS
    grp = RUN_GROUP_ROWS // rb
    tile = pl.program_id(0)
    cbuf = cbufs.at[tile % 2]

    def window(i):
        off = runstart_ref[i] & (ta - 1)
        return off, off + runlen_ref[i]

    def first_chunk(tile_idx, e, action):
        i = tile_idx * ne + e
        off, _ = window(i)
        dst = pl.multiple_of(runstart_ref[i] - off, ta)
        action(pltpu.make_async_copy(cbufs.at[tile_idx % 2, pl.ds(e * rb, rb)], xs_ref.at[pl.ds(dst, rb)], sem))

    @pl.when(tile == 0)
    def _():
        stage_ref[...] = jnp.zeros_like(stage_ref)

    hb = h_ref[...]
    riota = lax.broadcasted_iota(jnp.int32, (rb, td), 0)
    rowt = lax.broadcasted_iota(jnp.int32, (ta, D_MODEL), 0)

    def select(e_slice, shift):
        return jnp.where(riota == rt_ref[e_slice, :] - shift, 1.0, 0.0)

    for g in range(ne // grp):
        experts = range(g * grp, (g + 1) * grp)
        parts = [select(slice(e, e + 1), lshift_ref[tile * ne + e] - window(tile * ne + e)[0]) for e in experts]
        pt = jnp.concatenate(parts, axis=0).astype(BF16)
        cbuf[g * RUN_GROUP_ROWS:(g + 1) * RUN_GROUP_ROWS, :] = _dot(pt, hb).astype(BF16)

    for e in range(ne):
        i = tile * ne + e
        off, total = window(i)
        head = pl.ds(e * rb, ta)
        old = stage_ref[e]
        cbuf[head, :] = jnp.where(rowt < off, old, cbuf[head, :])
        last = e * rb + jnp.minimum(total // ta * ta, rb - ta)
        keep = jnp.logical_and(runlen_ref[i] > 0, total < rb)
        stage_ref[e] = jnp.where(keep, cbuf[pl.ds(pl.multiple_of(last, ta), ta), :], old)

    @pl.when(tile > 0)
    def _():
        for e in range(ne):
            first_chunk(tile - 1, e, lambda cp: cp.wait())

    for e in range(ne):
        first_chunk(tile, e, lambda cp: cp.start())

    def per_expert(e, carry):
        i = tile * ne + e
        off, total = window(i)

        def chunk(ch, c):
            pt = select(pl.ds(e, 1), lshift_ref[i] - off + ch * rb).astype(BF16)
            xbuf[...] = _dot(pt, hb).astype(BF16)
            dst = pl.multiple_of(runstart_ref[i] - off + ch * rb, ta)
            cp = pltpu.make_async_copy(xbuf, xs_ref.at[pl.ds(dst, rb)], xsem)
            cp.start()

            @pl.when(jnp.logical_and(ch == total // rb, total % rb != 0))
            def _():
                last = pl.multiple_of(total % rb // ta * ta, ta)
                stage_ref[e] = xbuf[pl.ds(last, ta), :]

            cp.wait()
            return c

        lax.fori_loop(1, (total + rb - 1) // rb, chunk, 0)
        return carry

    lax.fori_loop(0, ne, per_expert, 0)

    @pl.when(tile == pl.num_programs(0) - 1)
    def _():
        for e in range(ne):
            first_chunk(tile, e, lambda cp: cp.wait())

    @pl.when(tile == pl.num_programs(0) - 1)
    def _():
        zero_ref[...] = jnp.zeros_like(zero_ref)

        def fill(wait):
            def per_run(e, carry):
                start = padstart_ref[e]
                head = (-start) & (ta - 1)
                off = pl.multiple_of(start + head, ta)
                left = npad_ref[e] - head

                def whole(b, c):
                    dst = xs_ref.at[pl.ds(pl.multiple_of(off + b * td, ta), td)]
                    cp = pltpu.make_async_copy(zero_ref, dst, zsem)
                    cp.wait() if wait else cp.start()
                    return c

                n_whole = left // td
                lax.fori_loop(0, n_whole, whole, 0)
                off = pl.multiple_of(off + n_whole * td, ta)
                p = td // 2
                while p >= ta:
                    take = (left & p) != 0

                    @pl.when(take)
                    def _(off=off, p=p):
                        cp = pltpu.make_async_copy(zero_ref.at[pl.ds(0, p)], xs_ref.at[pl.ds(off, p)], zsem)
                        cp.wait() if wait else cp.start()

                    off = pl.multiple_of(off + jnp.where(take, p, 0), ta)
                    p //= 2
                return carry

            lax.fori_loop(0, N_EXPERTS + 1, per_run, 0)

        fill(wait=False)
        fill(wait=True)


def _dispatch(h2, rt, runstart, runlen, lshift, padstart, npad, n_slots):
    T = h2.shape[0]
    td = DISPATCH_ROWS
    ne = N_EXPERTS
    assert td == ROUTE_ROWS
    return pl.pallas_call(
        _dispatch_kernel,
        grid_spec=pltpu.PrefetchScalarGridSpec(
            num_scalar_prefetch=5,
            grid=(T // td,),
            in_specs=[pl.BlockSpec((td, D_MODEL), lambda i, *_: (i, 0)),
                      pl.BlockSpec((ne, td), lambda i, *_: (0, i))],
            out_specs=pl.BlockSpec(memory_space=pl.ANY),
            scratch_shapes=[pltpu.VMEM((2, ne * RUN_ROWS, D_MODEL), BF16), pltpu.VMEM((RUN_ROWS, D_MODEL), BF16),
                            pltpu.VMEM((td, D_MODEL), BF16), pltpu.VMEM((ne, BF16_TILE_ROWS, D_MODEL), BF16),
                            pltpu.SemaphoreType.DMA(()), pltpu.SemaphoreType.DMA(()),
                            pltpu.SemaphoreType.DMA(())],
        ),
        out_shape=jax.ShapeDtypeStruct((n_slots, D_MODEL), BF16),
        compiler_params=_vmem_limit(32),
        name="dispatch",
    )(runstart, runlen, lshift, padstart, npad, h2, rt)


def _expert_kernel(bexp_ref, first_ref, next_ref, nused_ref, xs_ref, wgu_hbm, bgu_ref, wd_hbm, bd_ref, y_ref,
                   wgu_st, wd_st, wgu_bf, wd_bf, wsem):
    tm = EXPERT_ROWS
    blk = pl.program_id(0)
    active = blk * tm < nused_ref[0]

    def weight_copies(e):
        return (pltpu.make_async_copy(wgu_hbm.at[e], wgu_st, wsem.at[0]),
                pltpu.make_async_copy(wd_hbm.at[e], wd_st, wsem.at[1]))

    @pl.when(blk == 0)
    def _():
        for cp in weight_copies(bexp_ref[0]):
            cp.start()

    @pl.when(jnp.logical_and(active, first_ref[blk] == 1))
    def _():
        for cp in weight_copies(bexp_ref[blk]):
            cp.wait()
        rc = CAST_ROWS

        def cast(i, carry):
            r = pl.ds(pl.multiple_of(i * rc, rc), rc)
            wgu_bf[r, :] = wgu_st[r, :].astype(BF16)
            wd_bf[r, :] = wd_st[r, :].astype(BF16)
            return carry

        lax.fori_loop(0, D_MODEL // rc, cast, 0)

        @pl.when(next_ref[blk] >= 0)
        def _():
            for cp in weight_copies(next_ref[blk]):
                cp.start()

    @pl.when(active)
    def _():
        xb = xs_ref[...]
        acc = jnp.zeros((tm, D_MODEL), F32)
        for f in range(D_FF // FF_CHUNK):
            lo = f * FF_CHUNK
            g = _dot(xb, wgu_bf[:, lo:lo + FF_CHUNK]) + bgu_ref[:, lo:lo + FF_CHUNK]
            u = _dot(xb, wgu_bf[:, D_FF + lo:D_FF + lo + FF_CHUNK]) + bgu_ref[:, D_FF + lo:D_FF + lo + FF_CHUNK]
            g = jnp.minimum(g, SWIGLU_LIMIT)
            u = jnp.clip(u, -SWIGLU_LIMIT, SWIGLU_LIMIT)
            act = g * jax.nn.sigmoid(SWIGLU_ALPHA * g) * (u + 1.0)
            acc = acc + _dot(act.astype(BF16), wd_bf[lo:lo + FF_CHUNK, :])
        y_ref[...] = (acc + bd_ref[...]).astype(y_ref.dtype)

    @pl.when(jnp.logical_not(active))
    def _():
        y_ref[...] = jnp.zeros_like(y_ref)


def _experts(xs, block_expert, first_block, next_expert, n_used, w_gate_up, b_gate_up, w_down, b_down):
    n_blocks = block_expert.shape[0]
    tm = EXPERT_ROWS
    assert D_FF == D_MODEL and xs.shape[0] == (n_blocks - 1) * tm
    bgu = b_gate_up[:, None, :]
    bd = b_down[:, None, :]
    last = n_blocks - 2
    return pl.pallas_call(
        _expert_kernel,
        grid_spec=pltpu.PrefetchScalarGridSpec(
            num_scalar_prefetch=4,
            grid=(n_blocks,),
            in_specs=[
                pl.BlockSpec((tm, D_MODEL), lambda i, be, fb, nx, nu: (jnp.minimum(i, last), 0)),
                pl.BlockSpec(memory_space=pl.ANY),
                pl.BlockSpec((None, 1, 2 * D_FF), lambda i, be, fb, nx, nu: (be[i], 0, 0)),
                pl.BlockSpec(memory_space=pl.ANY),
                pl.BlockSpec((None, 1, D_MODEL), lambda i, be, fb, nx, nu: (be[i], 0, 0)),
            ],
            out_specs=pl.BlockSpec((tm, D_MODEL), lambda i, be, fb, nx, nu: (i, 0)),
            scratch_shapes=[pltpu.VMEM((D_MODEL, 2 * D_FF), F32), pltpu.VMEM((D_FF, D_MODEL), F32),
                            pltpu.VMEM((D_MODEL, 2 * D_FF), BF16), pltpu.VMEM((D_FF, D_MODEL), BF16),
                            pltpu.SemaphoreType.DMA((2,))],
        ),
        out_shape=jax.ShapeDtypeStruct((n_blocks * tm, D_MODEL), BF16),
        compiler_params=_vmem_limit(48),
        name="experts",
    )(block_expert, first_block, next_expert, n_used, xs, w_gate_up, bgu, w_down, bd)


def _combine_kernel(fetch_ref, shift_ref, nch_ref, x1_ref, rt_ref, gt_ref, y_ref, o_ref,
                    buf, xbuf, sem, xsem):
    tc = ROUTE_ROWS
    ne = N_EXPERTS
    rb = RUN_ROWS
    grp = RUN_GROUP_ROWS // rb
    tile = pl.program_id(0)
    base = tile * ne
    slot = tile % 2

    def run_copy(tile_base, buf_slot, e):
        start = pl.multiple_of(fetch_ref[tile_base + e], BF16_TILE_ROWS)
        return pltpu.make_async_copy(y_ref.at[pl.ds(start, rb)], buf.at[buf_slot, pl.ds(e * rb, rb)],
                                     sem.at[buf_slot, e // grp])

    @pl.when(tile == 0)
    def _():
        for e in range(ne):
            run_copy(base, slot, e).start()

    @pl.when(tile + 1 < pl.num_programs(0))
    def _():
        for e in range(ne):
            run_copy(base + ne, 1 - slot, e).start()

    riota = lax.broadcasted_iota(jnp.int32, (rb, tc), 0)

    def select(e_slice, shift):
        return jnp.where(riota == rt_ref[e_slice, :] - shift, gt_ref[e_slice, :], 0.0)

    parts = []
    for e in range(ne):
        run_copy(base, slot, e).wait()
        parts.append(select(slice(e, e + 1), shift_ref[base + e]).astype(BF16))
    o_ref[...] = x1_ref[...] + _dot_tn(jnp.concatenate(parts, axis=0), buf[slot])

    def per_expert(e, carry):
        def chunk(ch, c):
            start = pl.multiple_of(fetch_ref[base + e] + ch * rb, BF16_TILE_ROWS)
            cp = pltpu.make_async_copy(y_ref.at[pl.ds(start, rb)], xbuf, xsem)
            cp.start()
            cp.wait()
            pt = select(pl.ds(e, 1), shift_ref[base + e] + ch * rb).astype(BF16)
            o_ref[...] += _dot_tn(pt, xbuf[...])
            return c

        lax.fori_loop(1, nch_ref[base + e], chunk, 0)
        return carry

    lax.fori_loop(0, ne, per_expert, 0)


def _combine(fetch, shift, nch, x1, rt, gt, y):
    T = x1.shape[0]
    tc = ROUTE_ROWS
    ne = N_EXPERTS
    return pl.pallas_call(
        _combine_kernel,
        grid_spec=pltpu.PrefetchScalarGridSpec(
            num_scalar_prefetch=3,
            grid=(T // tc,),
            in_specs=[pl.BlockSpec((tc, D_MODEL), lambda i, f, s, n: (i, 0)),
                      pl.BlockSpec((ne, tc), lambda i, f, s, n: (0, i)),
                      pl.BlockSpec((ne, tc), lambda i, f, s, n: (0, i)),
                      pl.BlockSpec(memory_space=pl.ANY)],
            out_specs=pl.BlockSpec((tc, D_MODEL), lambda i, f, s, n: (i, 0)),
            scratch_shapes=[pltpu.VMEM((2, ne * RUN_ROWS, D_MODEL), BF16), pltpu.VMEM((RUN_ROWS, D_MODEL), BF16),
                            pltpu.SemaphoreType.DMA((2, ne * RUN_ROWS // RUN_GROUP_ROWS)),
                            pltpu.SemaphoreType.DMA(())],
        ),
        out_shape=jax.ShapeDtypeStruct((T, D_MODEL), F32),
        compiler_params=_vmem_limit(32),
        name="combine",
    )(fetch, shift, nch, x1, rt, gt, y)


def _routing_tables(base, counts, T):
    tm = EXPERT_ROWS
    ne = N_EXPERTS
    n_tiles = T // ROUTE_ROWS
    n_blocks = -(-(T * TOP_K + ne * RUN_ROWS) // tm) + ne
    n_slots = n_blocks * tm
    sizes = counts[:, 0].astype(jnp.int32)
    psizes = (sizes + RUN_ROWS + tm - 1) // tm * tm
    pends = jnp.cumsum(psizes)
    pstarts = pends - psizes
    n_used = pends[-1:]
    padstart = jnp.concatenate([pstarts + sizes, n_used])
    npad = jnp.concatenate([psizes - sizes, n_slots - n_used])
    nb = n_blocks + 1
    blk = jnp.arange(nb, dtype=jnp.int32)
    block_expert = jnp.minimum(jnp.sum(pends[None, :] <= (blk * tm)[:, None], axis=1), ne - 1).astype(jnp.int32)
    first_block = jnp.concatenate([jnp.ones((1,), jnp.int32),
                                   (block_expert[1:] != block_expert[:-1]).astype(jnp.int32)])
    starts_group = jnp.logical_and(first_block == 1, blk * tm < n_used)
    pos = jnp.where(starts_group, blk, nb)
    later = jnp.concatenate([lax.cummin(pos[::-1])[::-1][1:], jnp.full((1,), nb, jnp.int32)])
    next_expert = jnp.where(later < nb, block_expert[jnp.minimum(later, nb - 1)], -1).astype(jnp.int32)
    tile_base = base[:, 0].astype(jnp.int32).reshape(n_tiles, ne)
    run_len = jnp.concatenate([tile_base[1:], sizes[None, :]], axis=0) - tile_base
    run_start = pstarts[None, :] + tile_base
    fetch = run_start // BF16_TILE_ROWS * BF16_TILE_ROWS
    shift = fetch - pstarts[None, :]
    nch = jnp.where(run_len > 0, (run_start - fetch + run_len + RUN_ROWS - 1) // RUN_ROWS, 0)
    flat = lambda a: a.reshape(n_tiles * ne).astype(jnp.int32)
    return (flat(run_start), flat(run_len), flat(tile_base), padstart, npad, n_slots,
            block_expert, first_block, next_expert, n_used, flat(fetch), flat(shift), flat(nch))


def kernel(x, norm1_g, w_in, q_norm_g, k_norm_g, conv_w, conv_b, w_mq, w_mk, b_igate, b_fgate, mh_norm_g,
           w_out, norm2_g, w_router, b_router, w_gate_up, b_gate_up, w_down, b_down):
    B, S, D = x.shape
    T = B * S
    assert D == D_MODEL and S % ATTN_ROWS == 0 and norm1_g.shape[0] == 1
    x2 = x.reshape(T, D)
    q, k, v, xm, vm, om, gates = _in_proj(x2, norm1_g[0], w_in[0], q_norm_g[0], k_norm_g[0])
    attn = _attention(q, k, v, B, S)
    ml = _mlstm(xm, vm, om, gates, conv_w[0], conv_b[0], w_mq[0], w_mk[0], b_igate[0], b_fgate[0],
                mh_norm_g[0], B, S)
    x1, h2, rt, gt, base, counts = _out_route(attn, ml, x2, w_out[0], norm2_g[0], w_router[0], b_router[0])
    (run_start, run_len, tile_base, padstart, npad, n_slots, block_expert, first_block, next_expert, n_used,
     fetch, shift, nch) = _routing_tables(base, counts, T)
    xs = _dispatch(h2, rt, run_start, run_len, tile_base, padstart, npad, n_slots)
    y = _experts(xs, block_expert, first_block, next_expert, n_used, w_gate_up[0], b_gate_up[0], w_down[0],
                 b_down[0])
    out = _combine(fetch, shift, nch, x1, rt, gt, y)
    return out.reshape(B, S, D)
```

```python
import jax
import jax.numpy as jnp
from jax import lax
from jax.experimental import pallas as pl
from jax.experimental.pallas import tpu as pltpu

F32 = jnp.float32
BF16 = jnp.bfloat16
NEG_INF = float("-inf")

D_MODEL = 1024
D_ATTN = 512
HEAD_DIM_ATTN = 64
N_HEADS_ATTN = 8
D_MLSTM = 512
HEAD_DIM_MLSTM = 128
N_HEADS_MLSTM = 4
CONV_K = 4
N_EXPERTS = 32
TOP_K = 4
D_FF = 1024
SWIGLU_LIMIT = 7.0
SWIGLU_ALPHA = 1.702
EPS = 1e-6
LOG2_E = 1.4426950408889634
DILATIONS = (1, 4, 16)
BRANCH_SPAN = 128

LANES = 128
SUBLANES = 8

IN_PROJ_ROWS = 512
ATTN_ROWS = 2048
MLSTM_CHUNK = 256
ROUTE_ROWS = 256
ROUTE_TILES_PER_STEP = 2
EXPERT_ROWS = 256
FF_CHUNK = 1024
CAST_ROWS = 128
DISPATCH_ROWS = 256
HEADNORM_LANES = 256
BF16_TILE_ROWS = 16
RUN_ROWS = 64
RUN_GROUP_ROWS = 256
NOT_ROUTED = -1.0e9


def _vmem_limit(mib):
    return pltpu.CompilerParams(vmem_limit_bytes=mib * 1024 * 1024)


def _split3(a):
    p1 = a.astype(BF16)
    r1 = a - p1.astype(F32)
    p2 = r1.astype(BF16)
    r2 = r1 - p2.astype(F32)
    return p1, p2, r2.astype(BF16)


def _dot(a, b):
    return jnp.dot(a, b, preferred_element_type=F32)


def _dot_nt(a, b):
    return lax.dot_general(a, b, (((1,), (1,)), ((), ())), preferred_element_type=F32)


def _dot_tn(a, b):
    return lax.dot_general(a, b, (((0,), (0,)), ((), ())), preferred_element_type=F32)


def _in_proj_kernel(x_ref, g1_ref, wqkv_ref, wm_ref, wgc_ref, qg_ref, kg_ref, hsum_ref,
                    q_ref, k_ref, v_ref, xm_ref, vm_ref, om_ref, gates_ref):
    x = x_ref[...]
    h = x * lax.rsqrt(jnp.mean(x * x, axis=-1, keepdims=True) + EPS) * g1_ref[...]
    hb = h.astype(BF16)
    hl = (h - hb.astype(F32)).astype(BF16)
    gw = HEADNORM_LANES

    def head_norm(z, g_ref):
        parts = []
        for c in range(D_ATTN // gw):
            zc = z[:, c * gw:(c + 1) * gw]
            ms = _dot((zc * zc).astype(BF16), hsum_ref[...]) * (1.0 / HEAD_DIM_ATTN)
            parts.append(zc * lax.rsqrt(ms + EPS))
        return jnp.concatenate(parts, axis=-1) * g_ref[...]

    zq = _dot(hb, wqkv_ref[:, 0:D_ATTN])
    q_ref[...] = head_norm(zq, qg_ref) * (HEAD_DIM_ATTN ** -0.5 * LOG2_E)
    zk = _dot(hb, wqkv_ref[:, D_ATTN:2 * D_ATTN])
    k_ref[...] = head_norm(zk, kg_ref)
    v_ref[...] = _dot(hb, wqkv_ref[:, 2 * D_ATTN:3 * D_ATTN])
    xm_ref[...] = _dot(hb, wm_ref[:, 0:D_MLSTM])
    vm_ref[...] = _dot(hb, wm_ref[:, D_MLSTM:2 * D_MLSTM])
    om_ref[...] = _dot(hb, wm_ref[:, 2 * D_MLSTM:3 * D_MLSTM])
    ng = 2 * N_HEADS_MLSTM
    rows = hb.shape[0]
    both = _dot(jnp.concatenate([hb, hl], axis=0), wgc_ref[...])
    gsum = both[:rows] + both[rows:]
    gsum = gsum + pltpu.roll(gsum, LANES - ng, axis=1)
    lane = lax.broadcasted_iota(jnp.int32, gsum.shape, 1)
    gates_ref[...] = jnp.where(lane < ng, gsum, 0.0)


def _in_proj(x2, g1, w_in, q_g, k_g):
    T = x2.shape[0]
    tm = IN_PROJ_ROWS
    n_qkv = 3 * D_ATTN
    n_m = 3 * D_MLSTM
    ng = 2 * N_HEADS_MLSTM
    wqkv = w_in[:, :n_qkv].astype(BF16)
    wm = w_in[:, n_qkv:n_qkv + n_m].astype(BF16)
    wg = w_in[:, n_qkv + n_m:]
    wgh = wg.astype(BF16)
    wgl = (wg - wgh.astype(F32)).astype(BF16)
    wgc = jnp.pad(jnp.concatenate([wgh, wgl], axis=1), ((0, 0), (0, LANES - 2 * ng)))
    lane = jnp.arange(HEADNORM_LANES)
    hsum = (lane[:, None] // HEAD_DIM_ATTN == lane[None, :] // HEAD_DIM_ATTN).astype(BF16)
    qg = jnp.tile(q_g, N_HEADS_ATTN)[None, :]
    kg = jnp.tile(k_g, N_HEADS_ATTN)[None, :]
    row = lambda n: pl.BlockSpec((tm, n), lambda i: (i, 0))
    full = lambda a: pl.BlockSpec(a.shape, lambda i: (0,) * a.ndim)
    outs = [jax.ShapeDtypeStruct((T, D_ATTN), F32)] * 3 + [jax.ShapeDtypeStruct((T, D_MLSTM), F32)] * 3
    outs.append(jax.ShapeDtypeStruct((T, LANES), F32))
    g1r = g1[None, :]
    return pl.pallas_call(
        _in_proj_kernel,
        grid=(T // tm,),
        in_specs=[row(D_MODEL), full(g1r), full(wqkv), full(wm), full(wgc), full(qg), full(kg), full(hsum)],
        out_specs=[row(D_ATTN)] * 3 + [row(D_MLSTM)] * 3 + [row(LANES)],
        out_shape=outs,
        compiler_params=_vmem_limit(48),
        name="in_proj",
    )(x2, g1r, wqkv, wm, wgc, qg, kg, hsum)


def _attn_tiles():
    tiles = []
    for br, d in enumerate(DILATIONS):
        group = d * BRANCH_SPAN
        for u in range(ATTN_ROWS // group):
            for c in range(d):
                tiles.append((br, d, u * group + c))
    return tiles


def _attn_kernel(slope_ref, q_ref, kp_ref, kc_ref, vp_ref, vc_ref, o_ref,
                 kwin, vwin, bias_scr, m_scr, l_scr, acc_scr):
    W = ATTN_ROWS
    n = BRANCH_SPAN
    step = pl.program_id(2)
    kwin[0:W, :] = kp_ref[...]
    kwin[W:2 * W, :] = kc_ref[...]
    vwin[0:W, :] = vp_ref[...]
    vwin[W:2 * W, :] = vc_ref[...]

    lane = lax.broadcasted_iota(jnp.int32, (n, LANES), 1)
    first_head = lane < HEAD_DIM_ATTN

    @pl.when(step == 0)
    def _():
        row = lax.broadcasted_iota(jnp.int32, (n, 2 * n), 0)
        col = lax.broadcasted_iota(jnp.int32, (n, 2 * n), 1)
        j = n + row - col
        valid = jnp.logical_and(j >= 0, j <= n)
        jf = j.astype(F32)
        for br, d in enumerate(DILATIONS):
            for hh in range(2):
                slope = slope_ref[:, hh * HEAD_DIM_ATTN:hh * HEAD_DIM_ATTN + 1]
                b = jnp.where(valid, -(slope * (float(d) * LOG2_E)) * jf, NEG_INF)
                bias_scr[br, hh, 0] = b
                bias_scr[br, hh, 1] = jnp.where(col >= n, b, NEG_INF)

    for br, d, qs in _attn_tiles():
        lo = W + qs - n * d
        q = q_ref[pl.ds(qs, n, stride=d), :]
        kk = jnp.concatenate([kwin[pl.ds(lo, n, stride=d), :], kwin[pl.ds(W + qs, n, stride=d), :]],
                             axis=0).astype(BF16)
        vv = jnp.concatenate([vwin[pl.ds(lo, n, stride=d), :], vwin[pl.ds(W + qs, n, stride=d), :]],
                             axis=0).astype(BF16)
        if lo < W:
            sel = jnp.where(step == 0, 1, 0)
        else:
            sel = 0
        qb = q.astype(BF16)
        zero = jnp.zeros_like(qb)
        q2 = jnp.concatenate([jnp.where(first_head, qb, zero), jnp.where(first_head, zero, qb)], axis=0)
        s2 = _dot_nt(q2, kk)
        ms, ls, ps = [], [], []
        for hh in range(2):
            s = s2[hh * n:(hh + 1) * n] + bias_scr[br, hh, sel]
            m = jnp.max(s, axis=-1, keepdims=True)
            p = jnp.exp2(s - m)
            ms.append(m)
            ls.append(jnp.sum(p, axis=-1, keepdims=True))
            ps.append(p.astype(BF16))
        pv = _dot(jnp.concatenate(ps, axis=0), vv)
        rows = pl.ds(qs, n, stride=d)
        m_scr[br, rows, :] = jnp.where(first_head, ms[0], ms[1])
        l_scr[br, rows, :] = jnp.where(first_head, ls[0], ls[1])
        acc_scr[br, rows, :] = jnp.where(first_head, pv[:n], pv[n:])

    chunk = 256

    def combine(i, carry):
        r = pl.ds(pl.multiple_of(i * chunk, chunk), chunk)
        m0, m1, m2 = m_scr[0, r, :], m_scr[1, r, :], m_scr[2, r, :]
        mm = jnp.maximum(jnp.maximum(m0, m1), m2)
        w0, w1, w2 = jnp.exp2(m0 - mm), jnp.exp2(m1 - mm), jnp.exp2(m2 - mm)
        num = w0 * acc_scr[0, r, :] + w1 * acc_scr[1, r, :] + w2 * acc_scr[2, r, :]
        den = w0 * l_scr[0, r, :] + w1 * l_scr[1, r, :] + w2 * l_scr[2, r, :]
        o_ref[r, :] = num / den
        return carry

    lax.fori_loop(0, W // chunk, combine, 0)


def _attention(q, k, v, B, S):
    W = ATTN_ROWS
    nb = S // W
    n_pairs = D_ATTN // LANES
    slopes = jnp.exp2(-8.0 * jnp.arange(1, N_HEADS_ATTN + 1, dtype=F32) / N_HEADS_ATTN)
    slope_l = jnp.repeat(slopes, HEAD_DIM_ATTN).reshape(n_pairs, 1, LANES)
    cur = pl.BlockSpec((W, LANES), lambda b, p, i: (b * nb + i, p))
    prev = pl.BlockSpec((W, LANES), lambda b, p, i: (b * nb + jnp.maximum(i - 1, 0), p))
    n = BRANCH_SPAN
    nbr = len(DILATIONS)
    return pl.pallas_call(
        _attn_kernel,
        grid=(B, n_pairs, nb),
        in_specs=[pl.BlockSpec((None, 1, LANES), lambda b, p, i: (p, 0, 0)), cur, prev, cur, prev, cur],
        out_specs=cur,
        out_shape=jax.ShapeDtypeStruct((B * S, D_ATTN), F32),
        scratch_shapes=[
            pltpu.VMEM((2 * W, LANES), F32), pltpu.VMEM((2 * W, LANES), F32),
            pltpu.VMEM((nbr, 2, 2, n, 2 * n), F32),
            pltpu.VMEM((nbr, W, LANES), F32), pltpu.VMEM((nbr, W, LANES), F32),
            pltpu.VMEM((nbr, W, LANES), F32),
        ],
        compiler_params=_vmem_limit(48),
        name="attention",
    )(slope_l, q, k, k, v, v)


def _mlstm_kernel(xm_ref, vm_ref, om_ref, gcol_ref, grow_ref, cw_ref, cb_ref, wq_ref, wk_ref,
                  bcol_ref, brow_ref, g_ref, o_ref, xs_ref, xtail_ref, c_ref, n_ref, m_ref):
    L = MLSTM_CHUNK
    dh = HEAD_DIM_MLSTM
    nh = N_HEADS_MLSTM

    @pl.when(pl.program_id(1) == 0)
    def _():
        xtail_ref[...] = jnp.zeros_like(xtail_ref)
        c_ref[...] = jnp.zeros_like(c_ref)
        n_ref[...] = jnp.zeros_like(n_ref)
        m_ref[...] = jnp.zeros_like(m_ref)

    x = xm_ref[...]
    xs_ref[0:SUBLANES, :] = xtail_ref[...]
    xs_ref[SUBLANES:SUBLANES + L, :] = x
    xtail_ref[...] = x[L - SUBLANES:L, :]
    xc = cb_ref[...] + x * cw_ref[CONV_K - 1:CONV_K, :]
    for back in range(1, CONV_K):
        xc = xc + xs_ref[SUBLANES - back:SUBLANES - back + L, :] * cw_ref[CONV_K - 1 - back:CONV_K - back, :]
    xc = xc * jax.nn.sigmoid(xc)

    gc = gcol_ref[...] + bcol_ref[...]
    gr = grow_ref[...] + brow_ref[...]
    ri = lax.broadcasted_iota(jnp.int32, (L, L), 0)
    ci = lax.broadcasted_iota(jnp.int32, (L, L), 1)
    causal = ri >= ci
    tril = jnp.where(causal, 1.0, 0.0).astype(BF16)
    triu = jnp.where(ri <= ci, 1.0, 0.0).astype(BF16)
    c1, c2, c3 = _split3(jax.nn.log_sigmoid(gc))
    bc = _dot(tril, c1) + _dot(tril, c2) + _dot(tril, c3)
    r1, r2, r3 = _split3(jax.nn.log_sigmoid(gr))
    brw = _dot(r1, triu) + _dot(r2, triu) + _dot(r3, triu)

    for h in range(nh):
        hs = slice(h * dh, (h + 1) * dh)
        b_col = bc[:, nh + h:nh + h + 1]
        a_col = gc[:, h:h + 1] - b_col
        a_row = gr[h:h + 1, :] - brw[nh + h:nh + h + 1, :]
        m_prev = m_ref[h:h + 1, 0:1]
        log_d = jnp.where(causal, b_col + a_row, NEG_INF)
        log_inter = b_col + m_prev
        m_t = jnp.maximum(log_inter, jnp.max(log_d, axis=-1, keepdims=True))
        d = jnp.exp(log_d - m_t)
        inter = jnp.exp(log_inter - m_t)

        xh = xc[:, hs].astype(BF16)
        qf = _dot(xh, wq_ref[h])
        kf = _dot(xh, wk_ref[h]) * (dh ** -0.5)
        qb = qf.astype(BF16)
        vb = vm_ref[:, hs].astype(BF16)
        s = _dot_nt(qb, kf.astype(BF16)) * d
        c_old = c_ref[h]
        n_old = n_ref[h:h + 1, :]
        num = _dot(s.astype(BF16), vb) + inter * _dot(qb, c_old.astype(BF16))
        den = jnp.sum(s, axis=-1, keepdims=True) + inter * jnp.sum(qf * n_old, axis=-1, keepdims=True)
        hh = num / jnp.maximum(jnp.abs(den), jnp.exp(-m_t))

        b_last = b_col[L - 1:L, :]
        log_w = b_last + a_col
        m_new = jnp.maximum(b_last + m_prev, jnp.max(log_w, axis=0, keepdims=True))
        kw = kf * jnp.exp(log_w - m_new)
        decay = jnp.exp(b_last + m_prev - m_new)
        c_ref[h] = decay * c_old + _dot_tn(kw.astype(BF16), vb)
        n_ref[h:h + 1, :] = decay * n_old + jnp.sum(kw, axis=0, keepdims=True)
        m_ref[h:h + 1, :] = jnp.broadcast_to(m_new, (1, LANES))

        hn = hh * lax.rsqrt(jnp.mean(hh * hh, axis=-1, keepdims=True) + EPS) * g_ref[:, hs]
        o_ref[:, hs] = jax.nn.sigmoid(om_ref[:, hs]) * hn


def _mlstm(xm, vm, om, gates, conv_w, conv_b, w_mq, w_mk, b_i, b_f, mh_g, B, S):
    L = MLSTM_CHUNK
    nc = S // L
    nh = N_HEADS_MLSTM
    ng = 2 * nh
    grow = gates[:, :ng].reshape(B, S, ng).transpose(0, 2, 1)
    bias = jnp.concatenate([b_i, b_f])
    bcol = jnp.pad(bias, (0, LANES - ng))[None, :]
    brow = bias[:, None]
    wq = w_mq.astype(BF16)
    wk = w_mk.astype(BF16)
    cb = conv_b[None, :]
    g = mh_g.reshape(1, D_MLSTM)
    rows = pl.BlockSpec((L, D_MLSTM), lambda b, c: (b * nc + c, 0))
    full = lambda a: pl.BlockSpec(a.shape, lambda b, c: (0,) * a.ndim)
    return pl.pallas_call(
        _mlstm_kernel,
        grid=(B, nc),
        in_specs=[rows, rows, rows, pl.BlockSpec((L, LANES), lambda b, c: (b * nc + c, 0)),
                  pl.BlockSpec((None, ng, L), lambda b, c: (b, 0, c)),
                  full(conv_w), full(cb), full(wq), full(wk), full(bcol), full(brow), full(g)],
        out_specs=rows,
        out_shape=jax.ShapeDtypeStruct((B * S, D_MLSTM), F32),
        scratch_shapes=[
            pltpu.VMEM((L + SUBLANES, D_MLSTM), F32), pltpu.VMEM((SUBLANES, D_MLSTM), F32),
            pltpu.VMEM((nh, HEAD_DIM_MLSTM, HEAD_DIM_MLSTM), F32),
            pltpu.VMEM((SUBLANES, LANES), F32), pltpu.VMEM((SUBLANES, LANES), F32),
        ],
        compiler_params=_vmem_limit(32),
        name="mlstm",
    )(xm, vm, om, gates, grow, conv_w, cb, wq, wk, bcol, brow, g)


def _out_route_kernel(attn_ref, ml_ref, x_ref, wo_ref, g2_ref, wr_ref, br_ref, earlier_ref,
                      x1_ref, h2_ref, rt_ref, gt_ref, base_ref, cnt_ref, carry_ref):
    @pl.when(pl.program_id(0) == 0)
    def _():
        carry_ref[...] = jnp.zeros_like(carry_ref)

    carry = carry_ref[...]
    for j in range(ROUTE_TILES_PER_STEP):
        carry = _route_tile(j, carry, attn_ref, ml_ref, x_ref, wo_ref, g2_ref, wr_ref, br_ref,
                            earlier_ref, x1_ref, h2_ref, rt_ref, gt_ref, base_ref)
    carry_ref[...] = carry
    cnt_ref[...] = carry


def _route_tile(j, carry, attn_ref, ml_ref, x_ref, wo_ref, g2_ref, wr_ref, br_ref, earlier_ref,
                x1_ref, h2_ref, rt_ref, gt_ref, base_ref):
    tm = ROUTE_ROWS
    ne = N_EXPERTS
    rows = slice(j * tm, (j + 1) * tm)

    x1 = (x_ref[rows, :] + _dot(attn_ref[rows, :].astype(BF16), wo_ref[0:D_ATTN, :])
          + _dot(ml_ref[rows, :].astype(BF16), wo_ref[D_ATTN:D_ATTN + D_MLSTM, :]))
    x1_ref[rows, :] = x1
    h2 = x1 * lax.rsqrt(jnp.mean(x1 * x1, axis=-1, keepdims=True) + EPS) * g2_ref[...]
    hb = h2.astype(BF16)
    h2_ref[rows, :] = hb
    hl = (h2 - hb.astype(F32)).astype(BF16)
    hi_pass = _dot_nt(wr_ref[...], hb)
    lt = (hi_pass[:ne] + hi_pass[ne:] + _dot_nt(wr_ref[0:ne, :], hl)) + br_ref[...]

    eidx = lax.broadcasted_iota(jnp.int32, (ne, tm), 0)
    vals, hots = [], []
    work = lt
    for _ in range(TOP_K):
        mx = jnp.max(work, axis=0, keepdims=True)
        idx = jnp.min(jnp.where(work == mx, eidx, ne), axis=0, keepdims=True)
        hot = eidx == idx
        vals.append(mx)
        hots.append(hot)
        work = jnp.where(hot, NEG_INF, work)
    ex = [jnp.exp(v - vals[0]) for v in vals]
    tot = ex[0] + ex[1] + ex[2] + ex[3]

    onehot = jnp.zeros((ne, tm), F32)
    gt = jnp.zeros((ne, tm), F32)
    for kk in range(TOP_K):
        onehot = onehot + jnp.where(hots[kk], 1.0, 0.0)
        gt = gt + jnp.where(hots[kk], ex[kk] / tot, 0.0)
    before = _dot(onehot.astype(BF16), earlier_ref[...]) + carry[:, 0:1]
    base_ref[j * ne:(j + 1) * ne, :] = carry
    rt_ref[:, rows] = jnp.where(onehot > 0.5, before, NOT_ROUTED).astype(jnp.int32)
    gt_ref[:, rows] = gt
    return carry + jnp.sum(onehot, axis=1, keepdims=True)


def _out_route(attn, ml, x2, w_out, g2, w_router, b_router):
    T = x2.shape[0]
    tm = ROUTE_ROWS
    ne = N_EXPERTS
    wo = w_out.astype(BF16)
    wrt = w_router.T
    wrh = wrt.astype(BF16)
    wr = jnp.concatenate([wrh, (wrt - wrh.astype(F32)).astype(BF16)], axis=0)
    br = b_router[:, None]
    g2r = g2[None, :]
    tok = jnp.arange(tm)
    earlier = (tok[:, None] < tok[None, :]).astype(BF16)
    ts = tm * ROUTE_TILES_PER_STEP
    row = lambda n: pl.BlockSpec((ts, n), lambda i: (i, 0))
    col = lambda n: pl.BlockSpec((n, ts), lambda i: (0, i))
    full = lambda a: pl.BlockSpec(a.shape, lambda i: (0,) * a.ndim)
    return pl.pallas_call(
        _out_route_kernel,
        grid=(T // ts,),
        in_specs=[row(D_ATTN), row(D_MLSTM), row(D_MODEL), full(wo), full(g2r), full(wr), full(br),
                  full(earlier)],
        out_specs=[row(D_MODEL), row(D_MODEL), col(ne), col(ne),
                   pl.BlockSpec((ROUTE_TILES_PER_STEP * ne, LANES), lambda i: (i, 0)),
                   pl.BlockSpec((ne, LANES), lambda i: (0, 0))],
        out_shape=[jax.ShapeDtypeStruct((T, D_MODEL), F32), jax.ShapeDtypeStruct((T, D_MODEL), BF16),
                   jax.ShapeDtypeStruct((ne, T), jnp.int32), jax.ShapeDtypeStruct((ne, T), F32),
                   jax.ShapeDtypeStruct((T // tm * ne, LANES), F32), jax.ShapeDtypeStruct((ne, LANES), F32)],
        scratch_shapes=[pltpu.VMEM((ne, LANES), F32)],
        compiler_params=_vmem_limit(32),
        name="out_route",
    )(attn, ml, x2, wo, g2r, wr, br, earlier)


def _dispatch_kernel(runstart_ref, runlen_ref, lshift_ref, padstart_ref, npad_ref,
                     h_ref, rt_ref, xs_ref, cbufs, xbuf, zero_ref, stage_ref, sem, xsem, zsem):
    td = DISPATCH_ROWS
    ne = N_EXPERTS
    rb = RUN_ROWS
    ta = BF16_TILE_ROWS
    grp = RUN_GROUP_ROWS // rb
    tile = pl.program_id(0)
    cbuf = cbufs.at[tile % 2]

    def window(i):
        off = runstart_ref[i] & (ta - 1)
        return off, off + runlen_ref[i]

    def first_chunk(tile_idx, e, action):
        i = tile_idx * ne + e
        off, _ = window(i)
        dst = pl.multiple_of(runstart_ref[i] - off, ta)
        action(pltpu.make_async_copy(cbufs.at[tile_idx % 2, pl.ds(e * rb, rb)], xs_ref.at[pl.ds(dst, rb)], sem))

    @pl.when(tile == 0)
    def _():
        stage_ref[...] = jnp.zeros_like(stage_ref)

    hb = h_ref[...]
    riota = lax.broadcasted_iota(jnp.int32, (rb, td), 0)
    rowt = lax.broadcasted_iota(jnp.int32, (ta, D_MODEL), 0)

    def select(e_slice, shift):
        return jnp.where(riota == rt_ref[e_slice, :] - shift, 1.0, 0.0)

    for g in range(ne // grp):
        experts = range(g * grp, (g + 1) * grp)
        parts = [select(slice(e, e + 1), lshift_ref[tile * ne + e] - window(tile * ne + e)[0]) for e in experts]
        pt = jnp.concatenate(parts, axis=0).astype(BF16)
        cbuf[g * RUN_GROUP_ROWS:(g + 1) * RUN_GROUP_ROWS, :] = _dot(pt, hb).astype(BF16)

    for e in range(ne):
        i = tile * ne + e
        off, total = window(i)
        head = pl.ds(e * rb, ta)
        old = stage_ref[e]
        cbuf[head, :] = jnp.where(rowt < off, old, cbuf[head, :])
        last = e * rb + jnp.minimum(total // ta * ta, rb - ta)
        keep = jnp.logical_and(runlen_ref[i] > 0, total < rb)
        stage_ref[e] = jnp.where(keep, cbuf[pl.ds(pl.multiple_of(last, ta), ta), :], old)

    @pl.when(tile > 0)
    def _():
        for e in range(ne):
            first_chunk(tile - 1, e, lambda cp: cp.wait())

    for e in range(ne):
        first_chunk(tile, e, lambda cp: cp.start())

    def per_expert(e, carry):
        i = tile * ne + e
        off, total = window(i)

        def chunk(ch, c):
            pt = select(pl.ds(e, 1), lshift_ref[i] - off + ch * rb).astype(BF16)
            xbuf[...] = _dot(pt, hb).astype(BF16)
            dst = pl.multiple_of(runstart_ref[i] - off + ch * rb, ta)
            cp = pltpu.make_async_copy(xbuf, xs_ref.at[pl.ds(dst, rb)], xsem)
            cp.start()

            @pl.when(jnp.logical_and(ch == total // rb, total % rb != 0))
            def _():
                last = pl.multiple_of(total % rb // ta * ta, ta)
                stage_ref[e] = xbuf[pl.ds(last, ta), :]

            cp.wait()
            return c

        lax.fori_loop(1, (total + rb - 1) // rb, chunk, 0)
        return carry

    @pl.when(runlen_ref[pl.num_programs(0) * ne + tile] > 1)
    def _():
        lax.fori_loop(0, ne, per_expert, 0)

    @pl.when(tile == pl.num_programs(0) - 1)
    def _():
        for e in range(ne):
            first_chunk(tile, e, lambda cp: cp.wait())

    @pl.when(tile == pl.num_programs(0) - 1)
    def _():
        zero_ref[...] = jnp.zeros_like(zero_ref)

        def fill(wait):
            def per_run(e, carry):
                start = padstart_ref[e]
                head = (-start) & (ta - 1)
                off = pl.multiple_of(start + head, ta)
                left = npad_ref[e] - head

                def whole(b, c):
                    dst = xs_ref.at[pl.ds(pl.multiple_of(off + b * td, ta), td)]
                    cp = pltpu.make_async_copy(zero_ref, dst, zsem)
                    cp.wait() if wait else cp.start()
                    return c

                n_whole = left // td
                lax.fori_loop(0, n_whole, whole, 0)
                off = pl.multiple_of(off + n_whole * td, ta)
                p = td // 2
                while p >= ta:
                    take = (left & p) != 0

                    @pl.when(take)
                    def _(off=off, p=p):
                        cp = pltpu.make_async_copy(zero_ref.at[pl.ds(0, p)], xs_ref.at[pl.ds(off, p)], zsem)
                        cp.wait() if wait else cp.start()

                    off = pl.multiple_of(off + jnp.where(take, p, 0), ta)
                    p //= 2
                return carry

            lax.fori_loop(0, N_EXPERTS + 1, per_run, 0)

        fill(wait=False)
        fill(wait=True)


def _dispatch(h2, rt, runstart, runlen, lshift, padstart, npad, n_slots):
    T = h2.shape[0]
    td = DISPATCH_ROWS
    ne = N_EXPERTS
    assert td == ROUTE_ROWS
    return pl.pallas_call(
        _dispatch_kernel,
        grid_spec=pltpu.PrefetchScalarGridSpec(
            num_scalar_prefetch=5,
            grid=(T // td,),
            in_specs=[pl.BlockSpec((td, D_MODEL), lambda i, *_: (i, 0)),
                      pl.BlockSpec((ne, td), lambda i, *_: (0, i))],
            out_specs=pl.BlockSpec(memory_space=pl.ANY),
            scratch_shapes=[pltpu.VMEM((2, ne * RUN_ROWS, D_MODEL), BF16), pltpu.VMEM((RUN_ROWS, D_MODEL), BF16),
                            pltpu.VMEM((td, D_MODEL), BF16), pltpu.VMEM((ne, BF16_TILE_ROWS, D_MODEL), BF16),
                            pltpu.SemaphoreType.DMA(()), pltpu.SemaphoreType.DMA(()),
                            pltpu.SemaphoreType.DMA(())],
        ),
        out_shape=jax.ShapeDtypeStruct((n_slots, D_MODEL), BF16),
        compiler_params=_vmem_limit(32),
        name="dispatch",
    )(runstart, runlen, lshift, padstart, npad, h2, rt)


def _expert_kernel(bexp_ref, first_ref, next_ref, nused_ref, xs_ref, wgu_hbm, bgu_ref, wd_hbm, bd_ref, y_ref,
                   wgu_st, wd_st, wgu_bf, wd_bf, wsem):
    tm = EXPERT_ROWS
    blk = pl.program_id(0)
    active = blk * tm < nused_ref[0]

    def weight_copies(e):
        return (pltpu.make_async_copy(wgu_hbm.at[e], wgu_st, wsem.at[0]),
                pltpu.make_async_copy(wd_hbm.at[e], wd_st, wsem.at[1]))

    @pl.when(blk == 0)
    def _():
        for cp in weight_copies(bexp_ref[0]):
            cp.start()

    @pl.when(jnp.logical_and(active, first_ref[blk] == 1))
    def _():
        for cp in weight_copies(bexp_ref[blk]):
            cp.wait()
        rc = CAST_ROWS

        def cast(i, carry):
            r = pl.ds(pl.multiple_of(i * rc, rc), rc)
            wgu_bf[r, :] = wgu_st[r, :].astype(BF16)
            wd_bf[r, :] = wd_st[r, :].astype(BF16)
            return carry

        lax.fori_loop(0, D_MODEL // rc, cast, 0)

        @pl.when(next_ref[blk] >= 0)
        def _():
            for cp in weight_copies(next_ref[blk]):
                cp.start()

    @pl.when(active)
    def _():
        xb = xs_ref[...]
        acc = jnp.zeros((tm, D_MODEL), F32)
        for f in range(D_FF // FF_CHUNK):
            lo = f * FF_CHUNK
            g = _dot(xb, wgu_bf[:, lo:lo + FF_CHUNK]) + bgu_ref[:, lo:lo + FF_CHUNK]
            u = _dot(xb, wgu_bf[:, D_FF + lo:D_FF + lo + FF_CHUNK]) + bgu_ref[:, D_FF + lo:D_FF + lo + FF_CHUNK]
            g = jnp.minimum(g, SWIGLU_LIMIT)
            u = jnp.clip(u, -SWIGLU_LIMIT, SWIGLU_LIMIT)
            act = g * jax.nn.sigmoid(SWIGLU_ALPHA * g) * (u + 1.0)
            acc = acc + _dot(act.astype(BF16), wd_bf[lo:lo + FF_CHUNK, :])
        y_ref[...] = (acc + bd_ref[...]).astype(y_ref.dtype)

    @pl.when(jnp.logical_not(active))
    def _():
        y_ref[...] = jnp.zeros_like(y_ref)


def _experts(xs, block_expert, first_block, next_expert, n_used, w_gate_up, b_gate_up, w_down, b_down):
    n_blocks = block_expert.shape[0]
    tm = EXPERT_ROWS
    assert D_FF == D_MODEL and xs.shape[0] == (n_blocks - 1) * tm
    bgu = b_gate_up[:, None, :]
    bd = b_down[:, None, :]
    last = n_blocks - 2
    return pl.pallas_call(
        _expert_kernel,
        grid_spec=pltpu.PrefetchScalarGridSpec(
            num_scalar_prefetch=4,
            grid=(n_blocks,),
            in_specs=[
                pl.BlockSpec((tm, D_MODEL), lambda i, be, fb, nx, nu: (jnp.minimum(i, last), 0)),
                pl.BlockSpec(memory_space=pl.ANY),
                pl.BlockSpec((None, 1, 2 * D_FF), lambda i, be, fb, nx, nu: (be[i], 0, 0)),
                pl.BlockSpec(memory_space=pl.ANY),
                pl.BlockSpec((None, 1, D_MODEL), lambda i, be, fb, nx, nu: (be[i], 0, 0)),
            ],
            out_specs=pl.BlockSpec((tm, D_MODEL), lambda i, be, fb, nx, nu: (i, 0)),
            scratch_shapes=[pltpu.VMEM((D_MODEL, 2 * D_FF), F32), pltpu.VMEM((D_FF, D_MODEL), F32),
                            pltpu.VMEM((D_MODEL, 2 * D_FF), BF16), pltpu.VMEM((D_FF, D_MODEL), BF16),
                            pltpu.SemaphoreType.DMA((2,))],
        ),
        out_shape=jax.ShapeDtypeStruct((n_blocks * tm, D_MODEL), BF16),
        compiler_params=_vmem_limit(48),
        name="experts",
    )(block_expert, first_block, next_expert, n_used, xs, w_gate_up, bgu, w_down, bd)


def _combine_kernel(fetch_ref, shift_ref, nch_ref, x1_ref, rt_ref, gt_ref, y_ref, o_ref,
                    buf, xbuf, sem, xsem):
    tc = ROUTE_ROWS
    ne = N_EXPERTS
    rb = RUN_ROWS
    grp = RUN_GROUP_ROWS // rb
    tile = pl.program_id(0)
    base = tile * ne
    slot = tile % 2

    def run_copy(tile_base, buf_slot, e):
        start = pl.multiple_of(fetch_ref[tile_base + e], BF16_TILE_ROWS)
        return pltpu.make_async_copy(y_ref.at[pl.ds(start, rb)], buf.at[buf_slot, pl.ds(e * rb, rb)],
                                     sem.at[buf_slot, e // grp])

    @pl.when(tile == 0)
    def _():
        for e in range(ne):
            run_copy(base, slot, e).start()

    @pl.when(tile + 1 < pl.num_programs(0))
    def _():
        for e in range(ne):
            run_copy(base + ne, 1 - slot, e).start()

    riota = lax.broadcasted_iota(jnp.int32, (rb, tc), 0)

    def select(e_slice, shift):
        return jnp.where(riota == rt_ref[e_slice, :] - shift, gt_ref[e_slice, :], 0.0)

    parts = []
    for e in range(ne):
        run_copy(base, slot, e).wait()
        parts.append(select(slice(e, e + 1), shift_ref[base + e]).astype(BF16))
    o_ref[...] = x1_ref[...] + _dot_tn(jnp.concatenate(parts, axis=0), buf[slot])

    def per_expert(e, carry):
        def chunk(ch, c):
            start = pl.multiple_of(fetch_ref[base + e] + ch * rb, BF16_TILE_ROWS)
            cp = pltpu.make_async_copy(y_ref.at[pl.ds(start, rb)], xbuf, xsem)
            cp.start()
            cp.wait()
            pt = select(pl.ds(e, 1), shift_ref[base + e] + ch * rb).astype(BF16)
            o_ref[...] += _dot_tn(pt, xbuf[...])
            return c

        lax.fori_loop(1, nch_ref[base + e], chunk, 0)
        return carry

    @pl.when(nch_ref[pl.num_programs(0) * ne + tile] > 1)
    def _():
        lax.fori_loop(0, ne, per_expert, 0)


def _combine(fetch, shift, nch, x1, rt, gt, y):
    T = x1.shape[0]
    tc = ROUTE_ROWS
    ne = N_EXPERTS
    return pl.pallas_call(
        _combine_kernel,
        grid_spec=pltpu.PrefetchScalarGridSpec(
            num_scalar_prefetch=3,
            grid=(T // tc,),
            in_specs=[pl.BlockSpec((tc, D_MODEL), lambda i, f, s, n: (i, 0)),
                      pl.BlockSpec((ne, tc), lambda i, f, s, n: (0, i)),
                      pl.BlockSpec((ne, tc), lambda i, f, s, n: (0, i)),
                      pl.BlockSpec(memory_space=pl.ANY)],
            out_specs=pl.BlockSpec((tc, D_MODEL), lambda i, f, s, n: (i, 0)),
            scratch_shapes=[pltpu.VMEM((2, ne * RUN_ROWS, D_MODEL), BF16), pltpu.VMEM((RUN_ROWS, D_MODEL), BF16),
                            pltpu.SemaphoreType.DMA((2, ne * RUN_ROWS // RUN_GROUP_ROWS)),
                            pltpu.SemaphoreType.DMA(())],
        ),
        out_shape=jax.ShapeDtypeStruct((T, D_MODEL), F32),
        compiler_params=_vmem_limit(32),
        name="combine",
    )(fetch, shift, nch, x1, rt, gt, y)


def _routing_tables(base, counts, T):
    tm = EXPERT_ROWS
    ne = N_EXPERTS
    n_tiles = T // ROUTE_ROWS
    n_blocks = -(-(T * TOP_K + ne * RUN_ROWS) // tm) + ne
    n_slots = n_blocks * tm
    sizes = counts[:, 0].astype(jnp.int32)
    psizes = (sizes + RUN_ROWS + tm - 1) // tm * tm
    pends = jnp.cumsum(psizes)
    pstarts = pends - psizes
    n_used = pends[-1:]
    padstart = jnp.concatenate([pstarts + sizes, n_used])
    npad = jnp.concatenate([psizes - sizes, n_slots - n_used])
    nb = n_blocks + 1
    blk = jnp.arange(nb, dtype=jnp.int32)
    block_expert = jnp.minimum(jnp.sum(pends[None, :] <= (blk * tm)[:, None], axis=1), ne - 1).astype(jnp.int32)
    first_block = jnp.concatenate([jnp.ones((1,), jnp.int32),
                                   (block_expert[1:] != block_expert[:-1]).astype(jnp.int32)])
    starts_group = jnp.logical_and(first_block == 1, blk * tm < n_used)
    pos = jnp.where(starts_group, blk, nb)
    later = jnp.concatenate([lax.cummin(pos[::-1])[::-1][1:], jnp.full((1,), nb, jnp.int32)])
    next_expert = jnp.where(later < nb, block_expert[jnp.minimum(later, nb - 1)], -1).astype(jnp.int32)
    tile_base = base[:, 0].astype(jnp.int32).reshape(n_tiles, ne)
    run_len = jnp.concatenate([tile_base[1:], sizes[None, :]], axis=0) - tile_base
    run_start = pstarts[None, :] + tile_base
    fetch = run_start // BF16_TILE_ROWS * BF16_TILE_ROWS
    shift = fetch - pstarts[None, :]
    nch = jnp.where(run_len > 0, (run_start - fetch + run_len + RUN_ROWS - 1) // RUN_ROWS, 0)
    wch = (run_start % BF16_TILE_ROWS + run_len + RUN_ROWS - 1) // RUN_ROWS
    flat = lambda a: a.reshape(n_tiles * ne).astype(jnp.int32)
    with_tile_max = lambda a, per_run: jnp.concatenate([flat(a), jnp.max(per_run, axis=1).astype(jnp.int32)])
    return (flat(run_start), with_tile_max(run_len, wch), flat(tile_base), padstart, npad, n_slots,
            block_expert, first_block, next_expert, n_used, flat(fetch), flat(shift), with_tile_max(nch, nch))


def kernel(x, norm1_g, w_in, q_norm_g, k_norm_g, conv_w, conv_b, w_mq, w_mk, b_igate, b_fgate, mh_norm_g,
           w_out, norm2_g, w_router, b_router, w_gate_up, b_gate_up, w_down, b_down):
    B, S, D = x.shape
    T = B * S
    assert D == D_MODEL and S % ATTN_ROWS == 0 and norm1_g.shape[0] == 1
    x2 = x.reshape(T, D)
    q, k, v, xm, vm, om, gates = _in_proj(x2, norm1_g[0], w_in[0], q_norm_g[0], k_norm_g[0])
    attn = _attention(q, k, v, B, S)
    ml = _mlstm(xm, vm, om, gates, conv_w[0], conv_b[0], w_mq[0], w_mk[0], b_igate[0], b_fgate[0],
                mh_norm_g[0], B, S)
    x1, h2, rt, gt, base, counts = _out_route(attn, ml, x2, w_out[0], norm2_g[0], w_router[0], b_router[0])
    (run_start, run_len, tile_base, padstart, npad, n_slots, block_expert, first_block, next_expert, n_used,
     fetch, shift, nch) = _routing_tables(base, counts, T)
    xs = _dispatch(h2, rt, run_start, run_len, tile_base, padstart, npad, n_slots)
    y = _experts(xs, block_expert, first_block, next_expert, n_used, w_gate_up[0], b_gate_up[0], w_down[0],
                 b_down[0])
    out = _combine(fetch, shift, nch, x1, rt, gt, y)
    return out.reshape(B, S, D)
```

```python
import jax
import jax.numpy as jnp
from jax import lax
from jax.experimental import pallas as pl
from jax.experimental.pallas import tpu as pltpu

F32 = jnp.float32
BF16 = jnp.bfloat16
NEG_INF = float("-inf")

D_MODEL = 1024
D_ATTN = 512
HEAD_DIM_ATTN = 64
N_HEADS_ATTN = 8
D_MLSTM = 512
HEAD_DIM_MLSTM = 128
N_HEADS_MLSTM = 4
CONV_K = 4
N_EXPERTS = 32
TOP_K = 4
D_FF = 1024
SWIGLU_LIMIT = 7.0
SWIGLU_ALPHA = 1.702
EPS = 1e-6
LOG2_E = 1.4426950408889634
DILATIONS = (1, 4, 16)
BRANCH_SPAN = 128

LANES = 128
SUBLANES = 8

IN_PROJ_ROWS = 512
ATTN_ROWS = 2048
MLSTM_CHUNK = 256
ROUTE_ROWS = 256
ROUTE_TILES_PER_STEP = 2
EXPERT_ROWS = 256
FF_CHUNK = 1024
CAST_ROWS = 128
DISPATCH_ROWS = 256
HEADNORM_LANES = 256
BF16_TILE_ROWS = 16
RUN_ROWS = 64
RUN_GROUP_ROWS = 256
NOT_ROUTED = -1.0e9


def _vmem_limit(mib):
    return pltpu.CompilerParams(vmem_limit_bytes=mib * 1024 * 1024)


def _split3(a):
    p1 = a.astype(BF16)
    r1 = a - p1.astype(F32)
    p2 = r1.astype(BF16)
    r2 = r1 - p2.astype(F32)
    return p1, p2, r2.astype(BF16)


def _dot(a, b):
    return jnp.dot(a, b, preferred_element_type=F32)


def _dot_nt(a, b):
    return lax.dot_general(a, b, (((1,), (1,)), ((), ())), preferred_element_type=F32)


def _dot_tn(a, b):
    return lax.dot_general(a, b, (((0,), (0,)), ((), ())), preferred_element_type=F32)


def _in_proj_kernel(x_ref, g1_ref, wqkv_ref, wm_ref, wgc_ref, qg_ref, kg_ref, hsum_ref,
                    q_ref, k_ref, v_ref, xm_ref, vm_ref, om_ref, gates_ref):
    x = x_ref[...]
    h = x * lax.rsqrt(jnp.mean(x * x, axis=-1, keepdims=True) + EPS) * g1_ref[...]
    hb = h.astype(BF16)
    hl = (h - hb.astype(F32)).astype(BF16)
    gw = HEADNORM_LANES

    def head_norm(z, g_ref):
        parts = []
        for c in range(D_ATTN // gw):
            zc = z[:, c * gw:(c + 1) * gw]
            ms = _dot((zc * zc).astype(BF16), hsum_ref[...]) * (1.0 / HEAD_DIM_ATTN)
            parts.append(zc * lax.rsqrt(ms + EPS))
        return jnp.concatenate(parts, axis=-1) * g_ref[...]

    zq = _dot(hb, wqkv_ref[:, 0:D_ATTN])
    q_ref[...] = head_norm(zq, qg_ref) * (HEAD_DIM_ATTN ** -0.5 * LOG2_E)
    zk = _dot(hb, wqkv_ref[:, D_ATTN:2 * D_ATTN])
    k_ref[...] = head_norm(zk, kg_ref)
    v_ref[...] = _dot(hb, wqkv_ref[:, 2 * D_ATTN:3 * D_ATTN])
    xm_ref[...] = _dot(hb, wm_ref[:, 0:D_MLSTM])
    vm_ref[...] = _dot(hb, wm_ref[:, D_MLSTM:2 * D_MLSTM])
    om_ref[...] = _dot(hb, wm_ref[:, 2 * D_MLSTM:3 * D_MLSTM])
    ng = 2 * N_HEADS_MLSTM
    rows = hb.shape[0]
    both = _dot(jnp.concatenate([hb, hl], axis=0), wgc_ref[...])
    gsum = both[:rows] + both[rows:]
    gsum = gsum + pltpu.roll(gsum, LANES - ng, axis=1)
    lane = lax.broadcasted_iota(jnp.int32, gsum.shape, 1)
    gates_ref[...] = jnp.where(lane < ng, gsum, 0.0)


def _in_proj(x2, g1, w_in, q_g, k_g):
    T = x2.shape[0]
    tm = IN_PROJ_ROWS
    n_qkv = 3 * D_ATTN
    n_m = 3 * D_MLSTM
    ng = 2 * N_HEADS_MLSTM
    wqkv = w_in[:, :n_qkv].astype(BF16)
    wm = w_in[:, n_qkv:n_qkv + n_m].astype(BF16)
    wg = w_in[:, n_qkv + n_m:]
    wgh = wg.astype(BF16)
    wgl = (wg - wgh.astype(F32)).astype(BF16)
    wgc = jnp.pad(jnp.concatenate([wgh, wgl], axis=1), ((0, 0), (0, LANES - 2 * ng)))
    lane = jnp.arange(HEADNORM_LANES)
    hsum = (lane[:, None] // HEAD_DIM_ATTN == lane[None, :] // HEAD_DIM_ATTN).astype(BF16)
    qg = jnp.tile(q_g, N_HEADS_ATTN)[None, :]
    kg = jnp.tile(k_g, N_HEADS_ATTN)[None, :]
    row = lambda n: pl.BlockSpec((tm, n), lambda i: (i, 0))
    full = lambda a: pl.BlockSpec(a.shape, lambda i: (0,) * a.ndim)
    outs = [jax.ShapeDtypeStruct((T, D_ATTN), F32)] * 3 + [jax.ShapeDtypeStruct((T, D_MLSTM), F32)] * 3
    outs.append(jax.ShapeDtypeStruct((T, LANES), F32))
    g1r = g1[None, :]
    return pl.pallas_call(
        _in_proj_kernel,
        grid=(T // tm,),
        in_specs=[row(D_MODEL), full(g1r), full(wqkv), full(wm), full(wgc), full(qg), full(kg), full(hsum)],
        out_specs=[row(D_ATTN)] * 3 + [row(D_MLSTM)] * 3 + [row(LANES)],
        out_shape=outs,
        compiler_params=_vmem_limit(48),
        name="in_proj",
    )(x2, g1r, wqkv, wm, wgc, qg, kg, hsum)


def _attn_tiles():
    tiles = []
    for br, d in enumerate(DILATIONS):
        group = d * BRANCH_SPAN
        for u in range(ATTN_ROWS // group):
            for c in range(d):
                tiles.append((br, d, u * group + c))
    return tiles


def _attn_kernel(slope_ref, q_ref, kp_ref, kc_ref, vp_ref, vc_ref, o_ref,
                 kwin, vwin, bias_scr, m_scr, l_scr, acc_scr):
    W = ATTN_ROWS
    n = BRANCH_SPAN
    step = pl.program_id(2)
    kwin[0:W, :] = kp_ref[...]
    kwin[W:2 * W, :] = kc_ref[...]
    vwin[0:W, :] = vp_ref[...]
    vwin[W:2 * W, :] = vc_ref[...]

    lane = lax.broadcasted_iota(jnp.int32, (n, LANES), 1)
    first_head = lane < HEAD_DIM_ATTN

    @pl.when(step == 0)
    def _():
        row = lax.broadcasted_iota(jnp.int32, (n, 2 * n), 0)
        col = lax.broadcasted_iota(jnp.int32, (n, 2 * n), 1)
        j = n + row - col
        valid = jnp.logical_and(j >= 0, j <= n)
        jf = j.astype(F32)
        for br, d in enumerate(DILATIONS):
            for hh in range(2):
                slope = slope_ref[:, hh * HEAD_DIM_ATTN:hh * HEAD_DIM_ATTN + 1]
                b = jnp.where(valid, -(slope * (float(d) * LOG2_E)) * jf, NEG_INF)
                bias_scr[br, hh, 0] = b
                bias_scr[br, hh, 1] = jnp.where(col >= n, b, NEG_INF)

    for br, d, qs in _attn_tiles():
        lo = W + qs - n * d
        q = q_ref[pl.ds(qs, n, stride=d), :]
        kk = jnp.concatenate([kwin[pl.ds(lo, n, stride=d), :], kwin[pl.ds(W + qs, n, stride=d), :]],
                             axis=0).astype(BF16)
        vv = jnp.concatenate([vwin[pl.ds(lo, n, stride=d), :], vwin[pl.ds(W + qs, n, stride=d), :]],
                             axis=0).astype(BF16)
        if lo < W:
            sel = jnp.where(step == 0, 1, 0)
        else:
            sel = 0
        qb = q.astype(BF16)
        zero = jnp.zeros_like(qb)
        q2 = jnp.concatenate([jnp.where(first_head, qb, zero), jnp.where(first_head, zero, qb)], axis=0)
        s2 = _dot_nt(q2, kk)
        ms, ls, ps = [], [], []
        for hh in range(2):
            s = s2[hh * n:(hh + 1) * n] + bias_scr[br, hh, sel]
            m = jnp.max(s, axis=-1, keepdims=True)
            p = jnp.exp2(s - m)
            ms.append(m)
            ls.append(jnp.sum(p, axis=-1, keepdims=True))
            ps.append(p.astype(BF16))
        pv = _dot(jnp.concatenate(ps, axis=0), vv)
        rows = pl.ds(qs, n, stride=d)
        m_scr[br, rows, :] = jnp.where(first_head, ms[0], ms[1])
        l_scr[br, rows, :] = jnp.where(first_head, ls[0], ls[1])
        acc_scr[br, rows, :] = jnp.where(first_head, pv[:n], pv[n:])

    chunk = 256

    def combine(i, carry):
        r = pl.ds(pl.multiple_of(i * chunk, chunk), chunk)
        m0, m1, m2 = m_scr[0, r, :], m_scr[1, r, :], m_scr[2, r, :]
        mm = jnp.maximum(jnp.maximum(m0, m1), m2)
        w0, w1, w2 = jnp.exp2(m0 - mm), jnp.exp2(m1 - mm), jnp.exp2(m2 - mm)
        num = w0 * acc_scr[0, r, :] + w1 * acc_scr[1, r, :] + w2 * acc_scr[2, r, :]
        den = w0 * l_scr[0, r, :] + w1 * l_scr[1, r, :] + w2 * l_scr[2, r, :]
        o_ref[r, :] = num / den
        return carry

    lax.fori_loop(0, W // chunk, combine, 0)


def _attention(q, k, v, B, S):
    W = ATTN_ROWS
    nb = S // W
    n_pairs = D_ATTN // LANES
    slopes = jnp.exp2(-8.0 * jnp.arange(1, N_HEADS_ATTN + 1, dtype=F32) / N_HEADS_ATTN)
    slope_l = jnp.repeat(slopes, HEAD_DIM_ATTN).reshape(n_pairs, 1, LANES)
    cur = pl.BlockSpec((W, LANES), lambda b, p, i: (b * nb + i, p))
    prev = pl.BlockSpec((W, LANES), lambda b, p, i: (b * nb + jnp.maximum(i - 1, 0), p))
    n = BRANCH_SPAN
    nbr = len(DILATIONS)
    return pl.pallas_call(
        _attn_kernel,
        grid=(B, n_pairs, nb),
        in_specs=[pl.BlockSpec((None, 1, LANES), lambda b, p, i: (p, 0, 0)), cur, prev, cur, prev, cur],
        out_specs=cur,
        out_shape=jax.ShapeDtypeStruct((B * S, D_ATTN), F32),
        scratch_shapes=[
            pltpu.VMEM((2 * W, LANES), F32), pltpu.VMEM((2 * W, LANES), F32),
            pltpu.VMEM((nbr, 2, 2, n, 2 * n), F32),
            pltpu.VMEM((nbr, W, LANES), F32), pltpu.VMEM((nbr, W, LANES), F32),
            pltpu.VMEM((nbr, W, LANES), F32),
        ],
        compiler_params=_vmem_limit(48),
        name="attention",
    )(slope_l, q, k, k, v, v)


def _mlstm_kernel(xm_ref, vm_ref, om_ref, gcol_ref, grow_ref, cw_ref, cb_ref, wq_ref, wk_ref,
                  bcol_ref, brow_ref, g_ref, o_ref, xs_ref, xtail_ref, c_ref, n_ref, m_ref):
    L = MLSTM_CHUNK
    dh = HEAD_DIM_MLSTM
    nh = N_HEADS_MLSTM

    @pl.when(pl.program_id(1) == 0)
    def _():
        xtail_ref[...] = jnp.zeros_like(xtail_ref)
        c_ref[...] = jnp.zeros_like(c_ref)
        n_ref[...] = jnp.zeros_like(n_ref)
        m_ref[...] = jnp.zeros_like(m_ref)

    x = xm_ref[...]
    xs_ref[0:SUBLANES, :] = xtail_ref[...]
    xs_ref[SUBLANES:SUBLANES + L, :] = x
    xtail_ref[...] = x[L - SUBLANES:L, :]
    xc = cb_ref[...] + x * cw_ref[CONV_K - 1:CONV_K, :]
    for back in range(1, CONV_K):
        xc = xc + xs_ref[SUBLANES - back:SUBLANES - back + L, :] * cw_ref[CONV_K - 1 - back:CONV_K - back, :]
    xc = xc * jax.nn.sigmoid(xc)

    gc = gcol_ref[...] + bcol_ref[...]
    gr = grow_ref[...] + brow_ref[...]
    ri = lax.broadcasted_iota(jnp.int32, (L, L), 0)
    ci = lax.broadcasted_iota(jnp.int32, (L, L), 1)
    causal = ri >= ci
    tril = jnp.where(causal, 1.0, 0.0).astype(BF16)
    triu = jnp.where(ri <= ci, 1.0, 0.0).astype(BF16)
    c1, c2, c3 = _split3(jax.nn.log_sigmoid(gc))
    bc = _dot(tril, c1) + _dot(tril, c2) + _dot(tril, c3)
    r1, r2, r3 = _split3(jax.nn.log_sigmoid(gr))
    brw = _dot(r1, triu) + _dot(r2, triu) + _dot(r3, triu)

    heads = range(nh)
    hs = [slice(h * dh, (h + 1) * dh) for h in heads]
    xh = [xc[:, hs[h]].astype(BF16) for h in heads]
    qf = [_dot(xh[h], wq_ref[h]) for h in heads]
    kf = [_dot(xh[h], wk_ref[h]) * (dh ** -0.5) for h in heads]
    qb = [qf[h].astype(BF16) for h in heads]
    vb = [vm_ref[:, hs[h]].astype(BF16) for h in heads]

    b_col = [bc[:, nh + h:nh + h + 1] for h in heads]
    a_col = [gc[:, h:h + 1] - b_col[h] for h in heads]
    a_row = [gr[h:h + 1, :] - brw[nh + h:nh + h + 1, :] for h in heads]
    m_prev = [m_ref[h:h + 1, 0:1] for h in heads]
    log_d = [jnp.where(causal, b_col[h] + a_row[h], NEG_INF) for h in heads]
    log_inter = [b_col[h] + m_prev[h] for h in heads]
    m_t = [jnp.maximum(log_inter[h], jnp.max(log_d[h], axis=-1, keepdims=True)) for h in heads]
    d = [jnp.exp(log_d[h] - m_t[h]) for h in heads]
    inter = [jnp.exp(log_inter[h] - m_t[h]) for h in heads]

    s = [_dot_nt(qb[h], kf[h].astype(BF16)) * d[h] for h in heads]
    c_old = [c_ref[h] for h in heads]
    n_old = [n_ref[h:h + 1, :] for h in heads]
    num = [_dot(s[h].astype(BF16), vb[h]) + inter[h] * _dot(qb[h], c_old[h].astype(BF16)) for h in heads]
    den = [jnp.sum(s[h], axis=-1, keepdims=True) + inter[h] * jnp.sum(qf[h] * n_old[h], axis=-1, keepdims=True)
           for h in heads]
    hh = [num[h] / jnp.maximum(jnp.abs(den[h]), jnp.exp(-m_t[h])) for h in heads]

    for h in heads:
        b_last = b_col[h][L - 1:L, :]
        log_w = b_last + a_col[h]
        m_new = jnp.maximum(b_last + m_prev[h], jnp.max(log_w, axis=0, keepdims=True))
        kw = kf[h] * jnp.exp(log_w - m_new)
        decay = jnp.exp(b_last + m_prev[h] - m_new)
        c_ref[h] = decay * c_old[h] + _dot_tn(kw.astype(BF16), vb[h])
        n_ref[h:h + 1, :] = decay * n_old[h] + jnp.sum(kw, axis=0, keepdims=True)
        m_ref[h:h + 1, :] = jnp.broadcast_to(m_new, (1, LANES))

    for h in heads:
        hn = hh[h] * lax.rsqrt(jnp.mean(hh[h] * hh[h], axis=-1, keepdims=True) + EPS) * g_ref[:, hs[h]]
        o_ref[:, hs[h]] = jax.nn.sigmoid(om_ref[:, hs[h]]) * hn


def _mlstm(xm, vm, om, gates, conv_w, conv_b, w_mq, w_mk, b_i, b_f, mh_g, B, S):
    L = MLSTM_CHUNK
    nc = S // L
    nh = N_HEADS_MLSTM
    ng = 2 * nh
    grow = gates[:, :ng].reshape(B, S, ng).transpose(0, 2, 1)
    bias = jnp.concatenate([b_i, b_f])
    bcol = jnp.pad(bias, (0, LANES - ng))[None, :]
    brow = bias[:, None]
    wq = w_mq.astype(BF16)
    wk = w_mk.astype(BF16)
    cb = conv_b[None, :]
    g = mh_g.reshape(1, D_MLSTM)
    rows = pl.BlockSpec((L, D_MLSTM), lambda b, c: (b * nc + c, 0))
    full = lambda a: pl.BlockSpec(a.shape, lambda b, c: (0,) * a.ndim)
    return pl.pallas_call(
        _mlstm_kernel,
        grid=(B, nc),
        in_specs=[rows, rows, rows, pl.BlockSpec((L, LANES), lambda b, c: (b * nc + c, 0)),
                  pl.BlockSpec((None, ng, L), lambda b, c: (b, 0, c)),
                  full(conv_w), full(cb), full(wq), full(wk), full(bcol), full(brow), full(g)],
        out_specs=rows,
        out_shape=jax.ShapeDtypeStruct((B * S, D_MLSTM), F32),
        scratch_shapes=[
            pltpu.VMEM((L + SUBLANES, D_MLSTM), F32), pltpu.VMEM((SUBLANES, D_MLSTM), F32),
            pltpu.VMEM((nh, HEAD_DIM_MLSTM, HEAD_DIM_MLSTM), F32),
            pltpu.VMEM((SUBLANES, LANES), F32), pltpu.VMEM((SUBLANES, LANES), F32),
        ],
        compiler_params=_vmem_limit(32),
        name="mlstm",
    )(xm, vm, om, gates, grow, conv_w, cb, wq, wk, bcol, brow, g)


def _out_route_kernel(attn_ref, ml_ref, x_ref, wo_ref, g2_ref, wr_ref, br_ref, earlier_ref,
                      x1_ref, h2_ref, rt_ref, gt_ref, base_ref, cnt_ref, carry_ref):
    @pl.when(pl.program_id(0) == 0)
    def _():
        carry_ref[...] = jnp.zeros_like(carry_ref)

    carry = carry_ref[...]
    for j in range(ROUTE_TILES_PER_STEP):
        carry = _route_tile(j, carry, attn_ref, ml_ref, x_ref, wo_ref, g2_ref, wr_ref, br_ref,
                            earlier_ref, x1_ref, h2_ref, rt_ref, gt_ref, base_ref)
    carry_ref[...] = carry
    cnt_ref[...] = carry


def _route_tile(j, carry, attn_ref, ml_ref, x_ref, wo_ref, g2_ref, wr_ref, br_ref, earlier_ref,
                x1_ref, h2_ref, rt_ref, gt_ref, base_ref):
    tm = ROUTE_ROWS
    ne = N_EXPERTS
    rows = slice(j * tm, (j + 1) * tm)

    x1 = (x_ref[rows, :] + _dot(attn_ref[rows, :].astype(BF16), wo_ref[0:D_ATTN, :])
          + _dot(ml_ref[rows, :].astype(BF16), wo_ref[D_ATTN:D_ATTN + D_MLSTM, :]))
    x1_ref[rows, :] = x1
    h2 = x1 * lax.rsqrt(jnp.mean(x1 * x1, axis=-1, keepdims=True) + EPS) * g2_ref[...]
    hb = h2.astype(BF16)
    h2_ref[rows, :] = hb
    hl = (h2 - hb.astype(F32)).astype(BF16)
    hi_pass = _dot_nt(wr_ref[...], hb)
    lt = (hi_pass[:ne] + hi_pass[ne:] + _dot_nt(wr_ref[0:ne, :], hl)) + br_ref[...]

    eidx = lax.broadcasted_iota(jnp.int32, (ne, tm), 0)
    vals, hots = [], []
    work = lt
    for _ in range(TOP_K):
        mx = jnp.max(work, axis=0, keepdims=True)
        idx = jnp.min(jnp.where(work == mx, eidx, ne), axis=0, keepdims=True)
        hot = eidx == idx
        vals.append(mx)
        hots.append(hot)
        work = jnp.where(hot, NEG_INF, work)
    ex = [jnp.exp(v - vals[0]) for v in vals]
    tot = ex[0] + ex[1] + ex[2] + ex[3]

    onehot = jnp.zeros((ne, tm), F32)
    gt = jnp.zeros((ne, tm), F32)
    for kk in range(TOP_K):
        onehot = onehot + jnp.where(hots[kk], 1.0, 0.0)
        gt = gt + jnp.where(hots[kk], ex[kk] / tot, 0.0)
    before = _dot(onehot.astype(BF16), earlier_ref[...]) + carry[:, 0:1]
    base_ref[j * ne:(j + 1) * ne, :] = carry
    rt_ref[:, rows] = jnp.where(onehot > 0.5, before, NOT_ROUTED).astype(jnp.int32)
    gt_ref[:, rows] = gt
    return carry + jnp.sum(onehot, axis=1, keepdims=True)


def _out_route(attn, ml, x2, w_out, g2, w_router, b_router):
    T = x2.shape[0]
    tm = ROUTE_ROWS
    ne = N_EXPERTS
    wo = w_out.astype(BF16)
    wrt = w_router.T
    wrh = wrt.astype(BF16)
    wr = jnp.concatenate([wrh, (wrt - wrh.astype(F32)).astype(BF16)], axis=0)
    br = b_router[:, None]
    g2r = g2[None, :]
    tok = jnp.arange(tm)
    earlier = (tok[:, None] < tok[None, :]).astype(BF16)
    ts = tm * ROUTE_TILES_PER_STEP
    row = lambda n: pl.BlockSpec((ts, n), lambda i: (i, 0))
    col = lambda n: pl.BlockSpec((n, ts), lambda i: (0, i))
    full = lambda a: pl.BlockSpec(a.shape, lambda i: (0,) * a.ndim)
    return pl.pallas_call(
        _out_route_kernel,
        grid=(T // ts,),
        in_specs=[row(D_ATTN), row(D_MLSTM), row(D_MODEL), full(wo), full(g2r), full(wr), full(br),
                  full(earlier)],
        out_specs=[row(D_MODEL), row(D_MODEL), col(ne), col(ne),
                   pl.BlockSpec((ROUTE_TILES_PER_STEP * ne, LANES), lambda i: (i, 0)),
                   pl.BlockSpec((ne, LANES), lambda i: (0, 0))],
        out_shape=[jax.ShapeDtypeStruct((T, D_MODEL), F32), jax.ShapeDtypeStruct((T, D_MODEL), BF16),
                   jax.ShapeDtypeStruct((ne, T), jnp.int32), jax.ShapeDtypeStruct((ne, T), F32),
                   jax.ShapeDtypeStruct((T // tm * ne, LANES), F32), jax.ShapeDtypeStruct((ne, LANES), F32)],
        scratch_shapes=[pltpu.VMEM((ne, LANES), F32)],
        compiler_params=_vmem_limit(32),
        name="out_route",
    )(attn, ml, x2, wo, g2r, wr, br, earlier)


def _dispatch_kernel(runstart_ref, runlen_ref, lshift_ref, padstart_ref, npad_ref,
                     h_ref, rt_ref, xs_ref, cbufs, xbuf, zero_ref, stage_ref, sem, xsem, zsem):
    td = DISPATCH_ROWS
    ne = N_EXPERTS
    rb = RUN_ROWS
    ta = BF16_TILE_ROWS
    grp = RUN_GROUP_ROWS // rb
    tile = pl.program_id(0)
    cbuf = cbufs.at[tile % 2]

    def window(i):
        off = runstart_ref[i] & (ta - 1)
        return off, off + runlen_ref[i]

    def first_chunk(tile_idx, e, action):
        i = tile_idx * ne + e
        off, _ = window(i)
        dst = pl.multiple_of(runstart_ref[i] - off, ta)
        action(pltpu.make_async_copy(cbufs.at[tile_idx % 2, pl.ds(e * rb, rb)], xs_ref.at[pl.ds(dst, rb)], sem))

    @pl.when(tile == 0)
    def _():
        stage_ref[...] = jnp.zeros_like(stage_ref)

    hb = h_ref[...]
    riota = lax.broadcasted_iota(jnp.int32, (rb, td), 0)
    rowt = lax.broadcasted_iota(jnp.int32, (ta, D_MODEL), 0)

    def select(e_slice, shift):
        return jnp.where(riota == rt_ref[e_slice, :] - shift, 1.0, 0.0)

    for g in range(ne // grp):
        experts = range(g * grp, (g + 1) * grp)
        parts = [select(slice(e, e + 1), lshift_ref[tile * ne + e] - window(tile * ne + e)[0]) for e in experts]
        pt = jnp.concatenate(parts, axis=0).astype(BF16)
        cbuf[g * RUN_GROUP_ROWS:(g + 1) * RUN_GROUP_ROWS, :] = _dot(pt, hb).astype(BF16)

    for e in range(ne):
        i = tile * ne + e
        off, total = window(i)
        head = pl.ds(e * rb, ta)
        old = stage_ref[e]
        cbuf[head, :] = jnp.where(rowt < off, old, cbuf[head, :])
        last = e * rb + jnp.minimum(total // ta * ta, rb - ta)
        keep = jnp.logical_and(runlen_ref[i] > 0, total < rb)
        stage_ref[e] = jnp.where(keep, cbuf[pl.ds(pl.multiple_of(last, ta), ta), :], old)

    @pl.when(tile > 0)
    def _():
        for e in range(ne):
            first_chunk(tile - 1, e, lambda cp: cp.wait())

    for e in range(ne):
        first_chunk(tile, e, lambda cp: cp.start())

    def per_expert(e, carry):
        i = tile * ne + e
        off, total = window(i)

        def chunk(ch, c):
            pt = select(pl.ds(e, 1), lshift_ref[i] - off + ch * rb).astype(BF16)
            xbuf[...] = _dot(pt, hb).astype(BF16)
            dst = pl.multiple_of(runstart_ref[i] - off + ch * rb, ta)
            cp = pltpu.make_async_copy(xbuf, xs_ref.at[pl.ds(dst, rb)], xsem)
            cp.start()

            @pl.when(jnp.logical_and(ch == total // rb, total % rb != 0))
            def _():
                last = pl.multiple_of(total % rb // ta * ta, ta)
                stage_ref[e] = xbuf[pl.ds(last, ta), :]

            cp.wait()
            return c

        lax.fori_loop(1, (total + rb - 1) // rb, chunk, 0)
        return carry

    @pl.when(runlen_ref[pl.num_programs(0) * ne + tile] > 1)
    def _():
        lax.fori_loop(0, ne, per_expert, 0)

    @pl.when(tile == pl.num_programs(0) - 1)
    def _():
        for e in range(ne):
            first_chunk(tile, e, lambda cp: cp.wait())

    @pl.when(tile == pl.num_programs(0) - 1)
    def _():
        zero_ref[...] = jnp.zeros_like(zero_ref)

        def fill(wait):
            def per_run(e, carry):
                start = padstart_ref[e]
                head = (-start) & (ta - 1)
                off = pl.multiple_of(start + head, ta)
                left = npad_ref[e] - head

                def whole(b, c):
                    dst = xs_ref.at[pl.ds(pl.multiple_of(off + b * td, ta), td)]
                    cp = pltpu.make_async_copy(zero_ref, dst, zsem)
                    cp.wait() if wait else cp.start()
                    return c

                n_whole = left // td
                lax.fori_loop(0, n_whole, whole, 0)
                off = pl.multiple_of(off + n_whole * td, ta)
                p = td // 2
                while p >= ta:
                    take = (left & p) != 0

                    @pl.when(take)
                    def _(off=off, p=p):
                        cp = pltpu.make_async_copy(zero_ref.at[pl.ds(0, p)], xs_ref.at[pl.ds(off, p)], zsem)
                        cp.wait() if wait else cp.start()

                    off = pl.multiple_of(off + jnp.where(take, p, 0), ta)
                    p //= 2
                return carry

            lax.fori_loop(0, N_EXPERTS + 1, per_run, 0)

        fill(wait=False)
        fill(wait=True)


def _dispatch(h2, rt, runstart, runlen, lshift, padstart, npad, n_slots):
    T = h2.shape[0]
    td = DISPATCH_ROWS
    ne = N_EXPERTS
    assert td == ROUTE_ROWS
    return pl.pallas_call(
        _dispatch_kernel,
        grid_spec=pltpu.PrefetchScalarGridSpec(
            num_scalar_prefetch=5,
            grid=(T // td,),
            in_specs=[pl.BlockSpec((td, D_MODEL), lambda i, *_: (i, 0)),
                      pl.BlockSpec((ne, td), lambda i, *_: (0, i))],
            out_specs=pl.BlockSpec(memory_space=pl.ANY),
            scratch_shapes=[pltpu.VMEM((2, ne * RUN_ROWS, D_MODEL), BF16), pltpu.VMEM((RUN_ROWS, D_MODEL), BF16),
                            pltpu.VMEM((td, D_MODEL), BF16), pltpu.VMEM((ne, BF16_TILE_ROWS, D_MODEL), BF16),
                            pltpu.SemaphoreType.DMA(()), pltpu.SemaphoreType.DMA(()),
                            pltpu.SemaphoreType.DMA(())],
        ),
        out_shape=jax.ShapeDtypeStruct((n_slots, D_MODEL), BF16),
        compiler_params=_vmem_limit(32),
        name="dispatch",
    )(runstart, runlen, lshift, padstart, npad, h2, rt)


def _expert_kernel(bexp_ref, first_ref, next_ref, nused_ref, xs_ref, wgu_hbm, bgu_ref, wd_hbm, bd_ref, y_ref,
                   wgu_st, wd_st, wgu_bf, wd_bf, wsem):
    tm = EXPERT_ROWS
    blk = pl.program_id(0)
    active = blk * tm < nused_ref[0]

    def weight_copies(e):
        return (pltpu.make_async_copy(wgu_hbm.at[e], wgu_st, wsem.at[0]),
                pltpu.make_async_copy(wd_hbm.at[e], wd_st, wsem.at[1]))

    @pl.when(blk == 0)
    def _():
        for cp in weight_copies(bexp_ref[0]):
            cp.start()

    @pl.when(jnp.logical_and(active, first_ref[blk] == 1))
    def _():
        for cp in weight_copies(bexp_ref[blk]):
            cp.wait()
        rc = CAST_ROWS

        def cast(i, carry):
            r = pl.ds(pl.multiple_of(i * rc, rc), rc)
            wgu_bf[r, :] = wgu_st[r, :].astype(BF16)
            wd_bf[r, :] = wd_st[r, :].astype(BF16)
            return carry

        lax.fori_loop(0, D_MODEL // rc, cast, 0)

        @pl.when(next_ref[blk] >= 0)
        def _():
            for cp in weight_copies(next_ref[blk]):
                cp.start()

    @pl.when(active)
    def _():
        xb = xs_ref[...]
        acc = jnp.zeros((tm, D_MODEL), F32)
        for f in range(D_FF // FF_CHUNK):
            lo = f * FF_CHUNK
            g = _dot(xb, wgu_bf[:, lo:lo + FF_CHUNK]) + bgu_ref[:, lo:lo + FF_CHUNK]
            u = _dot(xb, wgu_bf[:, D_FF + lo:D_FF + lo + FF_CHUNK]) + bgu_ref[:, D_FF + lo:D_FF + lo + FF_CHUNK]
            g = jnp.minimum(g, SWIGLU_LIMIT)
            u = jnp.clip(u, -SWIGLU_LIMIT, SWIGLU_LIMIT)
            act = g * jax.nn.sigmoid(SWIGLU_ALPHA * g) * (u + 1.0)
            acc = acc + _dot(act.astype(BF16), wd_bf[lo:lo + FF_CHUNK, :])
        y_ref[...] = (acc + bd_ref[...]).astype(y_ref.dtype)

    @pl.when(jnp.logical_not(active))
    def _():
        y_ref[...] = jnp.zeros_like(y_ref)


def _experts(xs, block_expert, first_block, next_expert, n_used, w_gate_up, b_gate_up, w_down, b_down):
    n_blocks = block_expert.shape[0]
    tm = EXPERT_ROWS
    assert D_FF == D_MODEL and xs.shape[0] == (n_blocks - 1) * tm
    bgu = b_gate_up[:, None, :]
    bd = b_down[:, None, :]
    last = n_blocks - 2
    return pl.pallas_call(
        _expert_kernel,
        grid_spec=pltpu.PrefetchScalarGridSpec(
            num_scalar_prefetch=4,
            grid=(n_blocks,),
            in_specs=[
                pl.BlockSpec((tm, D_MODEL), lambda i, be, fb, nx, nu: (jnp.minimum(i, last), 0)),
                pl.BlockSpec(memory_space=pl.ANY),
                pl.BlockSpec((None, 1, 2 * D_FF), lambda i, be, fb, nx, nu: (be[i], 0, 0)),
                pl.BlockSpec(memory_space=pl.ANY),
                pl.BlockSpec((None, 1, D_MODEL), lambda i, be, fb, nx, nu: (be[i], 0, 0)),
            ],
            out_specs=pl.BlockSpec((tm, D_MODEL), lambda i, be, fb, nx, nu: (i, 0)),
            scratch_shapes=[pltpu.VMEM((D_MODEL, 2 * D_FF), F32), pltpu.VMEM((D_FF, D_MODEL), F32),
                            pltpu.VMEM((D_MODEL, 2 * D_FF), BF16), pltpu.VMEM((D_FF, D_MODEL), BF16),
                            pltpu.SemaphoreType.DMA((2,))],
        ),
        out_shape=jax.ShapeDtypeStruct((n_blocks * tm, D_MODEL), BF16),
        compiler_params=_vmem_limit(48),
        name="experts",
    )(block_expert, first_block, next_expert, n_used, xs, w_gate_up, bgu, w_down, bd)


def _combine_kernel(fetch_ref, shift_ref, nch_ref, x1_ref, rt_ref, gt_ref, y_ref, o_ref,
                    buf, xbuf, sem, xsem):
    tc = ROUTE_ROWS
    ne = N_EXPERTS
    rb = RUN_ROWS
    grp = RUN_GROUP_ROWS // rb
    tile = pl.program_id(0)
    base = tile * ne
    slot = tile % 2

    def run_copy(tile_base, buf_slot, e):
        start = pl.multiple_of(fetch_ref[tile_base + e], BF16_TILE_ROWS)
        return pltpu.make_async_copy(y_ref.at[pl.ds(start, rb)], buf.at[buf_slot, pl.ds(e * rb, rb)],
                                     sem.at[buf_slot, e // grp])

    @pl.when(tile == 0)
    def _():
        for e in range(ne):
            run_copy(base, slot, e).start()

    @pl.when(tile + 1 < pl.num_programs(0))
    def _():
        for e in range(ne):
            run_copy(base + ne, 1 - slot, e).start()

    riota = lax.broadcasted_iota(jnp.int32, (rb, tc), 0)

    def select(e_slice, shift):
        return jnp.where(riota == rt_ref[e_slice, :] - shift, gt_ref[e_slice, :], 0.0)

    parts = []
    for e in range(ne):
        run_copy(base, slot, e).wait()
        parts.append(select(slice(e, e + 1), shift_ref[base + e]).astype(BF16))
    o_ref[...] = x1_ref[...] + _dot_tn(jnp.concatenate(parts, axis=0), buf[slot])

    def per_expert(e, carry):
        def chunk(ch, c):
            start = pl.multiple_of(fetch_ref[base + e] + ch * rb, BF16_TILE_ROWS)
            cp = pltpu.make_async_copy(y_ref.at[pl.ds(start, rb)], xbuf, xsem)
            cp.start()
            cp.wait()
            pt = select(pl.ds(e, 1), shift_ref[base + e] + ch * rb).astype(BF16)
            o_ref[...] += _dot_tn(pt, xbuf[...])
            return c

        lax.fori_loop(1, nch_ref[base + e], chunk, 0)
        return carry

    @pl.when(nch_ref[pl.num_programs(0) * ne + tile] > 1)
    def _():
        lax.fori_loop(0, ne, per_expert, 0)


def _combine(fetch, shift, nch, x1, rt, gt, y):
    T = x1.shape[0]
    tc = ROUTE_ROWS
    ne = N_EXPERTS
    return pl.pallas_call(
        _combine_kernel,
        grid_spec=pltpu.PrefetchScalarGridSpec(
            num_scalar_prefetch=3,
            grid=(T // tc,),
            in_specs=[pl.BlockSpec((tc, D_MODEL), lambda i, f, s, n: (i, 0)),
                      pl.BlockSpec((ne, tc), lambda i, f, s, n: (0, i)),
                      pl.BlockSpec((ne, tc), lambda i, f, s, n: (0, i)),
                      pl.BlockSpec(memory_space=pl.ANY)],
            out_specs=pl.BlockSpec((tc, D_MODEL), lambda i, f, s, n: (i, 0)),
            scratch_shapes=[pltpu.VMEM((2, ne * RUN_ROWS, D_MODEL), BF16), pltpu.VMEM((RUN_ROWS, D_MODEL), BF16),
                            pltpu.SemaphoreType.DMA((2, ne * RUN_ROWS // RUN_GROUP_ROWS)),
                            pltpu.SemaphoreType.DMA(())],
        ),
        out_shape=jax.ShapeDtypeStruct((T, D_MODEL), F32),
        compiler_params=_vmem_limit(32),
        name="combine",
    )(fetch, shift, nch, x1, rt, gt, y)


def _routing_tables(base, counts, T):
    tm = EXPERT_ROWS
    ne = N_EXPERTS
    n_tiles = T // ROUTE_ROWS
    n_blocks = -(-(T * TOP_K + ne * RUN_ROWS) // tm) + ne
    n_slots = n_blocks * tm
    sizes = counts[:, 0].astype(jnp.int32)
    psizes = (sizes + RUN_ROWS + tm - 1) // tm * tm
    pends = jnp.cumsum(psizes)
    pstarts = pends - psizes
    n_used = pends[-1:]
    padstart = jnp.concatenate([pstarts + sizes, n_used])
    npad = jnp.concatenate([psizes - sizes, n_slots - n_used])
    nb = n_blocks + 1
    blk = jnp.arange(nb, dtype=jnp.int32)
    block_expert = jnp.minimum(jnp.sum(pends[None, :] <= (blk * tm)[:, None], axis=1), ne - 1).astype(jnp.int32)
    first_block = jnp.concatenate([jnp.ones((1,), jnp.int32),
                                   (block_expert[1:] != block_expert[:-1]).astype(jnp.int32)])
    starts_group = jnp.logical_and(first_block == 1, blk * tm < n_used)
    pos = jnp.where(starts_group, blk, nb)
    later = jnp.concatenate([lax.cummin(pos[::-1])[::-1][1:], jnp.full((1,), nb, jnp.int32)])
    next_expert = jnp.where(later < nb, block_expert[jnp.minimum(later, nb - 1)], -1).astype(jnp.int32)
    tile_base = base[:, 0].astype(jnp.int32).reshape(n_tiles, ne)
    run_len = jnp.concatenate([tile_base[1:], sizes[None, :]], axis=0) - tile_base
    run_start = pstarts[None, :] + tile_base
    fetch = run_start // BF16_TILE_ROWS * BF16_TILE_ROWS
    shift = fetch - pstarts[None, :]
    nch = jnp.where(run_len > 0, (run_start - fetch + run_len + RUN_ROWS - 1) // RUN_ROWS, 0)
    wch = (run_start % BF16_TILE_ROWS + run_len + RUN_ROWS - 1) // RUN_ROWS
    flat = lambda a: a.reshape(n_tiles * ne).astype(jnp.int32)
    with_tile_max = lambda a, per_run: jnp.concatenate([flat(a), jnp.max(per_run, axis=1).astype(jnp.int32)])
    return (flat(run_start), with_tile_max(run_len, wch), flat(tile_base), padstart, npad, n_slots,
            block_expert, first_block, next_expert, n_used, flat(fetch), flat(shift), with_tile_max(nch, nch))


def kernel(x, norm1_g, w_in, q_norm_g, k_norm_g, conv_w, conv_b, w_mq, w_mk, b_igate, b_fgate, mh_norm_g,
           w_out, norm2_g, w_router, b_router, w_gate_up, b_gate_up, w_down, b_down):
    B, S, D = x.shape
    T = B * S
    assert D == D_MODEL and S % ATTN_ROWS == 0 and norm1_g.shape[0] == 1
    x2 = x.reshape(T, D)
    q, k, v, xm, vm, om, gates = _in_proj(x2, norm1_g[0], w_in[0], q_norm_g[0], k_norm_g[0])
    attn = _attention(q, k, v, B, S)
    ml = _mlstm(xm, vm, om, gates, conv_w[0], conv_b[0], w_mq[0], w_mk[0], b_igate[0], b_fgate[0],
                mh_norm_g[0], B, S)
    x1, h2, rt, gt, base, counts = _out_route(attn, ml, x2, w_out[0], norm2_g[0], w_router[0], b_router[0])
    (run_start, run_len, tile_base, padstart, npad, n_slots, block_expert, first_block, next_expert, n_used,
     fetch, shift, nch) = _routing_tables(base, counts, T)
    xs = _dispatch(h2, rt, run_start, run_len, tile_base, padstart, npad, n_slots)
    y = _experts(xs, block_expert, first_block, next_expert, n_used, w_gate_up[0], b_gate_up[0], w_down[0],
                 b_down[0])
    out = _combine(fetch, shift, nch, x1, rt, gt, y)
    return out.reshape(B, S, D)
```

```python
import jax
import jax.numpy as jnp
from jax import lax
from jax.experimental import pallas as pl
from jax.experimental.pallas import tpu as pltpu

F32 = jnp.float32
BF16 = jnp.bfloat16
NEG_INF = float("-inf")

D_MODEL = 1024
D_ATTN = 512
HEAD_DIM_ATTN = 64
N_HEADS_ATTN = 8
D_MLSTM = 512
HEAD_DIM_MLSTM = 128
N_HEADS_MLSTM = 4
CONV_K = 4
N_EXPERTS = 32
TOP_K = 4
D_FF = 1024
SWIGLU_LIMIT = 7.0
SWIGLU_ALPHA = 1.702
EPS = 1e-6
LOG2_E = 1.4426950408889634
DILATIONS = (1, 4, 16)
BRANCH_SPAN = 128

LANES = 128
SUBLANES = 8

IN_PROJ_ROWS = 512
ATTN_ROWS = 2048
MLSTM_CHUNK = 256
ROUTE_ROWS = 256
ROUTE_TILES_PER_STEP = 4
EXPERT_ROWS = 256
FF_CHUNK = 1024
CAST_ROWS = 128
DISPATCH_ROWS = 256
HEADNORM_LANES = 256
BF16_TILE_ROWS = 16
RUN_ROWS = 64
RUN_GROUP_ROWS = 256
NOT_ROUTED = -1.0e9


def _vmem_limit(mib):
    return pltpu.CompilerParams(vmem_limit_bytes=mib * 1024 * 1024)


def _split3(a):
    p1 = a.astype(BF16)
    r1 = a - p1.astype(F32)
    p2 = r1.astype(BF16)
    r2 = r1 - p2.astype(F32)
    return p1, p2, r2.astype(BF16)


def _dot(a, b):
    return jnp.dot(a, b, preferred_element_type=F32)


def _dot_nt(a, b):
    return lax.dot_general(a, b, (((1,), (1,)), ((), ())), preferred_element_type=F32)


def _dot_tn(a, b):
    return lax.dot_general(a, b, (((0,), (0,)), ((), ())), preferred_element_type=F32)


def _in_proj_kernel(x_ref, g1_ref, wqkv_ref, wm_ref, wgc_ref, qg_ref, kg_ref, hsum_ref,
                    q_ref, k_ref, v_ref, xm_ref, vm_ref, om_ref, gates_ref):
    x = x_ref[...]
    h = x * lax.rsqrt(jnp.mean(x * x, axis=-1, keepdims=True) + EPS) * g1_ref[...]
    hb = h.astype(BF16)
    hl = (h - hb.astype(F32)).astype(BF16)
    gw = HEADNORM_LANES

    def head_norm(z, g_ref):
        parts = []
        for c in range(D_ATTN // gw):
            zc = z[:, c * gw:(c + 1) * gw]
            ms = _dot((zc * zc).astype(BF16), hsum_ref[...]) * (1.0 / HEAD_DIM_ATTN)
            parts.append(zc * lax.rsqrt(ms + EPS))
        return jnp.concatenate(parts, axis=-1) * g_ref[...]

    zq = _dot(hb, wqkv_ref[:, 0:D_ATTN])
    q_ref[...] = head_norm(zq, qg_ref) * (HEAD_DIM_ATTN ** -0.5 * LOG2_E)
    zk = _dot(hb, wqkv_ref[:, D_ATTN:2 * D_ATTN])
    k_ref[...] = head_norm(zk, kg_ref)
    v_ref[...] = _dot(hb, wqkv_ref[:, 2 * D_ATTN:3 * D_ATTN])
    xm_ref[...] = _dot(hb, wm_ref[:, 0:D_MLSTM])
    vm_ref[...] = _dot(hb, wm_ref[:, D_MLSTM:2 * D_MLSTM])
    om_ref[...] = _dot(hb, wm_ref[:, 2 * D_MLSTM:3 * D_MLSTM])
    ng = 2 * N_HEADS_MLSTM
    rows = hb.shape[0]
    both = _dot(jnp.concatenate([hb, hl], axis=0), wgc_ref[...])
    gsum = both[:rows] + both[rows:]
    gsum = gsum + pltpu.roll(gsum, LANES - ng, axis=1)
    lane = lax.broadcasted_iota(jnp.int32, gsum.shape, 1)
    gates_ref[...] = jnp.where(lane < ng, gsum, 0.0)


def _in_proj(x2, g1, w_in, q_g, k_g):
    T = x2.shape[0]
    tm = IN_PROJ_ROWS
    n_qkv = 3 * D_ATTN
    n_m = 3 * D_MLSTM
    ng = 2 * N_HEADS_MLSTM
    wqkv = w_in[:, :n_qkv].astype(BF16)
    wm = w_in[:, n_qkv:n_qkv + n_m].astype(BF16)
    wg = w_in[:, n_qkv + n_m:]
    wgh = wg.astype(BF16)
    wgl = (wg - wgh.astype(F32)).astype(BF16)
    wgc = jnp.pad(jnp.concatenate([wgh, wgl], axis=1), ((0, 0), (0, LANES - 2 * ng)))
    lane = jnp.arange(HEADNORM_LANES)
    hsum = (lane[:, None] // HEAD_DIM_ATTN == lane[None, :] // HEAD_DIM_ATTN).astype(BF16)
    qg = jnp.tile(q_g, N_HEADS_ATTN)[None, :]
    kg = jnp.tile(k_g, N_HEADS_ATTN)[None, :]
    row = lambda n: pl.BlockSpec((tm, n), lambda i: (i, 0))
    full = lambda a: pl.BlockSpec(a.shape, lambda i: (0,) * a.ndim)
    outs = [jax.ShapeDtypeStruct((T, D_ATTN), F32)] * 3 + [jax.ShapeDtypeStruct((T, D_MLSTM), F32)] * 3
    outs.append(jax.ShapeDtypeStruct((T, LANES), F32))
    g1r = g1[None, :]
    return pl.pallas_call(
        _in_proj_kernel,
        grid=(T // tm,),
        in_specs=[row(D_MODEL), full(g1r), full(wqkv), full(wm), full(wgc), full(qg), full(kg), full(hsum)],
        out_specs=[row(D_ATTN)] * 3 + [row(D_MLSTM)] * 3 + [row(LANES)],
        out_shape=outs,
        compiler_params=_vmem_limit(48),
        name="in_proj",
    )(x2, g1r, wqkv, wm, wgc, qg, kg, hsum)


def _attn_tiles():
    tiles = []
    for br, d in enumerate(DILATIONS):
        group = d * BRANCH_SPAN
        for u in range(ATTN_ROWS // group):
            for c in range(d):
                tiles.append((br, d, u * group + c))
    return tiles


def _attn_kernel(slope_ref, q_ref, kp_ref, kc_ref, vp_ref, vc_ref, o_ref,
                 kwin, vwin, bias_scr, m_scr, l_scr, acc_scr):
    W = ATTN_ROWS
    n = BRANCH_SPAN
    step = pl.program_id(2)
    kwin[0:W, :] = kp_ref[...]
    kwin[W:2 * W, :] = kc_ref[...]
    vwin[0:W, :] = vp_ref[...]
    vwin[W:2 * W, :] = vc_ref[...]

    lane = lax.broadcasted_iota(jnp.int32, (n, LANES), 1)
    first_head = lane < HEAD_DIM_ATTN

    @pl.when(step == 0)
    def _():
        row = lax.broadcasted_iota(jnp.int32, (n, 2 * n), 0)
        col = lax.broadcasted_iota(jnp.int32, (n, 2 * n), 1)
        j = n + row - col
        valid = jnp.logical_and(j >= 0, j <= n)
        jf = j.astype(F32)
        for br, d in enumerate(DILATIONS):
            for hh in range(2):
                slope = slope_ref[:, hh * HEAD_DIM_ATTN:hh * HEAD_DIM_ATTN + 1]
                b = jnp.where(valid, -(slope * (float(d) * LOG2_E)) * jf, NEG_INF)
                bias_scr[br, hh, 0] = b
                bias_scr[br, hh, 1] = jnp.where(col >= n, b, NEG_INF)

    for br, d, qs in _attn_tiles():
        lo = W + qs - n * d
        q = q_ref[pl.ds(qs, n, stride=d), :]
        kk = jnp.concatenate([kwin[pl.ds(lo, n, stride=d), :], kwin[pl.ds(W + qs, n, stride=d), :]],
                             axis=0).astype(BF16)
        vv = jnp.concatenate([vwin[pl.ds(lo, n, stride=d), :], vwin[pl.ds(W + qs, n, stride=d), :]],
                             axis=0).astype(BF16)
        if lo < W:
            sel = jnp.where(step == 0, 1, 0)
        else:
            sel = 0
        qb = q.astype(BF16)
        zero = jnp.zeros_like(qb)
        q2 = jnp.concatenate([jnp.where(first_head, qb, zero), jnp.where(first_head, zero, qb)], axis=0)
        s2 = _dot_nt(q2, kk)
        ms, ls, ps = [], [], []
        for hh in range(2):
            s = s2[hh * n:(hh + 1) * n] + bias_scr[br, hh, sel]
            m = jnp.max(s, axis=-1, keepdims=True)
            p = jnp.exp2(s - m)
            ms.append(m)
            ls.append(jnp.sum(p, axis=-1, keepdims=True))
            ps.append(p.astype(BF16))
        pv = _dot(jnp.concatenate(ps, axis=0), vv)
        rows = pl.ds(qs, n, stride=d)
        m_scr[br, rows, :] = jnp.where(first_head, ms[0], ms[1])
        l_scr[br, rows, :] = jnp.where(first_head, ls[0], ls[1])
        acc_scr[br, rows, :] = jnp.where(first_head, pv[:n], pv[n:])

    chunk = 256

    def combine(i, carry):
        r = pl.ds(pl.multiple_of(i * chunk, chunk), chunk)
        m0, m1, m2 = m_scr[0, r, :], m_scr[1, r, :], m_scr[2, r, :]
        mm = jnp.maximum(jnp.maximum(m0, m1), m2)
        w0, w1, w2 = jnp.exp2(m0 - mm), jnp.exp2(m1 - mm), jnp.exp2(m2 - mm)
        num = w0 * acc_scr[0, r, :] + w1 * acc_scr[1, r, :] + w2 * acc_scr[2, r, :]
        den = w0 * l_scr[0, r, :] + w1 * l_scr[1, r, :] + w2 * l_scr[2, r, :]
        o_ref[r, :] = num / den
        return carry

    lax.fori_loop(0, W // chunk, combine, 0)


def _attention(q, k, v, B, S):
    W = ATTN_ROWS
    nb = S // W
    n_pairs = D_ATTN // LANES
    slopes = jnp.exp2(-8.0 * jnp.arange(1, N_HEADS_ATTN + 1, dtype=F32) / N_HEADS_ATTN)
    slope_l = jnp.repeat(slopes, HEAD_DIM_ATTN).reshape(n_pairs, 1, LANES)
    cur = pl.BlockSpec((W, LANES), lambda b, p, i: (b * nb + i, p))
    prev = pl.BlockSpec((W, LANES), lambda b, p, i: (b * nb + jnp.maximum(i - 1, 0), p))
    n = BRANCH_SPAN
    nbr = len(DILATIONS)
    return pl.pallas_call(
        _attn_kernel,
        grid=(B, n_pairs, nb),
        in_specs=[pl.BlockSpec((None, 1, LANES), lambda b, p, i: (p, 0, 0)), cur, prev, cur, prev, cur],
        out_specs=cur,
        out_shape=jax.ShapeDtypeStruct((B * S, D_ATTN), F32),
        scratch_shapes=[
            pltpu.VMEM((2 * W, LANES), F32), pltpu.VMEM((2 * W, LANES), F32),
            pltpu.VMEM((nbr, 2, 2, n, 2 * n), F32),
            pltpu.VMEM((nbr, W, LANES), F32), pltpu.VMEM((nbr, W, LANES), F32),
            pltpu.VMEM((nbr, W, LANES), F32),
        ],
        compiler_params=_vmem_limit(48),
        name="attention",
    )(slope_l, q, k, k, v, v)


def _mlstm_kernel(xm_ref, vm_ref, om_ref, gcol_ref, grow_ref, cw_ref, cb_ref, wq_ref, wk_ref,
                  bcol_ref, brow_ref, g_ref, o_ref, xs_ref, xtail_ref, c_ref, n_ref, m_ref):
    L = MLSTM_CHUNK
    dh = HEAD_DIM_MLSTM
    nh = N_HEADS_MLSTM

    @pl.when(pl.program_id(1) == 0)
    def _():
        xtail_ref[...] = jnp.zeros_like(xtail_ref)
        c_ref[...] = jnp.zeros_like(c_ref)
        n_ref[...] = jnp.zeros_like(n_ref)
        m_ref[...] = jnp.zeros_like(m_ref)

    x = xm_ref[...]
    xs_ref[0:SUBLANES, :] = xtail_ref[...]
    xs_ref[SUBLANES:SUBLANES + L, :] = x
    xtail_ref[...] = x[L - SUBLANES:L, :]
    xc = cb_ref[...] + x * cw_ref[CONV_K - 1:CONV_K, :]
    for back in range(1, CONV_K):
        xc = xc + xs_ref[SUBLANES - back:SUBLANES - back + L, :] * cw_ref[CONV_K - 1 - back:CONV_K - back, :]
    xc = xc * jax.nn.sigmoid(xc)

    gc = gcol_ref[...] + bcol_ref[...]
    gr = grow_ref[...] + brow_ref[...]
    ri = lax.broadcasted_iota(jnp.int32, (L, L), 0)
    ci = lax.broadcasted_iota(jnp.int32, (L, L), 1)
    causal = ri >= ci
    tril = jnp.where(causal, 1.0, 0.0).astype(BF16)
    triu = jnp.where(ri <= ci, 1.0, 0.0).astype(BF16)
    c1, c2, c3 = _split3(jax.nn.log_sigmoid(gc))
    bc = _dot(tril, c1) + _dot(tril, c2) + _dot(tril, c3)
    r1, r2, r3 = _split3(jax.nn.log_sigmoid(gr))
    brw = _dot(r1, triu) + _dot(r2, triu) + _dot(r3, triu)

    heads = range(nh)
    hs = [slice(h * dh, (h + 1) * dh) for h in heads]
    xh = [xc[:, hs[h]].astype(BF16) for h in heads]
    qf = [_dot(xh[h], wq_ref[h]) for h in heads]
    kf = [_dot(xh[h], wk_ref[h]) * (dh ** -0.5) for h in heads]
    qb = [qf[h].astype(BF16) for h in heads]
    vb = [vm_ref[:, hs[h]].astype(BF16) for h in heads]

    b_col = [bc[:, nh + h:nh + h + 1] for h in heads]
    a_col = [gc[:, h:h + 1] - b_col[h] for h in heads]
    a_row = [gr[h:h + 1, :] - brw[nh + h:nh + h + 1, :] for h in heads]
    m_prev = [m_ref[h:h + 1, 0:1] for h in heads]
    log_d = [jnp.where(causal, b_col[h] + a_row[h], NEG_INF) for h in heads]
    log_inter = [b_col[h] + m_prev[h] for h in heads]
    m_t = [jnp.maximum(log_inter[h], jnp.max(log_d[h], axis=-1, keepdims=True)) for h in heads]
    d = [jnp.exp(log_d[h] - m_t[h]) for h in heads]
    inter = [jnp.exp(log_inter[h] - m_t[h]) for h in heads]

    s = [_dot_nt(qb[h], kf[h].astype(BF16)) * d[h] for h in heads]
    c_old = [c_ref[h] for h in heads]
    n_old = [n_ref[h:h + 1, :] for h in heads]
    num = [_dot(s[h].astype(BF16), vb[h]) + inter[h] * _dot(qb[h], c_old[h].astype(BF16)) for h in heads]
    den = [jnp.sum(s[h], axis=-1, keepdims=True) + inter[h] * jnp.sum(qf[h] * n_old[h], axis=-1, keepdims=True)
           for h in heads]
    hh = [num[h] / jnp.maximum(jnp.abs(den[h]), jnp.exp(-m_t[h])) for h in heads]

    for h in heads:
        b_last = b_col[h][L - 1:L, :]
        log_w = b_last + a_col[h]
        m_new = jnp.maximum(b_last + m_prev[h], jnp.max(log_w, axis=0, keepdims=True))
        kw = kf[h] * jnp.exp(log_w - m_new)
        decay = jnp.exp(b_last + m_prev[h] - m_new)
        c_ref[h] = decay * c_old[h] + _dot_tn(kw.astype(BF16), vb[h])
        n_ref[h:h + 1, :] = decay * n_old[h] + jnp.sum(kw, axis=0, keepdims=True)
        m_ref[h:h + 1, :] = jnp.broadcast_to(m_new, (1, LANES))

    for h in heads:
        hn = hh[h] * lax.rsqrt(jnp.mean(hh[h] * hh[h], axis=-1, keepdims=True) + EPS) * g_ref[:, hs[h]]
        o_ref[:, hs[h]] = jax.nn.sigmoid(om_ref[:, hs[h]]) * hn


def _mlstm(xm, vm, om, gates, conv_w, conv_b, w_mq, w_mk, b_i, b_f, mh_g, B, S):
    L = MLSTM_CHUNK
    nc = S // L
    nh = N_HEADS_MLSTM
    ng = 2 * nh
    grow = gates[:, :ng].reshape(B, S, ng).transpose(0, 2, 1)
    bias = jnp.concatenate([b_i, b_f])
    bcol = jnp.pad(bias, (0, LANES - ng))[None, :]
    brow = bias[:, None]
    wq = w_mq.astype(BF16)
    wk = w_mk.astype(BF16)
    cb = conv_b[None, :]
    g = mh_g.reshape(1, D_MLSTM)
    rows = pl.BlockSpec((L, D_MLSTM), lambda b, c: (b * nc + c, 0))
    full = lambda a: pl.BlockSpec(a.shape, lambda b, c: (0,) * a.ndim)
    return pl.pallas_call(
        _mlstm_kernel,
        grid=(B, nc),
        in_specs=[rows, rows, rows, pl.BlockSpec((L, LANES), lambda b, c: (b * nc + c, 0)),
                  pl.BlockSpec((None, ng, L), lambda b, c: (b, 0, c)),
                  full(conv_w), full(cb), full(wq), full(wk), full(bcol), full(brow), full(g)],
        out_specs=rows,
        out_shape=jax.ShapeDtypeStruct((B * S, D_MLSTM), F32),
        scratch_shapes=[
            pltpu.VMEM((L + SUBLANES, D_MLSTM), F32), pltpu.VMEM((SUBLANES, D_MLSTM), F32),
            pltpu.VMEM((nh, HEAD_DIM_MLSTM, HEAD_DIM_MLSTM), F32),
            pltpu.VMEM((SUBLANES, LANES), F32), pltpu.VMEM((SUBLANES, LANES), F32),
        ],
        compiler_params=_vmem_limit(32),
        name="mlstm",
    )(xm, vm, om, gates, grow, conv_w, cb, wq, wk, bcol, brow, g)


def _out_route_kernel(attn_ref, ml_ref, x_ref, wo_ref, g2_ref, wr_ref, br_ref, earlier_ref,
                      x1_ref, h2_ref, rt_ref, gt_ref, base_ref, cnt_ref, carry_ref):
    @pl.when(pl.program_id(0) == 0)
    def _():
        carry_ref[...] = jnp.zeros_like(carry_ref)

    tiles = range(ROUTE_TILES_PER_STEP)
    split = [_project_tile(j, attn_ref, ml_ref, x_ref, wo_ref, g2_ref, x1_ref, h2_ref) for j in tiles]
    logits = [_router_logits(hb, hl, wr_ref, br_ref) for hb, hl in split]
    carry = carry_ref[...]
    for j in tiles:
        carry = _route_tile(j, logits[j], carry, earlier_ref, rt_ref, gt_ref, base_ref)
    carry_ref[...] = carry
    cnt_ref[...] = carry


def _project_tile(j, attn_ref, ml_ref, x_ref, wo_ref, g2_ref, x1_ref, h2_ref):
    tm = ROUTE_ROWS
    rows = slice(j * tm, (j + 1) * tm)
    x1 = (x_ref[rows, :] + _dot(attn_ref[rows, :].astype(BF16), wo_ref[0:D_ATTN, :])
          + _dot(ml_ref[rows, :].astype(BF16), wo_ref[D_ATTN:D_ATTN + D_MLSTM, :]))
    x1_ref[rows, :] = x1
    h2 = x1 * lax.rsqrt(jnp.mean(x1 * x1, axis=-1, keepdims=True) + EPS) * g2_ref[...]
    hb = h2.astype(BF16)
    h2_ref[rows, :] = hb
    return hb, (h2 - hb.astype(F32)).astype(BF16)


def _router_logits(hb, hl, wr_ref, br_ref):
    ne = N_EXPERTS
    hi_pass = _dot_nt(wr_ref[...], hb)
    return (hi_pass[:ne] + hi_pass[ne:] + _dot_nt(wr_ref[0:ne, :], hl)) + br_ref[...]


def _route_tile(j, lt, carry, earlier_ref, rt_ref, gt_ref, base_ref):
    tm = ROUTE_ROWS
    ne = N_EXPERTS
    rows = slice(j * tm, (j + 1) * tm)

    eidx = lax.broadcasted_iota(jnp.int32, (ne, tm), 0)
    vals, hots = [], []
    work = lt
    for _ in range(TOP_K):
        mx = jnp.max(work, axis=0, keepdims=True)
        idx = jnp.min(jnp.where(work == mx, eidx, ne), axis=0, keepdims=True)
        hot = eidx == idx
        vals.append(mx)
        hots.append(hot)
        work = jnp.where(hot, NEG_INF, work)
    ex = [jnp.exp(v - vals[0]) for v in vals]
    tot = ex[0] + ex[1] + ex[2] + ex[3]

    onehot = jnp.zeros((ne, tm), F32)
    gt = jnp.zeros((ne, tm), F32)
    for kk in range(TOP_K):
        onehot = onehot + jnp.where(hots[kk], 1.0, 0.0)
        gt = gt + jnp.where(hots[kk], ex[kk] / tot, 0.0)
    before = _dot(onehot.astype(BF16), earlier_ref[...]) + carry[:, 0:1]
    base_ref[j * ne:(j + 1) * ne, :] = carry
    rt_ref[:, rows] = jnp.where(onehot > 0.5, before, NOT_ROUTED).astype(jnp.int32)
    gt_ref[:, rows] = gt
    return carry + jnp.sum(onehot, axis=1, keepdims=True)


def _out_route(attn, ml, x2, w_out, g2, w_router, b_router):
    T = x2.shape[0]
    tm = ROUTE_ROWS
    ne = N_EXPERTS
    wo = w_out.astype(BF16)
    wrt = w_router.T
    wrh = wrt.astype(BF16)
    wr = jnp.concatenate([wrh, (wrt - wrh.astype(F32)).astype(BF16)], axis=0)
    br = b_router[:, None]
    g2r = g2[None, :]
    tok = jnp.arange(tm)
    earlier = (tok[:, None] < tok[None, :]).astype(BF16)
    ts = tm * ROUTE_TILES_PER_STEP
    row = lambda n: pl.BlockSpec((ts, n), lambda i: (i, 0))
    col = lambda n: pl.BlockSpec((n, ts), lambda i: (0, i))
    full = lambda a: pl.BlockSpec(a.shape, lambda i: (0,) * a.ndim)
    return pl.pallas_call(
        _out_route_kernel,
        grid=(T // ts,),
        in_specs=[row(D_ATTN), row(D_MLSTM), row(D_MODEL), full(wo), full(g2r), full(wr), full(br),
                  full(earlier)],
        out_specs=[row(D_MODEL), row(D_MODEL), col(ne), col(ne),
                   pl.BlockSpec((ROUTE_TILES_PER_STEP * ne, LANES), lambda i: (i, 0)),
                   pl.BlockSpec((ne, LANES), lambda i: (0, 0))],
        out_shape=[jax.ShapeDtypeStruct((T, D_MODEL), F32), jax.ShapeDtypeStruct((T, D_MODEL), BF16),
                   jax.ShapeDtypeStruct((ne, T), jnp.int32), jax.ShapeDtypeStruct((ne, T), F32),
                   jax.ShapeDtypeStruct((T // tm * ne, LANES), F32), jax.ShapeDtypeStruct((ne, LANES), F32)],
        scratch_shapes=[pltpu.VMEM((ne, LANES), F32)],
        compiler_params=_vmem_limit(32),
        name="out_route",
    )(attn, ml, x2, wo, g2r, wr, br, earlier)


def _dispatch_kernel(runstart_ref, runlen_ref, lshift_ref, padstart_ref, npad_ref,
                     h_ref, rt_ref, xs_ref, cbufs, xbuf, zero_ref, stage_ref, sem, xsem, zsem):
    td = DISPATCH_ROWS
    ne = N_EXPERTS
    rb = RUN_ROWS
    ta = BF16_TILE_ROWS
    grp = RUN_GROUP_ROWS // rb
    tile = pl.program_id(0)
    cbuf = cbufs.at[tile % 2]

    def window(i):
        off = runstart_ref[i] & (ta - 1)
        return off, off + runlen_ref[i]

    def first_chunk(tile_idx, e, action):
        i = tile_idx * ne + e
        off, _ = window(i)
        dst = pl.multiple_of(runstart_ref[i] - off, ta)
        action(pltpu.make_async_copy(cbufs.at[tile_idx % 2, pl.ds(e * rb, rb)], xs_ref.at[pl.ds(dst, rb)], sem))

    @pl.when(tile == 0)
    def _():
        stage_ref[...] = jnp.zeros_like(stage_ref)

    hb = h_ref[...]
    riota = lax.broadcasted_iota(jnp.int32, (rb, td), 0)
    rowt = lax.broadcasted_iota(jnp.int32, (ta, D_MODEL), 0)

    def select(e_slice, shift):
        return jnp.where(riota == rt_ref[e_slice, :] - shift, 1.0, 0.0)

    for g in range(ne // grp):
        experts = range(g * grp, (g + 1) * grp)
        parts = [select(slice(e, e + 1), lshift_ref[tile * ne + e] - window(tile * ne + e)[0]) for e in experts]
        pt = jnp.concatenate(parts, axis=0).astype(BF16)
        cbuf[g * RUN_GROUP_ROWS:(g + 1) * RUN_GROUP_ROWS, :] = _dot(pt, hb).astype(BF16)

    for e in range(ne):
        i = tile * ne + e
        off, total = window(i)
        head = pl.ds(e * rb, ta)
        old = stage_ref[e]
        cbuf[head, :] = jnp.where(rowt < off, old, cbuf[head, :])
        last = e * rb + jnp.minimum(total // ta * ta, rb - ta)
        keep = jnp.logical_and(runlen_ref[i] > 0, total < rb)
        stage_ref[e] = jnp.where(keep, cbuf[pl.ds(pl.multiple_of(last, ta), ta), :], old)

    @pl.when(tile > 0)
    def _():
        for e in range(ne):
            first_chunk(tile - 1, e, lambda cp: cp.wait())

    for e in range(ne):
        first_chunk(tile, e, lambda cp: cp.start())

    def per_expert(e, carry):
        i = tile * ne + e
        off, total = window(i)

        def chunk(ch, c):
            pt = select(pl.ds(e, 1), lshift_ref[i] - off + ch * rb).astype(BF16)
            xbuf[...] = _dot(pt, hb).astype(BF16)
            dst = pl.multiple_of(runstart_ref[i] - off + ch * rb, ta)
            cp = pltpu.make_async_copy(xbuf, xs_ref.at[pl.ds(dst, rb)], xsem)
            cp.start()

            @pl.when(jnp.logical_and(ch == total // rb, total % rb != 0))
            def _():
                last = pl.multiple_of(total % rb // ta * ta, ta)
                stage_ref[e] = xbuf[pl.ds(last, ta), :]

            cp.wait()
            return c

        lax.fori_loop(1, (total + rb - 1) // rb, chunk, 0)
        return carry

    @pl.when(runlen_ref[pl.num_programs(0) * ne + tile] > 1)
    def _():
        lax.fori_loop(0, ne, per_expert, 0)

    @pl.when(tile == pl.num_programs(0) - 1)
    def _():
        for e in range(ne):
            first_chunk(tile, e, lambda cp: cp.wait())

    @pl.when(tile == pl.num_programs(0) - 1)
    def _():
        zero_ref[...] = jnp.zeros_like(zero_ref)

        def fill(wait):
            def per_run(e, carry):
                start = padstart_ref[e]
                head = (-start) & (ta - 1)
                off = pl.multiple_of(start + head, ta)
                left = npad_ref[e] - head

                def whole(b, c):
                    dst = xs_ref.at[pl.ds(pl.multiple_of(off + b * td, ta), td)]
                    cp = pltpu.make_async_copy(zero_ref, dst, zsem)
                    cp.wait() if wait else cp.start()
                    return c

                n_whole = left // td
                lax.fori_loop(0, n_whole, whole, 0)
                off = pl.multiple_of(off + n_whole * td, ta)
                p = td // 2
                while p >= ta:
                    take = (left & p) != 0

                    @pl.when(take)
                    def _(off=off, p=p):
                        cp = pltpu.make_async_copy(zero_ref.at[pl.ds(0, p)], xs_ref.at[pl.ds(off, p)], zsem)
                        cp.wait() if wait else cp.start()

                    off = pl.multiple_of(off + jnp.where(take, p, 0), ta)
                    p //= 2
                return carry

            lax.fori_loop(0, N_EXPERTS + 1, per_run, 0)

        fill(wait=False)
        fill(wait=True)


def _dispatch(h2, rt, runstart, runlen, lshift, padstart, npad, n_slots):
    T = h2.shape[0]
    td = DISPATCH_ROWS
    ne = N_EXPERTS
    assert td == ROUTE_ROWS
    return pl.pallas_call(
        _dispatch_kernel,
        grid_spec=pltpu.PrefetchScalarGridSpec(
            num_scalar_prefetch=5,
            grid=(T // td,),
            in_specs=[pl.BlockSpec((td, D_MODEL), lambda i, *_: (i, 0)),
                      pl.BlockSpec((ne, td), lambda i, *_: (0, i))],
            out_specs=pl.BlockSpec(memory_space=pl.ANY),
            scratch_shapes=[pltpu.VMEM((2, ne * RUN_ROWS, D_MODEL), BF16), pltpu.VMEM((RUN_ROWS, D_MODEL), BF16),
                            pltpu.VMEM((td, D_MODEL), BF16), pltpu.VMEM((ne, BF16_TILE_ROWS, D_MODEL), BF16),
                            pltpu.SemaphoreType.DMA(()), pltpu.SemaphoreType.DMA(()),
                            pltpu.SemaphoreType.DMA(())],
        ),
        out_shape=jax.ShapeDtypeStruct((n_slots, D_MODEL), BF16),
        compiler_params=_vmem_limit(32),
        name="dispatch",
    )(runstart, runlen, lshift, padstart, npad, h2, rt)


def _expert_kernel(bexp_ref, first_ref, next_ref, nused_ref, xs_ref, wgu_hbm, bgu_ref, wd_hbm, bd_ref, y_ref,
                   wgu_st, wd_st, wgu_bf, wd_bf, wsem):
    tm = EXPERT_ROWS
    blk = pl.program_id(0)
    active = blk * tm < nused_ref[0]

    def weight_copies(e):
        return (pltpu.make_async_copy(wgu_hbm.at[e], wgu_st, wsem.at[0]),
                pltpu.make_async_copy(wd_hbm.at[e], wd_st, wsem.at[1]))

    @pl.when(blk == 0)
    def _():
        for cp in weight_copies(bexp_ref[0]):
            cp.start()

    @pl.when(jnp.logical_and(active, first_ref[blk] == 1))
    def _():
        for cp in weight_copies(bexp_ref[blk]):
            cp.wait()
        rc = CAST_ROWS

        def cast(i, carry):
            r = pl.ds(pl.multiple_of(i * rc, rc), rc)
            wgu_bf[r, :] = wgu_st[r, :].astype(BF16)
            wd_bf[r, :] = wd_st[r, :].astype(BF16)
            return carry

        lax.fori_loop(0, D_MODEL // rc, cast, 0)

        @pl.when(next_ref[blk] >= 0)
        def _():
            for cp in weight_copies(next_ref[blk]):
                cp.start()

    @pl.when(active)
    def _():
        xb = xs_ref[...]
        acc = jnp.zeros((tm, D_MODEL), F32)
        for f in range(D_FF // FF_CHUNK):
            lo = f * FF_CHUNK
            g = _dot(xb, wgu_bf[:, lo:lo + FF_CHUNK]) + bgu_ref[:, lo:lo + FF_CHUNK]
            u = _dot(xb, wgu_bf[:, D_FF + lo:D_FF + lo + FF_CHUNK]) + bgu_ref[:, D_FF + lo:D_FF + lo + FF_CHUNK]
            g = jnp.minimum(g, SWIGLU_LIMIT)
            u = jnp.clip(u, -SWIGLU_LIMIT, SWIGLU_LIMIT)
            act = g * jax.nn.sigmoid(SWIGLU_ALPHA * g) * (u + 1.0)
            acc = acc + _dot(act.astype(BF16), wd_bf[lo:lo + FF_CHUNK, :])
        y_ref[...] = (acc + bd_ref[...]).astype(y_ref.dtype)

    @pl.when(jnp.logical_not(active))
    def _():
        y_ref[...] = jnp.zeros_like(y_ref)


def _experts(xs, block_expert, first_block, next_expert, n_used, w_gate_up, b_gate_up, w_down, b_down):
    n_blocks = block_expert.shape[0]
    tm = EXPERT_ROWS
    assert D_FF == D_MODEL and xs.shape[0] == (n_blocks - 1) * tm
    bgu = b_gate_up[:, None, :]
    bd = b_down[:, None, :]
    last = n_blocks - 2
    return pl.pallas_call(
        _expert_kernel,
        grid_spec=pltpu.PrefetchScalarGridSpec(
            num_scalar_prefetch=4,
            grid=(n_blocks,),
            in_specs=[
                pl.BlockSpec((tm, D_MODEL), lambda i, be, fb, nx, nu: (jnp.minimum(i, last), 0)),
                pl.BlockSpec(memory_space=pl.ANY),
                pl.BlockSpec((None, 1, 2 * D_FF), lambda i, be, fb, nx, nu: (be[i], 0, 0)),
                pl.BlockSpec(memory_space=pl.ANY),
                pl.BlockSpec((None, 1, D_MODEL), lambda i, be, fb, nx, nu: (be[i], 0, 0)),
            ],
            out_specs=pl.BlockSpec((tm, D_MODEL), lambda i, be, fb, nx, nu: (i, 0)),
            scratch_shapes=[pltpu.VMEM((D_MODEL, 2 * D_FF), F32), pltpu.VMEM((D_FF, D_MODEL), F32),
                            pltpu.VMEM((D_MODEL, 2 * D_FF), BF16), pltpu.VMEM((D_FF, D_MODEL), BF16),
                            pltpu.SemaphoreType.DMA((2,))],
        ),
        out_shape=jax.ShapeDtypeStruct((n_blocks * tm, D_MODEL), BF16),
        compiler_params=_vmem_limit(48),
        name="experts",
    )(block_expert, first_block, next_expert, n_used, xs, w_gate_up, bgu, w_down, bd)


def _combine_kernel(fetch_ref, shift_ref, nch_ref, x1_ref, rt_ref, gt_ref, y_ref, o_ref,
                    buf, xbuf, sem, xsem):
    tc = ROUTE_ROWS
    ne = N_EXPERTS
    rb = RUN_ROWS
    grp = RUN_GROUP_ROWS // rb
    tile = pl.program_id(0)
    base = tile * ne
    slot = tile % 2

    def run_copy(tile_base, buf_slot, e):
        start = pl.multiple_of(fetch_ref[tile_base + e], BF16_TILE_ROWS)
        return pltpu.make_async_copy(y_ref.at[pl.ds(start, rb)], buf.at[buf_slot, pl.ds(e * rb, rb)],
                                     sem.at[buf_slot, e // grp])

    @pl.when(tile == 0)
    def _():
        for e in range(ne):
            run_copy(base, slot, e).start()

    @pl.when(tile + 1 < pl.num_programs(0))
    def _():
        for e in range(ne):
            run_copy(base + ne, 1 - slot, e).start()

    riota = lax.broadcasted_iota(jnp.int32, (rb, tc), 0)

    def select(e_slice, shift):
        return jnp.where(riota == rt_ref[e_slice, :] - shift, gt_ref[e_slice, :], 0.0)

    parts = []
    for e in range(ne):
        run_copy(base, slot, e).wait()
        parts.append(select(slice(e, e + 1), shift_ref[base + e]).astype(BF16))
    o_ref[...] = x1_ref[...] + _dot_tn(jnp.concatenate(parts, axis=0), buf[slot])

    def per_expert(e, carry):
        def chunk(ch, c):
            start = pl.multiple_of(fetch_ref[base + e] + ch * rb, BF16_TILE_ROWS)
            cp = pltpu.make_async_copy(y_ref.at[pl.ds(start, rb)], xbuf, xsem)
            cp.start()
            cp.wait()
            pt = select(pl.ds(e, 1), shift_ref[base + e] + ch * rb).astype(BF16)
            o_ref[...] += _dot_tn(pt, xbuf[...])
            return c

        lax.fori_loop(1, nch_ref[base + e], chunk, 0)
        return carry

    @pl.when(nch_ref[pl.num_programs(0) * ne + tile] > 1)
    def _():
        lax.fori_loop(0, ne, per_expert, 0)


def _combine(fetch, shift, nch, x1, rt, gt, y):
    T = x1.shape[0]
    tc = ROUTE_ROWS
    ne = N_EXPERTS
    return pl.pallas_call(
        _combine_kernel,
        grid_spec=pltpu.PrefetchScalarGridSpec(
            num_scalar_prefetch=3,
            grid=(T // tc,),
            in_specs=[pl.BlockSpec((tc, D_MODEL), lambda i, f, s, n: (i, 0)),
                      pl.BlockSpec((ne, tc), lambda i, f, s, n: (0, i)),
                      pl.BlockSpec((ne, tc), lambda i, f, s, n: (0, i)),
                      pl.BlockSpec(memory_space=pl.ANY)],
            out_specs=pl.BlockSpec((tc, D_MODEL), lambda i, f, s, n: (i, 0)),
            scratch_shapes=[pltpu.VMEM((2, ne * RUN_ROWS, D_MODEL), BF16), pltpu.VMEM((RUN_ROWS, D_MODEL), BF16),
                            pltpu.SemaphoreType.DMA((2, ne * RUN_ROWS // RUN_GROUP_ROWS)),
                            pltpu.SemaphoreType.DMA(())],
        ),
        out_shape=jax.ShapeDtypeStruct((T, D_MODEL), F32),
        compiler_params=_vmem_limit(32),
        name="combine",
    )(fetch, shift, nch, x1, rt, gt, y)


def _routing_tables(base, counts, T):
    tm = EXPERT_ROWS
    ne = N_EXPERTS
    n_tiles = T // ROUTE_ROWS
    n_blocks = -(-(T * TOP_K + ne * RUN_ROWS) // tm) + ne
    n_slots = n_blocks * tm
    sizes = counts[:, 0].astype(jnp.int32)
    psizes = (sizes + RUN_ROWS + tm - 1) // tm * tm
    pends = jnp.cumsum(psizes)
    pstarts = pends - psizes
    n_used = pends[-1:]
    padstart = jnp.concatenate([pstarts + sizes, n_used])
    npad = jnp.concatenate([psizes - sizes, n_slots - n_used])
    nb = n_blocks + 1
    blk = jnp.arange(nb, dtype=jnp.int32)
    block_expert = jnp.minimum(jnp.sum(pends[None, :] <= (blk * tm)[:, None], axis=1), ne - 1).astype(jnp.int32)
    first_block = jnp.concatenate([jnp.ones((1,), jnp.int32),
                                   (block_expert[1:] != block_expert[:-1]).astype(jnp.int32)])
    starts_group = jnp.logical_and(first_block == 1, blk * tm < n_used)
    pos = jnp.where(starts_group, blk, nb)
    later = jnp.concatenate([lax.cummin(pos[::-1])[::-1][1:], jnp.full((1,), nb, jnp.int32)])
    next_expert = jnp.where(later < nb, block_expert[jnp.minimum(later, nb - 1)], -1).astype(jnp.int32)
    tile_base = base[:, 0].astype(jnp.int32).reshape(n_tiles, ne)
    run_len = jnp.concatenate([tile_base[1:], sizes[None, :]], axis=0) - tile_base
    run_start = pstarts[None, :] + tile_base
    fetch = run_start // BF16_TILE_ROWS * BF16_TILE_ROWS
    shift = fetch - pstarts[None, :]
    nch = jnp.where(run_len > 0, (run_start - fetch + run_len + RUN_ROWS - 1) // RUN_ROWS, 0)
    wch = (run_start % BF16_TILE_ROWS + run_len + RUN_ROWS - 1) // RUN_ROWS
    flat = lambda a: a.reshape(n_tiles * ne).astype(jnp.int32)
    with_tile_max = lambda a, per_run: jnp.concatenate([flat(a), jnp.max(per_run, axis=1).astype(jnp.int32)])
    return (flat(run_start), with_tile_max(run_len, wch), flat(tile_base), padstart, npad, n_slots,
            block_expert, first_block, next_expert, n_used, flat(fetch), flat(shift), with_tile_max(nch, nch))


def kernel(x, norm1_g, w_in, q_norm_g, k_norm_g, conv_w, conv_b, w_mq, w_mk, b_igate, b_fgate, mh_norm_g,
           w_out, norm2_g, w_router, b_router, w_gate_up, b_gate_up, w_down, b_down):
    B, S, D = x.shape
    T = B * S
    assert D == D_MODEL and S % ATTN_ROWS == 0 and norm1_g.shape[0] == 1
    x2 = x.reshape(T, D)
    q, k, v, xm, vm, om, gates = _in_proj(x2, norm1_g[0], w_in[0], q_norm_g[0], k_norm_g[0])
    attn = _attention(q, k, v, B, S)
    ml = _mlstm(xm, vm, om, gates, conv_w[0], conv_b[0], w_mq[0], w_mk[0], b_igate[0], b_fgate[0],
                mh_norm_g[0], B, S)
    x1, h2, rt, gt, base, counts = _out_route(attn, ml, x2, w_out[0], norm2_g[0], w_router[0], b_router[0])
    (run_start, run_len, tile_base, padstart, npad, n_slots, block_expert, first_block, next_expert, n_used,
     fetch, shift, nch) = _routing_tables(base, counts, T)
    xs = _dispatch(h2, rt, run_start, run_len, tile_base, padstart, npad, n_slots)
    y = _experts(xs, block_expert, first_block, next_expert, n_used, w_gate_up[0], b_gate_up[0], w_down[0],
                 b_down[0])
    out = _combine(fetch, shift, nch, x1, rt, gt, y)
    return out.reshape(B, S, D)
```

```python
import jax
import jax.numpy as jnp
from jax import lax
from jax.experimental import pallas as pl
from jax.experimental.pallas import tpu as pltpu

F32 = jnp.float32
BF16 = jnp.bfloat16
NEG_INF = float("-inf")

D_MODEL = 1024
D_ATTN = 512
HEAD_DIM_ATTN = 64
N_HEADS_ATTN = 8
D_MLSTM = 512
HEAD_DIM_MLSTM = 128
N_HEADS_MLSTM = 4
CONV_K = 4
N_EXPERTS = 32
TOP_K = 4
D_FF = 1024
SWIGLU_LIMIT = 7.0
SWIGLU_ALPHA = 1.702
EPS = 1e-6
LOG2_E = 1.4426950408889634
DILATIONS = (1, 4, 16)
BRANCH_SPAN = 128

LANES = 128
SUBLANES = 8

IN_PROJ_ROWS = 512
ATTN_ROWS = 2048
MLSTM_CHUNK = 256
ROUTE_ROWS = 256
ROUTE_TILES_PER_STEP = 4
EXPERT_ROWS = 256
FF_CHUNK = 1024
CAST_ROWS = 128
DISPATCH_ROWS = 256
HEADNORM_LANES = 256
BF16_TILE_ROWS = 16
RUN_ROWS = 64
RUN_GROUP_ROWS = 256
NOT_ROUTED = -1.0e9


def _vmem_limit(mib):
    return pltpu.CompilerParams(vmem_limit_bytes=mib * 1024 * 1024)


def _split3(a):
    p1 = a.astype(BF16)
    r1 = a - p1.astype(F32)
    p2 = r1.astype(BF16)
    r2 = r1 - p2.astype(F32)
    return p1, p2, r2.astype(BF16)


def _dot(a, b):
    return jnp.dot(a, b, preferred_element_type=F32)


def _dot_nt(a, b):
    return lax.dot_general(a, b, (((1,), (1,)), ((), ())), preferred_element_type=F32)


def _dot_tn(a, b):
    return lax.dot_general(a, b, (((0,), (0,)), ((), ())), preferred_element_type=F32)


def _in_proj_kernel(x_ref, g1_ref, wqkv_ref, wm_ref, wgc_ref, qg_ref, kg_ref, hsum_ref,
                    q_ref, k_ref, v_ref, xm_ref, vm_ref, om_ref, gates_ref):
    x = x_ref[...]
    h = x * lax.rsqrt(jnp.mean(x * x, axis=-1, keepdims=True) + EPS) * g1_ref[...]
    hb = h.astype(BF16)
    hl = (h - hb.astype(F32)).astype(BF16)
    gw = HEADNORM_LANES

    def head_norm(z, g_ref):
        parts = []
        for c in range(D_ATTN // gw):
            zc = z[:, c * gw:(c + 1) * gw]
            ms = _dot((zc * zc).astype(BF16), hsum_ref[...]) * (1.0 / HEAD_DIM_ATTN)
            parts.append(zc * lax.rsqrt(ms + EPS))
        return jnp.concatenate(parts, axis=-1) * g_ref[...]

    zq = _dot(hb, wqkv_ref[:, 0:D_ATTN])
    q_ref[...] = head_norm(zq, qg_ref) * (HEAD_DIM_ATTN ** -0.5 * LOG2_E)
    zk = _dot(hb, wqkv_ref[:, D_ATTN:2 * D_ATTN])
    k_ref[...] = head_norm(zk, kg_ref)
    v_ref[...] = _dot(hb, wqkv_ref[:, 2 * D_ATTN:3 * D_ATTN])
    xm_ref[...] = _dot(hb, wm_ref[:, 0:D_MLSTM])
    vm_ref[...] = _dot(hb, wm_ref[:, D_MLSTM:2 * D_MLSTM])
    om_ref[...] = _dot(hb, wm_ref[:, 2 * D_MLSTM:3 * D_MLSTM])
    ng = 2 * N_HEADS_MLSTM
    rows = hb.shape[0]
    both = _dot(jnp.concatenate([hb, hl], axis=0), wgc_ref[...])
    gsum = both[:rows] + both[rows:]
    gsum = gsum + pltpu.roll(gsum, LANES - ng, axis=1)
    lane = lax.broadcasted_iota(jnp.int32, gsum.shape, 1)
    gates_ref[...] = jnp.where(lane < ng, gsum, 0.0)


def _in_proj(x2, g1, w_in, q_g, k_g):
    T = x2.shape[0]
    tm = IN_PROJ_ROWS
    n_qkv = 3 * D_ATTN
    n_m = 3 * D_MLSTM
    ng = 2 * N_HEADS_MLSTM
    wqkv = w_in[:, :n_qkv].astype(BF16)
    wm = w_in[:, n_qkv:n_qkv + n_m].astype(BF16)
    wg = w_in[:, n_qkv + n_m:]
    wgh = wg.astype(BF16)
    wgl = (wg - wgh.astype(F32)).astype(BF16)
    wgc = jnp.pad(jnp.concatenate([wgh, wgl], axis=1), ((0, 0), (0, LANES - 2 * ng)))
    lane = jnp.arange(HEADNORM_LANES)
    hsum = (lane[:, None] // HEAD_DIM_ATTN == lane[None, :] // HEAD_DIM_ATTN).astype(BF16)
    qg = jnp.tile(q_g, N_HEADS_ATTN)[None, :]
    kg = jnp.tile(k_g, N_HEADS_ATTN)[None, :]
    row = lambda n: pl.BlockSpec((tm, n), lambda i: (i, 0))
    full = lambda a: pl.BlockSpec(a.shape, lambda i: (0,) * a.ndim)
    outs = [jax.ShapeDtypeStruct((T, D_ATTN), F32)] * 3 + [jax.ShapeDtypeStruct((T, D_MLSTM), F32)] * 3
    outs.append(jax.ShapeDtypeStruct((T, LANES), F32))
    g1r = g1[None, :]
    return pl.pallas_call(
        _in_proj_kernel,
        grid=(T // tm,),
        in_specs=[row(D_MODEL), full(g1r), full(wqkv), full(wm), full(wgc), full(qg), full(kg), full(hsum)],
        out_specs=[row(D_ATTN)] * 3 + [row(D_MLSTM)] * 3 + [row(LANES)],
        out_shape=outs,
        compiler_params=_vmem_limit(48),
        name="in_proj",
    )(x2, g1r, wqkv, wm, wgc, qg, kg, hsum)


def _attn_tiles():
    tiles = []
    for br, d in enumerate(DILATIONS):
        group = d * BRANCH_SPAN
        for u in range(ATTN_ROWS // group):
            for c in range(d):
                tiles.append((br, d, u * group + c))
    return tiles


def _attn_kernel(slope_ref, q_ref, kp_ref, kc_ref, vp_ref, vc_ref, o_ref,
                 kwin, vwin, bias_scr, m_scr, l_scr, acc_scr):
    W = ATTN_ROWS
    n = BRANCH_SPAN
    step = pl.program_id(2)
    kwin[0:W, :] = kp_ref[...]
    kwin[W:2 * W, :] = kc_ref[...]
    vwin[0:W, :] = vp_ref[...]
    vwin[W:2 * W, :] = vc_ref[...]

    lane = lax.broadcasted_iota(jnp.int32, (n, LANES), 1)
    first_head = lane < HEAD_DIM_ATTN

    @pl.when(step == 0)
    def _():
        row = lax.broadcasted_iota(jnp.int32, (n, 2 * n), 0)
        col = lax.broadcasted_iota(jnp.int32, (n, 2 * n), 1)
        j = n + row - col
        valid = jnp.logical_and(j >= 0, j <= n)
        jf = j.astype(F32)
        for br, d in enumerate(DILATIONS):
            for hh in range(2):
                slope = slope_ref[:, hh * HEAD_DIM_ATTN:hh * HEAD_DIM_ATTN + 1]
                b = jnp.where(valid, -(slope * (float(d) * LOG2_E)) * jf, NEG_INF)
                bias_scr[br, hh, 0] = b
                bias_scr[br, hh, 1] = jnp.where(col >= n, b, NEG_INF)

    for br, d, qs in _attn_tiles():
        lo = W + qs - n * d
        q = q_ref[pl.ds(qs, n, stride=d), :]
        kk = jnp.concatenate([kwin[pl.ds(lo, n, stride=d), :], kwin[pl.ds(W + qs, n, stride=d), :]],
                             axis=0).astype(BF16)
        vv = jnp.concatenate([vwin[pl.ds(lo, n, stride=d), :], vwin[pl.ds(W + qs, n, stride=d), :]],
                             axis=0).astype(BF16)
        if lo < W:
            sel = jnp.where(step == 0, 1, 0)
        else:
            sel = 0
        qb = q.astype(BF16)
        zero = jnp.zeros_like(qb)
        q2 = jnp.concatenate([jnp.where(first_head, qb, zero), jnp.where(first_head, zero, qb)], axis=0)
        s2 = _dot_nt(q2, kk)
        ms, ls, ps = [], [], []
        for hh in range(2):
            s = s2[hh * n:(hh + 1) * n] + bias_scr[br, hh, sel]
            m = jnp.max(s, axis=-1, keepdims=True)
            p = jnp.exp2(s - m)
            ms.append(m)
            ls.append(jnp.sum(p, axis=-1, keepdims=True))
            ps.append(p.astype(BF16))
        pv = _dot(jnp.concatenate(ps, axis=0), vv)
        rows = pl.ds(qs, n, stride=d)
        m_scr[br, rows, :] = jnp.where(first_head, ms[0], ms[1])
        l_scr[br, rows, :] = jnp.where(first_head, ls[0], ls[1])
        acc_scr[br, rows, :] = jnp.where(first_head, pv[:n], pv[n:])

    chunk = 256

    def combine(i, carry):
        r = pl.ds(pl.multiple_of(i * chunk, chunk), chunk)
        m0, m1, m2 = m_scr[0, r, :], m_scr[1, r, :], m_scr[2, r, :]
        mm = jnp.maximum(jnp.maximum(m0, m1), m2)
        w0, w1, w2 = jnp.exp2(m0 - mm), jnp.exp2(m1 - mm), jnp.exp2(m2 - mm)
        num = w0 * acc_scr[0, r, :] + w1 * acc_scr[1, r, :] + w2 * acc_scr[2, r, :]
        den = w0 * l_scr[0, r, :] + w1 * l_scr[1, r, :] + w2 * l_scr[2, r, :]
        o_ref[r, :] = num / den
        return carry

    lax.fori_loop(0, W // chunk, combine, 0)


def _attention(q, k, v, B, S):
    W = ATTN_ROWS
    nb = S // W
    n_pairs = D_ATTN // LANES
    slopes = jnp.exp2(-8.0 * jnp.arange(1, N_HEADS_ATTN + 1, dtype=F32) / N_HEADS_ATTN)
    slope_l = jnp.repeat(slopes, HEAD_DIM_ATTN).reshape(n_pairs, 1, LANES)
    cur = pl.BlockSpec((W, LANES), lambda b, p, i: (b * nb + i, p))
    prev = pl.BlockSpec((W, LANES), lambda b, p, i: (b * nb + jnp.maximum(i - 1, 0), p))
    n = BRANCH_SPAN
    nbr = len(DILATIONS)
    return pl.pallas_call(
        _attn_kernel,
        grid=(B, n_pairs, nb),
        in_specs=[pl.BlockSpec((None, 1, LANES), lambda b, p, i: (p, 0, 0)), cur, prev, cur, prev, cur],
        out_specs=cur,
        out_shape=jax.ShapeDtypeStruct((B * S, D_ATTN), F32),
        scratch_shapes=[
            pltpu.VMEM((2 * W, LANES), F32), pltpu.VMEM((2 * W, LANES), F32),
            pltpu.VMEM((nbr, 2, 2, n, 2 * n), F32),
            pltpu.VMEM((nbr, W, LANES), F32), pltpu.VMEM((nbr, W, LANES), F32),
            pltpu.VMEM((nbr, W, LANES), F32),
        ],
        compiler_params=_vmem_limit(48),
        name="attention",
    )(slope_l, q, k, k, v, v)


def _mlstm_kernel(xm_ref, vm_ref, om_ref, gcol_ref, grow_ref, cw_ref, cb_ref, wq_ref, wk_ref,
                  bcol_ref, brow_ref, g_ref, o_ref, xs_ref, xtail_ref, c_ref, n_ref, m_ref):
    L = MLSTM_CHUNK
    dh = HEAD_DIM_MLSTM
    nh = N_HEADS_MLSTM

    @pl.when(pl.program_id(1) == 0)
    def _():
        xtail_ref[...] = jnp.zeros_like(xtail_ref)
        c_ref[...] = jnp.zeros_like(c_ref)
        n_ref[...] = jnp.zeros_like(n_ref)
        m_ref[...] = jnp.zeros_like(m_ref)

    x = xm_ref[...]
    xs_ref[0:SUBLANES, :] = xtail_ref[...]
    xs_ref[SUBLANES:SUBLANES + L, :] = x
    xtail_ref[...] = x[L - SUBLANES:L, :]
    xc = cb_ref[...] + x * cw_ref[CONV_K - 1:CONV_K, :]
    for back in range(1, CONV_K):
        xc = xc + xs_ref[SUBLANES - back:SUBLANES - back + L, :] * cw_ref[CONV_K - 1 - back:CONV_K - back, :]
    xc = xc * jax.nn.sigmoid(xc)

    gc = gcol_ref[...] + bcol_ref[...]
    gr = grow_ref[...] + brow_ref[...]
    ri = lax.broadcasted_iota(jnp.int32, (L, L), 0)
    ci = lax.broadcasted_iota(jnp.int32, (L, L), 1)
    causal = ri >= ci
    tril = jnp.where(causal, 1.0, 0.0).astype(BF16)
    triu = jnp.where(ri <= ci, 1.0, 0.0).astype(BF16)
    c1, c2, c3 = _split3(jax.nn.log_sigmoid(gc))
    bc = _dot(tril, c1) + _dot(tril, c2) + _dot(tril, c3)
    r1, r2, r3 = _split3(jax.nn.log_sigmoid(gr))
    brw = _dot(r1, triu) + _dot(r2, triu) + _dot(r3, triu)

    heads = range(nh)
    hs = [slice(h * dh, (h + 1) * dh) for h in heads]
    xh = [xc[:, hs[h]].astype(BF16) for h in heads]
    qf = [_dot(xh[h], wq_ref[h]) for h in heads]
    kf = [_dot(xh[h], wk_ref[h]) * (dh ** -0.5) for h in heads]
    qb = [qf[h].astype(BF16) for h in heads]
    vb = [vm_ref[:, hs[h]].astype(BF16) for h in heads]

    b_col = [bc[:, nh + h:nh + h + 1] for h in heads]
    a_col = [gc[:, h:h + 1] - b_col[h] for h in heads]
    a_row = [gr[h:h + 1, :] - brw[nh + h:nh + h + 1, :] for h in heads]
    m_prev = [m_ref[h:h + 1, 0:1] for h in heads]
    log_d = [jnp.where(causal, b_col[h] + a_row[h], NEG_INF) for h in heads]
    log_inter = [b_col[h] + m_prev[h] for h in heads]
    m_t = [jnp.maximum(log_inter[h], jnp.max(log_d[h], axis=-1, keepdims=True)) for h in heads]
    d = [jnp.exp(log_d[h] - m_t[h]) for h in heads]
    inter = [jnp.exp(log_inter[h] - m_t[h]) for h in heads]

    s = [_dot_nt(qb[h], kf[h].astype(BF16)) * d[h] for h in heads]
    c_old = [c_ref[h] for h in heads]
    n_old = [n_ref[h:h + 1, :] for h in heads]
    num = [_dot(s[h].astype(BF16), vb[h]) + inter[h] * _dot(qb[h], c_old[h].astype(BF16)) for h in heads]
    den = [jnp.sum(s[h], axis=-1, keepdims=True) + inter[h] * jnp.sum(qf[h] * n_old[h], axis=-1, keepdims=True)
           for h in heads]
    hh = [num[h] / jnp.maximum(jnp.abs(den[h]), jnp.exp(-m_t[h])) for h in heads]

    b_last = [b_col[h][L - 1:L, :] for h in heads]
    log_w = [b_last[h] + a_col[h] for h in heads]
    m_new = [jnp.maximum(b_last[h] + m_prev[h], jnp.max(log_w[h], axis=0, keepdims=True)) for h in heads]
    kw = [kf[h] * jnp.exp(log_w[h] - m_new[h]) for h in heads]
    decay = [jnp.exp(b_last[h] + m_prev[h] - m_new[h]) for h in heads]
    upd = [_dot_tn(kw[h].astype(BF16), vb[h]) for h in heads]
    for h in heads:
        c_ref[h] = decay[h] * c_old[h] + upd[h]
        n_ref[h:h + 1, :] = decay[h] * n_old[h] + jnp.sum(kw[h], axis=0, keepdims=True)
        m_ref[h:h + 1, :] = jnp.broadcast_to(m_new[h], (1, LANES))

    for h in heads:
        hn = hh[h] * lax.rsqrt(jnp.mean(hh[h] * hh[h], axis=-1, keepdims=True) + EPS) * g_ref[:, hs[h]]
        o_ref[:, hs[h]] = jax.nn.sigmoid(om_ref[:, hs[h]]) * hn


def _mlstm(xm, vm, om, gates, conv_w, conv_b, w_mq, w_mk, b_i, b_f, mh_g, B, S):
    L = MLSTM_CHUNK
    nc = S // L
    nh = N_HEADS_MLSTM
    ng = 2 * nh
    grow = gates[:, :ng].reshape(B, S, ng).transpose(0, 2, 1)
    bias = jnp.concatenate([b_i, b_f])
    bcol = jnp.pad(bias, (0, LANES - ng))[None, :]
    brow = bias[:, None]
    wq = w_mq.astype(BF16)
    wk = w_mk.astype(BF16)
    cb = conv_b[None, :]
    g = mh_g.reshape(1, D_MLSTM)
    rows = pl.BlockSpec((L, D_MLSTM), lambda b, c: (b * nc + c, 0))
    full = lambda a: pl.BlockSpec(a.shape, lambda b, c: (0,) * a.ndim)
    return pl.pallas_call(
        _mlstm_kernel,
        grid=(B, nc),
        in_specs=[rows, rows, rows, pl.BlockSpec((L, LANES), lambda b, c: (b * nc + c, 0)),
                  pl.BlockSpec((None, ng, L), lambda b, c: (b, 0, c)),
                  full(conv_w), full(cb), full(wq), full(wk), full(bcol), full(brow), full(g)],
        out_specs=rows,
        out_shape=jax.ShapeDtypeStruct((B * S, D_MLSTM), F32),
        scratch_shapes=[
            pltpu.VMEM((L + SUBLANES, D_MLSTM), F32), pltpu.VMEM((SUBLANES, D_MLSTM), F32),
            pltpu.VMEM((nh, HEAD_DIM_MLSTM, HEAD_DIM_MLSTM), F32),
            pltpu.VMEM((SUBLANES, LANES), F32), pltpu.VMEM((SUBLANES, LANES), F32),
        ],
        compiler_params=_vmem_limit(32),
        name="mlstm",
    )(xm, vm, om, gates, grow, conv_w, cb, wq, wk, bcol, brow, g)


def _out_route_kernel(attn_ref, ml_ref, x_ref, wo_ref, g2_ref, wr_ref, br_ref, earlier_ref,
                      x1_ref, h2_ref, rt_ref, gt_ref, base_ref, cnt_ref, carry_ref):
    @pl.when(pl.program_id(0) == 0)
    def _():
        carry_ref[...] = jnp.zeros_like(carry_ref)

    tiles = range(ROUTE_TILES_PER_STEP)
    split = [_project_tile(j, attn_ref, ml_ref, x_ref, wo_ref, g2_ref, x1_ref, h2_ref) for j in tiles]
    logits = [_router_logits(hb, hl, wr_ref, br_ref) for hb, hl in split]
    carry = carry_ref[...]
    for j in tiles:
        carry = _route_tile(j, logits[j], carry, earlier_ref, rt_ref, gt_ref, base_ref)
    carry_ref[...] = carry
    cnt_ref[...] = carry


def _project_tile(j, attn_ref, ml_ref, x_ref, wo_ref, g2_ref, x1_ref, h2_ref):
    tm = ROUTE_ROWS
    rows = slice(j * tm, (j + 1) * tm)
    x1 = (x_ref[rows, :] + _dot(attn_ref[rows, :].astype(BF16), wo_ref[0:D_ATTN, :])
          + _dot(ml_ref[rows, :].astype(BF16), wo_ref[D_ATTN:D_ATTN + D_MLSTM, :]))
    x1_ref[rows, :] = x1
    h2 = x1 * lax.rsqrt(jnp.mean(x1 * x1, axis=-1, keepdims=True) + EPS) * g2_ref[...]
    hb = h2.astype(BF16)
    h2_ref[rows, :] = hb
    return hb, (h2 - hb.astype(F32)).astype(BF16)


def _router_logits(hb, hl, wr_ref, br_ref):
    ne = N_EXPERTS
    hi_pass = _dot_nt(wr_ref[...], hb)
    return (hi_pass[:ne] + hi_pass[ne:] + _dot_nt(wr_ref[0:ne, :], hl)) + br_ref[...]


def _route_tile(j, lt, carry, earlier_ref, rt_ref, gt_ref, base_ref):
    tm = ROUTE_ROWS
    ne = N_EXPERTS
    rows = slice(j * tm, (j + 1) * tm)

    eidx = lax.broadcasted_iota(jnp.int32, (ne, tm), 0)
    vals, hots = [], []
    work = lt
    for _ in range(TOP_K):
        mx = jnp.max(work, axis=0, keepdims=True)
        idx = jnp.min(jnp.where(work == mx, eidx, ne), axis=0, keepdims=True)
        hot = eidx == idx
        vals.append(mx)
        hots.append(hot)
        work = jnp.where(hot, NEG_INF, work)
    ex = [jnp.exp(v - vals[0]) for v in vals]
    tot = ex[0] + ex[1] + ex[2] + ex[3]

    onehot = jnp.zeros((ne, tm), F32)
    gt = jnp.zeros((ne, tm), F32)
    for kk in range(TOP_K):
        onehot = onehot + jnp.where(hots[kk], 1.0, 0.0)
        gt = gt + jnp.where(hots[kk], ex[kk] / tot, 0.0)
    before = _dot(onehot.astype(BF16), earlier_ref[...]) + carry[:, 0:1]
    base_ref[j * ne:(j + 1) * ne, :] = carry
    rt_ref[:, rows] = jnp.where(onehot > 0.5, before, NOT_ROUTED).astype(jnp.int32)
    gt_ref[:, rows] = gt
    return carry + jnp.sum(onehot, axis=1, keepdims=True)


def _out_route(attn, ml, x2, w_out, g2, w_router, b_router):
    T = x2.shape[0]
    tm = ROUTE_ROWS
    ne = N_EXPERTS
    wo = w_out.astype(BF16)
    wrt = w_router.T
    wrh = wrt.astype(BF16)
    wr = jnp.concatenate([wrh, (wrt - wrh.astype(F32)).astype(BF16)], axis=0)
    br = b_router[:, None]
    g2r = g2[None, :]
    tok = jnp.arange(tm)
    earlier = (tok[:, None] < tok[None, :]).astype(BF16)
    ts = tm * ROUTE_TILES_PER_STEP
    row = lambda n: pl.BlockSpec((ts, n), lambda i: (i, 0))
    col = lambda n: pl.BlockSpec((n, ts), lambda i: (0, i))
    full = lambda a: pl.BlockSpec(a.shape, lambda i: (0,) * a.ndim)
    return pl.pallas_call(
        _out_route_kernel,
        grid=(T // ts,),
        in_specs=[row(D_ATTN), row(D_MLSTM), row(D_MODEL), full(wo), full(g2r), full(wr), full(br),
                  full(earlier)],
        out_specs=[row(D_MODEL), row(D_MODEL), col(ne), col(ne),
                   pl.BlockSpec((ROUTE_TILES_PER_STEP * ne, LANES), lambda i: (i, 0)),
                   pl.BlockSpec((ne, LANES), lambda i: (0, 0))],
        out_shape=[jax.ShapeDtypeStruct((T, D_MODEL), F32), jax.ShapeDtypeStruct((T, D_MODEL), BF16),
                   jax.ShapeDtypeStruct((ne, T), jnp.int32), jax.ShapeDtypeStruct((ne, T), F32),
                   jax.ShapeDtypeStruct((T // tm * ne, LANES), F32), jax.ShapeDtypeStruct((ne, LANES), F32)],
        scratch_shapes=[pltpu.VMEM((ne, LANES), F32)],
        compiler_params=_vmem_limit(32),
        name="out_route",
    )(attn, ml, x2, wo, g2r, wr, br, earlier)


def _dispatch_kernel(runstart_ref, runlen_ref, lshift_ref, padstart_ref, npad_ref,
                     h_ref, rt_ref, xs_ref, cbufs, xbuf, zero_ref, stage_ref, sem, xsem, zsem):
    td = DISPATCH_ROWS
    ne = N_EXPERTS
    rb = RUN_ROWS
    ta = BF16_TILE_ROWS
    grp = RUN_GROUP_ROWS // rb
    tile = pl.program_id(0)
    cbuf = cbufs.at[tile % 2]

    def window(i):
        off = runstart_ref[i] & (ta - 1)
        return off, off + runlen_ref[i]

    def first_chunk(tile_idx, e, action):
        i = tile_idx * ne + e
        off, _ = window(i)
        dst = pl.multiple_of(runstart_ref[i] - off, ta)
        action(pltpu.make_async_copy(cbufs.at[tile_idx % 2, pl.ds(e * rb, rb)], xs_ref.at[pl.ds(dst, rb)], sem))

    @pl.when(tile == 0)
    def _():
        stage_ref[...] = jnp.zeros_like(stage_ref)

    hb = h_ref[...]
    riota = lax.broadcasted_iota(jnp.int32, (rb, td), 0)
    rowt = lax.broadcasted_iota(jnp.int32, (ta, D_MODEL), 0)

    def select(e_slice, shift):
        return jnp.where(riota == rt_ref[e_slice, :] - shift, 1.0, 0.0)

    for g in range(ne // grp):
        experts = range(g * grp, (g + 1) * grp)
        parts = [select(slice(e, e + 1), lshift_ref[tile * ne + e] - window(tile * ne + e)[0]) for e in experts]
        pt = jnp.concatenate(parts, axis=0).astype(BF16)
        cbuf[g * RUN_GROUP_ROWS:(g + 1) * RUN_GROUP_ROWS, :] = _dot(pt, hb).astype(BF16)

    for e in range(ne):
        i = tile * ne + e
        off, total = window(i)
        head = pl.ds(e * rb, ta)
        old = stage_ref[e]
        cbuf[head, :] = jnp.where(rowt < off, old, cbuf[head, :])
        last = e * rb + jnp.minimum(total // ta * ta, rb - ta)
        keep = jnp.logical_and(runlen_ref[i] > 0, total < rb)
        stage_ref[e] = jnp.where(keep, cbuf[pl.ds(pl.multiple_of(last, ta), ta), :], old)

    @pl.when(tile > 0)
    def _():
        for e in range(ne):
            first_chunk(tile - 1, e, lambda cp: cp.wait())

    for e in range(ne):
        first_chunk(tile, e, lambda cp: cp.start())

    def per_expert(e, carry):
        i = tile * ne + e
        off, total = window(i)

        def chunk(ch, c):
            pt = select(pl.ds(e, 1), lshift_ref[i] - off + ch * rb).astype(BF16)
            xbuf[...] = _dot(pt, hb).astype(BF16)
            dst = pl.multiple_of(runstart_ref[i] - off + ch * rb, ta)
            cp = pltpu.make_async_copy(xbuf, xs_ref.at[pl.ds(dst, rb)], xsem)
            cp.start()

            @pl.when(jnp.logical_and(ch == total // rb, total % rb != 0))
            def _():
                last = pl.multiple_of(total % rb // ta * ta, ta)
                stage_ref[e] = xbuf[pl.ds(last, ta), :]

            cp.wait()
            return c

        lax.fori_loop(1, (total + rb - 1) // rb, chunk, 0)
        return carry

    @pl.when(runlen_ref[pl.num_programs(0) * ne + tile] > 1)
    def _():
        lax.fori_loop(0, ne, per_expert, 0)

    @pl.when(tile == pl.num_programs(0) - 1)
    def _():
        for e in range(ne):
            first_chunk(tile, e, lambda cp: cp.wait())

    @pl.when(tile == pl.num_programs(0) - 1)
    def _():
        zero_ref[...] = jnp.zeros_like(zero_ref)

        def fill(wait):
            def per_run(e, carry):
                start = padstart_ref[e]
                head = (-start) & (ta - 1)
                off = pl.multiple_of(start + head, ta)
                left = npad_ref[e] - head

                def whole(b, c):
                    dst = xs_ref.at[pl.ds(pl.multiple_of(off + b * td, ta), td)]
                    cp = pltpu.make_async_copy(zero_ref, dst, zsem)
                    cp.wait() if wait else cp.start()
                    return c

                n_whole = left // td
                lax.fori_loop(0, n_whole, whole, 0)
                off = pl.multiple_of(off + n_whole * td, ta)
                p = td // 2
                while p >= ta:
                    take = (left & p) != 0

                    @pl.when(take)
                    def _(off=off, p=p):
                        cp = pltpu.make_async_copy(zero_ref.at[pl.ds(0, p)], xs_ref.at[pl.ds(off, p)], zsem)
                        cp.wait() if wait else cp.start()

                    off = pl.multiple_of(off + jnp.where(take, p, 0), ta)
                    p //= 2
                return carry

            lax.fori_loop(0, N_EXPERTS + 1, per_run, 0)

        fill(wait=False)
        fill(wait=True)


def _dispatch(h2, rt, runstart, runlen, lshift, padstart, npad, n_slots):
    T = h2.shape[0]
    td = DISPATCH_ROWS
    ne = N_EXPERTS
    assert td == ROUTE_ROWS
    return pl.pallas_call(
        _dispatch_kernel,
        grid_spec=pltpu.PrefetchScalarGridSpec(
            num_scalar_prefetch=5,
            grid=(T // td,),
            in_specs=[pl.BlockSpec((td, D_MODEL), lambda i, *_: (i, 0)),
                      pl.BlockSpec((ne, td), lambda i, *_: (0, i))],
            out_specs=pl.BlockSpec(memory_space=pl.ANY),
            scratch_shapes=[pltpu.VMEM((2, ne * RUN_ROWS, D_MODEL), BF16), pltpu.VMEM((RUN_ROWS, D_MODEL), BF16),
                            pltpu.VMEM((td, D_MODEL), BF16), pltpu.VMEM((ne, BF16_TILE_ROWS, D_MODEL), BF16),
                            pltpu.SemaphoreType.DMA(()), pltpu.SemaphoreType.DMA(()),
                            pltpu.SemaphoreType.DMA(())],
        ),
        out_shape=jax.ShapeDtypeStruct((n_slots, D_MODEL), BF16),
        compiler_params=_vmem_limit(32),
        name="dispatch",
    )(runstart, runlen, lshift, padstart, npad, h2, rt)


def _expert_kernel(bexp_ref, first_ref, next_ref, nused_ref, xs_ref, wgu_hbm, bgu_ref, wd_hbm, bd_ref, y_ref,
                   wgu_st, wd_st, wgu_bf, wd_bf, wsem):
    tm = EXPERT_ROWS
    blk = pl.program_id(0)
    active = blk * tm < nused_ref[0]

    def weight_copies(e):
        return (pltpu.make_async_copy(wgu_hbm.at[e], wgu_st, wsem.at[0]),
                pltpu.make_async_copy(wd_hbm.at[e], wd_st, wsem.at[1]))

    @pl.when(blk == 0)
    def _():
        for cp in weight_copies(bexp_ref[0]):
            cp.start()

    @pl.when(jnp.logical_and(active, first_ref[blk] == 1))
    def _():
        for cp in weight_copies(bexp_ref[blk]):
            cp.wait()
        rc = CAST_ROWS

        def cast(i, carry):
            r = pl.ds(pl.multiple_of(i * rc, rc), rc)
            wgu_bf[r, :] = wgu_st[r, :].astype(BF16)
            wd_bf[r, :] = wd_st[r, :].astype(BF16)
            return carry

        lax.fori_loop(0, D_MODEL // rc, cast, 0)

        @pl.when(next_ref[blk] >= 0)
        def _():
            for cp in weight_copies(next_ref[blk]):
                cp.start()

    @pl.when(active)
    def _():
        xb = xs_ref[...]
        acc = jnp.zeros((tm, D_MODEL), F32)
        for f in range(D_FF // FF_CHUNK):
            lo = f * FF_CHUNK
            g = _dot(xb, wgu_bf[:, lo:lo + FF_CHUNK]) + bgu_ref[:, lo:lo + FF_CHUNK]
            u = _dot(xb, wgu_bf[:, D_FF + lo:D_FF + lo + FF_CHUNK]) + bgu_ref[:, D_FF + lo:D_FF + lo + FF_CHUNK]
            g = jnp.minimum(g, SWIGLU_LIMIT)
            u = jnp.clip(u, -SWIGLU_LIMIT, SWIGLU_LIMIT)
            act = g * jax.nn.sigmoid(SWIGLU_ALPHA * g) * (u + 1.0)
            acc = acc + _dot(act.astype(BF16), wd_bf[lo:lo + FF_CHUNK, :])
        y_ref[...] = (acc + bd_ref[...]).astype(y_ref.dtype)

    @pl.when(jnp.logical_not(active))
    def _():
        y_ref[...] = jnp.zeros_like(y_ref)


def _experts(xs, block_expert, first_block, next_expert, n_used, w_gate_up, b_gate_up, w_down, b_down):
    n_blocks = block_expert.shape[0]
    tm = EXPERT_ROWS
    assert D_FF == D_MODEL and xs.shape[0] == (n_blocks - 1) * tm
    bgu = b_gate_up[:, None, :]
    bd = b_down[:, None, :]
    last = n_blocks - 2
    return pl.pallas_call(
        _expert_kernel,
        grid_spec=pltpu.PrefetchScalarGridSpec(
            num_scalar_prefetch=4,
            grid=(n_blocks,),
            in_specs=[
                pl.BlockSpec((tm, D_MODEL), lambda i, be, fb, nx, nu: (jnp.minimum(i, last), 0)),
                pl.BlockSpec(memory_space=pl.ANY),
                pl.BlockSpec((None, 1, 2 * D_FF), lambda i, be, fb, nx, nu: (be[i], 0, 0)),
                pl.BlockSpec(memory_space=pl.ANY),
                pl.BlockSpec((None, 1, D_MODEL), lambda i, be, fb, nx, nu: (be[i], 0, 0)),
            ],
            out_specs=pl.BlockSpec((tm, D_MODEL), lambda i, be, fb, nx, nu: (i, 0)),
            scratch_shapes=[pltpu.VMEM((D_MODEL, 2 * D_FF), F32), pltpu.VMEM((D_FF, D_MODEL), F32),
                            pltpu.VMEM((D_MODEL, 2 * D_FF), BF16), pltpu.VMEM((D_FF, D_MODEL), BF16),
                            pltpu.SemaphoreType.DMA((2,))],
        ),
        out_shape=jax.ShapeDtypeStruct((n_blocks * tm, D_MODEL), BF16),
        compiler_params=_vmem_limit(48),
        name="experts",
    )(block_expert, first_block, next_expert, n_used, xs, w_gate_up, bgu, w_down, bd)


def _combine_kernel(fetch_ref, shift_ref, nch_ref, x1_ref, rt_ref, gt_ref, y_ref, o_ref,
                    buf, xbuf, sem, xsem):
    tc = ROUTE_ROWS
    ne = N_EXPERTS
    rb = RUN_ROWS
    grp = RUN_GROUP_ROWS // rb
    tile = pl.program_id(0)
    base = tile * ne
    slot = tile % 2

    def run_copy(tile_base, buf_slot, e):
        start = pl.multiple_of(fetch_ref[tile_base + e], BF16_TILE_ROWS)
        return pltpu.make_async_copy(y_ref.at[pl.ds(start, rb)], buf.at[buf_slot, pl.ds(e * rb, rb)],
                                     sem.at[buf_slot, e // grp])

    @pl.when(tile == 0)
    def _():
        for e in range(ne):
            run_copy(base, slot, e).start()

    @pl.when(tile + 1 < pl.num_programs(0))
    def _():
        for e in range(ne):
            run_copy(base + ne, 1 - slot, e).start()

    riota = lax.broadcasted_iota(jnp.int32, (rb, tc), 0)

    def select(e_slice, shift):
        return jnp.where(riota == rt_ref[e_slice, :] - shift, gt_ref[e_slice, :], 0.0)

    parts = []
    for e in range(ne):
        run_copy(base, slot, e).wait()
        parts.append(select(slice(e, e + 1), shift_ref[base + e]).astype(BF16))
    o_ref[...] = x1_ref[...] + _dot_tn(jnp.concatenate(parts, axis=0), buf[slot])

    def per_expert(e, carry):
        def chunk(ch, c):
            start = pl.multiple_of(fetch_ref[base + e] + ch * rb, BF16_TILE_ROWS)
            cp = pltpu.make_async_copy(y_ref.at[pl.ds(start, rb)], xbuf, xsem)
            cp.start()
            cp.wait()
            pt = select(pl.ds(e, 1), shift_ref[base + e] + ch * rb).astype(BF16)
            o_ref[...] += _dot_tn(pt, xbuf[...])
            return c

        lax.fori_loop(1, nch_ref[base + e], chunk, 0)
        return carry

    @pl.when(nch_ref[pl.num_programs(0) * ne + tile] > 1)
    def _():
        lax.fori_loop(0, ne, per_expert, 0)


def _combine(fetch, shift, nch, x1, rt, gt, y):
    T = x1.shape[0]
    tc = ROUTE_ROWS
    ne = N_EXPERTS
    return pl.pallas_call(
        _combine_kernel,
        grid_spec=pltpu.PrefetchScalarGridSpec(
            num_scalar_prefetch=3,
            grid=(T // tc,),
            in_specs=[pl.BlockSpec((tc, D_MODEL), lambda i, f, s, n: (i, 0)),
                      pl.BlockSpec((ne, tc), lambda i, f, s, n: (0, i)),
                      pl.BlockSpec((ne, tc), lambda i, f, s, n: (0, i)),
                      pl.BlockSpec(memory_space=pl.ANY)],
            out_specs=pl.BlockSpec((tc, D_MODEL), lambda i, f, s, n: (i, 0)),
            scratch_shapes=[pltpu.VMEM((2, ne * RUN_ROWS, D_MODEL), BF16), pltpu.VMEM((RUN_ROWS, D_MODEL), BF16),
                            pltpu.SemaphoreType.DMA((2, ne * RUN_ROWS // RUN_GROUP_ROWS)),
                            pltpu.SemaphoreType.DMA(())],
        ),
        out_shape=jax.ShapeDtypeStruct((T, D_MODEL), F32),
        compiler_params=_vmem_limit(32),
        name="combine",
    )(fetch, shift, nch, x1, rt, gt, y)


def _routing_tables(base, counts, T):
    tm = EXPERT_ROWS
    ne = N_EXPERTS
    n_tiles = T // ROUTE_ROWS
    n_blocks = -(-(T * TOP_K + ne * RUN_ROWS) // tm) + ne
    n_slots = n_blocks * tm
    sizes = counts[:, 0].astype(jnp.int32)
    psizes = (sizes + RUN_ROWS + tm - 1) // tm * tm
    pends = jnp.cumsum(psizes)
    pstarts = pends - psizes
    n_used = pends[-1:]
    padstart = jnp.concatenate([pstarts + sizes, n_used])
    npad = jnp.concatenate([psizes - sizes, n_slots - n_used])
    nb = n_blocks + 1
    blk = jnp.arange(nb, dtype=jnp.int32)
    block_expert = jnp.minimum(jnp.sum(pends[None, :] <= (blk * tm)[:, None], axis=1), ne - 1).astype(jnp.int32)
    first_block = jnp.concatenate([jnp.ones((1,), jnp.int32),
                                   (block_expert[1:] != block_expert[:-1]).astype(jnp.int32)])
    starts_group = jnp.logical_and(first_block == 1, blk * tm < n_used)
    pos = jnp.where(starts_group, blk, nb)
    later = jnp.concatenate([lax.cummin(pos[::-1])[::-1][1:], jnp.full((1,), nb, jnp.int32)])
    next_expert = jnp.where(later < nb, block_expert[jnp.minimum(later, nb - 1)], -1).astype(jnp.int32)
    tile_base = base[:, 0].astype(jnp.int32).reshape(n_tiles, ne)
    run_len = jnp.concatenate([tile_base[1:], sizes[None, :]], axis=0) - tile_base
    run_start = pstarts[None, :] + tile_base
    fetch = run_start // BF16_TILE_ROWS * BF16_TILE_ROWS
    shift = fetch - pstarts[None, :]
    nch = jnp.where(run_len > 0, (run_start - fetch + run_len + RUN_ROWS - 1) // RUN_ROWS, 0)
    wch = (run_start % BF16_TILE_ROWS + run_len + RUN_ROWS - 1) // RUN_ROWS
    flat = lambda a: a.reshape(n_tiles * ne).astype(jnp.int32)
    with_tile_max = lambda a, per_run: jnp.concatenate([flat(a), jnp.max(per_run, axis=1).astype(jnp.int32)])
    return (flat(run_start), with_tile_max(run_len, wch), flat(tile_base), padstart, npad, n_slots,
            block_expert, first_block, next_expert, n_used, flat(fetch), flat(shift), with_tile_max(nch, nch))


def kernel(x, norm1_g, w_in, q_norm_g, k_norm_g, conv_w, conv_b, w_mq, w_mk, b_igate, b_fgate, mh_norm_g,
           w_out, norm2_g, w_router, b_router, w_gate_up, b_gate_up, w_down, b_down):
    B, S, D = x.shape
    T = B * S
    assert D == D_MODEL and S % ATTN_ROWS == 0 and norm1_g.shape[0] == 1
    x2 = x.reshape(T, D)
    q, k, v, xm, vm, om, gates = _in_proj(x2, norm1_g[0], w_in[0], q_norm_g[0], k_norm_g[0])
    attn = _attention(q, k, v, B, S)
    ml = _mlstm(xm, vm, om, gates, conv_w[0], conv_b[0], w_mq[0], w_mk[0], b_igate[0], b_fgate[0],
                mh_norm_g[0], B, S)
    x1, h2, rt, gt, base, counts = _out_route(attn, ml, x2, w_out[0], norm2_g[0], w_router[0], b_router[0])
    (run_start, run_len, tile_base, padstart, npad, n_slots, block_expert, first_block, next_expert, n_used,
     fetch, shift, nch) = _routing_tables(base, counts, T)
    xs = _dispatch(h2, rt, run_start, run_len, tile_base, padstart, npad, n_slots)
    y = _experts(xs, block_expert, first_block, next_expert, n_used, w_gate_up[0], b_gate_up[0], w_down[0],
                 b_down[0])
    out = _combine(fetch, shift, nch, x1, rt, gt, y)
    return out.reshape(B, S, D)
```
